```python
import jax, jax.numpy as jnp
from jax import lax
import numpy as np

D_MODEL = 2048
BATCH = 4
SEQ = 4096
DEPTH = 2

GRID_W = 64
ROPE_THETA = 10000.0
EPS = 1e-6
HEAD_DIM = 128
D_MIX = D_MODEL
Q_BLOCK = 128
A_HEADS = 6
A_KV_HEADS = 2
B_HEADS = 6
B_Q_LORA = 384
B_KV_LORA = 256
B_QK_NOPE = 128
B_QK_ROPE = 64
B_V_DIM = 128
C_HEADS = 4
C_DQK = 128
C_DV = 128
CONV_W = 5
MLSTM_CHUNK = 64
C_N_GATES = 4 * C_HEADS
PROJ_WIDTHS = (A_HEADS * HEAD_DIM, A_KV_HEADS * HEAD_DIM, A_KV_HEADS * HEAD_DIM,
               B_Q_LORA, B_KV_LORA, B_QK_ROPE,
               C_HEADS * C_DQK, C_HEADS * C_DQK, C_HEADS * C_DV, C_HEADS * C_DV, C_N_GATES)
PROJ_DIM = sum(PROJ_WIDTHS)
N_EXPERTS = 64
TOP_K = 6
EXPERT_FF = 512
SHARED_FF = 512
ROUTED_SCALE = 2.5
MOE_BLOCK = 128
N_MOD = 6

kernel_name = "hybrid_gqa_mla_mlstm_moe_encoder"


def rms_norm(x, gain):
    xf = x.astype(jnp.float32)
    y = xf * lax.rsqrt(jnp.mean(xf * xf, axis=-1, keepdims=True) + EPS)
    return (y * gain.astype(jnp.float32)).astype(x.dtype)


def axial_rope_tables(S, rot_dim):
    rows = S // GRID_W
    row = jnp.repeat(jnp.arange(rows, dtype=jnp.float32), GRID_W)
    col = jnp.tile(jnp.arange(GRID_W, dtype=jnp.float32), rows)
    axis_dim = rot_dim // 2
    inv_freq = ROPE_THETA ** (-jnp.arange(0, axis_dim, 2, dtype=jnp.float32) / axis_dim)
    ang_r = row[:, None] * inv_freq
    ang_c = col[:, None] * inv_freq
    return (jnp.cos(ang_r), jnp.sin(ang_r), jnp.cos(ang_c), jnp.sin(ang_c))


def _rotate(x, cos, sin):
    x1, x2 = jnp.split(x, 2, axis=-1)
    cos = cos.astype(x.dtype)
    sin = sin.astype(x.dtype)
    return jnp.concatenate([x1 * cos - x2 * sin, x2 * cos + x1 * sin], axis=-1)


def apply_axial_rope(x, tabs):
    cos_r, sin_r, cos_c, sin_c = tabs
    half = x.shape[-1] // 2
    return jnp.concatenate([_rotate(x[..., :half], cos_r, sin_r),
                            _rotate(x[..., half:], cos_c, sin_c)], axis=-1)


def gqa_attention(q, k, v):
    B, KVH, G, S, D = q.shape
    nb = S // Q_BLOCK
    qb = jnp.moveaxis(q.reshape(B, KVH, G, nb, Q_BLOCK, D), 3, 0)

    def one_block(qblk):
        s = jnp.einsum('bkgqd,bksd->bkgqs', qblk, k).astype(jnp.float32) * (D ** -0.5)
        p = jax.nn.softmax(s, axis=-1).astype(v.dtype)
        return jnp.einsum('bkgqs,bksd->bkgqd', p, v)

    o = lax.map(one_block, qb)
    return jnp.moveaxis(o, 0, 3).reshape(B, KVH * G, S, D)


def mla_attention(q_nope, q_rope, k_nope, k_rope, v):
    B, H, S, dn = q_nope.shape
    dr = q_rope.shape[-1]
    nb = S // Q_BLOCK
    scale = (dn + dr) ** -0.5
    qn = jnp.moveaxis(q_nope.reshape(B, H, nb, Q_BLOCK, dn), 2, 0)
    qr = jnp.moveaxis(q_rope.reshape(B, H, nb, Q_BLOCK, dr), 2, 0)

    def one_block(args):
        qn_b, qr_b = args
        s = (jnp.einsum('bhqd,bhsd->bhqs', qn_b, k_nope)
             + jnp.einsum('bhqd,bsd->bhqs', qr_b, k_rope)).astype(jnp.float32) * scale
        p = jax.nn.softmax(s, axis=-1).astype(v.dtype)
        return jnp.einsum('bhqs,bhsd->bhqd', p, v)

    o = lax.map(one_block, (qn, qr))
    return jnp.moveaxis(o, 0, 2).reshape(B, H, S, v.shape[-1])


def centred_depthwise_conv(x, w, b):
    C = x.shape[-1]
    pad = CONV_W // 2
    y = lax.conv_general_dilated(x, w.astype(x.dtype)[:, None, :], (1,), [(pad, pad)],
                                 dimension_numbers=('NWC', 'WIO', 'NWC'), feature_group_count=C)
    return y + b.astype(x.dtype)


def mlstm_chunkwise(q, k, v, log_i, log_f):
    B, H, S, dk = q.shape
    dv = v.shape[-1]
    L = MLSTM_CHUNK
    nc = S // L
    q = q.reshape(B, H, nc, L, dk)
    k = k.reshape(B, H, nc, L, dk)
    v = v.reshape(B, H, nc, L, dv)
    log_i = log_i.reshape(B, H, nc, L)
    log_f = log_f.reshape(B, H, nc, L)
    b = jnp.cumsum(log_f, axis=-1)
    g = b[..., -1]
    a = g[..., None] - b + log_i

    def chunk_step(carry, xs):
        C, n, m = carry
        g_c, a_c, k_c, v_c = xs
        m_new = jnp.maximum(g_c + m, jnp.max(a_c, axis=-1))
        decay = jnp.exp(g_c + m - m_new)
        wk = jnp.exp(a_c - m_new[..., None])[..., None] * k_c
        C_new = decay[..., None, None] * C + jnp.einsum('bhld,bhlv->bhdv', wk, v_c)
        n_new = decay[..., None] * n + jnp.sum(wk, axis=2)
        return (C_new, n_new, m_new), (C, n, m)

    init = (jnp.zeros((B, H, dk, dv), jnp.float32), jnp.zeros((B, H, dk), jnp.float32),
            jnp.zeros((B, H), jnp.float32))
    xs = (jnp.moveaxis(g, 2, 0), jnp.moveaxis(a, 2, 0), jnp.moveaxis(k, 2, 0), jnp.moveaxis(v, 2, 0))
    _, (C_prev, n_prev, m_prev) = lax.scan(chunk_step, init, xs)
    C_prev = jnp.moveaxis(C_prev, 0, 2)
    n_prev = jnp.moveaxis(n_prev, 0, 2)
    m_prev = jnp.moveaxis(m_prev, 0, 2)

    seen = jnp.tril(jnp.ones((L, L), dtype=bool))
    d = b[..., :, None] - b[..., None, :] + log_i[..., None, :]
    d = jnp.where(seen, d, -jnp.inf)
    m_inter = b + m_prev[..., None]
    m_j = jnp.maximum(m_inter, jnp.max(d, axis=-1))
    s = jnp.einsum('bhcjd,bhcsd->bhcjs', q, k) * jnp.exp(d - m_j[..., None])
    inter = jnp.exp(m_inter - m_j)
    num = (jnp.einsum('bhcjs,bhcsv->bhcjv', s, v)
           + inter[..., None] * jnp.einsum('bhcjd,bhcdv->bhcjv', q, C_prev))
    den = jnp.sum(s, axis=-1) + inter * jnp.einsum('bhcjd,bhcd->bhcj', q, n_prev)
    h = num / jnp.maximum(jnp.abs(den), jnp.exp(-m_j))[..., None]
    return h.reshape(B, H, S, dv)


def hybrid_mixer(h, w_in, a_q_gain, a_k_gain, b_cq_gain, b_ckv_gain, w_uq, w_ukv,
                 c_conv_w, c_conv_b, c_gate_b, head_out_gain, w_out, rope_a, rope_b):
    B, S, _ = h.shape
    proj = h @ w_in
    offs = np.cumsum(PROJ_WIDTHS)[:-1].tolist()
    a_q, a_k, a_v, b_cq, b_ckv, b_kr, c_q, c_k, c_v, c_o, c_g = jnp.split(proj, offs, axis=-1)

    q = rms_norm(a_q.reshape(B, S, A_HEADS, HEAD_DIM), a_q_gain).transpose(0, 2, 1, 3)
    k = rms_norm(a_k.reshape(B, S, A_KV_HEADS, HEAD_DIM), a_k_gain).transpose(0, 2, 1, 3)
    v = a_v.reshape(B, S, A_KV_HEADS, HEAD_DIM).transpose(0, 2, 1, 3)
    q = apply_axial_rope(q, rope_a).reshape(B, A_KV_HEADS, A_HEADS // A_KV_HEADS, S, HEAD_DIM)
    k = apply_axial_rope(k, rope_a)
    y_a = gqa_attention(q, k, v).transpose(0, 2, 1, 3).reshape(B, S, A_HEADS * HEAD_DIM)

    cq = (rms_norm(b_cq, b_cq_gain) @ w_uq).reshape(B, S, B_HEADS, B_QK_NOPE + B_QK_ROPE)
    cq = cq.transpose(0, 2, 1, 3)
    q_nope = cq[..., :B_QK_NOPE]
    q_rope = apply_axial_rope(cq[..., B_QK_NOPE:], rope_b)
    kv = (rms_norm(b_ckv, b_ckv_gain) @ w_ukv).reshape(B, S, B_HEADS, B_QK_NOPE + B_V_DIM)
    kv = kv.transpose(0, 2, 1, 3)
    k_nope = kv[..., :B_QK_NOPE]
    v_b = kv[..., B_QK_NOPE:]
    k_rope = apply_axial_rope(b_kr, rope_b)
    y_b = mla_attention(q_nope, q_rope, k_nope, k_rope, v_b).transpose(0, 2, 1, 3)
    y_b = y_b.reshape(B, S, B_HEADS * B_V_DIM)

    qk = jax.nn.silu(centred_depthwise_conv(jnp.concatenate([c_q, c_k], axis=-1), c_conv_w, c_conv_b))
    qc, kc = jnp.split(qk, 2, axis=-1)
    to_heads = lambda t: t.reshape(B, S, C_HEADS, -1).transpose(0, 2, 1, 3).astype(jnp.float32)
    qc = to_heads(qc)
    kc = to_heads(kc) * (C_DQK ** -0.5)
    vc = to_heads(c_v)
    gates = (c_g + c_gate_b).astype(jnp.float32).reshape(B, S, 4, C_HEADS).transpose(2, 0, 3, 1)
    i_fw, f_fw, i_bw, f_bw = gates[0], gates[1], gates[2], gates[3]
    flip = lambda t: jnp.flip(t, axis=2)
    h_fw = mlstm_chunkwise(qc, kc, vc, i_fw, jax.nn.log_sigmoid(f_fw))
    h_bw = flip(mlstm_chunkwise(flip(qc), flip(kc), flip(vc), flip(i_bw), flip(jax.nn.log_sigmoid(f_bw))))
    y_c = (h_fw + h_bw).transpose(0, 2, 1, 3).reshape(B, S, C_HEADS * C_DV).astype(h.dtype)

    n_heads_total = D_MIX // HEAD_DIM
    y = jnp.concatenate([y_a, y_b, y_c], axis=-1).reshape(B, S, n_heads_total, HEAD_DIM)
    y = rms_norm(y, head_out_gain.reshape(n_heads_total, HEAD_DIM)).reshape(B, S, D_MIX)
    attn_w = (A_HEADS + B_HEADS) * HEAD_DIM
    y = jnp.concatenate([y[..., :attn_w], y[..., attn_w:] * jax.nn.sigmoid(c_o)], axis=-1)
    return y @ w_out


def moe_ffn(h, router_w, router_bias, w_gate, w_up, w_down, ws_gate, ws_up, ws_down):
    T, D = h.shape
    E = router_w.shape[1]
    scores = jax.nn.sigmoid((h @ router_w).astype(jnp.float32))
    _, idx = lax.top_k(scores + router_bias.astype(jnp.float32), TOP_K)
    sel = jnp.take_along_axis(scores, idx, axis=1)
    gates = sel / jnp.sum(sel, axis=-1, keepdims=True) * ROUTED_SCALE

    TK = T * TOP_K
    flat_e = idx.reshape(-1)
    flat_tok = jnp.arange(TK, dtype=jnp.int32) // TOP_K
    flat_g = gates.reshape(-1)
    order = jnp.argsort(flat_e)
    se, stok, sg = flat_e[order], flat_tok[order], flat_g[order]
    counts = jnp.bincount(flat_e, length=E)
    padded = (counts + MOE_BLOCK - 1) // MOE_BLOCK * MOE_BLOCK
    start = jnp.cumsum(counts) - counts
    pend = jnp.cumsum(padded)
    pstart = pend - padded
    dest = pstart[se] + jnp.arange(TK, dtype=jnp.int32) - start[se]
    n_blocks = TK // MOE_BLOCK + E
    P = n_blocks * MOE_BLOCK
    buf_tok = jnp.zeros((P,), jnp.int32).at[dest].set(stok)
    buf_g = jnp.zeros((P,), jnp.float32).at[dest].set(sg)
    blk_e = jnp.minimum(jnp.searchsorted(pend, jnp.arange(n_blocks, dtype=jnp.int32) * MOE_BLOCK,
                                         side='right'), E - 1).astype(jnp.int32)

    def block_step(out, blk):
        tok, g, e = blk
        xb = h[tok]
        a = jax.nn.silu(xb @ w_gate[e]) * (xb @ w_up[e])
        y = (a @ w_down[e]) * g[:, None].astype(a.dtype)
        return out.at[tok].add(y), None

    routed, _ = lax.scan(block_step, jnp.zeros_like(h),
                         (buf_tok.reshape(n_blocks, MOE_BLOCK), buf_g.reshape(n_blocks, MOE_BLOCK), blk_e))
    shared = (jax.nn.silu(h @ ws_gate) * (h @ ws_up)) @ ws_down
    return shared + routed


def setup_inputs(seed: int = 0) -> dict:
    key = jax.random.key(seed)
    ks = jax.random.split(key, 32)
    f32 = jnp.float32
    nrm = lambda k, shape, std: jax.random.normal(k, shape, f32) * std
    gain = lambda k, shape: 1.0 + 0.02 * jax.random.normal(k, shape, f32)
    L, D = DEPTH, D_MODEL
    i_bias = 0.1 * jax.random.normal(ks[13], (L, 2, C_HEADS), f32)
    f_bias = jnp.linspace(3.0, 6.0, C_HEADS, dtype=f32) + 0.1 * jax.random.normal(ks[14], (L, 2, C_HEADS), f32)
    c_gate_b = jnp.concatenate([i_bias[:, 0], f_bias[:, 0], i_bias[:, 1], f_bias[:, 1]], axis=-1)
    return {
        "x": jax.random.normal(ks[0], (BATCH, SEQ, D), f32),
        "c": jax.random.normal(ks[1], (BATCH, D), f32),
        "ada_w": nrm(ks[2], (L, D, N_MOD * D), 0.5 * D ** -0.5),
        "ada_b": nrm(ks[3], (L, N_MOD * D), 0.02),
        "pre_mix_gain": gain(ks[4], (L, D)),
        "w_in": nrm(ks[5], (L, D, PROJ_DIM), D ** -0.5),
        "a_q_gain": gain(ks[6], (L, HEAD_DIM)),
        "a_k_gain": gain(ks[7], (L, HEAD_DIM)),
        "b_cq_gain": gain(ks[8], (L, B_Q_LORA)),
        "b_ckv_gain": gain(ks[9], (L, B_KV_LORA)),
        "w_uq": nrm(ks[10], (L, B_Q_LORA, B_HEADS * (B_QK_NOPE + B_QK_ROPE)), B_Q_LORA ** -0.5),
        "w_ukv": nrm(ks[11], (L, B_KV_LORA, B_HEADS * (B_QK_NOPE + B_V_DIM)), B_KV_LORA ** -0.5),
        "c_conv_w": nrm(ks[12], (L, CONV_W, 2 * C_HEADS * C_DQK), CONV_W ** -0.5),
        "c_conv_b": nrm(ks[15], (L, 2 * C_HEADS * C_DQK), 0.02),
        "c_gate_b": c_gate_b,
        "head_out_gain": gain(ks[16], (L, D_MIX)),
        "w_out": nrm(ks[17], (L, D_MIX, D), D_MIX ** -0.5),
        "post_mix_gain": gain(ks[18], (L, D)),
        "pre_ffn_gain": gain(ks[19], (L, D)),
        "router_w": nrm(ks[20], (L, D, N_EXPERTS), D ** -0.5),
        "router_bias": nrm(ks[21], (L, N_EXPERTS), 0.01),
        "exp_w_gate": nrm(ks[22], (L, N_EXPERTS, D, EXPERT_FF), D ** -0.5),
        "exp_w_up": nrm(ks[23], (L, N_EXPERTS, D, EXPERT_FF), D ** -0.5),
        "exp_w_down": nrm(ks[24], (L, N_EXPERTS, EXPERT_FF, D), EXPERT_FF ** -0.5),
        "sh_w_gate": nrm(ks[25], (L, D, SHARED_FF), D ** -0.5),
        "sh_w_up": nrm(ks[26], (L, D, SHARED_FF), D ** -0.5),
        "sh_w_down": nrm(ks[27], (L, SHARED_FF, D), SHARED_FF ** -0.5),
        "post_ffn_gain": gain(ks[28], (L, D)),
    }


def reference(x, c, ada_w, ada_b, pre_mix_gain, w_in, a_q_gain, a_k_gain, b_cq_gain, b_ckv_gain,
              w_uq, w_ukv, c_conv_w, c_conv_b, c_gate_b, head_out_gain, w_out, post_mix_gain,
              pre_ffn_gain, router_w, router_bias, exp_w_gate, exp_w_up, exp_w_down,
              sh_w_gate, sh_w_up, sh_w_down, post_ffn_gain):
    B, S, D = x.shape
    rope_a = axial_rope_tables(S, HEAD_DIM)
    rope_b = axial_rope_tables(S, B_QK_ROPE)
    c_act = jax.nn.silu(c)
    for l in range(DEPTH):
        mod = (c_act @ ada_w[l] + ada_b[l]).reshape(B, N_MOD, D)[:, :, None, :]
        sh1, sc1, g1, sh2, sc2, g2 = [mod[:, i] for i in range(N_MOD)]
        h = rms_norm(x, pre_mix_gain[l]) * (1.0 + sc1) + sh1
        y = hybrid_mixer(h, w_in[l], a_q_gain[l], a_k_gain[l], b_cq_gain[l], b_ckv_gain[l],
                         w_uq[l], w_ukv[l], c_conv_w[l], c_conv_b[l], c_gate_b[l],
                         head_out_gain[l], w_out[l], rope_a, rope_b)
        x = x + g1 * rms_norm(y, post_mix_gain[l])
        h = rms_norm(x, pre_ffn_gain[l]) * (1.0 + sc2) + sh2
        y = moe_ffn(h.reshape(B * S, D), router_w[l], router_bias[l], exp_w_gate[l], exp_w_up[l],
                    exp_w_down[l], sh_w_gate[l], sh_w_up[l], sh_w_down[l]).reshape(B, S, D)
        x = x + g2 * rms_norm(y, post_ffn_gain[l])
    return x
```

```python
import functools

import numpy as np
import jax
import jax.numpy as jnp
from jax import lax
from jax.experimental import pallas as pl
from jax.experimental.pallas import tpu as pltpu

D_MODEL = 2048
GRID_W = 64
ROPE_THETA = 10000.0
EPS = 1e-6
HEAD_DIM = 128
A_HEADS, A_KV_HEADS = 6, 2
B_HEADS, B_Q_LORA, B_KV_LORA = 6, 384, 256
B_QK_NOPE, B_QK_ROPE, B_V_DIM = 128, 64, 128
C_HEADS, C_DQK, C_DV = 4, 128, 128
CONV_W = 5
MLSTM_CHUNK = 64
N_EXPERTS, TOP_K, EXPERT_FF, SHARED_FF = 64, 6, 512, 512
ROUTED_SCALE = 2.5
N_MOD = 6

COL_AQ, COL_AK, COL_AV = 0, 768, 1024
COL_BC = 1280
COL_MISC = 1920
COL_CV, COL_CO = 2048, 2560
COL_CQK = 3072
PROJ_PAD = 4096
MISC_GATE_LANE = 64

MOE_BLK = 256
VMEM_LIMIT = 56 * 1024 * 1024

BF16 = jnp.bfloat16
F32 = jnp.float32


def _cparams(sem):
    return pltpu.CompilerParams(dimension_semantics=("arbitrary",) * len(sem),
                                vmem_limit_bytes=VMEM_LIMIT)


def _rms(x, gain):
    return x * lax.rsqrt(jnp.mean(x * x, axis=-1, keepdims=True) + EPS) * gain


def _dot(a, b):
    return jnp.dot(a, b, preferred_element_type=F32)


def _dot_t(a, b):
    return lax.dot_general(a, b, (((1,), (1,)), ((), ())), preferred_element_type=F32)


def _mod_kernel(c_ref, w_ref, b_ref, o_ref):
    c = c_ref[...]
    c_act = (c * jax.nn.sigmoid(c)).astype(BF16)
    o_ref[...] = _dot(c_act, w_ref[...].astype(BF16)) + b_ref[...]


def _modulation(c_pad, ada_w, ada_b):
    L, D, N = ada_w.shape
    M = c_pad.shape[0]
    tn = 1536
    return pl.pallas_call(
        _mod_kernel,
        grid=(L, N // tn),
        in_specs=[pl.BlockSpec((M, D), lambda l, j: (0, 0)),
                  pl.BlockSpec((None, D, tn), lambda l, j: (l, 0, j)),
                  pl.BlockSpec((None, 1, tn), lambda l, j: (l, 0, j))],
        out_specs=pl.BlockSpec((None, M, tn), lambda l, j: (l, 0, j)),
        out_shape=jax.ShapeDtypeStruct((L, M, N), F32),
        compiler_params=_cparams(("parallel", "parallel")),
        name="adaln_mod",
    )(c_pad, ada_w, ada_b.reshape(L, 1, N))


def _in_proj_kernel(x_ref, gain_ref, mod_ref, w_ref, o_ref, h_scr):
    @pl.when(pl.program_id(1) == 0)
    def _():
        h = _rms(x_ref[...], gain_ref[...])
        h = h * (1.0 + mod_ref[1:2, :]) + mod_ref[0:1, :]
        h_scr[...] = h.astype(BF16)

    o_ref[...] = _dot(h_scr[...], w_ref[...])


def _in_proj(x2, gain, mod, w_in_p, S):
    T, D = x2.shape
    N = w_in_p.shape[1]
    tm, tn = 512, 1024
    nb = S // tm
    return pl.pallas_call(
        _in_proj_kernel,
        grid=(T // tm, N // tn),
        in_specs=[pl.BlockSpec((tm, D), lambda i, j: (i, 0)),
                  pl.BlockSpec((1, D), lambda i, j: (0, 0)),
                  pl.BlockSpec((None, N_MOD, D), lambda i, j: (i // nb, 0, 0)),
                  pl.BlockSpec((D, tn), lambda i, j: (0, j))],
        out_specs=pl.BlockSpec((tm, tn), lambda i, j: (i, j)),
        out_shape=jax.ShapeDtypeStruct((T, N), F32),
        scratch_shapes=[pltpu.VMEM((tm, D), BF16)],
        compiler_params=_cparams(("parallel", "arbitrary")),
        name="in_proj",
    )(x2, gain, mod, w_in_p)


def _rope128(y, cos, sin_lo, sin_hi):
    return y * cos + pltpu.roll(y, 96, 1) * sin_lo + pltpu.roll(y, 32, 1) * sin_hi


def _gqa_prep_kernel(q_ref, k_ref, v_ref, cos_ref, slo_ref, shi_ref, qg_ref, kg_ref,
                     qo_ref, ko_ref, vo_ref):
    cos, slo, shi = cos_ref[...], slo_ref[...], shi_ref[...]
    scale = HEAD_DIM ** -0.5
    for h in range(A_HEADS):
        sl = slice(h * HEAD_DIM, (h + 1) * HEAD_DIM)
        y = _rope128(_rms(q_ref[:, sl], qg_ref[...]), cos, slo, shi)
        qo_ref[:, sl] = (y * scale).astype(BF16)
    for h in range(A_KV_HEADS):
        sl = slice(h * HEAD_DIM, (h + 1) * HEAD_DIM)
        y = _rope128(_rms(k_ref[:, sl], kg_ref[...]), cos, slo, shi)
        ko_ref[:, sl] = y.astype(BF16)
    vo_ref[...] = v_ref[...].astype(BF16)


def _gqa_prep(proj, tabs_a, q_gain, k_gain, S):
    T = proj.shape[0]
    tm = 512
    nb = S // tm
    qw, kw = A_HEADS * HEAD_DIM, A_KV_HEADS * HEAD_DIM
    tab_spec = pl.BlockSpec((tm, HEAD_DIM), lambda i: (i % nb, 0))
    g_spec = pl.BlockSpec((1, HEAD_DIM), lambda i: (0, 0))
    return pl.pallas_call(
        _gqa_prep_kernel,
        grid=(T // tm,),
        in_specs=[pl.BlockSpec((tm, qw), lambda i: (i, COL_AQ // qw)),
                  pl.BlockSpec((tm, kw), lambda i: (i, COL_AK // kw)),
                  pl.BlockSpec((tm, kw), lambda i: (i, COL_AV // kw)),
                  tab_spec, tab_spec, tab_spec, g_spec, g_spec],
        out_specs=[pl.BlockSpec((tm, qw), lambda i: (i, 0)),
                   pl.BlockSpec((tm, kw), lambda i: (i, 0)),
                   pl.BlockSpec((tm, kw), lambda i: (i, 0))],
        out_shape=[jax.ShapeDtypeStruct((T, qw), BF16),
                   jax.ShapeDtypeStruct((T, kw), BF16),
                   jax.ShapeDtypeStruct((T, kw), BF16)],
        compiler_params=_cparams(("parallel",)),
        name="gqa_prep",
    )(proj, proj, proj, *tabs_a, q_gain, k_gain)


def _rope64x2(y, cos, sin_lo, sin_hi):
    return y * cos + pltpu.roll(y, 112, 1) * sin_lo + pltpu.roll(y, 16, 1) * sin_hi


def _mla_prep_kernel(bc_ref, misc_ref, cos_ref, slo_ref, shi_ref, cqg_ref, ckvg_ref,
                     wuq_ref, wukv_ref, qo_ref, ko_ref, vo_ref):
    cos, slo, shi = cos_ref[...], slo_ref[...], shi_ref[...]
    lane = lax.broadcasted_iota(jnp.int32, cos.shape, 1)
    first = lane < B_QK_ROPE
    scale = (B_QK_NOPE + B_QK_ROPE) ** -0.5
    nope_w = B_HEADS * B_QK_NOPE

    cq = _rms(bc_ref[:, :B_Q_LORA], cqg_ref[...]).astype(BF16)
    q = _dot(cq, wuq_ref[...]) * scale
    for p in range(B_HEADS // 2):
        pair = _rope64x2(q[:, nope_w + p * 128: nope_w + (p + 1) * 128], cos, slo, shi)
        for half in range(2):
            h = 2 * p + half
            qo_ref[:, h * 256: h * 256 + 128] = q[:, h * 128:(h + 1) * 128].astype(BF16)
            keep = first if half == 0 else jnp.logical_not(first)
            qo_ref[:, h * 256 + 128: (h + 1) * 256] = jnp.where(keep, pair, 0.0).astype(BF16)

    kr = jnp.where(first, misc_ref[...], 0.0)
    kr = kr + pltpu.roll(kr, B_QK_ROPE, 1)
    kr = _rope64x2(kr, cos, slo, shi).astype(BF16)
    ckv = _rms(bc_ref[:, B_Q_LORA:], ckvg_ref[...]).astype(BF16)
    kv = _dot(ckv, wukv_ref[...])
    for h in range(B_HEADS):
        ko_ref[:, h * 256: h * 256 + 128] = kv[:, h * 128:(h + 1) * 128].astype(BF16)
        ko_ref[:, h * 256 + 128: (h + 1) * 256] = kr
    vo_ref[...] = kv[:, nope_w:].astype(BF16)


def _mla_prep(proj, tabs_b, cq_gain, ckv_gain, w_uq_p, w_ukv_p, S):
    T = proj.shape[0]
    tm = 512
    nb = S // tm
    bcw = B_Q_LORA + B_KV_LORA
    tab_spec = pl.BlockSpec((tm, 128), lambda i: (i % nb, 0))
    full = lambda a: pl.BlockSpec(a.shape, lambda i: (0, 0))
    return pl.pallas_call(
        _mla_prep_kernel,
        grid=(T // tm,),
        in_specs=[pl.BlockSpec((tm, bcw), lambda i: (i, COL_BC // bcw)),
                  pl.BlockSpec((tm, 128), lambda i: (i, COL_MISC // 128)),
                  tab_spec, tab_spec, tab_spec, full(cq_gain), full(ckv_gain),
                  full(w_uq_p), full(w_ukv_p)],
        out_specs=[pl.BlockSpec((tm, B_HEADS * 256), lambda i: (i, 0)),
                   pl.BlockSpec((tm, B_HEADS * 256), lambda i: (i, 0)),
                   pl.BlockSpec((tm, B_HEADS * B_V_DIM), lambda i: (i, 0))],
        out_shape=[jax.ShapeDtypeStruct((T, B_HEADS * 256), BF16),
                   jax.ShapeDtypeStruct((T, B_HEADS * 256), BF16),
                   jax.ShapeDtypeStruct((T, B_HEADS * B_V_DIM), BF16)],
        compiler_params=_cparams(("parallel",)),
        name="mla_prep",
    )(proj, proj, *tabs_b, cq_gain, ckv_gain, w_uq_p, w_ukv_p)


def _attn_kernel(q_ref, k_ref, v_ref, o_ref, m_scr, l_scr, acc_scr, *, G, dq, dv, tk):
    tq = q_ref.shape[0]
    S = k_ref.shape[0]
    q = jnp.concatenate([q_ref[:, g * dq:(g + 1) * dq] for g in range(G)], axis=0)
    m_scr[...] = jnp.full(m_scr.shape, -jnp.inf, F32)
    l_scr[...] = jnp.zeros(l_scr.shape, F32)
    acc_scr[...] = jnp.zeros(acc_scr.shape, F32)

    def body(c, carry):
        off = pl.multiple_of(c * tk, tk)
        s = _dot_t(q, k_ref[pl.ds(off, tk), :])
        m_old = m_scr[...]
        m_new = jnp.maximum(m_old, jnp.max(s, axis=-1, keepdims=True))
        alpha = jnp.exp(m_old - m_new)
        p = jnp.exp(s - m_new)
        l_scr[...] = alpha * l_scr[...] + jnp.sum(p, axis=-1, keepdims=True)
        acc_scr[...] = alpha * acc_scr[...] + _dot(p.astype(BF16), v_ref[pl.ds(off, tk), :])
        m_scr[...] = m_new
        return carry

    lax.fori_loop(0, S // tk, body, 0)
    out = acc_scr[...] / l_scr[...]
    for g in range(G):
        o_ref[:, g * dv:(g + 1) * dv] = out[g * tq:(g + 1) * tq, :]


def _attention(q, k, v, *, B, S, Hk, G, dq, dv, tq, tk):
    kern = functools.partial(_attn_kernel, G=G, dq=dq, dv=dv, tk=tk)
    M = G * tq
    return pl.pallas_call(
        kern,
        grid=(B, Hk, S // tq),
        in_specs=[pl.BlockSpec((None, tq, G * dq), lambda b, h, i: (b, i, h)),
                  pl.BlockSpec((None, S, dq), lambda b, h, i: (b, 0, h)),
                  pl.BlockSpec((None, S, dv), lambda b, h, i: (b, 0, h))],
        out_specs=pl.BlockSpec((None, tq, G * dv), lambda b, h, i: (b, i, h)),
        out_shape=jax.ShapeDtypeStruct((B, S, Hk * G * dv), F32),
        scratch_shapes=[pltpu.VMEM((M, 1), F32), pltpu.VMEM((M, 1), F32), pltpu.VMEM((M, dv), F32)],
        compiler_params=_cparams(("parallel", "parallel", "arbitrary")),
        name=f"attn_g{G}_d{dq}",
    )(q, k, v)


def _conv_kernel(x_ref, w_ref, b_ref, o_ref, *, k_scale_from):
    x = x_ref[...]
    S = x.shape[0]
    row = lax.broadcasted_iota(jnp.int32, x.shape, 0)
    pad = CONV_W // 2
    y = x * w_ref[pad:pad + 1, :] + b_ref[...]
    for j in range(CONV_W):
        d = j - pad
        if d == 0:
            continue
        shifted = pltpu.roll(x, (-d) % S, 0)
        valid = (row + d >= 0) & (row + d < S)
        y = y + jnp.where(valid, shifted, 0.0) * w_ref[j:j + 1, :]
    y = y * jax.nn.sigmoid(y)
    scale = jnp.where(pl.program_id(1) >= k_scale_from, C_DQK ** -0.5, 1.0)
    o_ref[...] = y * scale


def _conv_silu(proj3, conv_w, conv_b):
    B, S, _ = proj3.shape
    W = 2 * C_HEADS * C_DQK
    tc = 128
    kern = functools.partial(_conv_kernel, k_scale_from=(C_HEADS * C_DQK) // tc)
    return pl.pallas_call(
        kern,
        grid=(B, W // tc),
        in_specs=[pl.BlockSpec((None, S, tc), lambda b, j: (b, 0, COL_CQK // tc + j)),
                  pl.BlockSpec((CONV_W, tc), lambda b, j: (0, j)),
                  pl.BlockSpec((1, tc), lambda b, j: (0, j))],
        out_specs=pl.BlockSpec((None, S, tc), lambda b, j: (b, 0, j)),
        out_shape=jax.ShapeDtypeStruct((B, S, W), F32),
        compiler_params=_cparams(("parallel", "parallel")),
        name="conv_silu",
    )(proj3, conv_w, conv_b)


def _mlstm_kernel(q_ref, k_ref, v_ref, g_ref, gb_ref, o_ref, C_scr, n_scr, m_scr):
    d = pl.program_id(0)
    L = MLSTM_CHUNK

    @pl.when(pl.program_id(2) == 0)
    def _():
        C_scr[...] = jnp.zeros(C_scr.shape, F32)
        n_scr[...] = jnp.zeros(n_scr.shape, F32)
        m_scr[...] = jnp.zeros(m_scr.shape, F32)

    gates = g_ref[...] + gb_ref[...]
    glane = lax.broadcasted_iota(jnp.int32, gates.shape, 1)
    jj = lax.broadcasted_iota(jnp.int32, (L, L), 0)
    ss = lax.broadcasted_iota(jnp.int32, (L, L), 1)
    eye = jj == ss
    sgn = 1 - 2 * d
    seen = (ss - jj) * sgn <= 0
    seen_t = (jj - ss) * sgn <= 0
    neg_inf = jnp.float32(-jnp.inf)

    for h in range(C_HEADS):
        sl = slice(h * C_DQK, (h + 1) * C_DQK)
        i_lane = MISC_GATE_LANE + d * (2 * C_HEADS) + h
        f_lane = i_lane + C_HEADS
        i_col = jnp.sum(jnp.where(glane == i_lane, gates, 0.0), axis=1, keepdims=True)
        f_pre = jnp.sum(jnp.where(glane == f_lane, gates, 0.0), axis=1, keepdims=True)
        f_col = jnp.minimum(f_pre, 0.0) - jnp.log1p(jnp.exp(-jnp.abs(f_pre)))
        f_row = jnp.sum(jnp.where(eye, f_col, 0.0), axis=0, keepdims=True)
        i_row = jnp.sum(jnp.where(eye, i_col, 0.0), axis=0, keepdims=True)
        b_col = jnp.sum(jnp.where(seen, f_row, 0.0), axis=1, keepdims=True)
        b_row = jnp.sum(jnp.where(seen_t, f_col, 0.0), axis=0, keepdims=True)
        g_tot = jnp.sum(f_col, axis=0, keepdims=True)
        a_col = g_tot - b_col + i_col

        q = q_ref[:, sl]
        k = k_ref[:, sl]
        v = v_ref[:, sl]
        qb, kb, vb = q.astype(BF16), k.astype(BF16), v.astype(BF16)
        C_prev = C_scr[h]
        n_prev = n_scr[h]
        m_prev = m_scr[h][:, 0:1]

        dmat = jnp.where(seen, b_col - b_row + i_row, neg_inf)
        m_inter = b_col + m_prev
        m_j = jnp.maximum(m_inter, jnp.max(dmat, axis=1, keepdims=True))
        s = _dot_t(qb, kb) * jnp.exp(dmat - m_j)
        inter = jnp.exp(m_inter - m_j)
        num = _dot(s.astype(BF16), vb) + inter * _dot(qb, C_prev.astype(BF16))
        den = jnp.sum(s, axis=1, keepdims=True) + inter * jnp.sum(q * n_prev, axis=1, keepdims=True)
        o_ref[:, sl] = num / jnp.maximum(jnp.abs(den), jnp.exp(-m_j))

        m_new = jnp.maximum(g_tot + m_prev, jnp.max(a_col, axis=0, keepdims=True))
        decay = jnp.exp(g_tot + m_prev - m_new)
        wk = jnp.exp(a_col - m_new) * k
        upd = lax.dot_general(wk.astype(BF16), vb, (((0,), (0,)), ((), ())),
                              preferred_element_type=F32)
        C_scr[h] = decay * C_prev + upd
        n_scr[h] = decay * n_prev + jnp.sum(wk, axis=0, keepdims=True)
        m_scr[h] = jnp.broadcast_to(m_new, m_scr.shape[1:])


def _mlstm(qk3, proj3, gate_b_row):
    B, S, _ = proj3.shape
    L = MLSTM_CHUNK
    nc = S // L
    W = C_HEADS * C_DQK
    ceff = lambda d, c: c + d * (nc - 1 - 2 * c)
    return pl.pallas_call(
        _mlstm_kernel,
        grid=(2, B, nc),
        in_specs=[pl.BlockSpec((None, L, W), lambda d, b, c: (b, ceff(d, c), 0)),
                  pl.BlockSpec((None, L, W), lambda d, b, c: (b, ceff(d, c), 1)),
                  pl.BlockSpec((None, L, W), lambda d, b, c: (b, ceff(d, c), COL_CV // W)),
                  pl.BlockSpec((None, L, 128), lambda d, b, c: (b, ceff(d, c), COL_MISC // 128)),
                  pl.BlockSpec((1, 128), lambda d, b, c: (0, 0))],
        out_specs=pl.BlockSpec((None, None, L, W), lambda d, b, c: (d, b, ceff(d, c), 0)),
        out_shape=jax.ShapeDtypeStruct((2, B, S, W), F32),
        scratch_shapes=[pltpu.VMEM((C_HEADS, C_DQK, C_DV), F32),
                        pltpu.VMEM((C_HEADS, 1, C_DQK), F32),
                        pltpu.VMEM((C_HEADS, 1, 128), F32)],
        compiler_params=_cparams(("parallel", "parallel", "arbitrary")),
        name="mlstm",
    )(qk3, qk3, proj3, proj3, gate_b_row)


ROW_TILE = 8
HALF_D = D_MODEL // 2
U32 = jnp.uint32
HI_MASK = 0xFFFF0000


def _pack_rows(y, o_ref):
    n = y.shape[0]
    for s in range(ROW_TILE):
        lo = y[:, s * 128:(s + 1) * 128].astype(BF16).astype(F32)
        hi = y[:, HALF_D + s * 128: HALF_D + (s + 1) * 128].astype(BF16).astype(F32)
        w = (lax.bitcast_convert_type(lo, U32) >> 16) | (lax.bitcast_convert_type(hi, U32) & U32(HI_MASK))
        o_ref[pl.ds(s, n, stride=ROW_TILE), :] = w


def _unpack_rows(ref, n):
    lo, hi = [], []
    for s in range(ROW_TILE):
        w = ref[pl.ds(s, n, stride=ROW_TILE), :]
        lo.append(lax.bitcast_convert_type(w << 16, F32))
        hi.append(lax.bitcast_convert_type(w & U32(HI_MASK), F32))
    return lo + hi


def _merge_kernel(ya_ref, yb_ref, hc_ref, co_ref, x_ref, mod_ref, hg_ref, wout_ref, pmg_ref,
                  pfg_ref, rw_ref, xo_ref, h2_ref, h2p_ref, lg_ref, y_scr):
    attn_heads = A_HEADS + B_HEADS
    for h in range(attn_heads + C_HEADS):
        sl = slice(h * HEAD_DIM, (h + 1) * HEAD_DIM)
        g = hg_ref[:, sl]
        if h < A_HEADS:
            y = _rms(ya_ref[:, sl], g)
        elif h < attn_heads:
            y = _rms(yb_ref[:, (h - A_HEADS) * HEAD_DIM:(h - A_HEADS + 1) * HEAD_DIM], g)
        else:
            cs = slice((h - attn_heads) * HEAD_DIM, (h - attn_heads + 1) * HEAD_DIM)
            y = _rms(hc_ref[0, :, cs] + hc_ref[1, :, cs], g) * jax.nn.sigmoid(co_ref[:, cs])
        y_scr[:, sl] = y.astype(BF16)
    y = _dot(y_scr[...], wout_ref[...])
    x_new = x_ref[...] + mod_ref[2:3, :] * _rms(y, pmg_ref[...])
    xo_ref[...] = x_new
    h2 = _rms(x_new, pfg_ref[...]) * (1.0 + mod_ref[4:5, :]) + mod_ref[3:4, :]
    h2_ref[...] = h2.astype(BF16)
    _pack_rows(h2, h2p_ref)
    lg_ref[...] = jnp.dot(h2, rw_ref[...], preferred_element_type=F32,
                          precision=lax.Precision.HIGHEST)


def _merge(ya, yb, hc, proj, x2, mod, head_gain, w_out_b, pm_gain, pf_gain, router_w, S):
    T, D = x2.shape
    tm = 256
    nb = S // tm
    aw, cw = A_HEADS * HEAD_DIM, C_HEADS * C_DV
    row = lambda w: pl.BlockSpec((tm, w), lambda i: (i, 0))
    full = lambda a: pl.BlockSpec(a.shape, lambda i: (0,) * a.ndim)
    return pl.pallas_call(
        _merge_kernel,
        grid=(T // tm,),
        in_specs=[row(aw), row(aw),
                  pl.BlockSpec((2, tm, cw), lambda i: (0, i, 0)),
                  pl.BlockSpec((tm, cw), lambda i: (i, COL_CO // cw)),
                  row(D),
                  pl.BlockSpec((None, N_MOD, D), lambda i: (i // nb, 0, 0)),
                  full(head_gain), full(w_out_b), full(pm_gain), full(pf_gain), full(router_w)],
        out_specs=[row(D), row(D), pl.BlockSpec((tm * ROW_TILE, 128), lambda i: (i, 0)),
                   row(N_EXPERTS)],
        out_shape=[jax.ShapeDtypeStruct((T, D), F32),
                   jax.ShapeDtypeStruct((T, D), BF16),
                   jax.ShapeDtypeStruct((T * ROW_TILE, 128), U32),
                   jax.ShapeDtypeStruct((T, N_EXPERTS), F32)],
        scratch_shapes=[pltpu.VMEM((tm, D), BF16)],
        compiler_params=_cparams(("parallel",)),
        name="merge_out_proj",
    )(ya, yb, hc, proj, x2, mod, head_gain, w_out_b, pm_gain, pf_gain, router_w)


def _router_kernel(lg_ref, bias_ref, idx_ref, gate_ref, mask_ref):
    scores = jax.nn.sigmoid(lg_ref[...])
    sel = scores + bias_ref[...]
    lane = lax.broadcasted_iota(jnp.int32, scores.shape, 1).astype(F32)
    col = lax.broadcasted_iota(jnp.int32, idx_ref.shape, 1)
    idx = jnp.zeros(idx_ref.shape, F32)
    gate = jnp.zeros(gate_ref.shape, F32)
    mask = jnp.zeros(scores.shape, F32)
    for kk in range(TOP_K):
        mx = jnp.max(sel, axis=1, keepdims=True)
        am = jnp.min(jnp.where(sel == mx, lane, float(N_EXPERTS)), axis=1, keepdims=True)
        hit = lane == am
        sc = jnp.sum(jnp.where(hit, scores, 0.0), axis=1, keepdims=True)
        idx = jnp.where(col == kk, am, idx)
        gate = jnp.where(col == kk, sc, gate)
        mask = jnp.where(hit, 1.0, mask)
        sel = jnp.where(hit, -jnp.inf, sel)
    gate = gate / jnp.sum(gate, axis=1, keepdims=True) * ROUTED_SCALE
    idx_ref[...] = idx.astype(jnp.int32)
    gate_ref[...] = gate
    mask_ref[...] = mask.astype(jnp.int32)


def _router(logits, bias):
    T, E = logits.shape
    tm = 1024
    row = lambda w: pl.BlockSpec((tm, w), lambda i: (i, 0))
    return pl.pallas_call(
        _router_kernel,
        grid=(T // tm,),
        in_specs=[row(E), pl.BlockSpec((1, E), lambda i: (0, 0))],
        out_specs=[row(8), row(8), row(E)],
        out_shape=[jax.ShapeDtypeStruct((T, 8), jnp.int32),
                   jax.ShapeDtypeStruct((T, 8), F32),
                   jax.ShapeDtypeStruct((T, E), jnp.int32)],
        compiler_params=_cparams(("parallel",)),
        name="router_topk",
    )(logits, bias)


TAIL_BITS = MOE_BLK.bit_length() - 1


def _tail_copies(tail_ref, zero_scr, xs_ref, zsem, fn):
    for e in range(N_EXPERTS):
        start_row = tail_ref[2 * e]
        n_tail = tail_ref[2 * e + 1]
        for bit in range(TAIL_BITS):
            size = 1 << bit
            cur = start_row + (n_tail & ~(2 * size - 1))
            cur = pl.multiple_of(cur * ROW_TILE, ROW_TILE)

            @pl.when((n_tail & size) != 0)
            def _():
                fn(pltpu.make_async_copy(zero_scr.at[pl.ds(0, size * ROW_TILE), :],
                                         xs_ref.at[pl.ds(cur, size * ROW_TILE), :], zsem))


def _row_tile(ref, row):
    return ref.at[pl.ds(pl.multiple_of(row * ROW_TILE, ROW_TILE), ROW_TILE), :]


def _dispatch_kernel(tail_ref, pos_ref, h_ref, xs_ref, zero_scr, sem, zsem, *, tm):
    i = pl.program_id(0)

    def start(r, c):
        src = _row_tile(h_ref, i * tm + r)
        for kk in range(TOP_K):
            pltpu.make_async_copy(src, _row_tile(xs_ref, pos_ref[0, r * TOP_K + kk]), sem).start()
        return c

    def wait_one(r, c):
        pltpu.make_async_copy(_row_tile(h_ref, 0), _row_tile(xs_ref, 0), sem).wait()
        return c

    lax.fori_loop(0, tm, start, 0)

    @pl.when(i == 0)
    def _():
        zero_scr[...] = jnp.zeros(zero_scr.shape, zero_scr.dtype)
        _tail_copies(tail_ref, zero_scr, xs_ref, zsem, lambda cp: cp.start())
        _tail_copies(tail_ref, zero_scr, xs_ref, zsem, lambda cp: cp.wait())

    @pl.when(i > 0)
    def _():
        lax.fori_loop(0, tm * TOP_K, wait_one, 0)

    @pl.when(i == pl.num_programs(0) - 1)
    def _():
        lax.fori_loop(0, tm * TOP_K, wait_one, 0)


def _dispatch(h2p, pos3, tail_info, P):
    T = h2p.shape[0] // ROW_TILE
    tm = pos3.shape[2] // TOP_K
    kern = functools.partial(_dispatch_kernel, tm=tm)
    return pl.pallas_call(
        kern,
        grid_spec=pltpu.PrefetchScalarGridSpec(
            num_scalar_prefetch=1,
            grid=(T // tm,),
            in_specs=[pl.BlockSpec((None, 1, tm * TOP_K), lambda i, tail: (i, 0, 0),
                                   memory_space=pltpu.SMEM),
                      pl.BlockSpec(memory_space=pl.ANY)],
            out_specs=pl.BlockSpec(memory_space=pl.ANY),
            scratch_shapes=[pltpu.VMEM((MOE_BLK // 2 * ROW_TILE, 128), U32),
                            pltpu.SemaphoreType.DMA(()), pltpu.SemaphoreType.DMA(())]),
        out_shape=jax.ShapeDtypeStruct((P * ROW_TILE, 128), U32),
        compiler_params=_cparams(("arbitrary",)),
        name="moe_dispatch",
    )(tail_info, pos3, h2p)


def _expert_kernel(blk_e_ref, nused_ref, xs_ref, wg_ref, wu_ref, wd_ref, ys_ref, wg_b, wu_b, wd_b):
    i = pl.program_id(0)
    e = blk_e_ref[i]
    e_prev = blk_e_ref[jnp.maximum(i - 1, 0)]
    used = i < nused_ref[0]

    @pl.when(used & ((i == 0) | (e != e_prev)))
    def _():
        wg_b[...] = wg_ref[...].astype(BF16)
        wu_b[...] = wu_ref[...].astype(BF16)
        wd_b[...] = wd_ref[...].astype(BF16)

    @pl.when(used)
    def _():
        x = jnp.concatenate([c.astype(BF16) for c in _unpack_rows(xs_ref, MOE_BLK)], axis=1)
        gte = _dot(x, wg_b[...])
        a = gte * jax.nn.sigmoid(gte) * _dot(x, wu_b[...])
        _pack_rows(_dot(a.astype(BF16), wd_b[...]), ys_ref)

    @pl.when(jnp.logical_not(used))
    def _():
        ys_ref[...] = jnp.zeros(ys_ref.shape, ys_ref.dtype)


def _experts(xs, blk_e, nused, w_gate, w_up, w_down, layer):
    P = xs.shape[0] // ROW_TILE
    D, F = w_gate.shape[-2:]
    NB = P // MOE_BLK
    last = lambda i, nu: jnp.minimum(i, nu[0] - 1)
    return pl.pallas_call(
        _expert_kernel,
        grid_spec=pltpu.PrefetchScalarGridSpec(
            num_scalar_prefetch=2,
            grid=(NB,),
            in_specs=[pl.BlockSpec((MOE_BLK * ROW_TILE, 128), lambda i, be, nu: (last(i, nu), 0)),
                      pl.BlockSpec((None, None, D, F), lambda i, be, nu: (layer, be[i], 0, 0)),
                      pl.BlockSpec((None, None, D, F), lambda i, be, nu: (layer, be[i], 0, 0)),
                      pl.BlockSpec((None, None, F, D), lambda i, be, nu: (layer, be[i], 0, 0))],
            out_specs=pl.BlockSpec((MOE_BLK * ROW_TILE, 128), lambda i, be, nu: (i, 0)),
            scratch_shapes=[pltpu.VMEM((D, F), BF16), pltpu.VMEM((D, F), BF16),
                            pltpu.VMEM((F, D), BF16)]),
        out_shape=jax.ShapeDtypeStruct((P * ROW_TILE, 128), U32),
        compiler_params=_cparams(("arbitrary",)),
        name="moe_experts",
    )(blk_e, nused, xs, w_gate, w_up, w_down)


def _combine_kernel(pos_ref, h_ref, gate_ref, x_ref, mod_ref, wsg_ref, wsu_ref, wsd_ref, pg_ref,
                    ys_ref, xo_ref, gbuf, sem, *, tm):
    def row_copy(r, kk):
        return pltpu.make_async_copy(_row_tile(ys_ref, pos_ref[0, r * TOP_K + kk]),
                                     _row_tile(gbuf.at[kk], r), sem)

    def start(r, c):
        for kk in range(TOP_K):
            row_copy(r, kk).start()
        return c

    def wait(r, c):
        for kk in range(TOP_K):
            row_copy(r, kk).wait()
        return c

    lax.fori_loop(0, tm, start, 0)
    h = h_ref[...]
    gte = _dot(h, wsg_ref[...])
    a = gte * jax.nn.sigmoid(gte) * _dot(h, wsu_ref[...])
    y = _dot(a.astype(BF16), wsd_ref[...])
    lax.fori_loop(0, tm, wait, 0)
    gate = gate_ref[...]
    for kk in range(TOP_K):
        y = y + gate[:, kk:kk + 1] * jnp.concatenate(_unpack_rows(gbuf.at[kk], tm), axis=1)
    xo_ref[...] = x_ref[...] + mod_ref[5:6, :] * _rms(y, pg_ref[...])


def _combine(pos3, h2, gate8, x2, mod, ws_gate_b, ws_up_b, ws_down_b, post_gain, ys, S):
    T, D = x2.shape
    tm = pos3.shape[2] // TOP_K
    nb = S // tm
    kern = functools.partial(_combine_kernel, tm=tm)
    row = lambda w: pl.BlockSpec((tm, w), lambda i: (i, 0))
    full = lambda a: pl.BlockSpec(a.shape, lambda i: (0,) * a.ndim)
    return pl.pallas_call(
        kern,
        grid=(T // tm,),
        in_specs=[pl.BlockSpec((None, 1, tm * TOP_K), lambda i: (i, 0, 0), memory_space=pltpu.SMEM),
                  row(D), row(8), row(D),
                  pl.BlockSpec((None, N_MOD, D), lambda i: (i // nb, 0, 0)),
                  full(ws_gate_b), full(ws_up_b), full(ws_down_b), full(post_gain),
                  pl.BlockSpec(memory_space=pl.ANY)],
        out_specs=row(D),
        out_shape=jax.ShapeDtypeStruct((T, D), F32),
        scratch_shapes=[pltpu.VMEM((TOP_K, tm * ROW_TILE, 128), U32), pltpu.SemaphoreType.DMA(())],
        compiler_params=_cparams(("arbitrary",)),
        name="moe_combine",
    )(pos3, h2, gate8, x2, mod, ws_gate_b, ws_up_b, ws_down_b, post_gain, ys)


def _route_plan(idx8, mask, n_blocks):
    idx = idx8[:, :TOP_K]
    counts = jnp.sum(mask, axis=0)
    rank = jnp.cumsum(mask, axis=0) - mask
    padded = (counts + MOE_BLK - 1) // MOE_BLK * MOE_BLK
    pend = jnp.cumsum(padded)
    pstart = pend - padded
    pos = jnp.take_along_axis(pstart[None, :] + rank, idx, axis=1).astype(jnp.int32)
    blk_start = jnp.arange(n_blocks, dtype=jnp.int32) * MOE_BLK
    blk_e = jnp.minimum(jnp.searchsorted(pend, blk_start, side="right"), N_EXPERTS - 1)
    nused = (pend[-1:] // MOE_BLK).astype(jnp.int32)
    tail = jnp.stack([pstart + counts, padded - counts], axis=1).reshape(-1).astype(jnp.int32)
    return pos, blk_e.astype(jnp.int32), nused, tail


def _rope_tables(S, rot_dim):
    rows = S // GRID_W
    row = jnp.repeat(jnp.arange(rows, dtype=F32), GRID_W)
    col = jnp.tile(jnp.arange(GRID_W, dtype=F32), rows)
    axis_dim = rot_dim // 2
    inv_freq = ROPE_THETA ** (-jnp.arange(0, axis_dim, 2, dtype=F32) / axis_dim)
    ang_r = row[:, None] * inv_freq
    ang_c = col[:, None] * inv_freq
    z = jnp.zeros_like(ang_r)
    cos = jnp.concatenate([jnp.cos(ang_r)] * 2 + [jnp.cos(ang_c)] * 2, axis=1)
    sin_lo = jnp.concatenate([-jnp.sin(ang_r), z, -jnp.sin(ang_c), z], axis=1)
    sin_hi = jnp.concatenate([z, jnp.sin(ang_r), z, jnp.sin(ang_c)], axis=1)
    reps = 128 // rot_dim
    return tuple(jnp.tile(t, (1, reps)) for t in (cos, sin_lo, sin_hi))


def _prep_w_in(w):
    widths = (768, 256, 256, B_Q_LORA, B_KV_LORA, B_QK_ROPE, 512, 512, 512, 512, 4 * C_HEADS)
    offs = np.cumsum(widths)[:-1].tolist()
    a_q, a_k, a_v, b_cq, b_ckv, b_kr, c_q, c_k, c_v, c_o, c_g = jnp.split(w, offs, axis=-1)
    pad = jnp.zeros(w.shape[:-1] + (128 - B_QK_ROPE - 4 * C_HEADS,), w.dtype)
    out = jnp.concatenate([a_q, a_k, a_v, b_cq, b_ckv, b_kr, c_g, pad, c_v, c_o, c_q, c_k], axis=-1)
    assert out.shape[-1] == PROJ_PAD
    return out.astype(BF16)


def _heads_split(w, n_heads, first):
    K = w.shape[0]
    w3 = w.reshape(K, n_heads, -1)
    return jnp.concatenate([w3[:, :, :first].reshape(K, -1), w3[:, :, first:].reshape(K, -1)],
                           axis=1).astype(BF16)


def kernel(x, c, ada_w, ada_b, pre_mix_gain, w_in, a_q_gain, a_k_gain, b_cq_gain, b_ckv_gain, w_uq, w_ukv, c_conv_w, c_conv_b, c_gate_b, head_out_gain, w_out, post_mix_gain, pre_ffn_gain, router_w, router_bias, exp_w_gate, exp_w_up, exp_w_down, sh_w_gate, sh_w_up, sh_w_down, post_ffn_gain):
    B, S, D = x.shape
    T = B * S
    L = ada_w.shape[0]
    n_blocks = T * TOP_K // MOE_BLK + N_EXPERTS
    row2 = lambda v: v.reshape(1, -1)

    tabs_a = _rope_tables(S, HEAD_DIM)
    tabs_b = _rope_tables(S, B_QK_ROPE)
    c_pad = jnp.zeros((8, D), F32).at[:B].set(c)
    mods = _modulation(c_pad, ada_w, ada_b)[:, :B].reshape(L, B, N_MOD, D)

    x2 = x.reshape(T, D)
    for l in range(L):
        mod = mods[l]
        proj = _in_proj(x2, row2(pre_mix_gain[l]), mod, _prep_w_in(w_in[l]), S)
        proj3 = proj.reshape(B, S, PROJ_PAD)

        qa, ka, va = _gqa_prep(proj, tabs_a, row2(a_q_gain[l]), row2(a_k_gain[l]), S)
        ya = _attention(qa.reshape(B, S, -1), ka.reshape(B, S, -1), va.reshape(B, S, -1),
                        B=B, S=S, Hk=A_KV_HEADS, G=A_HEADS // A_KV_HEADS, dq=HEAD_DIM, dv=HEAD_DIM,
                        tq=256, tk=512)

        qb, kb, vb = _mla_prep(proj, tabs_b, row2(b_cq_gain[l]), row2(b_ckv_gain[l]),
                               _heads_split(w_uq[l], B_HEADS, B_QK_NOPE),
                               _heads_split(w_ukv[l], B_HEADS, B_QK_NOPE), S)
        yb = _attention(qb.reshape(B, S, -1), kb.reshape(B, S, -1), vb.reshape(B, S, -1),
                        B=B, S=S, Hk=B_HEADS, G=1, dq=256, dv=B_V_DIM, tq=512, tk=512)

        qk3 = _conv_silu(proj3, c_conv_w[l], row2(c_conv_b[l]))
        gate_b_row = jnp.zeros((1, 128), F32).at[0, MISC_GATE_LANE:MISC_GATE_LANE + 4 * C_HEADS].set(c_gate_b[l])
        hc = _mlstm(qk3, proj3, gate_b_row).reshape(2, T, C_HEADS * C_DV)

        x2, h2, h2p, logits = _merge(ya.reshape(T, -1), yb.reshape(T, -1), hc, proj, x2, mod,
                                row2(head_out_gain[l]), w_out[l].astype(BF16), row2(post_mix_gain[l]),
                                row2(pre_ffn_gain[l]), router_w[l], S)

        idx8, gate8, mask = _router(logits, row2(router_bias[l]))
        pos, blk_e, nused, tail = _route_plan(idx8, mask, n_blocks)
        tm_route = 128
        pos3 = pos.reshape(T // tm_route, 1, tm_route * TOP_K)
        xs = _dispatch(h2p, pos3, tail, n_blocks * MOE_BLK)
        ys = _experts(xs, blk_e, nused, exp_w_gate, exp_w_up, exp_w_down, l)
        x2 = _combine(pos3, h2, gate8, x2, mod, sh_w_gate[l].astype(BF16), sh_w_up[l].astype(BF16),
                      sh_w_down[l].astype(BF16), row2(post_ffn_gain[l]), ys, S)
    return x2.reshape(B, S, D)
```

```python
import functools

import numpy as np
import jax
import jax.numpy as jnp
from jax import lax
from jax.experimental import pallas as pl
from jax.experimental.pallas import tpu as pltpu

D_MODEL = 2048
GRID_W = 64
ROPE_THETA = 10000.0
EPS = 1e-6
HEAD_DIM = 128
A_HEADS, A_KV_HEADS = 6, 2
B_HEADS, B_Q_LORA, B_KV_LORA = 6, 384, 256
B_QK_NOPE, B_QK_ROPE, B_V_DIM = 128, 64, 128
C_HEADS, C_DQK, C_DV = 4, 128, 128
CONV_W = 5
MLSTM_CHUNK = 64
N_EXPERTS, TOP_K, EXPERT_FF, SHARED_FF = 64, 6, 512, 512
ROUTED_SCALE = 2.5
N_MOD = 6

COL_AQ, COL_AK, COL_AV = 0, 768, 1024
COL_BC = 1280
COL_MISC = 1920
COL_CV, COL_CO = 2048, 2560
COL_CQK = 3072
PROJ_PAD = 4096
MISC_GATE_LANE = 64

MOE_BLK = 256
DISPATCH_TM = 512
COMBINE_TM = 256
VMEM_LIMIT = 56 * 1024 * 1024

BF16 = jnp.bfloat16
F32 = jnp.float32
LOG2_E = 1.4426950408889634


def _cparams(sem):
    return pltpu.CompilerParams(dimension_semantics=("arbitrary",) * len(sem),
                                vmem_limit_bytes=VMEM_LIMIT)


def _rms(x, gain):
    return x * lax.rsqrt(jnp.mean(x * x, axis=-1, keepdims=True) + EPS) * gain


def _dot(a, b):
    return jnp.dot(a, b, preferred_element_type=F32)


def _dot_t(a, b):
    return lax.dot_general(a, b, (((1,), (1,)), ((), ())), preferred_element_type=F32)


def _mod_kernel(c_ref, w_ref, b_ref, o_ref):
    c = c_ref[...]
    c_act = (c * jax.nn.sigmoid(c)).astype(BF16)
    o_ref[...] = _dot(c_act, w_ref[...].astype(BF16)) + b_ref[...]


def _modulation(c_pad, ada_w, ada_b):
    L, D, N = ada_w.shape
    M = c_pad.shape[0]
    tn = 1536
    return pl.pallas_call(
        _mod_kernel,
        grid=(L, N // tn),
        in_specs=[pl.BlockSpec((M, D), lambda l, j: (0, 0)),
                  pl.BlockSpec((None, D, tn), lambda l, j: (l, 0, j)),
                  pl.BlockSpec((None, 1, tn), lambda l, j: (l, 0, j))],
        out_specs=pl.BlockSpec((None, M, tn), lambda l, j: (l, 0, j)),
        out_shape=jax.ShapeDtypeStruct((L, M, N), F32),
        compiler_params=_cparams(("parallel", "parallel")),
        name="adaln_mod",
    )(c_pad, ada_w, ada_b.reshape(L, 1, N))


def _in_proj_kernel(x_ref, gain_ref, mod_ref, w_ref, o_ref, h_scr):
    @pl.when(pl.program_id(1) == 0)
    def _():
        h = _rms(x_ref[...], gain_ref[...])
        h = h * (1.0 + mod_ref[1:2, :]) + mod_ref[0:1, :]
        h_scr[...] = h.astype(BF16)

    o_ref[...] = _dot(h_scr[...], w_ref[...])


def _in_proj(x2, gain, mod, w_in_p, S):
    T, D = x2.shape
    N = w_in_p.shape[1]
    tm, tn = 512, 1024
    nb = S // tm
    return pl.pallas_call(
        _in_proj_kernel,
        grid=(T // tm, N // tn),
        in_specs=[pl.BlockSpec((tm, D), lambda i, j: (i, 0)),
                  pl.BlockSpec((1, D), lambda i, j: (0, 0)),
                  pl.BlockSpec((None, N_MOD, D), lambda i, j: (i // nb, 0, 0)),
                  pl.BlockSpec((D, tn), lambda i, j: (0, j))],
        out_specs=pl.BlockSpec((tm, tn), lambda i, j: (i, j)),
        out_shape=jax.ShapeDtypeStruct((T, N), F32),
        scratch_shapes=[pltpu.VMEM((tm, D), BF16)],
        compiler_params=_cparams(("parallel", "arbitrary")),
        name="in_proj",
    )(x2, gain, mod, w_in_p)


def _rope128(y, cos, sin_lo, sin_hi):
    return y * cos + pltpu.roll(y, 96, 1) * sin_lo + pltpu.roll(y, 32, 1) * sin_hi


def _gqa_prep_kernel(q_ref, k_ref, v_ref, cos_ref, slo_ref, shi_ref, qg_ref, kg_ref,
                     qo_ref, ko_ref, vo_ref):
    cos, slo, shi = cos_ref[...], slo_ref[...], shi_ref[...]
    scale = HEAD_DIM ** -0.5 * LOG2_E
    for h in range(A_HEADS):
        sl = slice(h * HEAD_DIM, (h + 1) * HEAD_DIM)
        y = _rope128(_rms(q_ref[:, sl], qg_ref[...]), cos, slo, shi)
        qo_ref[:, sl] = (y * scale).astype(BF16)
    for h in range(A_KV_HEADS):
        sl = slice(h * HEAD_DIM, (h + 1) * HEAD_DIM)
        y = _rope128(_rms(k_ref[:, sl], kg_ref[...]), cos, slo, shi)
        ko_ref[:, sl] = y.astype(BF16)
    vo_ref[...] = v_ref[...].astype(BF16)


def _gqa_prep(proj, tabs_a, q_gain, k_gain, S):
    T = proj.shape[0]
    tm = 512
    nb = S // tm
    qw, kw = A_HEADS * HEAD_DIM, A_KV_HEADS * HEAD_DIM
    tab_spec = pl.BlockSpec((tm, HEAD_DIM), lambda i: (i % nb, 0))
    g_spec = pl.BlockSpec((1, HEAD_DIM), lambda i: (0, 0))
    return pl.pallas_call(
        _gqa_prep_kernel,
        grid=(T // tm,),
        in_specs=[pl.BlockSpec((tm, qw), lambda i: (i, COL_AQ // qw)),
                  pl.BlockSpec((tm, kw), lambda i: (i, COL_AK // kw)),
                  pl.BlockSpec((tm, kw), lambda i: (i, COL_AV // kw)),
                  tab_spec, tab_spec, tab_spec, g_spec, g_spec],
        out_specs=[pl.BlockSpec((tm, qw), lambda i: (i, 0)),
                   pl.BlockSpec((tm, kw), lambda i: (i, 0)),
                   pl.BlockSpec((tm, kw), lambda i: (i, 0))],
        out_shape=[jax.ShapeDtypeStruct((T, qw), BF16),
                   jax.ShapeDtypeStruct((T, kw), BF16),
                   jax.ShapeDtypeStruct((T, kw), BF16)],
        compiler_params=_cparams(("parallel",)),
        name="gqa_prep",
    )(proj, proj, proj, *tabs_a, q_gain, k_gain)


def _rope64x2(y, cos, sin_lo, sin_hi):
    return y * cos + pltpu.roll(y, 112, 1) * sin_lo + pltpu.roll(y, 16, 1) * sin_hi


def _mla_prep_kernel(bc_ref, misc_ref, cos_ref, slo_ref, shi_ref, cqg_ref, ckvg_ref,
                     wuq_ref, wukv_ref, qo_ref, ko_ref, vo_ref):
    cos, slo, shi = cos_ref[...], slo_ref[...], shi_ref[...]
    lane = lax.broadcasted_iota(jnp.int32, cos.shape, 1)
    first = lane < B_QK_ROPE
    scale = (B_QK_NOPE + B_QK_ROPE) ** -0.5 * LOG2_E
    nope_w = B_HEADS * B_QK_NOPE

    cq = _rms(bc_ref[:, :B_Q_LORA], cqg_ref[...]).astype(BF16)
    q = _dot(cq, wuq_ref[...]) * scale
    for p in range(B_HEADS // 2):
        pair = _rope64x2(q[:, nope_w + p * 128: nope_w + (p + 1) * 128], cos, slo, shi)
        for half in range(2):
            h = 2 * p + half
            qo_ref[:, h * 256: h * 256 + 128] = q[:, h * 128:(h + 1) * 128].astype(BF16)
            keep = first if half == 0 else jnp.logical_not(first)
            qo_ref[:, h * 256 + 128: (h + 1) * 256] = jnp.where(keep, pair, 0.0).astype(BF16)

    kr = jnp.where(first, misc_ref[...], 0.0)
    kr = kr + pltpu.roll(kr, B_QK_ROPE, 1)
    kr = _rope64x2(kr, cos, slo, shi).astype(BF16)
    ckv = _rms(bc_ref[:, B_Q_LORA:], ckvg_ref[...]).astype(BF16)
    kv = _dot(ckv, wukv_ref[...])
    for h in range(B_HEADS):
        ko_ref[:, h * 256: h * 256 + 128] = kv[:, h * 128:(h + 1) * 128].astype(BF16)
        ko_ref[:, h * 256 + 128: (h + 1) * 256] = kr
    vo_ref[...] = kv[:, nope_w:].astype(BF16)


def _mla_prep(proj, tabs_b, cq_gain, ckv_gain, w_uq_p, w_ukv_p, S):
    T = proj.shape[0]
    tm = 512
    nb = S // tm
    bcw = B_Q_LORA + B_KV_LORA
    tab_spec = pl.BlockSpec((tm, 128), lambda i: (i % nb, 0))
    full = lambda a: pl.BlockSpec(a.shape, lambda i: (0, 0))
    return pl.pallas_call(
        _mla_prep_kernel,
        grid=(T // tm,),
        in_specs=[pl.BlockSpec((tm, bcw), lambda i: (i, COL_BC // bcw)),
                  pl.BlockSpec((tm, 128), lambda i: (i, COL_MISC // 128)),
                  tab_spec, tab_spec, tab_spec, full(cq_gain), full(ckv_gain),
                  full(w_uq_p), full(w_ukv_p)],
        out_specs=[pl.BlockSpec((tm, B_HEADS * 256), lambda i: (i, 0)),
                   pl.BlockSpec((tm, B_HEADS * 256), lambda i: (i, 0)),
                   pl.BlockSpec((tm, B_HEADS * B_V_DIM), lambda i: (i, 0))],
        out_shape=[jax.ShapeDtypeStruct((T, B_HEADS * 256), BF16),
                   jax.ShapeDtypeStruct((T, B_HEADS * 256), BF16),
                   jax.ShapeDtypeStruct((T, B_HEADS * B_V_DIM), BF16)],
        compiler_params=_cparams(("parallel",)),
        name="mla_prep",
    )(proj, proj, *tabs_b, cq_gain, ckv_gain, w_uq_p, w_ukv_p)


def _attn_kernel(q_ref, k_ref, v_ref, o_ref, m_scr, l_scr, acc_scr, *, G, dq, dv, tk, sub):
    tq = q_ref.shape[0]
    S = k_ref.shape[0]
    groups = [(g, r) for g in range(G) for r in range(tq // sub)]
    m_scr[...] = jnp.full(m_scr.shape, -jnp.inf, F32)
    l_scr[...] = jnp.zeros(l_scr.shape, F32)
    acc_scr[...] = jnp.zeros(acc_scr.shape, F32)

    def body(c, carry):
        off = pl.multiple_of(c * tk, tk)
        k = k_ref[pl.ds(off, tk), :]
        v = v_ref[pl.ds(off, tk), :]
        for n, (g, r) in enumerate(groups):
            rows = slice(n * sub, (n + 1) * sub)
            s = _dot_t(q_ref[r * sub:(r + 1) * sub, g * dq:(g + 1) * dq], k)
            blocks = [s[:, j * 128:(j + 1) * 128] for j in range(tk // 128)]
            m_blk = functools.reduce(jnp.maximum, blocks)
            m_old = m_scr[rows, :]
            m_new = jnp.maximum(m_old, jnp.max(m_blk, axis=-1, keepdims=True))
            alpha = jnp.exp2(m_old - m_new)
            p_blocks = [jnp.exp2(b - m_new) for b in blocks]
            l_scr[rows, :] = alpha * l_scr[rows, :] + functools.reduce(jnp.add, p_blocks)
            p = jnp.concatenate([b.astype(BF16) for b in p_blocks], axis=1)
            acc_scr[rows, :] = alpha * acc_scr[rows, :] + _dot(p, v)
            m_scr[rows, :] = m_new
        return carry

    lax.fori_loop(0, S // tk, body, 0)
    for n, (g, r) in enumerate(groups):
        rows = slice(n * sub, (n + 1) * sub)
        l = jnp.sum(l_scr[rows, :], axis=-1, keepdims=True)
        o_ref[r * sub:(r + 1) * sub, g * dv:(g + 1) * dv] = acc_scr[rows, :] / l


def _attention(q, k, v, *, B, S, Hk, G, dq, dv, tq, tk):
    assert dv == 128, "row statistics are kept 128 lanes wide to match the value width"
    kern = functools.partial(_attn_kernel, G=G, dq=dq, dv=dv, tk=tk, sub=256)
    M = G * tq
    return pl.pallas_call(
        kern,
        grid=(B, Hk, S // tq),
        in_specs=[pl.BlockSpec((None, tq, G * dq), lambda b, h, i: (b, i, h)),
                  pl.BlockSpec((None, S, dq), lambda b, h, i: (b, 0, h)),
                  pl.BlockSpec((None, S, dv), lambda b, h, i: (b, 0, h))],
        out_specs=pl.BlockSpec((None, tq, G * dv), lambda b, h, i: (b, i, h)),
        out_shape=jax.ShapeDtypeStruct((B, S, Hk * G * dv), F32),
        scratch_shapes=[pltpu.VMEM((M, 128), F32), pltpu.VMEM((M, 128), F32), pltpu.VMEM((M, dv), F32)],
        compiler_params=_cparams(("parallel", "parallel", "arbitrary")),
        name=f"attn_g{G}_d{dq}",
    )(q, k, v)


def _conv_kernel(x_ref, w_ref, b_ref, o_ref, *, k_scale_from):
    x = x_ref[...]
    S = x.shape[0]
    row = lax.broadcasted_iota(jnp.int32, x.shape, 0)
    pad = CONV_W // 2
    y = x * w_ref[pad:pad + 1, :] + b_ref[...]
    for j in range(CONV_W):
        d = j - pad
        if d == 0:
            continue
        shifted = pltpu.roll(x, (-d) % S, 0)
        valid = (row + d >= 0) & (row + d < S)
        y = y + jnp.where(valid, shifted, 0.0) * w_ref[j:j + 1, :]
    y = y * jax.nn.sigmoid(y)
    scale = jnp.where(pl.program_id(1) >= k_scale_from, C_DQK ** -0.5, 1.0)
    o_ref[...] = y * scale


def _conv_silu(proj3, conv_w, conv_b):
    B, S, _ = proj3.shape
    W = 2 * C_HEADS * C_DQK
    tc = 128
    kern = functools.partial(_conv_kernel, k_scale_from=(C_HEADS * C_DQK) // tc)
    return pl.pallas_call(
        kern,
        grid=(B, W // tc),
        in_specs=[pl.BlockSpec((None, S, tc), lambda b, j: (b, 0, COL_CQK // tc + j)),
                  pl.BlockSpec((CONV_W, tc), lambda b, j: (0, j)),
                  pl.BlockSpec((1, tc), lambda b, j: (0, j))],
        out_specs=pl.BlockSpec((None, S, tc), lambda b, j: (b, 0, j)),
        out_shape=jax.ShapeDtypeStruct((B, S, W), F32),
        compiler_params=_cparams(("parallel", "parallel")),
        name="conv_silu",
    )(proj3, conv_w, conv_b)


def _mlstm_kernel(q_ref, k_ref, v_ref, g_ref, gb_ref, o_ref, C_scr, n_scr, m_scr):
    d = pl.program_id(0)
    L = MLSTM_CHUNK

    @pl.when(pl.program_id(2) == 0)
    def _():
        C_scr[...] = jnp.zeros(C_scr.shape, F32)
        n_scr[...] = jnp.zeros(n_scr.shape, F32)
        m_scr[...] = jnp.zeros(m_scr.shape, F32)

    gates = g_ref[...] + gb_ref[...]
    glane = lax.broadcasted_iota(jnp.int32, gates.shape, 1)
    jj = lax.broadcasted_iota(jnp.int32, (L, L), 0)
    ss = lax.broadcasted_iota(jnp.int32, (L, L), 1)
    eye = jj == ss
    sgn = 1 - 2 * d
    seen = (ss - jj) * sgn <= 0
    seen_t = (jj - ss) * sgn <= 0
    neg_inf = jnp.float32(-jnp.inf)

    for h in range(C_HEADS):
        sl = slice(h * C_DQK, (h + 1) * C_DQK)
        i_lane = MISC_GATE_LANE + d * (2 * C_HEADS) + h
        f_lane = i_lane + C_HEADS
        i_col = jnp.sum(jnp.where(glane == i_lane, gates, 0.0), axis=1, keepdims=True)
        f_pre = jnp.sum(jnp.where(glane == f_lane, gates, 0.0), axis=1, keepdims=True)
        f_col = jnp.minimum(f_pre, 0.0) - jnp.log1p(jnp.exp(-jnp.abs(f_pre)))
        f_row = jnp.sum(jnp.where(eye, f_col, 0.0), axis=0, keepdims=True)
        i_row = jnp.sum(jnp.where(eye, i_col, 0.0), axis=0, keepdims=True)
        b_col = jnp.sum(jnp.where(seen, f_row, 0.0), axis=1, keepdims=True)
        b_row = jnp.sum(jnp.where(seen_t, f_col, 0.0), axis=0, keepdims=True)
        g_tot = jnp.sum(f_col, axis=0, keepdims=True)
        a_col = g_tot - b_col + i_col

        q = q_ref[:, sl]
        k = k_ref[:, sl]
        v = v_ref[:, sl]
        qb, kb, vb = q.astype(BF16), k.astype(BF16), v.astype(BF16)
        C_prev = C_scr[h]
        n_prev = n_scr[h]
        m_prev = m_scr[h][:, 0:1]

        dmat = jnp.where(seen, b_col - b_row + i_row, neg_inf)
        m_inter = b_col + m_prev
        m_j = jnp.maximum(m_inter, jnp.max(dmat, axis=1, keepdims=True))
        s = _dot_t(qb, kb) * jnp.exp(dmat - m_j)
        inter = jnp.exp(m_inter - m_j)
        num = _dot(s.astype(BF16), vb) + inter * _dot(qb, C_prev.astype(BF16))
        den = jnp.sum(s, axis=1, keepdims=True) + inter * jnp.sum(q * n_prev, axis=1, keepdims=True)
        o_ref[:, sl] = num / jnp.maximum(jnp.abs(den), jnp.exp(-m_j))

        m_new = jnp.maximum(g_tot + m_prev, jnp.max(a_col, axis=0, keepdims=True))
        decay = jnp.exp(g_tot + m_prev - m_new)
        wk = jnp.exp(a_col - m_new) * k
        upd = lax.dot_general(wk.astype(BF16), vb, (((0,), (0,)), ((), ())),
                              preferred_element_type=F32)
        C_scr[h] = decay * C_prev + upd
        n_scr[h] = decay * n_prev + jnp.sum(wk, axis=0, keepdims=True)
        m_scr[h] = jnp.broadcast_to(m_new, m_scr.shape[1:])


def _mlstm(qk3, proj3, gate_b_row):
    B, S, _ = proj3.shape
    L = MLSTM_CHUNK
    nc = S // L
    W = C_HEADS * C_DQK
    ceff = lambda d, c: c + d * (nc - 1 - 2 * c)
    return pl.pallas_call(
        _mlstm_kernel,
        grid=(2, B, nc),
        in_specs=[pl.BlockSpec((None, L, W), lambda d, b, c: (b, ceff(d, c), 0)),
                  pl.BlockSpec((None, L, W), lambda d, b, c: (b, ceff(d, c), 1)),
                  pl.BlockSpec((None, L, W), lambda d, b, c: (b, ceff(d, c), COL_CV // W)),
                  pl.BlockSpec((None, L, 128), lambda d, b, c: (b, ceff(d, c), COL_MISC // 128)),
                  pl.BlockSpec((1, 128), lambda d, b, c: (0, 0))],
        out_specs=pl.BlockSpec((None, None, L, W), lambda d, b, c: (d, b, ceff(d, c), 0)),
        out_shape=jax.ShapeDtypeStruct((2, B, S, W), F32),
        scratch_shapes=[pltpu.VMEM((C_HEADS, C_DQK, C_DV), F32),
                        pltpu.VMEM((C_HEADS, 1, C_DQK), F32),
                        pltpu.VMEM((C_HEADS, 1, 128), F32)],
        compiler_params=_cparams(("parallel", "parallel", "arbitrary")),
        name="mlstm",
    )(qk3, qk3, proj3, proj3, gate_b_row)


ROW_TILE = 8
HALF_D = D_MODEL // 2
U32 = jnp.uint32
HI_MASK = 0xFFFF0000


def _pack_rows(y, o_ref):
    n = y.shape[0]
    for s in range(ROW_TILE):
        lo = y[:, s * 128:(s + 1) * 128].astype(BF16).astype(F32)
        hi = y[:, HALF_D + s * 128: HALF_D + (s + 1) * 128].astype(BF16).astype(F32)
        w = (lax.bitcast_convert_type(lo, U32) >> 16) | (lax.bitcast_convert_type(hi, U32) & U32(HI_MASK))
        o_ref[pl.ds(s, n, stride=ROW_TILE), :] = w


def _unpack_rows(ref, n):
    lo, hi = [], []
    for s in range(ROW_TILE):
        w = ref[pl.ds(s, n, stride=ROW_TILE), :]
        lo.append(lax.bitcast_convert_type(w << 16, F32))
        hi.append(lax.bitcast_convert_type(w & U32(HI_MASK), F32))
    return lo + hi


def _merge_kernel(ya_ref, yb_ref, hc_ref, co_ref, x_ref, mod_ref, hg_ref, wout_ref, pmg_ref,
                  pfg_ref, rw_ref, xo_ref, h2_ref, h2p_ref, lg_ref, y_scr):
    attn_heads = A_HEADS + B_HEADS
    for h in range(attn_heads + C_HEADS):
        sl = slice(h * HEAD_DIM, (h + 1) * HEAD_DIM)
        g = hg_ref[:, sl]
        if h < A_HEADS:
            y = _rms(ya_ref[:, sl], g)
        elif h < attn_heads:
            y = _rms(yb_ref[:, (h - A_HEADS) * HEAD_DIM:(h - A_HEADS + 1) * HEAD_DIM], g)
        else:
            cs = slice((h - attn_heads) * HEAD_DIM, (h - attn_heads + 1) * HEAD_DIM)
            y = _rms(hc_ref[0, :, cs] + hc_ref[1, :, cs], g) * jax.nn.sigmoid(co_ref[:, cs])
        y_scr[:, sl] = y.astype(BF16)
    y = _dot(y_scr[...], wout_ref[...])
    x_new = x_ref[...] + mod_ref[2:3, :] * _rms(y, pmg_ref[...])
    xo_ref[...] = x_new
    h2 = _rms(x_new, pfg_ref[...]) * (1.0 + mod_ref[4:5, :]) + mod_ref[3:4, :]
    h2_ref[...] = h2.astype(BF16)
    _pack_rows(h2, h2p_ref)
    lg_ref[...] = jnp.dot(h2, rw_ref[...], preferred_element_type=F32,
                          precision=lax.Precision.HIGHEST)


def _merge(ya, yb, hc, proj, x2, mod, head_gain, w_out_b, pm_gain, pf_gain, router_w, S):
    T, D = x2.shape
    tm = 256
    nb = S // tm
    aw, cw = A_HEADS * HEAD_DIM, C_HEADS * C_DV
    row = lambda w: pl.BlockSpec((tm, w), lambda i: (i, 0))
    full = lambda a: pl.BlockSpec(a.shape, lambda i: (0,) * a.ndim)
    return pl.pallas_call(
        _merge_kernel,
        grid=(T // tm,),
        in_specs=[row(aw), row(aw),
                  pl.BlockSpec((2, tm, cw), lambda i: (0, i, 0)),
                  pl.BlockSpec((tm, cw), lambda i: (i, COL_CO // cw)),
                  row(D),
                  pl.BlockSpec((None, N_MOD, D), lambda i: (i // nb, 0, 0)),
                  full(head_gain), full(w_out_b), full(pm_gain), full(pf_gain), full(router_w)],
        out_specs=[row(D), row(D), pl.BlockSpec((tm * ROW_TILE, 128), lambda i: (i, 0)),
                   row(N_EXPERTS)],
        out_shape=[jax.ShapeDtypeStruct((T, D), F32),
                   jax.ShapeDtypeStruct((T, D), BF16),
                   jax.ShapeDtypeStruct((T * ROW_TILE, 128), U32),
                   jax.ShapeDtypeStruct((T, N_EXPERTS), F32)],
        scratch_shapes=[pltpu.VMEM((tm, D), BF16)],
        compiler_params=_cparams(("parallel",)),
        name="merge_out_proj",
    )(ya, yb, hc, proj, x2, mod, head_gain, w_out_b, pm_gain, pf_gain, router_w)


def _router_kernel(lg_ref, bias_ref, idx_ref, gate_ref, mask_ref):
    scores = jax.nn.sigmoid(lg_ref[...])
    sel = scores + bias_ref[...]
    lane = lax.broadcasted_iota(jnp.int32, scores.shape, 1).astype(F32)
    col = lax.broadcasted_iota(jnp.int32, idx_ref.shape, 1)
    idx = jnp.zeros(idx_ref.shape, F32)
    gate = jnp.zeros(gate_ref.shape, F32)
    mask = jnp.zeros(scores.shape, F32)
    for kk in range(TOP_K):
        mx = jnp.max(sel, axis=1, keepdims=True)
        am = jnp.min(jnp.where(sel == mx, lane, float(N_EXPERTS)), axis=1, keepdims=True)
        hit = lane == am
        sc = jnp.sum(jnp.where(hit, scores, 0.0), axis=1, keepdims=True)
        idx = jnp.where(col == kk, am, idx)
        gate = jnp.where(col == kk, sc, gate)
        mask = jnp.where(hit, 1.0, mask)
        sel = jnp.where(hit, -jnp.inf, sel)
    gate = gate / jnp.sum(gate, axis=1, keepdims=True) * ROUTED_SCALE
    idx_ref[...] = idx.astype(jnp.int32)
    gate_ref[...] = gate
    mask_ref[...] = mask.astype(jnp.int32)


def _router(logits, bias):
    T, E = logits.shape
    tm = 1024
    row = lambda w: pl.BlockSpec((tm, w), lambda i: (i, 0))
    return pl.pallas_call(
        _router_kernel,
        grid=(T // tm,),
        in_specs=[row(E), pl.BlockSpec((1, E), lambda i: (0, 0))],
        out_specs=[row(8), row(8), row(E)],
        out_shape=[jax.ShapeDtypeStruct((T, 8), jnp.int32),
                   jax.ShapeDtypeStruct((T, 8), F32),
                   jax.ShapeDtypeStruct((T, E), jnp.int32)],
        compiler_params=_cparams(("parallel",)),
        name="router_topk",
    )(logits, bias)


TAIL_BITS = MOE_BLK.bit_length() - 1


def _tail_copies(tail_ref, zero_scr, xs_ref, zsem, fn):
    for e in range(N_EXPERTS):
        start_row = tail_ref[2 * e]
        n_tail = tail_ref[2 * e + 1]
        for bit in range(TAIL_BITS):
            size = 1 << bit
            cur = start_row + (n_tail & ~(2 * size - 1))
            cur = pl.multiple_of(cur * ROW_TILE, ROW_TILE)

            @pl.when((n_tail & size) != 0)
            def _():
                fn(pltpu.make_async_copy(zero_scr.at[pl.ds(0, size * ROW_TILE), :],
                                         xs_ref.at[pl.ds(cur, size * ROW_TILE), :], zsem))


def _row_tile(ref, row):
    return ref.at[pl.ds(pl.multiple_of(row * ROW_TILE, ROW_TILE), ROW_TILE), :]


def _dispatch_kernel(tail_ref, pos_ref, h_ref, xs_ref, zero_scr, sem, zsem, *, tm):
    i = pl.program_id(0)

    def start(r, c):
        src = _row_tile(h_ref, r)
        for kk in range(TOP_K):
            pltpu.make_async_copy(src, _row_tile(xs_ref, pos_ref[0, r * TOP_K + kk]), sem).start()
        return c

    lax.fori_loop(0, tm, start, 0)

    @pl.when(i == 0)
    def _():
        zero_scr[...] = jnp.zeros(zero_scr.shape, zero_scr.dtype)
        _tail_copies(tail_ref, zero_scr, xs_ref, zsem, lambda cp: cp.start())
        _tail_copies(tail_ref, zero_scr, xs_ref, zsem, lambda cp: cp.wait())

    for kk in range(TOP_K):
        pltpu.make_async_copy(h_ref, xs_ref.at[pl.ds(0, tm * ROW_TILE), :], sem).wait()


def _dispatch(h2p, pos3, tail_info, P):
    T = h2p.shape[0] // ROW_TILE
    tm = pos3.shape[2] // TOP_K
    kern = functools.partial(_dispatch_kernel, tm=tm)
    return pl.pallas_call(
        kern,
        grid_spec=pltpu.PrefetchScalarGridSpec(
            num_scalar_prefetch=1,
            grid=(T // tm,),
            in_specs=[pl.BlockSpec((None, 1, tm * TOP_K), lambda i, tail: (i, 0, 0),
                                   memory_space=pltpu.SMEM),
                      pl.BlockSpec((tm * ROW_TILE, 128), lambda i, tail: (i, 0))],
            out_specs=pl.BlockSpec(memory_space=pl.ANY),
            scratch_shapes=[pltpu.VMEM((MOE_BLK // 2 * ROW_TILE, 128), U32),
                            pltpu.SemaphoreType.DMA(()), pltpu.SemaphoreType.DMA(())]),
        out_shape=jax.ShapeDtypeStruct((P * ROW_TILE, 128), U32),
        compiler_params=_cparams(("arbitrary",)),
        name="moe_dispatch",
    )(tail_info, pos3, h2p)


def _expert_kernel(blk_e_ref, nused_ref, xs_ref, wg_ref, wu_ref, wd_ref, ys_ref, wg_b, wu_b, wd_b):
    i = pl.program_id(0)
    e = blk_e_ref[i]
    e_prev = blk_e_ref[jnp.maximum(i - 1, 0)]
    used = i < nused_ref[0]

    @pl.when(used & ((i == 0) | (e != e_prev)))
    def _():
        wg_b[...] = wg_ref[...].astype(BF16)
        wu_b[...] = wu_ref[...].astype(BF16)
        wd_b[...] = wd_ref[...].astype(BF16)

    @pl.when(used)
    def _():
        x = jnp.concatenate([c.astype(BF16) for c in _unpack_rows(xs_ref, MOE_BLK)], axis=1)
        gte = _dot(x, wg_b[...])
        a = gte * jax.nn.sigmoid(gte) * _dot(x, wu_b[...])
        _pack_rows(_dot(a.astype(BF16), wd_b[...]), ys_ref)

    @pl.when(jnp.logical_not(used))
    def _():
        ys_ref[...] = jnp.zeros(ys_ref.shape, ys_ref.dtype)


def _experts(xs, blk_e, nused, w_gate, w_up, w_down, layer):
    P = xs.shape[0] // ROW_TILE
    D, F = w_gate.shape[-2:]
    NB = P // MOE_BLK
    last = lambda i, nu: jnp.minimum(i, nu[0] - 1)
    return pl.pallas_call(
        _expert_kernel,
        grid_spec=pltpu.PrefetchScalarGridSpec(
            num_scalar_prefetch=2,
            grid=(NB,),
            in_specs=[pl.BlockSpec((MOE_BLK * ROW_TILE, 128), lambda i, be, nu: (last(i, nu), 0)),
                      pl.BlockSpec((None, None, D, F), lambda i, be, nu: (layer, be[i], 0, 0)),
                      pl.BlockSpec((None, None, D, F), lambda i, be, nu: (layer, be[i], 0, 0)),
                      pl.BlockSpec((None, None, F, D), lambda i, be, nu: (layer, be[i], 0, 0))],
            out_specs=pl.BlockSpec((MOE_BLK * ROW_TILE, 128), lambda i, be, nu: (i, 0)),
            scratch_shapes=[pltpu.VMEM((D, F), BF16), pltpu.VMEM((D, F), BF16),
                            pltpu.VMEM((F, D), BF16)]),
        out_shape=jax.ShapeDtypeStruct((P * ROW_TILE, 128), U32),
        compiler_params=_cparams(("arbitrary",)),
        name="moe_experts",
    )(blk_e, nused, xs, w_gate, w_up, w_down)


def _combine_kernel(pos_ref, h_ref, gate_ref, x_ref, mod_ref, wsg_ref, wsu_ref, wsd_ref, pg_ref,
                    ys_ref, xo_ref, gbuf, sem, *, tm):
    def row_copy(r, kk):
        return pltpu.make_async_copy(_row_tile(ys_ref, pos_ref[0, r * TOP_K + kk]),
                                     _row_tile(gbuf.at[kk], r), sem)

    def start(r, c):
        for kk in range(TOP_K):
            row_copy(r, kk).start()
        return c

    lax.fori_loop(0, tm, start, 0)
    h = h_ref[...]
    gte = _dot(h, wsg_ref[...])
    a = gte * jax.nn.sigmoid(gte) * _dot(h, wsu_ref[...])
    y = _dot(a.astype(BF16), wsd_ref[...])
    for kk in range(TOP_K):
        pltpu.make_async_copy(ys_ref.at[pl.ds(0, tm * ROW_TILE), :], gbuf.at[kk], sem).wait()
    gate = gate_ref[...]
    for kk in range(TOP_K):
        y = y + gate[:, kk:kk + 1] * jnp.concatenate(_unpack_rows(gbuf.at[kk], tm), axis=1)
    xo_ref[...] = x_ref[...] + mod_ref[5:6, :] * _rms(y, pg_ref[...])


def _combine(pos3, h2, gate8, x2, mod, ws_gate_b, ws_up_b, ws_down_b, post_gain, ys, S):
    T, D = x2.shape
    tm = pos3.shape[2] // TOP_K
    nb = S // tm
    kern = functools.partial(_combine_kernel, tm=tm)
    row = lambda w: pl.BlockSpec((tm, w), lambda i: (i, 0))
    full = lambda a: pl.BlockSpec(a.shape, lambda i: (0,) * a.ndim)
    return pl.pallas_call(
        kern,
        grid=(T // tm,),
        in_specs=[pl.BlockSpec((None, 1, tm * TOP_K), lambda i: (i, 0, 0), memory_space=pltpu.SMEM),
                  row(D), row(8), row(D),
                  pl.BlockSpec((None, N_MOD, D), lambda i: (i // nb, 0, 0)),
                  full(ws_gate_b), full(ws_up_b), full(ws_down_b), full(post_gain),
                  pl.BlockSpec(memory_space=pl.ANY)],
        out_specs=row(D),
        out_shape=jax.ShapeDtypeStruct((T, D), F32),
        scratch_shapes=[pltpu.VMEM((TOP_K, tm * ROW_TILE, 128), U32), pltpu.SemaphoreType.DMA(())],
        compiler_params=_cparams(("arbitrary",)),
        name="moe_combine",
    )(pos3, h2, gate8, x2, mod, ws_gate_b, ws_up_b, ws_down_b, post_gain, ys)


def _route_plan(idx8, mask, n_blocks):
    idx = idx8[:, :TOP_K]
    counts = jnp.sum(mask, axis=0)
    rank = jnp.cumsum(mask, axis=0) - mask
    padded = (counts + MOE_BLK - 1) // MOE_BLK * MOE_BLK
    pend = jnp.cumsum(padded)
    pstart = pend - padded
    pos = jnp.take_along_axis(pstart[None, :] + rank, idx, axis=1).astype(jnp.int32)
    blk_start = jnp.arange(n_blocks, dtype=jnp.int32) * MOE_BLK
    blk_e = jnp.minimum(jnp.searchsorted(pend, blk_start, side="right"), N_EXPERTS - 1)
    nused = (pend[-1:] // MOE_BLK).astype(jnp.int32)
    tail = jnp.stack([pstart + counts, padded - counts], axis=1).reshape(-1).astype(jnp.int32)
    return pos, blk_e.astype(jnp.int32), nused, tail


def _rope_tables(S, rot_dim):
    rows = S // GRID_W
    row = jnp.repeat(jnp.arange(rows, dtype=F32), GRID_W)
    col = jnp.tile(jnp.arange(GRID_W, dtype=F32), rows)
    axis_dim = rot_dim // 2
    inv_freq = ROPE_THETA ** (-jnp.arange(0, axis_dim, 2, dtype=F32) / axis_dim)
    ang_r = row[:, None] * inv_freq
    ang_c = col[:, None] * inv_freq
    z = jnp.zeros_like(ang_r)
    cos = jnp.concatenate([jnp.cos(ang_r)] * 2 + [jnp.cos(ang_c)] * 2, axis=1)
    sin_lo = jnp.concatenate([-jnp.sin(ang_r), z, -jnp.sin(ang_c), z], axis=1)
    sin_hi = jnp.concatenate([z, jnp.sin(ang_r), z, jnp.sin(ang_c)], axis=1)
    reps = 128 // rot_dim
    return tuple(jnp.tile(t, (1, reps)) for t in (cos, sin_lo, sin_hi))


def _prep_w_in(w):
    widths = (768, 256, 256, B_Q_LORA, B_KV_LORA, B_QK_ROPE, 512, 512, 512, 512, 4 * C_HEADS)
    offs = np.cumsum(widths)[:-1].tolist()
    a_q, a_k, a_v, b_cq, b_ckv, b_kr, c_q, c_k, c_v, c_o, c_g = jnp.split(w, offs, axis=-1)
    pad = jnp.zeros(w.shape[:-1] + (128 - B_QK_ROPE - 4 * C_HEADS,), w.dtype)
    out = jnp.concatenate([a_q, a_k, a_v, b_cq, b_ckv, b_kr, c_g, pad, c_v, c_o, c_q, c_k], axis=-1)
    assert out.shape[-1] == PROJ_PAD
    return out.astype(BF16)


def _heads_split(w, n_heads, first):
    K = w.shape[0]
    w3 = w.reshape(K, n_heads, -1)
    return jnp.concatenate([w3[:, :, :first].reshape(K, -1), w3[:, :, first:].reshape(K, -1)],
                           axis=1).astype(BF16)


def kernel(x, c, ada_w, ada_b, pre_mix_gain, w_in, a_q_gain, a_k_gain, b_cq_gain, b_ckv_gain, w_uq, w_ukv, c_conv_w, c_conv_b, c_gate_b, head_out_gain, w_out, post_mix_gain, pre_ffn_gain, router_w, router_bias, exp_w_gate, exp_w_up, exp_w_down, sh_w_gate, sh_w_up, sh_w_down, post_ffn_gain):
    B, S, D = x.shape
    T = B * S
    L = ada_w.shape[0]
    n_blocks = T * TOP_K // MOE_BLK + N_EXPERTS
    row2 = lambda v: v.reshape(1, -1)

    tabs_a = _rope_tables(S, HEAD_DIM)
    tabs_b = _rope_tables(S, B_QK_ROPE)
    c_pad = jnp.zeros((8, D), F32).at[:B].set(c)
    mods = _modulation(c_pad, ada_w, ada_b)[:, :B].reshape(L, B, N_MOD, D)

    x2 = x.reshape(T, D)
    for l in range(L):
        mod = mods[l]
        proj = _in_proj(x2, row2(pre_mix_gain[l]), mod, _prep_w_in(w_in[l]), S)
        proj3 = proj.reshape(B, S, PROJ_PAD)

        qa, ka, va = _gqa_prep(proj, tabs_a, row2(a_q_gain[l]), row2(a_k_gain[l]), S)
        ya = _attention(qa.reshape(B, S, -1), ka.reshape(B, S, -1), va.reshape(B, S, -1),
                        B=B, S=S, Hk=A_KV_HEADS, G=A_HEADS // A_KV_HEADS, dq=HEAD_DIM, dv=HEAD_DIM,
                        tq=256, tk=512)

        qb, kb, vb = _mla_prep(proj, tabs_b, row2(b_cq_gain[l]), row2(b_ckv_gain[l]),
                               _heads_split(w_uq[l], B_HEADS, B_QK_NOPE),
                               _heads_split(w_ukv[l], B_HEADS, B_QK_NOPE), S)
        yb = _attention(qb.reshape(B, S, -1), kb.reshape(B, S, -1), vb.reshape(B, S, -1),
                        B=B, S=S, Hk=B_HEADS, G=1, dq=256, dv=B_V_DIM, tq=512, tk=512)

        qk3 = _conv_silu(proj3, c_conv_w[l], row2(c_conv_b[l]))
        gate_b_row = jnp.zeros((1, 128), F32).at[0, MISC_GATE_LANE:MISC_GATE_LANE + 4 * C_HEADS].set(c_gate_b[l])
        hc = _mlstm(qk3, proj3, gate_b_row).reshape(2, T, C_HEADS * C_DV)

        x2, h2, h2p, logits = _merge(ya.reshape(T, -1), yb.reshape(T, -1), hc, proj, x2, mod,
                                row2(head_out_gain[l]), w_out[l].astype(BF16), row2(post_mix_gain[l]),
                                row2(pre_ffn_gain[l]), router_w[l], S)

        idx8, gate8, mask = _router(logits, row2(router_bias[l]))
        pos, blk_e, nused, tail = _route_plan(idx8, mask, n_blocks)
        xs = _dispatch(h2p, pos.reshape(T // DISPATCH_TM, 1, DISPATCH_TM * TOP_K), tail,
                       n_blocks * MOE_BLK)
        ys = _experts(xs, blk_e, nused, exp_w_gate, exp_w_up, exp_w_down, l)
        x2 = _combine(pos.reshape(T // COMBINE_TM, 1, COMBINE_TM * TOP_K), h2, gate8, x2, mod, sh_w_gate[l].astype(BF16), sh_w_up[l].astype(BF16),
                      sh_w_down[l].astype(BF16), row2(post_ffn_gain[l]), ys, S)
    return x2.reshape(B, S, D)
```

```python
import functools

import numpy as np
import jax
import jax.numpy as jnp
from jax import lax
from jax.experimental import pallas as pl
from jax.experimental.pallas import tpu as pltpu

D_MODEL = 2048
GRID_W = 64
ROPE_THETA = 10000.0
EPS = 1e-6
HEAD_DIM = 128
A_HEADS, A_KV_HEADS = 6, 2
B_HEADS, B_Q_LORA, B_KV_LORA = 6, 384, 256
B_QK_NOPE, B_QK_ROPE, B_V_DIM = 128, 64, 128
C_HEADS, C_DQK, C_DV = 4, 128, 128
CONV_W = 5
MLSTM_CHUNK = 64
N_EXPERTS, TOP_K, EXPERT_FF, SHARED_FF = 64, 6, 512, 512
ROUTED_SCALE = 2.5
N_MOD = 6

COL_AQ, COL_AK, COL_AV = 0, 768, 1024
COL_BC = 1280
COL_MISC = 1920
COL_CV, COL_CO = 2048, 2560
COL_CQK = 3072
PROJ_PAD = 4096
MISC_GATE_LANE = 64

MOE_BLK = 256
DISPATCH_TM = 512
COMBINE_TM = 256
VMEM_LIMIT = 56 * 1024 * 1024

BF16 = jnp.bfloat16
F32 = jnp.float32
LOG2_E = 1.4426950408889634


def _cparams(sem):
    return pltpu.CompilerParams(dimension_semantics=("arbitrary",) * len(sem),
                                vmem_limit_bytes=VMEM_LIMIT)


def _rms(x, gain):
    return x * lax.rsqrt(jnp.mean(x * x, axis=-1, keepdims=True) + EPS) * gain


def _dot(a, b):
    return jnp.dot(a, b, preferred_element_type=F32)


def _dot_t(a, b):
    return lax.dot_general(a, b, (((1,), (1,)), ((), ())), preferred_element_type=F32)


def _mod_kernel(c_ref, w_ref, b_ref, o_ref):
    c = c_ref[...]
    c_act = (c * jax.nn.sigmoid(c)).astype(BF16)
    o_ref[...] = _dot(c_act, w_ref[...].astype(BF16)) + b_ref[...]


def _modulation(c_pad, ada_w, ada_b):
    L, D, N = ada_w.shape
    M = c_pad.shape[0]
    tn = 1536
    return pl.pallas_call(
        _mod_kernel,
        grid=(L, N // tn),
        in_specs=[pl.BlockSpec((M, D), lambda l, j: (0, 0)),
                  pl.BlockSpec((None, D, tn), lambda l, j: (l, 0, j)),
                  pl.BlockSpec((None, 1, tn), lambda l, j: (l, 0, j))],
        out_specs=pl.BlockSpec((None, M, tn), lambda l, j: (l, 0, j)),
        out_shape=jax.ShapeDtypeStruct((L, M, N), F32),
        compiler_params=_cparams(("parallel", "parallel")),
        name="adaln_mod",
    )(c_pad, ada_w, ada_b.reshape(L, 1, N))


def _in_proj_kernel(x_ref, gain_ref, mod_ref, w_ref, o_ref, *, tn):
    h = _rms(x_ref[...], gain_ref[...])
    h = (h * (1.0 + mod_ref[1:2, :]) + mod_ref[0:1, :]).astype(BF16)
    for j in range(w_ref.shape[1] // tn):
        o_ref[:, j * tn:(j + 1) * tn] = _dot(h, w_ref[:, j * tn:(j + 1) * tn])


def _in_proj(x2, gain, mod, w_in_p, S):
    T, D = x2.shape
    N = w_in_p.shape[1]
    tm = 256
    nb = S // tm
    return pl.pallas_call(
        functools.partial(_in_proj_kernel, tn=1024),
        grid=(T // tm,),
        in_specs=[pl.BlockSpec((tm, D), lambda i: (i, 0)),
                  pl.BlockSpec((1, D), lambda i: (0, 0)),
                  pl.BlockSpec((None, N_MOD, D), lambda i: (i // nb, 0, 0)),
                  pl.BlockSpec((D, N), lambda i: (0, 0))],
        out_specs=pl.BlockSpec((tm, N), lambda i: (i, 0)),
        out_shape=jax.ShapeDtypeStruct((T, N), F32),
        compiler_params=_cparams(("arbitrary",)),
        name="in_proj",
    )(x2, gain, mod, w_in_p)


def _rope128(y, cos, sin_lo, sin_hi):
    return y * cos + pltpu.roll(y, 96, 1) * sin_lo + pltpu.roll(y, 32, 1) * sin_hi


def _gqa_prep_kernel(q_ref, k_ref, v_ref, cos_ref, slo_ref, shi_ref, qg_ref, kg_ref,
                     qo_ref, ko_ref, vo_ref):
    cos, slo, shi = cos_ref[...], slo_ref[...], shi_ref[...]
    scale = HEAD_DIM ** -0.5 * LOG2_E
    for h in range(A_HEADS):
        sl = slice(h * HEAD_DIM, (h + 1) * HEAD_DIM)
        y = _rope128(_rms(q_ref[:, sl], qg_ref[...]), cos, slo, shi)
        qo_ref[:, sl] = (y * scale).astype(BF16)
    for h in range(A_KV_HEADS):
        sl = slice(h * HEAD_DIM, (h + 1) * HEAD_DIM)
        y = _rope128(_rms(k_ref[:, sl], kg_ref[...]), cos, slo, shi)
        ko_ref[:, sl] = y.astype(BF16)
    vo_ref[...] = v_ref[...].astype(BF16)


def _gqa_prep(proj, tabs_a, q_gain, k_gain, S):
    T = proj.shape[0]
    tm = 512
    nb = S // tm
    qw, kw = A_HEADS * HEAD_DIM, A_KV_HEADS * HEAD_DIM
    tab_spec = pl.BlockSpec((tm, HEAD_DIM), lambda i: (i % nb, 0))
    g_spec = pl.BlockSpec((1, HEAD_DIM), lambda i: (0, 0))
    return pl.pallas_call(
        _gqa_prep_kernel,
        grid=(T // tm,),
        in_specs=[pl.BlockSpec((tm, qw), lambda i: (i, COL_AQ // qw)),
                  pl.BlockSpec((tm, kw), lambda i: (i, COL_AK // kw)),
                  pl.BlockSpec((tm, kw), lambda i: (i, COL_AV // kw)),
                  tab_spec, tab_spec, tab_spec, g_spec, g_spec],
        out_specs=[pl.BlockSpec((tm, qw), lambda i: (i, 0)),
                   pl.BlockSpec((tm, kw), lambda i: (i, 0)),
                   pl.BlockSpec((tm, kw), lambda i: (i, 0))],
        out_shape=[jax.ShapeDtypeStruct((T, qw), BF16),
                   jax.ShapeDtypeStruct((T, kw), BF16),
                   jax.ShapeDtypeStruct((T, kw), BF16)],
        compiler_params=_cparams(("parallel",)),
        name="gqa_prep",
    )(proj, proj, proj, *tabs_a, q_gain, k_gain)


def _rope64x2(y, cos, sin_lo, sin_hi):
    return y * cos + pltpu.roll(y, 112, 1) * sin_lo + pltpu.roll(y, 16, 1) * sin_hi


def _mla_prep_kernel(bc_ref, misc_ref, cos_ref, slo_ref, shi_ref, cqg_ref, ckvg_ref,
                     wuq_ref, wukv_ref, qo_ref, ko_ref, vo_ref):
    cos, slo, shi = cos_ref[...], slo_ref[...], shi_ref[...]
    lane = lax.broadcasted_iota(jnp.int32, cos.shape, 1)
    first = lane < B_QK_ROPE
    scale = (B_QK_NOPE + B_QK_ROPE) ** -0.5 * LOG2_E
    nope_w = B_HEADS * B_QK_NOPE

    cq = _rms(bc_ref[:, :B_Q_LORA], cqg_ref[...]).astype(BF16)
    q = _dot(cq, wuq_ref[...]) * scale
    for p in range(B_HEADS // 2):
        pair = _rope64x2(q[:, nope_w + p * 128: nope_w + (p + 1) * 128], cos, slo, shi)
        for half in range(2):
            h = 2 * p + half
            qo_ref[:, h * 256: h * 256 + 128] = q[:, h * 128:(h + 1) * 128].astype(BF16)
            keep = first if half == 0 else jnp.logical_not(first)
            qo_ref[:, h * 256 + 128: (h + 1) * 256] = jnp.where(keep, pair, 0.0).astype(BF16)

    kr = jnp.where(first, misc_ref[...], 0.0)
    kr = kr + pltpu.roll(kr, B_QK_ROPE, 1)
    kr = _rope64x2(kr, cos, slo, shi).astype(BF16)
    ckv = _rms(bc_ref[:, B_Q_LORA:], ckvg_ref[...]).astype(BF16)
    kv = _dot(ckv, wukv_ref[...])
    for h in range(B_HEADS):
        ko_ref[:, h * 256: h * 256 + 128] = kv[:, h * 128:(h + 1) * 128].astype(BF16)
        ko_ref[:, h * 256 + 128: (h + 1) * 256] = kr
    vo_ref[...] = kv[:, nope_w:].astype(BF16)


def _mla_prep(proj, tabs_b, cq_gain, ckv_gain, w_uq_p, w_ukv_p, S):
    T = proj.shape[0]
    tm = 512
    nb = S // tm
    bcw = B_Q_LORA + B_KV_LORA
    tab_spec = pl.BlockSpec((tm, 128), lambda i: (i % nb, 0))
    full = lambda a: pl.BlockSpec(a.shape, lambda i: (0, 0))
    return pl.pallas_call(
        _mla_prep_kernel,
        grid=(T // tm,),
        in_specs=[pl.BlockSpec((tm, bcw), lambda i: (i, COL_BC // bcw)),
                  pl.BlockSpec((tm, 128), lambda i: (i, COL_MISC // 128)),
                  tab_spec, tab_spec, tab_spec, full(cq_gain), full(ckv_gain),
                  full(w_uq_p), full(w_ukv_p)],
        out_specs=[pl.BlockSpec((tm, B_HEADS * 256), lambda i: (i, 0)),
                   pl.BlockSpec((tm, B_HEADS * 256), lambda i: (i, 0)),
                   pl.BlockSpec((tm, B_HEADS * B_V_DIM), lambda i: (i, 0))],
        out_shape=[jax.ShapeDtypeStruct((T, B_HEADS * 256), BF16),
                   jax.ShapeDtypeStruct((T, B_HEADS * 256), BF16),
                   jax.ShapeDtypeStruct((T, B_HEADS * B_V_DIM), BF16)],
        compiler_params=_cparams(("parallel",)),
        name="mla_prep",
    )(proj, proj, *tabs_b, cq_gain, ckv_gain, w_uq_p, w_ukv_p)


def _attn_kernel(q_ref, k_ref, v_ref, o_ref, m_scr, l_scr, acc_scr, *, G, dq, dv, tk, sub):
    tq = q_ref.shape[0]
    S = k_ref.shape[0]
    groups = [(g, r) for g in range(G) for r in range(tq // sub)]
    m_scr[...] = jnp.full(m_scr.shape, -jnp.inf, F32)
    l_scr[...] = jnp.zeros(l_scr.shape, F32)
    acc_scr[...] = jnp.zeros(acc_scr.shape, F32)

    stages = [(c, n) for c in range(S // tk) for n in range(len(groups))]

    def scores(stage):
        c, n = stage
        g, r = groups[n]
        return _dot_t(q_ref[r * sub:(r + 1) * sub, g * dq:(g + 1) * dq], k_ref[c * tk:(c + 1) * tk, :])

    s_next = scores(stages[0])
    for i, (c, n) in enumerate(stages):
        s = s_next
        if i + 1 < len(stages):
            s_next = scores(stages[i + 1])
        rows = slice(n * sub, (n + 1) * sub)
        blocks = [s[:, j * 128:(j + 1) * 128] for j in range(tk // 128)]
        m_blk = functools.reduce(jnp.maximum, blocks)
        m_old = m_scr[rows, :]
        m_new = jnp.maximum(m_old, jnp.max(m_blk, axis=-1, keepdims=True))
        alpha = jnp.exp2(m_old - m_new)
        p_blocks = [jnp.exp2(b - m_new) for b in blocks]
        l_scr[rows, :] = alpha * l_scr[rows, :] + functools.reduce(jnp.add, p_blocks)
        p = jnp.concatenate([b.astype(BF16) for b in p_blocks], axis=1)
        acc_scr[rows, :] = alpha * acc_scr[rows, :] + _dot(p, v_ref[c * tk:(c + 1) * tk, :])
        m_scr[rows, :] = m_new

    for n, (g, r) in enumerate(groups):
        rows = slice(n * sub, (n + 1) * sub)
        l = jnp.sum(l_scr[rows, :], axis=-1, keepdims=True)
        o_ref[r * sub:(r + 1) * sub, g * dv:(g + 1) * dv] = acc_scr[rows, :] / l


def _attention(q, k, v, *, B, S, Hk, G, dq, dv, tq, tk):
    assert dv == 128, "row statistics are kept 128 lanes wide to match the value width"
    kern = functools.partial(_attn_kernel, G=G, dq=dq, dv=dv, tk=tk, sub=256)
    M = G * tq
    return pl.pallas_call(
        kern,
        grid=(B, Hk, S // tq),
        in_specs=[pl.BlockSpec((None, tq, G * dq), lambda b, h, i: (b, i, h)),
                  pl.BlockSpec((None, S, dq), lambda b, h, i: (b, 0, h)),
                  pl.BlockSpec((None, S, dv), lambda b, h, i: (b, 0, h))],
        out_specs=pl.BlockSpec((None, tq, G * dv), lambda b, h, i: (b, i, h)),
        out_shape=jax.ShapeDtypeStruct((B, S, Hk * G * dv), F32),
        scratch_shapes=[pltpu.VMEM((M, 128), F32), pltpu.VMEM((M, 128), F32), pltpu.VMEM((M, dv), F32)],
        compiler_params=_cparams(("parallel", "parallel", "arbitrary")),
        name=f"attn_g{G}_d{dq}",
    )(q, k, v)


def _conv_kernel(x_ref, w_ref, b_ref, o_ref, *, k_scale_from):
    x = x_ref[...]
    S = x.shape[0]
    row = lax.broadcasted_iota(jnp.int32, x.shape, 0)
    pad = CONV_W // 2
    y = x * w_ref[pad:pad + 1, :] + b_ref[...]
    for j in range(CONV_W):
        d = j - pad
        if d == 0:
            continue
        shifted = pltpu.roll(x, (-d) % S, 0)
        valid = (row + d >= 0) & (row + d < S)
        y = y + jnp.where(valid, shifted, 0.0) * w_ref[j:j + 1, :]
    y = y * jax.nn.sigmoid(y)
    scale = jnp.where(pl.program_id(1) >= k_scale_from, C_DQK ** -0.5, 1.0)
    o_ref[...] = y * scale


def _conv_silu(proj3, conv_w, conv_b):
    B, S, _ = proj3.shape
    W = 2 * C_HEADS * C_DQK
    tc = 128
    kern = functools.partial(_conv_kernel, k_scale_from=(C_HEADS * C_DQK) // tc)
    return pl.pallas_call(
        kern,
        grid=(B, W // tc),
        in_specs=[pl.BlockSpec((None, S, tc), lambda b, j: (b, 0, COL_CQK // tc + j)),
                  pl.BlockSpec((CONV_W, tc), lambda b, j: (0, j)),
                  pl.BlockSpec((1, tc), lambda b, j: (0, j))],
        out_specs=pl.BlockSpec((None, S, tc), lambda b, j: (b, 0, j)),
        out_shape=jax.ShapeDtypeStruct((B, S, W), F32),
        compiler_params=_cparams(("parallel", "parallel")),
        name="conv_silu",
    )(proj3, conv_w, conv_b)


def _mlstm_kernel(q_ref, k_ref, v_ref, g_ref, gb_ref, o_ref, C_scr, n_scr, m_scr):
    d = pl.program_id(0)
    L = MLSTM_CHUNK

    @pl.when(pl.program_id(2) == 0)
    def _():
        C_scr[...] = jnp.zeros(C_scr.shape, F32)
        n_scr[...] = jnp.zeros(n_scr.shape, F32)
        m_scr[...] = jnp.zeros(m_scr.shape, F32)

    gates = g_ref[...] + gb_ref[...]
    glane = lax.broadcasted_iota(jnp.int32, gates.shape, 1)
    jj = lax.broadcasted_iota(jnp.int32, (L, L), 0)
    ss = lax.broadcasted_iota(jnp.int32, (L, L), 1)
    eye = jj == ss
    sgn = 1 - 2 * d
    seen = (ss - jj) * sgn <= 0
    seen_t = (jj - ss) * sgn <= 0
    neg_inf = jnp.float32(-jnp.inf)

    for h in range(C_HEADS):
        sl = slice(h * C_DQK, (h + 1) * C_DQK)
        i_lane = MISC_GATE_LANE + d * (2 * C_HEADS) + h
        f_lane = i_lane + C_HEADS
        i_col = jnp.sum(jnp.where(glane == i_lane, gates, 0.0), axis=1, keepdims=True)
        f_pre = jnp.sum(jnp.where(glane == f_lane, gates, 0.0), axis=1, keepdims=True)
        f_col = jnp.minimum(f_pre, 0.0) - jnp.log1p(jnp.exp(-jnp.abs(f_pre)))
        f_row = jnp.sum(jnp.where(eye, f_col, 0.0), axis=0, keepdims=True)
        i_row = jnp.sum(jnp.where(eye, i_col, 0.0), axis=0, keepdims=True)
        b_col = jnp.sum(jnp.where(seen, f_row, 0.0), axis=1, keepdims=True)
        b_row = jnp.sum(jnp.where(seen_t, f_col, 0.0), axis=0, keepdims=True)
        g_tot = jnp.sum(f_col, axis=0, keepdims=True)
        a_col = g_tot - b_col + i_col

        q = q_ref[:, sl]
        k = k_ref[:, sl]
        v = v_ref[:, sl]
        qb, kb, vb = q.astype(BF16), k.astype(BF16), v.astype(BF16)
        C_prev = C_scr[h]
        n_prev = n_scr[h]
        m_prev = m_scr[h][:, 0:1]

        dmat = jnp.where(seen, b_col - b_row + i_row, neg_inf)
        m_inter = b_col + m_prev
        m_j = jnp.maximum(m_inter, jnp.max(dmat, axis=1, keepdims=True))
        s = _dot_t(qb, kb) * jnp.exp(dmat - m_j)
        inter = jnp.exp(m_inter - m_j)
        num = _dot(s.astype(BF16), vb) + inter * _dot(qb, C_prev.astype(BF16))
        den = jnp.sum(s, axis=1, keepdims=True) + inter * jnp.sum(q * n_prev, axis=1, keepdims=True)
        o_ref[:, sl] = num / jnp.maximum(jnp.abs(den), jnp.exp(-m_j))

        m_new = jnp.maximum(g_tot + m_prev, jnp.max(a_col, axis=0, keepdims=True))
        decay = jnp.exp(g_tot + m_prev - m_new)
        wk = jnp.exp(a_col - m_new) * k
        upd = lax.dot_general(wk.astype(BF16), vb, (((0,), (0,)), ((), ())),
                              preferred_element_type=F32)
        C_scr[h] = decay * C_prev + upd
        n_scr[h] = decay * n_prev + jnp.sum(wk, axis=0, keepdims=True)
        m_scr[h] = jnp.broadcast_to(m_new, m_scr.shape[1:])


def _mlstm(qk3, proj3, gate_b_row):
    B, S, _ = proj3.shape
    L = MLSTM_CHUNK
    nc = S // L
    W = C_HEADS * C_DQK
    ceff = lambda d, c: c + d * (nc - 1 - 2 * c)
    return pl.pallas_call(
        _mlstm_kernel,
        grid=(2, B, nc),
        in_specs=[pl.BlockSpec((None, L, W), lambda d, b, c: (b, ceff(d, c), 0)),
                  pl.BlockSpec((None, L, W), lambda d, b, c: (b, ceff(d, c), 1)),
                  pl.BlockSpec((None, L, W), lambda d, b, c: (b, ceff(d, c), COL_CV // W)),
                  pl.BlockSpec((None, L, 128), lambda d, b, c: (b, ceff(d, c), COL_MISC // 128)),
                  pl.BlockSpec((1, 128), lambda d, b, c: (0, 0))],
        out_specs=pl.BlockSpec((None, None, L, W), lambda d, b, c: (d, b, ceff(d, c), 0)),
        out_shape=jax.ShapeDtypeStruct((2, B, S, W), F32),
        scratch_shapes=[pltpu.VMEM((C_HEADS, C_DQK, C_DV), F32),
                        pltpu.VMEM((C_HEADS, 1, C_DQK), F32),
                        pltpu.VMEM((C_HEADS, 1, 128), F32)],
        compiler_params=_cparams(("parallel", "parallel", "arbitrary")),
        name="mlstm",
    )(qk3, qk3, proj3, proj3, gate_b_row)


ROW_TILE = 8
HALF_D = D_MODEL // 2
U32 = jnp.uint32
HI_MASK = 0xFFFF0000


def _pack_rows(y, o_ref):
    n = y.shape[0]
    for s in range(ROW_TILE):
        lo = y[:, s * 128:(s + 1) * 128].astype(BF16).astype(F32)
        hi = y[:, HALF_D + s * 128: HALF_D + (s + 1) * 128].astype(BF16).astype(F32)
        w = (lax.bitcast_convert_type(lo, U32) >> 16) | (lax.bitcast_convert_type(hi, U32) & U32(HI_MASK))
        o_ref[pl.ds(s, n, stride=ROW_TILE), :] = w


def _unpack_rows(ref, n):
    lo, hi = [], []
    for s in range(ROW_TILE):
        w = ref[pl.ds(s, n, stride=ROW_TILE), :]
        lo.append(lax.bitcast_convert_type(w << 16, F32))
        hi.append(lax.bitcast_convert_type(w & U32(HI_MASK), F32))
    return lo + hi


def _merge_kernel(ya_ref, yb_ref, hc_ref, co_ref, x_ref, mod_ref, hg_ref, wout_ref, pmg_ref,
                  pfg_ref, rw_ref, xo_ref, h2_ref, h2p_ref, lg_ref, y_scr):
    attn_heads = A_HEADS + B_HEADS
    for h in range(attn_heads + C_HEADS):
        sl = slice(h * HEAD_DIM, (h + 1) * HEAD_DIM)
        g = hg_ref[:, sl]
        if h < A_HEADS:
            y = _rms(ya_ref[:, sl], g)
        elif h < attn_heads:
            y = _rms(yb_ref[:, (h - A_HEADS) * HEAD_DIM:(h - A_HEADS + 1) * HEAD_DIM], g)
        else:
            cs = slice((h - attn_heads) * HEAD_DIM, (h - attn_heads + 1) * HEAD_DIM)
            y = _rms(hc_ref[0, :, cs] + hc_ref[1, :, cs], g) * jax.nn.sigmoid(co_ref[:, cs])
        y_scr[:, sl] = y.astype(BF16)
    y = _dot(y_scr[...], wout_ref[...])
    x_new = x_ref[...] + mod_ref[2:3, :] * _rms(y, pmg_ref[...])
    xo_ref[...] = x_new
    h2 = _rms(x_new, pfg_ref[...]) * (1.0 + mod_ref[4:5, :]) + mod_ref[3:4, :]
    h2_ref[...] = h2.astype(BF16)
    _pack_rows(h2, h2p_ref)
    h_hi = h2.astype(BF16)
    h_lo = (h2 - h_hi.astype(F32)).astype(BF16)
    a = _dot(h_hi, rw_ref[...])
    b = _dot(h_lo, rw_ref[...])
    E = N_EXPERTS
    lg_ref[...] = a[:, :E] + (a[:, E:] + b[:, :E]) + b[:, E:]


def _merge(ya, yb, hc, proj, x2, mod, head_gain, w_out_b, pm_gain, pf_gain, router_w, S):
    T, D = x2.shape
    tm = 256
    nb = S // tm
    aw, cw = A_HEADS * HEAD_DIM, C_HEADS * C_DV
    row = lambda w: pl.BlockSpec((tm, w), lambda i: (i, 0))
    full = lambda a: pl.BlockSpec(a.shape, lambda i: (0,) * a.ndim)
    return pl.pallas_call(
        _merge_kernel,
        grid=(T // tm,),
        in_specs=[row(aw), row(aw),
                  pl.BlockSpec((2, tm, cw), lambda i: (0, i, 0)),
                  pl.BlockSpec((tm, cw), lambda i: (i, COL_CO // cw)),
                  row(D),
                  pl.BlockSpec((None, N_MOD, D), lambda i: (i // nb, 0, 0)),
                  full(head_gain), full(w_out_b), full(pm_gain), full(pf_gain), full(router_w)],
        out_specs=[row(D), row(D), pl.BlockSpec((tm * ROW_TILE, 128), lambda i: (i, 0)),
                   row(N_EXPERTS)],
        out_shape=[jax.ShapeDtypeStruct((T, D), F32),
                   jax.ShapeDtypeStruct((T, D), BF16),
                   jax.ShapeDtypeStruct((T * ROW_TILE, 128), U32),
                   jax.ShapeDtypeStruct((T, N_EXPERTS), F32)],
        scratch_shapes=[pltpu.VMEM((tm, D), BF16)],
        compiler_params=_cparams(("parallel",)),
        name="merge_out_proj",
    )(ya, yb, hc, proj, x2, mod, head_gain, w_out_b, pm_gain, pf_gain, router_w)


def _router_kernel(lg_ref, bias_ref, idx_ref, gate_ref, mask_ref):
    scores = jax.nn.sigmoid(lg_ref[...])
    sel = scores + bias_ref[...]
    lane = lax.broadcasted_iota(jnp.int32, scores.shape, 1).astype(F32)
    col = lax.broadcasted_iota(jnp.int32, idx_ref.shape, 1)
    idx = jnp.zeros(idx_ref.shape, F32)
    gate = jnp.zeros(gate_ref.shape, F32)
    mask = jnp.zeros(scores.shape, F32)
    for kk in range(TOP_K):
        mx = jnp.max(sel, axis=1, keepdims=True)
        am = jnp.min(jnp.where(sel == mx, lane, float(N_EXPERTS)), axis=1, keepdims=True)
        hit = lane == am
        sc = jnp.sum(jnp.where(hit, scores, 0.0), axis=1, keepdims=True)
        idx = jnp.where(col == kk, am, idx)
        gate = jnp.where(col == kk, sc, gate)
        mask = jnp.where(hit, 1.0, mask)
        sel = jnp.where(hit, -jnp.inf, sel)
    gate = gate / jnp.sum(gate, axis=1, keepdims=True) * ROUTED_SCALE
    idx_ref[...] = idx.astype(jnp.int32)
    gate_ref[...] = gate
    mask_ref[...] = mask.astype(jnp.int32)


def _router(logits, bias):
    T, E = logits.shape
    tm = 1024
    row = lambda w: pl.BlockSpec((tm, w), lambda i: (i, 0))
    return pl.pallas_call(
        _router_kernel,
        grid=(T // tm,),
        in_specs=[row(E), pl.BlockSpec((1, E), lambda i: (0, 0))],
        out_specs=[row(8), row(8), row(E)],
        out_shape=[jax.ShapeDtypeStruct((T, 8), jnp.int32),
                   jax.ShapeDtypeStruct((T, 8), F32),
                   jax.ShapeDtypeStruct((T, E), jnp.int32)],
        compiler_params=_cparams(("parallel",)),
        name="router_topk",
    )(logits, bias)


TAIL_BITS = MOE_BLK.bit_length() - 1


def _tail_copies(tail_ref, zero_scr, xs_ref, zsem, fn):
    for e in range(N_EXPERTS):
        start_row = tail_ref[2 * e]
        n_tail = tail_ref[2 * e + 1]
        for bit in range(TAIL_BITS):
            size = 1 << bit
            cur = start_row + (n_tail & ~(2 * size - 1))
            cur = pl.multiple_of(cur * ROW_TILE, ROW_TILE)

            @pl.when((n_tail & size) != 0)
            def _():
                fn(pltpu.make_async_copy(zero_scr.at[pl.ds(0, size * ROW_TILE), :],
                                         xs_ref.at[pl.ds(cur, size * ROW_TILE), :], zsem))


def _row_tile(ref, row):
    return ref.at[pl.ds(pl.multiple_of(row * ROW_TILE, ROW_TILE), ROW_TILE), :]


def _dispatch_kernel(tail_ref, pos_ref, h_ref, xs_ref, zero_scr, sem, zsem, *, tm):
    i = pl.program_id(0)

    def start(r, c):
        src = _row_tile(h_ref, r)
        for kk in range(TOP_K):
            pltpu.make_async_copy(src, _row_tile(xs_ref, pos_ref[0, r * TOP_K + kk]), sem).start()
        return c

    lax.fori_loop(0, tm, start, 0)

    @pl.when(i == 0)
    def _():
        zero_scr[...] = jnp.zeros(zero_scr.shape, zero_scr.dtype)
        _tail_copies(tail_ref, zero_scr, xs_ref, zsem, lambda cp: cp.start())
        _tail_copies(tail_ref, zero_scr, xs_ref, zsem, lambda cp: cp.wait())

    for kk in range(TOP_K):
        pltpu.make_async_copy(h_ref, xs_ref.at[pl.ds(0, tm * ROW_TILE), :], sem).wait()


def _dispatch(h2p, pos3, tail_info, P):
    T = h2p.shape[0] // ROW_TILE
    tm = pos3.shape[2] // TOP_K
    kern = functools.partial(_dispatch_kernel, tm=tm)
    return pl.pallas_call(
        kern,
        grid_spec=pltpu.PrefetchScalarGridSpec(
            num_scalar_prefetch=1,
            grid=(T // tm,),
            in_specs=[pl.BlockSpec((None, 1, tm * TOP_K), lambda i, tail: (i, 0, 0),
                                   memory_space=pltpu.SMEM),
                      pl.BlockSpec((tm * ROW_TILE, 128), lambda i, tail: (i, 0))],
            out_specs=pl.BlockSpec(memory_space=pl.ANY),
            scratch_shapes=[pltpu.VMEM((MOE_BLK // 2 * ROW_TILE, 128), U32),
                            pltpu.SemaphoreType.DMA(()), pltpu.SemaphoreType.DMA(())]),
        out_shape=jax.ShapeDtypeStruct((P * ROW_TILE, 128), U32),
        compiler_params=_cparams(("arbitrary",)),
        name="moe_dispatch",
    )(tail_info, pos3, h2p)


def _expert_kernel(blk_e_ref, nused_ref, next_ref, xs_ref, wg_hbm, wu_hbm, wd_hbm, ys_ref,
                   wg_f, wu_f, wd_f, wg_b, wu_b, wd_b, sems, *, layer):
    i = pl.program_id(0)
    e = blk_e_ref[i]
    e_prev = blk_e_ref[jnp.maximum(i - 1, 0)]
    used = i < nused_ref[0]

    def fetch(ex):
        return [pltpu.make_async_copy(hbm.at[layer, ex], buf, sems.at[n])
                for n, (hbm, buf) in enumerate(((wg_hbm, wg_f), (wu_hbm, wu_f), (wd_hbm, wd_f)))]

    @pl.when(i == 0)
    def _():
        for cp in fetch(e):
            cp.start()

    @pl.when(used & ((i == 0) | (e != e_prev)))
    def _():
        for cp in fetch(e):
            cp.wait()
        wg_b[...] = wg_f[...].astype(BF16)
        wu_b[...] = wu_f[...].astype(BF16)
        wd_b[...] = wd_f[...].astype(BF16)
        e_next = next_ref[e]

        @pl.when(e_next >= 0)
        def _():
            for cp in fetch(e_next):
                cp.start()

    @pl.when(used)
    def _():
        x = jnp.concatenate([c.astype(BF16) for c in _unpack_rows(xs_ref, MOE_BLK)], axis=1)
        gte = _dot(x, wg_b[...])
        a = gte * jax.nn.sigmoid(gte) * _dot(x, wu_b[...])
        _pack_rows(_dot(a.astype(BF16), wd_b[...]), ys_ref)

    @pl.when(jnp.logical_not(used))
    def _():
        ys_ref[...] = jnp.zeros(ys_ref.shape, ys_ref.dtype)


def _experts(xs, blk_e, nused, next_e, w_gate, w_up, w_down, layer):
    P = xs.shape[0] // ROW_TILE
    D, F = w_gate.shape[-2:]
    NB = P // MOE_BLK
    last = lambda i, nu: jnp.minimum(i, nu[0] - 1)
    hbm = pl.BlockSpec(memory_space=pl.ANY)
    return pl.pallas_call(
        functools.partial(_expert_kernel, layer=layer),
        grid_spec=pltpu.PrefetchScalarGridSpec(
            num_scalar_prefetch=3,
            grid=(NB,),
            in_specs=[pl.BlockSpec((MOE_BLK * ROW_TILE, 128), lambda i, be, nu, nx: (last(i, nu), 0)),
                      hbm, hbm, hbm],
            out_specs=pl.BlockSpec((MOE_BLK * ROW_TILE, 128), lambda i, be, nu, nx: (i, 0)),
            scratch_shapes=[pltpu.VMEM((D, F), F32), pltpu.VMEM((D, F), F32), pltpu.VMEM((F, D), F32),
                            pltpu.VMEM((D, F), BF16), pltpu.VMEM((D, F), BF16),
                            pltpu.VMEM((F, D), BF16), pltpu.SemaphoreType.DMA((3,))]),
        out_shape=jax.ShapeDtypeStruct((P * ROW_TILE, 128), U32),
        compiler_params=_cparams(("arbitrary",)),
        name="moe_experts",
    )(blk_e, nused, next_e, xs, w_gate, w_up, w_down)


def _combine_kernel(pos_ref, pos_next_ref, h_ref, gate_ref, x_ref, mod_ref, wsg_ref, wsu_ref, wsd_ref,
                    pg_ref, ys_ref, xo_ref, gbuf, sems, *, tm):
    i = pl.program_id(0)
    slot = i % 2

    def gather(p_ref, s):
        def start(r, c):
            for kk in range(TOP_K):
                pltpu.make_async_copy(_row_tile(ys_ref, p_ref[0, r * TOP_K + kk]),
                                      _row_tile(gbuf.at[s, kk], r), sems.at[s]).start()
            return c
        lax.fori_loop(0, tm, start, 0)

    @pl.when(i == 0)
    def _():
        gather(pos_ref, 0)

    @pl.when(i + 1 < pl.num_programs(0))
    def _():
        gather(pos_next_ref, 1 - slot)

    h = h_ref[...]
    gte = _dot(h, wsg_ref[...])
    a = gte * jax.nn.sigmoid(gte) * _dot(h, wsu_ref[...])
    y = _dot(a.astype(BF16), wsd_ref[...])
    for kk in range(TOP_K):
        pltpu.make_async_copy(ys_ref.at[pl.ds(0, tm * ROW_TILE), :], gbuf.at[slot, kk],
                              sems.at[slot]).wait()
    gate = gate_ref[...]
    for kk in range(TOP_K):
        y = y + gate[:, kk:kk + 1] * jnp.concatenate(_unpack_rows(gbuf.at[slot, kk], tm), axis=1)
    xo_ref[...] = x_ref[...] + mod_ref[5:6, :] * _rms(y, pg_ref[...])


def _combine(pos3, h2, gate8, x2, mod, ws_gate_b, ws_up_b, ws_down_b, post_gain, ys, S):
    T, D = x2.shape
    tm = pos3.shape[2] // TOP_K
    nb = S // tm
    kern = functools.partial(_combine_kernel, tm=tm)
    row = lambda w: pl.BlockSpec((tm, w), lambda i: (i, 0))
    full = lambda a: pl.BlockSpec(a.shape, lambda i: (0,) * a.ndim)
    n_steps = T // tm
    return pl.pallas_call(
        kern,
        grid=(n_steps,),
        in_specs=[pl.BlockSpec((None, 1, tm * TOP_K), lambda i: (i, 0, 0), memory_space=pltpu.SMEM),
                  pl.BlockSpec((None, 1, tm * TOP_K), lambda i: (jnp.minimum(i + 1, n_steps - 1), 0, 0),
                               memory_space=pltpu.SMEM),
                  row(D), row(8), row(D),
                  pl.BlockSpec((None, N_MOD, D), lambda i: (i // nb, 0, 0)),
                  full(ws_gate_b), full(ws_up_b), full(ws_down_b), full(post_gain),
                  pl.BlockSpec(memory_space=pl.ANY)],
        out_specs=row(D),
        out_shape=jax.ShapeDtypeStruct((T, D), F32),
        scratch_shapes=[pltpu.VMEM((2, TOP_K, tm * ROW_TILE, 128), U32),
                        pltpu.SemaphoreType.DMA((2,))],
        compiler_params=_cparams(("arbitrary",)),
        name="moe_combine",
    )(pos3, pos3, h2, gate8, x2, mod, ws_gate_b, ws_up_b, ws_down_b, post_gain, ys)


def _route_plan(idx8, mask, n_blocks):
    idx = idx8[:, :TOP_K]
    counts = jnp.sum(mask, axis=0)
    rank = jnp.cumsum(mask, axis=0) - mask
    padded = (counts + MOE_BLK - 1) // MOE_BLK * MOE_BLK
    pend = jnp.cumsum(padded)
    pstart = pend - padded
    pos = jnp.take_along_axis(pstart[None, :] + rank, idx, axis=1).astype(jnp.int32)
    blk_start = jnp.arange(n_blocks, dtype=jnp.int32) * MOE_BLK
    blk_e = jnp.minimum(jnp.sum(pend[None, :] <= blk_start[:, None], axis=1), N_EXPERTS - 1)
    nused = (pend[-1:] // MOE_BLK).astype(jnp.int32)
    tail = jnp.stack([pstart + counts, padded - counts], axis=1).reshape(-1).astype(jnp.int32)
    ids = jnp.where(counts > 0, jnp.arange(N_EXPERTS), N_EXPERTS)
    after = jnp.concatenate([lax.cummin(ids, reverse=True)[1:], jnp.full((1,), N_EXPERTS)])
    next_e = jnp.where(after < N_EXPERTS, after, -1).astype(jnp.int32)
    return pos, blk_e.astype(jnp.int32), nused, tail, next_e


def _rope_tables(S, rot_dim):
    rows = S // GRID_W
    row = jnp.repeat(jnp.arange(rows, dtype=F32), GRID_W)
    col = jnp.tile(jnp.arange(GRID_W, dtype=F32), rows)
    axis_dim = rot_dim // 2
    inv_freq = ROPE_THETA ** (-jnp.arange(0, axis_dim, 2, dtype=F32) / axis_dim)
    ang_r = row[:, None] * inv_freq
    ang_c = col[:, None] * inv_freq
    z = jnp.zeros_like(ang_r)
    cos = jnp.concatenate([jnp.cos(ang_r)] * 2 + [jnp.cos(ang_c)] * 2, axis=1)
    sin_lo = jnp.concatenate([-jnp.sin(ang_r), z, -jnp.sin(ang_c), z], axis=1)
    sin_hi = jnp.concatenate([z, jnp.sin(ang_r), z, jnp.sin(ang_c)], axis=1)
    reps = 128 // rot_dim
    return tuple(jnp.tile(t, (1, reps)) for t in (cos, sin_lo, sin_hi))


def _prep_w_in(w):
    widths = (768, 256, 256, B_Q_LORA, B_KV_LORA, B_QK_ROPE, 512, 512, 512, 512, 4 * C_HEADS)
    offs = np.cumsum(widths)[:-1].tolist()
    a_q, a_k, a_v, b_cq, b_ckv, b_kr, c_q, c_k, c_v, c_o, c_g = jnp.split(w, offs, axis=-1)
    pad = jnp.zeros(w.shape[:-1] + (128 - B_QK_ROPE - 4 * C_HEADS,), w.dtype)
    out = jnp.concatenate([a_q, a_k, a_v, b_cq, b_ckv, b_kr, c_g, pad, c_v, c_o, c_q, c_k], axis=-1)
    assert out.shape[-1] == PROJ_PAD
    return out.astype(BF16)


def _split_hi_lo(w):
    hi = w.astype(BF16)
    lo = (w - hi.astype(F32)).astype(BF16)
    return jnp.concatenate([hi, lo], axis=1)


def _heads_split(w, n_heads, first):
    K = w.shape[0]
    w3 = w.reshape(K, n_heads, -1)
    return jnp.concatenate([w3[:, :, :first].reshape(K, -1), w3[:, :, first:].reshape(K, -1)],
                           axis=1).astype(BF16)


def kernel(x, c, ada_w, ada_b, pre_mix_gain, w_in, a_q_gain, a_k_gain, b_cq_gain, b_ckv_gain, w_uq, w_ukv, c_conv_w, c_conv_b, c_gate_b, head_out_gain, w_out, post_mix_gain, pre_ffn_gain, router_w, router_bias, exp_w_gate, exp_w_up, exp_w_down, sh_w_gate, sh_w_up, sh_w_down, post_ffn_gain):
    B, S, D = x.shape
    T = B * S
    L = ada_w.shape[0]
    n_blocks = T * TOP_K // MOE_BLK + N_EXPERTS
    row2 = lambda v: v.reshape(1, -1)

    tabs_a = _rope_tables(S, HEAD_DIM)
    tabs_b = _rope_tables(S, B_QK_ROPE)
    c_pad = jnp.zeros((8, D), F32).at[:B].set(c)
    mods = _modulation(c_pad, ada_w, ada_b)[:, :B].reshape(L, B, N_MOD, D)

    x2 = x.reshape(T, D)
    for l in range(L):
        mod = mods[l]
        proj = _in_proj(x2, row2(pre_mix_gain[l]), mod, _prep_w_in(w_in[l]), S)
        proj3 = proj.reshape(B, S, PROJ_PAD)

        qa, ka, va = _gqa_prep(proj, tabs_a, row2(a_q_gain[l]), row2(a_k_gain[l]), S)
        ya = _attention(qa.reshape(B, S, -1), ka.reshape(B, S, -1), va.reshape(B, S, -1),
                        B=B, S=S, Hk=A_KV_HEADS, G=A_HEADS // A_KV_HEADS, dq=HEAD_DIM, dv=HEAD_DIM,
                        tq=256, tk=512)

        qb, kb, vb = _mla_prep(proj, tabs_b, row2(b_cq_gain[l]), row2(b_ckv_gain[l]),
                               _heads_split(w_uq[l], B_HEADS, B_QK_NOPE),
                               _heads_split(w_ukv[l], B_HEADS, B_QK_NOPE), S)
        yb = _attention(qb.reshape(B, S, -1), kb.reshape(B, S, -1), vb.reshape(B, S, -1),
                        B=B, S=S, Hk=B_HEADS, G=1, dq=256, dv=B_V_DIM, tq=512, tk=512)

        qk3 = _conv_silu(proj3, c_conv_w[l], row2(c_conv_b[l]))
        gate_b_row = jnp.zeros((1, 128), F32).at[0, MISC_GATE_LANE:MISC_GATE_LANE + 4 * C_HEADS].set(c_gate_b[l])
        hc = _mlstm(qk3, proj3, gate_b_row).reshape(2, T, C_HEADS * C_DV)

        x2, h2, h2p, logits = _merge(ya.reshape(T, -1), yb.reshape(T, -1), hc, proj, x2, mod,
                                row2(head_out_gain[l]), w_out[l].astype(BF16), row2(post_mix_gain[l]),
                                row2(pre_ffn_gain[l]), _split_hi_lo(router_w[l]), S)

        idx8, gate8, mask = _router(logits, row2(router_bias[l]))
        pos, blk_e, nused, tail, next_e = _route_plan(idx8, mask, n_blocks)
        xs = _dispatch(h2p, pos.reshape(T // DISPATCH_TM, 1, DISPATCH_TM * TOP_K), tail,
                       n_blocks * MOE_BLK)
        ys = _experts(xs, blk_e, nused, next_e, exp_w_gate, exp_w_up, exp_w_down, l)
        x2 = _combine(pos.reshape(T // COMBINE_TM, 1, COMBINE_TM * TOP_K), h2, gate8, x2, mod, sh_w_gate[l].astype(BF16), sh_w_up[l].astype(BF16),
                      sh_w_down[l].astype(BF16), row2(post_ffn_gain[l]), ys, S)
    return x2.reshape(B, S, D)
```

```python
import functools

import numpy as np
import jax
import jax.numpy as jnp
from jax import lax
from jax.experimental import pallas as pl
from jax.experimental.pallas import tpu as pltpu

D_MODEL = 2048
GRID_W = 64
ROPE_THETA = 10000.0
EPS = 1e-6
HEAD_DIM = 128
A_HEADS, A_KV_HEADS = 6, 2
B_HEADS, B_Q_LORA, B_KV_LORA = 6, 384, 256
B_QK_NOPE, B_QK_ROPE, B_V_DIM = 128, 64, 128
C_HEADS, C_DQK, C_DV = 4, 128, 128
CONV_W = 5
MLSTM_CHUNK = 64
N_EXPERTS, TOP_K, EXPERT_FF, SHARED_FF = 64, 6, 512, 512
ROUTED_SCALE = 2.5
N_MOD = 6

COL_AQ, COL_AK, COL_AV = 0, 768, 1024
COL_BC = 1280
COL_MISC = 1920
COL_CV, COL_CO = 2048, 2560
COL_CQK = 3072
PROJ_PAD = 4096
MISC_GATE_LANE = 64

MOE_BLK = 256
DISPATCH_TM = 512
COMBINE_TM = 256
MLSTM_BB = 2
VMEM_LIMIT = 56 * 1024 * 1024

BF16 = jnp.bfloat16
F32 = jnp.float32
LOG2_E = 1.4426950408889634


def _cparams(sem):
    return pltpu.CompilerParams(dimension_semantics=("arbitrary",) * len(sem),
                                vmem_limit_bytes=VMEM_LIMIT)


def _rms(x, gain):
    return x * lax.rsqrt(jnp.mean(x * x, axis=-1, keepdims=True) + EPS) * gain


def _dot(a, b):
    return jnp.dot(a, b, preferred_element_type=F32)


def _dot_t(a, b):
    return lax.dot_general(a, b, (((1,), (1,)), ((), ())), preferred_element_type=F32)


def _mod_kernel(c_ref, w_ref, b_ref, o_ref):
    c = c_ref[...]
    c_act = (c * jax.nn.sigmoid(c)).astype(BF16)
    o_ref[...] = _dot(c_act, w_ref[...].astype(BF16)) + b_ref[...]


def _modulation(c_pad, ada_w, ada_b):
    L, D, N = ada_w.shape
    M = c_pad.shape[0]
    tn = 1536
    return pl.pallas_call(
        _mod_kernel,
        grid=(L, N // tn),
        in_specs=[pl.BlockSpec((M, D), lambda l, j: (0, 0)),
                  pl.BlockSpec((None, D, tn), lambda l, j: (l, 0, j)),
                  pl.BlockSpec((None, 1, tn), lambda l, j: (l, 0, j))],
        out_specs=pl.BlockSpec((None, M, tn), lambda l, j: (l, 0, j)),
        out_shape=jax.ShapeDtypeStruct((L, M, N), F32),
        compiler_params=_cparams(("parallel", "parallel")),
        name="adaln_mod",
    )(c_pad, ada_w, ada_b.reshape(L, 1, N))


def _in_proj_kernel(x_ref, gain_ref, mod_ref, w_ref, o_ref, *, tn):
    h = _rms(x_ref[...], gain_ref[...])
    h = (h * (1.0 + mod_ref[1:2, :]) + mod_ref[0:1, :]).astype(BF16)
    for j in range(w_ref.shape[1] // tn):
        o_ref[:, j * tn:(j + 1) * tn] = _dot(h, w_ref[:, j * tn:(j + 1) * tn])


def _in_proj(x2, gain, mod, w_in_p, S):
    T, D = x2.shape
    N = w_in_p.shape[1]
    tm = 256
    nb = S // tm
    return pl.pallas_call(
        functools.partial(_in_proj_kernel, tn=1024),
        grid=(T // tm,),
        in_specs=[pl.BlockSpec((tm, D), lambda i: (i, 0)),
                  pl.BlockSpec((1, D), lambda i: (0, 0)),
                  pl.BlockSpec((None, N_MOD, D), lambda i: (i // nb, 0, 0)),
                  pl.BlockSpec((D, N), lambda i: (0, 0))],
        out_specs=pl.BlockSpec((tm, N), lambda i: (i, 0)),
        out_shape=jax.ShapeDtypeStruct((T, N), F32),
        compiler_params=_cparams(("arbitrary",)),
        name="in_proj",
    )(x2, gain, mod, w_in_p)


def _rope128(y, cos, sin_lo, sin_hi):
    return y * cos + pltpu.roll(y, 96, 1) * sin_lo + pltpu.roll(y, 32, 1) * sin_hi


def _gqa_prep_kernel(q_ref, k_ref, v_ref, cos_ref, slo_ref, shi_ref, qg_ref, kg_ref,
                     qo_ref, ko_ref, vo_ref):
    cos, slo, shi = cos_ref[...], slo_ref[...], shi_ref[...]
    scale = HEAD_DIM ** -0.5 * LOG2_E
    for h in range(A_HEADS):
        sl = slice(h * HEAD_DIM, (h + 1) * HEAD_DIM)
        y = _rope128(_rms(q_ref[:, sl], qg_ref[...]), cos, slo, shi)
        qo_ref[:, sl] = (y * scale).astype(BF16)
    for h in range(A_KV_HEADS):
        sl = slice(h * HEAD_DIM, (h + 1) * HEAD_DIM)
        y = _rope128(_rms(k_ref[:, sl], kg_ref[...]), cos, slo, shi)
        ko_ref[:, sl] = y.astype(BF16)
    vo_ref[...] = v_ref[...].astype(BF16)


def _gqa_prep(proj, tabs_a, q_gain, k_gain, S):
    T = proj.shape[0]
    tm = 512
    nb = S // tm
    qw, kw = A_HEADS * HEAD_DIM, A_KV_HEADS * HEAD_DIM
    tab_spec = pl.BlockSpec((tm, HEAD_DIM), lambda i: (i % nb, 0))
    g_spec = pl.BlockSpec((1, HEAD_DIM), lambda i: (0, 0))
    return pl.pallas_call(
        _gqa_prep_kernel,
        grid=(T // tm,),
        in_specs=[pl.BlockSpec((tm, qw), lambda i: (i, COL_AQ // qw)),
                  pl.BlockSpec((tm, kw), lambda i: (i, COL_AK // kw)),
                  pl.BlockSpec((tm, kw), lambda i: (i, COL_AV // kw)),
                  tab_spec, tab_spec, tab_spec, g_spec, g_spec],
        out_specs=[pl.BlockSpec((tm, qw), lambda i: (i, 0)),
                   pl.BlockSpec((tm, kw), lambda i: (i, 0)),
                   pl.BlockSpec((tm, kw), lambda i: (i, 0))],
        out_shape=[jax.ShapeDtypeStruct((T, qw), BF16),
                   jax.ShapeDtypeStruct((T, kw), BF16),
                   jax.ShapeDtypeStruct((T, kw), BF16)],
        compiler_params=_cparams(("parallel",)),
        name="gqa_prep",
    )(proj, proj, proj, *tabs_a, q_gain, k_gain)


def _rope64x2(y, cos, sin_lo, sin_hi):
    return y * cos + pltpu.roll(y, 112, 1) * sin_lo + pltpu.roll(y, 16, 1) * sin_hi


def _mla_prep_kernel(bc_ref, misc_ref, cos_ref, slo_ref, shi_ref, cqg_ref, ckvg_ref,
                     wuq_ref, wukv_ref, qo_ref, ko_ref, vo_ref):
    cos, slo, shi = cos_ref[...], slo_ref[...], shi_ref[...]
    lane = lax.broadcasted_iota(jnp.int32, cos.shape, 1)
    first = lane < B_QK_ROPE
    scale = (B_QK_NOPE + B_QK_ROPE) ** -0.5 * LOG2_E
    nope_w = B_HEADS * B_QK_NOPE

    cq = _rms(bc_ref[:, :B_Q_LORA], cqg_ref[...]).astype(BF16)
    q = _dot(cq, wuq_ref[...]) * scale
    for p in range(B_HEADS // 2):
        pair = _rope64x2(q[:, nope_w + p * 128: nope_w + (p + 1) * 128], cos, slo, shi)
        for half in range(2):
            h = 2 * p + half
            qo_ref[:, h * 256: h * 256 + 128] = q[:, h * 128:(h + 1) * 128].astype(BF16)
            keep = first if half == 0 else jnp.logical_not(first)
            qo_ref[:, h * 256 + 128: (h + 1) * 256] = jnp.where(keep, pair, 0.0).astype(BF16)

    kr = jnp.where(first, misc_ref[...], 0.0)
    kr = kr + pltpu.roll(kr, B_QK_ROPE, 1)
    kr = _rope64x2(kr, cos, slo, shi).astype(BF16)
    ckv = _rms(bc_ref[:, B_Q_LORA:], ckvg_ref[...]).astype(BF16)
    kv = _dot(ckv, wukv_ref[...])
    for h in range(B_HEADS):
        ko_ref[:, h * 256: h * 256 + 128] = kv[:, h * 128:(h + 1) * 128].astype(BF16)
        ko_ref[:, h * 256 + 128: (h + 1) * 256] = kr
    vo_ref[...] = kv[:, nope_w:].astype(BF16)


def _mla_prep(proj, tabs_b, cq_gain, ckv_gain, w_uq_p, w_ukv_p, S):
    T = proj.shape[0]
    tm = 512
    nb = S // tm
    bcw = B_Q_LORA + B_KV_LORA
    tab_spec = pl.BlockSpec((tm, 128), lambda i: (i % nb, 0))
    full = lambda a: pl.BlockSpec(a.shape, lambda i: (0, 0))
    return pl.pallas_call(
        _mla_prep_kernel,
        grid=(T // tm,),
        in_specs=[pl.BlockSpec((tm, bcw), lambda i: (i, COL_BC // bcw)),
                  pl.BlockSpec((tm, 128), lambda i: (i, COL_MISC // 128)),
                  tab_spec, tab_spec, tab_spec, full(cq_gain), full(ckv_gain),
                  full(w_uq_p), full(w_ukv_p)],
        out_specs=[pl.BlockSpec((tm, B_HEADS * 256), lambda i: (i, 0)),
                   pl.BlockSpec((tm, B_HEADS * 256), lambda i: (i, 0)),
                   pl.BlockSpec((tm, B_HEADS * B_V_DIM), lambda i: (i, 0))],
        out_shape=[jax.ShapeDtypeStruct((T, B_HEADS * 256), BF16),
                   jax.ShapeDtypeStruct((T, B_HEADS * 256), BF16),
                   jax.ShapeDtypeStruct((T, B_HEADS * B_V_DIM), BF16)],
        compiler_params=_cparams(("parallel",)),
        name="mla_prep",
    )(proj, proj, *tabs_b, cq_gain, ckv_gain, w_uq_p, w_ukv_p)


def _attn_kernel(q_ref, k_ref, v_ref, o_ref, m_scr, l_scr, acc_scr, *, G, dq, dv, tk, sub):
    tq = q_ref.shape[0]
    S = k_ref.shape[0]
    groups = [(g, r) for g in range(G) for r in range(tq // sub)]
    m_scr[...] = jnp.full(m_scr.shape, -jnp.inf, F32)
    l_scr[...] = jnp.zeros(l_scr.shape, F32)
    acc_scr[...] = jnp.zeros(acc_scr.shape, F32)

    stages = [(c, n) for c in range(S // tk) for n in range(len(groups))]

    def scores(stage):
        c, n = stage
        g, r = groups[n]
        return _dot_t(q_ref[r * sub:(r + 1) * sub, g * dq:(g + 1) * dq], k_ref[c * tk:(c + 1) * tk, :])

    s_next = scores(stages[0])
    for i, (c, n) in enumerate(stages):
        s = s_next
        if i + 1 < len(stages):
            s_next = scores(stages[i + 1])
        rows = slice(n * sub, (n + 1) * sub)
        blocks = [s[:, j * 128:(j + 1) * 128] for j in range(tk // 128)]
        m_blk = functools.reduce(jnp.maximum, blocks)
        m_old = m_scr[rows, :]
        m_new = jnp.maximum(m_old, jnp.max(m_blk, axis=-1, keepdims=True))
        alpha = jnp.exp2(m_old - m_new)
        p_blocks = [jnp.exp2(b - m_new) for b in blocks]
        l_scr[rows, :] = alpha * l_scr[rows, :] + functools.reduce(jnp.add, p_blocks)
        p = jnp.concatenate([b.astype(BF16) for b in p_blocks], axis=1)
        acc_scr[rows, :] = alpha * acc_scr[rows, :] + _dot(p, v_ref[c * tk:(c + 1) * tk, :])
        m_scr[rows, :] = m_new

    for n, (g, r) in enumerate(groups):
        rows = slice(n * sub, (n + 1) * sub)
        l = jnp.sum(l_scr[rows, :], axis=-1, keepdims=True)
        o_ref[r * sub:(r + 1) * sub, g * dv:(g + 1) * dv] = acc_scr[rows, :] / l


def _attention(q, k, v, *, B, S, Hk, G, dq, dv, tq, tk):
    assert dv == 128, "row statistics are kept 128 lanes wide to match the value width"
    kern = functools.partial(_attn_kernel, G=G, dq=dq, dv=dv, tk=tk, sub=256)
    M = G * tq
    return pl.pallas_call(
        kern,
        grid=(B, Hk, S // tq),
        in_specs=[pl.BlockSpec((None, tq, G * dq), lambda b, h, i: (b, i, h)),
                  pl.BlockSpec((None, S, dq), lambda b, h, i: (b, 0, h)),
                  pl.BlockSpec((None, S, dv), lambda b, h, i: (b, 0, h))],
        out_specs=pl.BlockSpec((None, tq, G * dv), lambda b, h, i: (b, i, h)),
        out_shape=jax.ShapeDtypeStruct((B, S, Hk * G * dv), F32),
        scratch_shapes=[pltpu.VMEM((M, 128), F32), pltpu.VMEM((M, 128), F32), pltpu.VMEM((M, dv), F32)],
        compiler_params=_cparams(("parallel", "parallel", "arbitrary")),
        name=f"attn_g{G}_d{dq}",
    )(q, k, v)


def _conv_kernel(x_ref, w_ref, b_ref, o_ref, *, k_scale_from):
    x = x_ref[...]
    S = x.shape[0]
    row = lax.broadcasted_iota(jnp.int32, x.shape, 0)
    pad = CONV_W // 2
    y = x * w_ref[pad:pad + 1, :] + b_ref[...]
    for j in range(CONV_W):
        d = j - pad
        if d == 0:
            continue
        shifted = pltpu.roll(x, (-d) % S, 0)
        valid = (row + d >= 0) & (row + d < S)
        y = y + jnp.where(valid, shifted, 0.0) * w_ref[j:j + 1, :]
    y = y * jax.nn.sigmoid(y)
    scale = jnp.where(pl.program_id(1) >= k_scale_from, C_DQK ** -0.5, 1.0)
    o_ref[...] = y * scale


def _conv_silu(proj3, conv_w, conv_b):
    B, S, _ = proj3.shape
    W = 2 * C_HEADS * C_DQK
    tc = 128
    kern = functools.partial(_conv_kernel, k_scale_from=(C_HEADS * C_DQK) // tc)
    return pl.pallas_call(
        kern,
        grid=(B, W // tc),
        in_specs=[pl.BlockSpec((None, S, tc), lambda b, j: (b, 0, COL_CQK // tc + j)),
                  pl.BlockSpec((CONV_W, tc), lambda b, j: (0, j)),
                  pl.BlockSpec((1, tc), lambda b, j: (0, j))],
        out_specs=pl.BlockSpec((None, S, tc), lambda b, j: (b, 0, j)),
        out_shape=jax.ShapeDtypeStruct((B, S, W), F32),
        compiler_params=_cparams(("parallel", "parallel")),
        name="conv_silu",
    )(proj3, conv_w, conv_b)


def _log_sigmoid(x):
    return jnp.minimum(x, 0.0) - jnp.log1p(jnp.exp(-jnp.abs(x)))


def _mlstm_kernel(qf_ref, kf_ref, vf_ref, gf_ref, qr_ref, kr_ref, vr_ref, gr_ref, gb_ref,
                  of_ref, or_ref, C_scr, n_scr, m_scr, *, BB):
    L = MLSTM_CHUNK

    @pl.when(pl.program_id(1) == 0)
    def _():
        C_scr[...] = jnp.zeros(C_scr.shape, F32)
        n_scr[...] = jnp.zeros(n_scr.shape, F32)
        m_scr[...] = jnp.zeros(m_scr.shape, F32)

    glane = lax.broadcasted_iota(jnp.int32, (L, 128), 1)
    jj = lax.broadcasted_iota(jnp.int32, (L, L), 0)
    ss = lax.broadcasted_iota(jnp.int32, (L, L), 1)
    eye = jj == ss
    neg_inf = jnp.float32(-jnp.inf)
    dirs = ((qf_ref, kf_ref, vf_ref, gf_ref, of_ref, ss <= jj, jj <= ss),
            (qr_ref, kr_ref, vr_ref, gr_ref, or_ref, ss >= jj, jj >= ss))

    chains = []
    for d, (q_ref, k_ref, v_ref, g_ref, o_ref, seen, seen_t) in enumerate(dirs):
        for bb in range(BB):
            gates = g_ref[bb] + gb_ref[...]
            lsig = _log_sigmoid(gates)
            for h in range(C_HEADS):
                c = dict(st=(d * BB + bb) * C_HEADS + h, sl=slice(h * C_DQK, (h + 1) * C_DQK), bb=bb,
                         q_ref=q_ref, k_ref=k_ref, v_ref=v_ref, o_ref=o_ref, seen=seen, seen_t=seen_t)
                i_lane = MISC_GATE_LANE + d * (2 * C_HEADS) + h
                c["i_col"] = jnp.sum(jnp.where(glane == i_lane, gates, 0.0), axis=1, keepdims=True)
                c["f_col"] = jnp.sum(jnp.where(glane == i_lane + C_HEADS, lsig, 0.0), axis=1, keepdims=True)
                chains.append(c)
    for c in chains:
        c["f_row"] = jnp.sum(jnp.where(eye, c["f_col"], 0.0), axis=0, keepdims=True)
        c["i_row"] = jnp.sum(jnp.where(eye, c["i_col"], 0.0), axis=0, keepdims=True)
        c["b_row"] = jnp.sum(jnp.where(c["seen_t"], c["f_col"], 0.0), axis=0, keepdims=True)
        c["g_tot"] = jnp.sum(c["f_col"], axis=0, keepdims=True)
    for c in chains:
        c["b_col"] = jnp.sum(jnp.where(c["seen"], c["f_row"], 0.0), axis=1, keepdims=True)
    for c in chains:
        c["m_prev"] = m_scr[c["st"]][:, 0:1]
        c["dmat"] = jnp.where(c["seen"], c["b_col"] - c["b_row"] + c["i_row"], neg_inf)
        c["m_inter"] = c["b_col"] + c["m_prev"]
        c["a_col"] = c["g_tot"] - c["b_col"] + c["i_col"]
    for c in chains:
        c["m_j"] = jnp.maximum(c["m_inter"], jnp.max(c["dmat"], axis=1, keepdims=True))
        c["m_new"] = jnp.maximum(c["g_tot"] + c["m_prev"], jnp.max(c["a_col"], axis=0, keepdims=True))
    for c in chains:
        c["q"] = c["q_ref"][c["bb"], :, c["sl"]]
        c["k"] = c["k_ref"][c["bb"], :, c["sl"]]
        c["qb"], c["kb"] = c["q"].astype(BF16), c["k"].astype(BF16)
        c["vb"] = c["v_ref"][c["bb"], :, c["sl"]].astype(BF16)
        c["qk"] = _dot_t(c["qb"], c["kb"])
    for c in chains:
        c["qC"] = _dot(c["qb"], C_scr[c["st"]].astype(BF16))
    for c in chains:
        c["s"] = c["qk"] * jnp.exp(c["dmat"] - c["m_j"])
        c["inter"] = jnp.exp(c["m_inter"] - c["m_j"])
        c["wk"] = jnp.exp(c["a_col"] - c["m_new"]) * c["k"]
        c["decay"] = jnp.exp(c["g_tot"] + c["m_prev"] - c["m_new"])
    for c in chains:
        c["sv"] = _dot(c["s"].astype(BF16), c["vb"])
    for c in chains:
        c["upd"] = lax.dot_general(c["wk"].astype(BF16), c["vb"], (((0,), (0,)), ((), ())),
                                   preferred_element_type=F32)
    for c in chains:
        n_prev = n_scr[c["st"]]
        den = (jnp.sum(c["s"], axis=1, keepdims=True)
               + c["inter"] * jnp.sum(c["q"] * n_prev, axis=1, keepdims=True))
        num = c["sv"] + c["inter"] * c["qC"]
        c["o_ref"][c["bb"], :, c["sl"]] = num / jnp.maximum(jnp.abs(den), jnp.exp(-c["m_j"]))
        n_scr[c["st"]] = c["decay"] * n_prev + jnp.sum(c["wk"], axis=0, keepdims=True)
    for c in chains:
        C_scr[c["st"]] = c["decay"] * C_scr[c["st"]] + c["upd"]
        m_scr[c["st"]] = jnp.broadcast_to(c["m_new"], m_scr.shape[1:])


def _mlstm(qk3, proj3, gate_b_row, BB=MLSTM_BB):
    B, S, _ = proj3.shape
    L = MLSTM_CHUNK
    nc = S // L
    W = C_HEADS * C_DQK
    n_chain = 2 * BB * C_HEADS
    fwd = lambda col: (lambda b, c: (b, c, col))
    rev = lambda col: (lambda b, c: (b, nc - 1 - c, col))
    blk = lambda w, imap: pl.BlockSpec((BB, L, w), imap)
    out = jax.ShapeDtypeStruct((B, S, W), F32)
    return pl.pallas_call(
        functools.partial(_mlstm_kernel, BB=BB),
        grid=(B // BB, nc),
        in_specs=[blk(W, fwd(0)), blk(W, fwd(1)), blk(W, fwd(COL_CV // W)), blk(128, fwd(COL_MISC // 128)),
                  blk(W, rev(0)), blk(W, rev(1)), blk(W, rev(COL_CV // W)), blk(128, rev(COL_MISC // 128)),
                  pl.BlockSpec((1, 128), lambda b, c: (0, 0))],
        out_specs=[blk(W, fwd(0)), blk(W, rev(0))],
        out_shape=[out, out],
        scratch_shapes=[pltpu.VMEM((n_chain, C_DQK, C_DV), F32),
                        pltpu.VMEM((n_chain, 1, C_DQK), F32),
                        pltpu.VMEM((n_chain, 1, 128), F32)],
        compiler_params=_cparams(("arbitrary", "arbitrary")),
        name="mlstm",
    )(qk3, qk3, proj3, proj3, qk3, qk3, proj3, proj3, gate_b_row)


ROW_TILE = 8
HALF_D = D_MODEL // 2
U32 = jnp.uint32
HI_MASK = 0xFFFF0000


def _pack_rows(y, o_ref):
    n = y.shape[0]
    for s in range(ROW_TILE):
        lo = y[:, s * 128:(s + 1) * 128].astype(BF16).astype(F32)
        hi = y[:, HALF_D + s * 128: HALF_D + (s + 1) * 128].astype(BF16).astype(F32)
        w = (lax.bitcast_convert_type(lo, U32) >> 16) | (lax.bitcast_convert_type(hi, U32) & U32(HI_MASK))
        o_ref[pl.ds(s, n, stride=ROW_TILE), :] = w


def _unpack_rows(ref, n):
    lo, hi = [], []
    for s in range(ROW_TILE):
        w = ref[pl.ds(s, n, stride=ROW_TILE), :]
        lo.append(lax.bitcast_convert_type(w << 16, F32))
        hi.append(lax.bitcast_convert_type(w & U32(HI_MASK), F32))
    return lo + hi


def _merge_kernel(ya_ref, yb_ref, hf_ref, hr_ref, co_ref, x_ref, mod_ref, hg_ref, wout_ref, pmg_ref,
                  pfg_ref, rw_ref, xo_ref, h2_ref, h2p_ref, lg_ref, y_scr):
    attn_heads = A_HEADS + B_HEADS
    for h in range(attn_heads + C_HEADS):
        sl = slice(h * HEAD_DIM, (h + 1) * HEAD_DIM)
        g = hg_ref[:, sl]
        if h < A_HEADS:
            y = _rms(ya_ref[:, sl], g)
        elif h < attn_heads:
            y = _rms(yb_ref[:, (h - A_HEADS) * HEAD_DIM:(h - A_HEADS + 1) * HEAD_DIM], g)
        else:
            cs = slice((h - attn_heads) * HEAD_DIM, (h - attn_heads + 1) * HEAD_DIM)
            y = _rms(hf_ref[:, cs] + hr_ref[:, cs], g) * jax.nn.sigmoid(co_ref[:, cs])
        y_scr[:, sl] = y.astype(BF16)
    y = _dot(y_scr[...], wout_ref[...])
    x_new = x_ref[...] + mod_ref[2:3, :] * _rms(y, pmg_ref[...])
    xo_ref[...] = x_new
    h2 = _rms(x_new, pfg_ref[...]) * (1.0 + mod_ref[4:5, :]) + mod_ref[3:4, :]
    h2_ref[...] = h2.astype(BF16)
    _pack_rows(h2, h2p_ref)
    h_hi = h2.astype(BF16)
    h_lo = (h2 - h_hi.astype(F32)).astype(BF16)
    a = _dot(h_hi, rw_ref[...])
    b = _dot(h_lo, rw_ref[...])
    E = N_EXPERTS
    lg_ref[...] = a[:, :E] + (a[:, E:] + b[:, :E]) + b[:, E:]


def _merge(ya, yb, hf, hr, proj, x2, mod, head_gain, w_out_b, pm_gain, pf_gain, router_w, S):
    T, D = x2.shape
    tm = 256
    nb = S // tm
    aw, cw = A_HEADS * HEAD_DIM, C_HEADS * C_DV
    row = lambda w: pl.BlockSpec((tm, w), lambda i: (i, 0))
    full = lambda a: pl.BlockSpec(a.shape, lambda i: (0,) * a.ndim)
    return pl.pallas_call(
        _merge_kernel,
        grid=(T // tm,),
        in_specs=[row(aw), row(aw), row(cw), row(cw),
                  pl.BlockSpec((tm, cw), lambda i: (i, COL_CO // cw)),
                  row(D),
                  pl.BlockSpec((None, N_MOD, D), lambda i: (i // nb, 0, 0)),
                  full(head_gain), full(w_out_b), full(pm_gain), full(pf_gain), full(router_w)],
        out_specs=[row(D), row(D), pl.BlockSpec((tm * ROW_TILE, 128), lambda i: (i, 0)),
                   row(N_EXPERTS)],
        out_shape=[jax.ShapeDtypeStruct((T, D), F32),
                   jax.ShapeDtypeStruct((T, D), BF16),
                   jax.ShapeDtypeStruct((T * ROW_TILE, 128), U32),
                   jax.ShapeDtypeStruct((T, N_EXPERTS), F32)],
        scratch_shapes=[pltpu.VMEM((tm, D), BF16)],
        compiler_params=_cparams(("parallel",)),
        name="merge_out_proj",
    )(ya, yb, hf, hr, proj, x2, mod, head_gain, w_out_b, pm_gain, pf_gain, router_w)


def _router_kernel(lg_ref, bias_ref, idx_ref, gate_ref, mask_ref):
    scores = jax.nn.sigmoid(lg_ref[...])
    sel = scores + bias_ref[...]
    lane = lax.broadcasted_iota(jnp.int32, scores.shape, 1).astype(F32)
    col = lax.broadcasted_iota(jnp.int32, idx_ref.shape, 1)
    idx = jnp.zeros(idx_ref.shape, F32)
    gate = jnp.zeros(gate_ref.shape, F32)
    mask = jnp.zeros(scores.shape, F32)
    for kk in range(TOP_K):
        mx = jnp.max(sel, axis=1, keepdims=True)
        am = jnp.min(jnp.where(sel == mx, lane, float(N_EXPERTS)), axis=1, keepdims=True)
        hit = lane == am
        sc = jnp.sum(jnp.where(hit, scores, 0.0), axis=1, keepdims=True)
        idx = jnp.where(col == kk, am, idx)
        gate = jnp.where(col == kk, sc, gate)
        mask = jnp.where(hit, 1.0, mask)
        sel = jnp.where(hit, -jnp.inf, sel)
    gate = gate / jnp.sum(gate, axis=1, keepdims=True) * ROUTED_SCALE
    idx_ref[...] = idx.astype(jnp.int32)
    gate_ref[...] = gate
    mask_ref[...] = mask.astype(jnp.int32)


def _router(logits, bias):
    T, E = logits.shape
    tm = 1024
    row = lambda w: pl.BlockSpec((tm, w), lambda i: (i, 0))
    return pl.pallas_call(
        _router_kernel,
        grid=(T // tm,),
        in_specs=[row(E), pl.BlockSpec((1, E), lambda i: (0, 0))],
        out_specs=[row(8), row(8), row(E)],
        out_shape=[jax.ShapeDtypeStruct((T, 8), jnp.int32),
                   jax.ShapeDtypeStruct((T, 8), F32),
                   jax.ShapeDtypeStruct((T, E), jnp.int32)],
        compiler_params=_cparams(("parallel",)),
        name="router_topk",
    )(logits, bias)


TAIL_BITS = MOE_BLK.bit_length() - 1


def _tail_copies(tail_ref, zero_scr, xs_ref, zsem, fn):
    for e in range(N_EXPERTS):
        start_row = tail_ref[2 * e]
        n_tail = tail_ref[2 * e + 1]
        for bit in range(TAIL_BITS):
            size = 1 << bit
            cur = start_row + (n_tail & ~(2 * size - 1))
            cur = pl.multiple_of(cur * ROW_TILE, ROW_TILE)

            @pl.when((n_tail & size) != 0)
            def _():
                fn(pltpu.make_async_copy(zero_scr.at[pl.ds(0, size * ROW_TILE), :],
                                         xs_ref.at[pl.ds(cur, size * ROW_TILE), :], zsem))


def _row_tile(ref, row):
    return ref.at[pl.ds(pl.multiple_of(row * ROW_TILE, ROW_TILE), ROW_TILE), :]


def _dispatch_kernel(tail_ref, pos_ref, h_ref, xs_ref, zero_scr, sem, zsem, *, tm):
    i = pl.program_id(0)

    def start(r, c):
        src = _row_tile(h_ref, r)
        for kk in range(TOP_K):
            pltpu.make_async_copy(src, _row_tile(xs_ref, pos_ref[0, r * TOP_K + kk]),
                                  sem).start(priority=kk % 2)
        return c

    lax.fori_loop(0, tm, start, 0)

    @pl.when(i == 0)
    def _():
        zero_scr[...] = jnp.zeros(zero_scr.shape, zero_scr.dtype)
        _tail_copies(tail_ref, zero_scr, xs_ref, zsem, lambda cp: cp.start())
        _tail_copies(tail_ref, zero_scr, xs_ref, zsem, lambda cp: cp.wait())

    for kk in range(TOP_K):
        pltpu.make_async_copy(h_ref, xs_ref.at[pl.ds(0, tm * ROW_TILE), :], sem).wait()


def _dispatch(h2p, pos3, tail_info, P):
    T = h2p.shape[0] // ROW_TILE
    tm = pos3.shape[2] // TOP_K
    kern = functools.partial(_dispatch_kernel, tm=tm)
    return pl.pallas_call(
        kern,
        grid_spec=pltpu.PrefetchScalarGridSpec(
            num_scalar_prefetch=1,
            grid=(T // tm,),
            in_specs=[pl.BlockSpec((None, 1, tm * TOP_K), lambda i, tail: (i, 0, 0),
                                   memory_space=pltpu.SMEM),
                      pl.BlockSpec((tm * ROW_TILE, 128), lambda i, tail: (i, 0))],
            out_specs=pl.BlockSpec(memory_space=pl.ANY),
            scratch_shapes=[pltpu.VMEM((MOE_BLK // 2 * ROW_TILE, 128), U32),
                            pltpu.SemaphoreType.DMA(()), pltpu.SemaphoreType.DMA(())]),
        out_shape=jax.ShapeDtypeStruct((P * ROW_TILE, 128), U32),
        compiler_params=_cparams(("arbitrary",)),
        name="moe_dispatch",
    )(tail_info, pos3, h2p)


def _expert_kernel(blk_e_ref, nused_ref, next_ref, xs_ref, wg_hbm, wu_hbm, wd_hbm, ys_ref,
                   wg_f, wu_f, wd_f, wg_b, wu_b, wd_b, sems, *, layer):
    i = pl.program_id(0)
    e = blk_e_ref[i]
    e_prev = blk_e_ref[jnp.maximum(i - 1, 0)]
    used = i < nused_ref[0]

    def fetch(ex):
        return [pltpu.make_async_copy(hbm.at[layer, ex], buf, sems.at[n])
                for n, (hbm, buf) in enumerate(((wg_hbm, wg_f), (wu_hbm, wu_f), (wd_hbm, wd_f)))]

    @pl.when(i == 0)
    def _():
        for cp in fetch(e):
            cp.start()

    @pl.when(used & ((i == 0) | (e != e_prev)))
    def _():
        for cp in fetch(e):
            cp.wait()
        wg_b[...] = wg_f[...].astype(BF16)
        wu_b[...] = wu_f[...].astype(BF16)
        wd_b[...] = wd_f[...].astype(BF16)
        e_next = next_ref[e]

        @pl.when(e_next >= 0)
        def _():
            for cp in fetch(e_next):
                cp.start()

    @pl.when(used)
    def _():
        x = jnp.concatenate([c.astype(BF16) for c in _unpack_rows(xs_ref, MOE_BLK)], axis=1)
        gte = _dot(x, wg_b[...])
        a = gte * jax.nn.sigmoid(gte) * _dot(x, wu_b[...])
        _pack_rows(_dot(a.astype(BF16), wd_b[...]), ys_ref)

    @pl.when(jnp.logical_not(used))
    def _():
        ys_ref[...] = jnp.zeros(ys_ref.shape, ys_ref.dtype)


def _experts(xs, blk_e, nused, next_e, w_gate, w_up, w_down, layer):
    P = xs.shape[0] // ROW_TILE
    D, F = w_gate.shape[-2:]
    NB = P // MOE_BLK
    last = lambda i, nu: jnp.minimum(i, nu[0] - 1)
    hbm = pl.BlockSpec(memory_space=pl.ANY)
    return pl.pallas_call(
        functools.partial(_expert_kernel, layer=layer),
        grid_spec=pltpu.PrefetchScalarGridSpec(
            num_scalar_prefetch=3,
            grid=(NB,),
            in_specs=[pl.BlockSpec((MOE_BLK * ROW_TILE, 128), lambda i, be, nu, nx: (last(i, nu), 0)),
                      hbm, hbm, hbm],
            out_specs=pl.BlockSpec((MOE_BLK * ROW_TILE, 128), lambda i, be, nu, nx: (i, 0)),
            scratch_shapes=[pltpu.VMEM((D, F), F32), pltpu.VMEM((D, F), F32), pltpu.VMEM((F, D), F32),
                            pltpu.VMEM((D, F), BF16), pltpu.VMEM((D, F), BF16),
                            pltpu.VMEM((F, D), BF16), pltpu.SemaphoreType.DMA((3,))]),
        out_shape=jax.ShapeDtypeStruct((P * ROW_TILE, 128), U32),
        compiler_params=_cparams(("arbitrary",)),
        name="moe_experts",
    )(blk_e, nused, next_e, xs, w_gate, w_up, w_down)


def _combine_kernel(pos_ref, pos_next_ref, h_ref, gate_ref, x_ref, mod_ref, wsg_ref, wsu_ref, wsd_ref,
                    pg_ref, ys_ref, xo_ref, gbuf, sems, *, tm):
    i = pl.program_id(0)
    slot = i % 2

    def gather(p_ref, s):
        def start(r, c):
            for kk in range(TOP_K):
                pltpu.make_async_copy(_row_tile(ys_ref, p_ref[0, r * TOP_K + kk]),
                                      _row_tile(gbuf.at[s, kk], r), sems.at[s]).start(priority=kk % 2)
            return c
        lax.fori_loop(0, tm, start, 0)

    @pl.when(i == 0)
    def _():
        gather(pos_ref, 0)

    @pl.when(i + 1 < pl.num_programs(0))
    def _():
        gather(pos_next_ref, 1 - slot)

    h = h_ref[...]
    gte = _dot(h, wsg_ref[...])
    a = gte * jax.nn.sigmoid(gte) * _dot(h, wsu_ref[...])
    y = _dot(a.astype(BF16), wsd_ref[...])
    for kk in range(TOP_K):
        pltpu.make_async_copy(ys_ref.at[pl.ds(0, tm * ROW_TILE), :], gbuf.at[slot, kk],
                              sems.at[slot]).wait()
    gate = gate_ref[...]
    for kk in range(TOP_K):
        y = y + gate[:, kk:kk + 1] * jnp.concatenate(_unpack_rows(gbuf.at[slot, kk], tm), axis=1)
    xo_ref[...] = x_ref[...] + mod_ref[5:6, :] * _rms(y, pg_ref[...])


def _combine(pos3, h2, gate8, x2, mod, ws_gate_b, ws_up_b, ws_down_b, post_gain, ys, S):
    T, D = x2.shape
    tm = pos3.shape[2] // TOP_K
    nb = S // tm
    kern = functools.partial(_combine_kernel, tm=tm)
    row = lambda w: pl.BlockSpec((tm, w), lambda i: (i, 0))
    full = lambda a: pl.BlockSpec(a.shape, lambda i: (0,) * a.ndim)
    n_steps = T // tm
    return pl.pallas_call(
        kern,
        grid=(n_steps,),
        in_specs=[pl.BlockSpec((None, 1, tm * TOP_K), lambda i: (i, 0, 0), memory_space=pltpu.SMEM),
                  pl.BlockSpec((None, 1, tm * TOP_K), lambda i: (jnp.minimum(i + 1, n_steps - 1), 0, 0),
                               memory_space=pltpu.SMEM),
                  row(D), row(8), row(D),
                  pl.BlockSpec((None, N_MOD, D), lambda i: (i // nb, 0, 0)),
                  full(ws_gate_b), full(ws_up_b), full(ws_down_b), full(post_gain),
                  pl.BlockSpec(memory_space=pl.ANY)],
        out_specs=row(D),
        out_shape=jax.ShapeDtypeStruct((T, D), F32),
        scratch_shapes=[pltpu.VMEM((2, TOP_K, tm * ROW_TILE, 128), U32),
                        pltpu.SemaphoreType.DMA((2,))],
        compiler_params=_cparams(("arbitrary",)),
        name="moe_combine",
    )(pos3, pos3, h2, gate8, x2, mod, ws_gate_b, ws_up_b, ws_down_b, post_gain, ys)


def _route_plan(idx8, mask, n_blocks):
    idx = idx8[:, :TOP_K]
    counts = jnp.sum(mask, axis=0)
    rank = jnp.cumsum(mask, axis=0) - mask
    padded = (counts + MOE_BLK - 1) // MOE_BLK * MOE_BLK
    pend = jnp.cumsum(padded)
    pstart = pend - padded
    pos = jnp.take_along_axis(pstart[None, :] + rank, idx, axis=1).astype(jnp.int32)
    blk_start = jnp.arange(n_blocks, dtype=jnp.int32) * MOE_BLK
    blk_e = jnp.minimum(jnp.sum(pend[None, :] <= blk_start[:, None], axis=1), N_EXPERTS - 1)
    nused = (pend[-1:] // MOE_BLK).astype(jnp.int32)
    tail = jnp.stack([pstart + counts, padded - counts], axis=1).reshape(-1).astype(jnp.int32)
    ids = jnp.where(counts > 0, jnp.arange(N_EXPERTS), N_EXPERTS)
    after = jnp.concatenate([lax.cummin(ids, reverse=True)[1:], jnp.full((1,), N_EXPERTS)])
    next_e = jnp.where(after < N_EXPERTS, after, -1).astype(jnp.int32)
    return pos, blk_e.astype(jnp.int32), nused, tail, next_e


def _rope_tables(S, rot_dim):
    rows = S // GRID_W
    row = jnp.repeat(jnp.arange(rows, dtype=F32), GRID_W)
    col = jnp.tile(jnp.arange(GRID_W, dtype=F32), rows)
    axis_dim = rot_dim // 2
    inv_freq = ROPE_THETA ** (-jnp.arange(0, axis_dim, 2, dtype=F32) / axis_dim)
    ang_r = row[:, None] * inv_freq
    ang_c = col[:, None] * inv_freq
    z = jnp.zeros_like(ang_r)
    cos = jnp.concatenate([jnp.cos(ang_r)] * 2 + [jnp.cos(ang_c)] * 2, axis=1)
    sin_lo = jnp.concatenate([-jnp.sin(ang_r), z, -jnp.sin(ang_c), z], axis=1)
    sin_hi = jnp.concatenate([z, jnp.sin(ang_r), z, jnp.sin(ang_c)], axis=1)
    reps = 128 // rot_dim
    return tuple(jnp.tile(t, (1, reps)) for t in (cos, sin_lo, sin_hi))


def _prep_w_in(w):
    widths = (768, 256, 256, B_Q_LORA, B_KV_LORA, B_QK_ROPE, 512, 512, 512, 512, 4 * C_HEADS)
    offs = np.cumsum(widths)[:-1].tolist()
    a_q, a_k, a_v, b_cq, b_ckv, b_kr, c_q, c_k, c_v, c_o, c_g = jnp.split(w, offs, axis=-1)
    pad = jnp.zeros(w.shape[:-1] + (128 - B_QK_ROPE - 4 * C_HEADS,), w.dtype)
    out = jnp.concatenate([a_q, a_k, a_v, b_cq, b_ckv, b_kr, c_g, pad, c_v, c_o, c_q, c_k], axis=-1)
    assert out.shape[-1] == PROJ_PAD
    return out.astype(BF16)


def _split_hi_lo(w):
    hi = w.astype(BF16)
    lo = (w - hi.astype(F32)).astype(BF16)
    return jnp.concatenate([hi, lo], axis=1)


def _heads_split(w, n_heads, first):
    K = w.shape[0]
    w3 = w.reshape(K, n_heads, -1)
    return jnp.concatenate([w3[:, :, :first].reshape(K, -1), w3[:, :, first:].reshape(K, -1)],
                           axis=1).astype(BF16)


def kernel(x, c, ada_w, ada_b, pre_mix_gain, w_in, a_q_gain, a_k_gain, b_cq_gain, b_ckv_gain, w_uq, w_ukv, c_conv_w, c_conv_b, c_gate_b, head_out_gain, w_out, post_mix_gain, pre_ffn_gain, router_w, router_bias, exp_w_gate, exp_w_up, exp_w_down, sh_w_gate, sh_w_up, sh_w_down, post_ffn_gain):
    B, S, D = x.shape
    T = B * S
    L = ada_w.shape[0]
    n_blocks = T * TOP_K // MOE_BLK + N_EXPERTS
    row2 = lambda v: v.reshape(1, -1)

    tabs_a = _rope_tables(S, HEAD_DIM)
    tabs_b = _rope_tables(S, B_QK_ROPE)
    c_pad = jnp.zeros((8, D), F32).at[:B].set(c)
    mods = _modulation(c_pad, ada_w, ada_b)[:, :B].reshape(L, B, N_MOD, D)

    x2 = x.reshape(T, D)
    for l in range(L):
        mod = mods[l]
        proj = _in_proj(x2, row2(pre_mix_gain[l]), mod, _prep_w_in(w_in[l]), S)
        proj3 = proj.reshape(B, S, PROJ_PAD)

        qa, ka, va = _gqa_prep(proj, tabs_a, row2(a_q_gain[l]), row2(a_k_gain[l]), S)
        ya = _attention(qa.reshape(B, S, -1), ka.reshape(B, S, -1), va.reshape(B, S, -1),
                        B=B, S=S, Hk=A_KV_HEADS, G=A_HEADS // A_KV_HEADS, dq=HEAD_DIM, dv=HEAD_DIM,
                        tq=256, tk=512)

        qb, kb, vb = _mla_prep(proj, tabs_b, row2(b_cq_gain[l]), row2(b_ckv_gain[l]),
                               _heads_split(w_uq[l], B_HEADS, B_QK_NOPE),
                               _heads_split(w_ukv[l], B_HEADS, B_QK_NOPE), S)
        yb = _attention(qb.reshape(B, S, -1), kb.reshape(B, S, -1), vb.reshape(B, S, -1),
                        B=B, S=S, Hk=B_HEADS, G=1, dq=256, dv=B_V_DIM, tq=512, tk=512)

        qk3 = _conv_silu(proj3, c_conv_w[l], row2(c_conv_b[l]))
        gate_b_row = jnp.zeros((1, 128), F32).at[0, MISC_GATE_LANE:MISC_GATE_LANE + 4 * C_HEADS].set(c_gate_b[l])
        hf, hr = _mlstm(qk3, proj3, gate_b_row)

        x2, h2, h2p, logits = _merge(ya.reshape(T, -1), yb.reshape(T, -1), hf.reshape(T, -1),
                                hr.reshape(T, -1), proj, x2, mod,
                                row2(head_out_gain[l]), w_out[l].astype(BF16), row2(post_mix_gain[l]),
                                row2(pre_ffn_gain[l]), _split_hi_lo(router_w[l]), S)

        idx8, gate8, mask = _router(logits, row2(router_bias[l]))
        pos, blk_e, nused, tail, next_e = _route_plan(idx8, mask, n_blocks)
        xs = _dispatch(h2p, pos.reshape(T // DISPATCH_TM, 1, DISPATCH_TM * TOP_K), tail,
                       n_blocks * MOE_BLK)
        ys = _experts(xs, blk_e, nused, next_e, exp_w_gate, exp_w_up, exp_w_down, l)
        x2 = _combine(pos.reshape(T // COMBINE_TM, 1, COMBINE_TM * TOP_K), h2, gate8, x2, mod, sh_w_gate[l].astype(BF16), sh_w_up[l].astype(BF16),
                      sh_w_down[l].astype(BF16), row2(post_ffn_gain[l]), ys, S)
    return x2.reshape(B, S, D)
```

```python
import functools

import numpy as np
import jax
import jax.numpy as jnp
from jax import lax
from jax.experimental import pallas as pl
from jax.experimental.pallas import tpu as pltpu

D_MODEL = 2048
GRID_W = 64
ROPE_THETA = 10000.0
EPS = 1e-6
HEAD_DIM = 128
A_HEADS, A_KV_HEADS = 6, 2
B_HEADS, B_Q_LORA, B_KV_LORA = 6, 384, 256
B_QK_NOPE, B_QK_ROPE, B_V_DIM = 128, 64, 128
C_HEADS, C_DQK, C_DV = 4, 128, 128
CONV_W = 5
MLSTM_CHUNK = 64
N_EXPERTS, TOP_K, EXPERT_FF, SHARED_FF = 64, 6, 512, 512
ROUTED_SCALE = 2.5
N_MOD = 6

COL_AQ, COL_AK, COL_AV = 0, 768, 1024
COL_BC = 1280
COL_MISC = 1920
COL_CV, COL_CO = 2048, 2560
COL_CQK = 3072
PROJ_PAD = 4096
MISC_GATE_LANE = 64
PC_CV, PC_CO, PC_CQK = 0, COL_CO - COL_CV, COL_CQK - COL_CV

MOE_BLK = 256
DISPATCH_TM = 512
COMBINE_TM = 256
MLSTM_BB = 2
VMEM_LIMIT = 56 * 1024 * 1024

BF16 = jnp.bfloat16
F32 = jnp.float32
LOG2_E = 1.4426950408889634


def _cparams(sem):
    return pltpu.CompilerParams(dimension_semantics=("arbitrary",) * len(sem),
                                vmem_limit_bytes=VMEM_LIMIT)


def _rms(x, gain):
    return x * lax.rsqrt(jnp.mean(x * x, axis=-1, keepdims=True) + EPS) * gain


def _dot(a, b):
    return jnp.dot(a, b, preferred_element_type=F32)


def _dot_t(a, b):
    return lax.dot_general(a, b, (((1,), (1,)), ((), ())), preferred_element_type=F32)


def _mod_kernel(c_ref, w_ref, b_ref, o_ref):
    c = c_ref[...]
    c_act = (c * jax.nn.sigmoid(c)).astype(BF16)
    o_ref[...] = _dot(c_act, w_ref[...].astype(BF16)) + b_ref[...]


def _modulation(c_pad, ada_w, ada_b):
    L, D, N = ada_w.shape
    M = c_pad.shape[0]
    tn = 1536
    return pl.pallas_call(
        _mod_kernel,
        grid=(L, N // tn),
        in_specs=[pl.BlockSpec((M, D), lambda l, j: (0, 0)),
                  pl.BlockSpec((None, D, tn), lambda l, j: (l, 0, j)),
                  pl.BlockSpec((None, 1, tn), lambda l, j: (l, 0, j))],
        out_specs=pl.BlockSpec((None, M, tn), lambda l, j: (l, 0, j)),
        out_shape=jax.ShapeDtypeStruct((L, M, N), F32),
        compiler_params=_cparams(("parallel", "parallel")),
        name="adaln_mod",
    )(c_pad, ada_w, ada_b.reshape(L, 1, N))


def _in_proj_kernel(x_ref, gain_ref, mod_ref, w_ref,
                    cosa_ref, sloa_ref, shia_ref, cosb_ref, slob_ref, shib_ref,
                    qg_ref, kg_ref, cqg_ref, ckvg_ref, wuq_ref, wukv_ref,
                    qa_ref, ka_ref, va_ref, qb_ref, kb_ref, vb_ref, pc_ref, misc_ref):
    h = _rms(x_ref[...], gain_ref[...])
    h = (h * (1.0 + mod_ref[1:2, :]) + mod_ref[0:1, :]).astype(BF16)
    pa = _dot(h, w_ref[:, COL_AQ:COL_BC])
    _gqa_prep(pa, (cosa_ref[...], sloa_ref[...], shia_ref[...]), qg_ref[...], kg_ref[...],
              qa_ref, ka_ref, va_ref)
    pb = _dot(h, w_ref[:, COL_BC:COL_CV])
    misc = pb[:, COL_MISC - COL_BC:]
    misc_ref[...] = misc
    _mla_prep(pb[:, :COL_MISC - COL_BC], misc, (cosb_ref[...], slob_ref[...], shib_ref[...]),
              cqg_ref[...], ckvg_ref[...], wuq_ref, wukv_ref, qb_ref, kb_ref, vb_ref)
    tn = 1024
    for j in range((PROJ_PAD - COL_CV) // tn):
        pc_ref[:, j * tn:(j + 1) * tn] = _dot(h, w_ref[:, COL_CV + j * tn: COL_CV + (j + 1) * tn])


def _in_proj(x2, gain, mod, w_in_p, tabs_a, tabs_b, q_gain, k_gain, cq_gain, ckv_gain, w_uq_p, w_ukv_p, S):
    T, D = x2.shape
    N = w_in_p.shape[1]
    tm = 256
    nb = S // tm
    row = lambda w: pl.BlockSpec((tm, w), lambda i: (i, 0))
    full = lambda a: pl.BlockSpec(a.shape, lambda i: (0,) * a.ndim)
    tab = pl.BlockSpec((tm, 128), lambda i: (i % nb, 0))
    qw, kw = A_HEADS * HEAD_DIM, A_KV_HEADS * HEAD_DIM
    bqw, bvw, cw = B_HEADS * 256, B_HEADS * B_V_DIM, PROJ_PAD - COL_CV
    widths = (qw, kw, kw, bqw, bqw, bvw)
    return pl.pallas_call(
        _in_proj_kernel,
        grid=(T // tm,),
        in_specs=[row(D), full(gain),
                  pl.BlockSpec((None, N_MOD, D), lambda i: (i // nb, 0, 0)),
                  pl.BlockSpec(w_in_p.shape, lambda i: (0, 0), pipeline_mode=pl.Buffered(1)),
                  tab, tab, tab, tab, tab, tab,
                  full(q_gain), full(k_gain), full(cq_gain), full(ckv_gain), full(w_uq_p), full(w_ukv_p)],
        out_specs=[row(w) for w in widths] + [row(cw), row(128)],
        out_shape=[jax.ShapeDtypeStruct((T, w), BF16) for w in widths]
                  + [jax.ShapeDtypeStruct((T, cw), F32), jax.ShapeDtypeStruct((T, 128), F32)],
        compiler_params=_cparams(("arbitrary",)),
        name="in_proj",
    )(x2, gain, mod, w_in_p, *tabs_a, *tabs_b, q_gain, k_gain, cq_gain, ckv_gain, w_uq_p, w_ukv_p)


def _rope128(y, cos, sin_lo, sin_hi):
    return y * cos + pltpu.roll(y, 96, 1) * sin_lo + pltpu.roll(y, 32, 1) * sin_hi


def _gqa_prep(pa, tabs, q_gain, k_gain, qo_ref, ko_ref, vo_ref):
    cos, slo, shi = tabs
    scale = HEAD_DIM ** -0.5 * LOG2_E
    for h in range(A_HEADS):
        sl = slice(h * HEAD_DIM, (h + 1) * HEAD_DIM)
        y = _rope128(_rms(pa[:, COL_AQ + h * HEAD_DIM: COL_AQ + (h + 1) * HEAD_DIM], q_gain), cos, slo, shi)
        qo_ref[:, sl] = (y * scale).astype(BF16)
    for h in range(A_KV_HEADS):
        sl = slice(h * HEAD_DIM, (h + 1) * HEAD_DIM)
        y = _rope128(_rms(pa[:, COL_AK + h * HEAD_DIM: COL_AK + (h + 1) * HEAD_DIM], k_gain), cos, slo, shi)
        ko_ref[:, sl] = y.astype(BF16)
    vo_ref[...] = pa[:, COL_AV:COL_BC].astype(BF16)


def _rope64x2(y, cos, sin_lo, sin_hi):
    return y * cos + pltpu.roll(y, 112, 1) * sin_lo + pltpu.roll(y, 16, 1) * sin_hi


def _mla_prep(bc, misc, tabs, cq_gain, ckv_gain, wuq_ref, wukv_ref, qo_ref, ko_ref, vo_ref):
    cos, slo, shi = tabs
    lane = lax.broadcasted_iota(jnp.int32, cos.shape, 1)
    first = lane < B_QK_ROPE
    scale = (B_QK_NOPE + B_QK_ROPE) ** -0.5 * LOG2_E
    nope_w = B_HEADS * B_QK_NOPE

    cq = _rms(bc[:, :B_Q_LORA], cq_gain).astype(BF16)
    q = _dot(cq, wuq_ref[...]) * scale
    for p in range(B_HEADS // 2):
        pair = _rope64x2(q[:, nope_w + p * 128: nope_w + (p + 1) * 128], cos, slo, shi)
        for half in range(2):
            h = 2 * p + half
            qo_ref[:, h * 256: h * 256 + 128] = q[:, h * 128:(h + 1) * 128].astype(BF16)
            keep = first if half == 0 else jnp.logical_not(first)
            qo_ref[:, h * 256 + 128: (h + 1) * 256] = jnp.where(keep, pair, 0.0).astype(BF16)

    kr = jnp.where(first, misc, 0.0)
    kr = kr + pltpu.roll(kr, B_QK_ROPE, 1)
    kr = _rope64x2(kr, cos, slo, shi).astype(BF16)
    ckv = _rms(bc[:, B_Q_LORA:], ckv_gain).astype(BF16)
    kv = _dot(ckv, wukv_ref[...])
    for h in range(B_HEADS):
        ko_ref[:, h * 256: h * 256 + 128] = kv[:, h * 128:(h + 1) * 128].astype(BF16)
        ko_ref[:, h * 256 + 128: (h + 1) * 256] = kr
    vo_ref[...] = kv[:, nope_w:].astype(BF16)


def _attn_kernel(q_ref, k_ref, v_ref, o_ref, m_scr, l_scr, acc_scr, *, G, dq, dv, tk, sub):
    tq = q_ref.shape[0]
    S = k_ref.shape[0]
    groups = [(g, r) for g in range(G) for r in range(tq // sub)]
    m_scr[...] = jnp.full(m_scr.shape, -jnp.inf, F32)
    l_scr[...] = jnp.zeros(l_scr.shape, F32)
    acc_scr[...] = jnp.zeros(acc_scr.shape, F32)

    stages = [(c, n) for c in range(S // tk) for n in range(len(groups))]

    def scores(stage):
        c, n = stage
        g, r = groups[n]
        return _dot_t(q_ref[r * sub:(r + 1) * sub, g * dq:(g + 1) * dq], k_ref[c * tk:(c + 1) * tk, :])

    s_next = scores(stages[0])
    for i, (c, n) in enumerate(stages):
        s = s_next
        if i + 1 < len(stages):
            s_next = scores(stages[i + 1])
        rows = slice(n * sub, (n + 1) * sub)
        blocks = [s[:, j * 128:(j + 1) * 128] for j in range(tk // 128)]
        m_blk = functools.reduce(jnp.maximum, blocks)
        m_old = m_scr[rows, :]
        m_new = jnp.maximum(m_old, jnp.max(m_blk, axis=-1, keepdims=True))
        alpha = jnp.exp2(m_old - m_new)
        p_blocks = [jnp.exp2(b - m_new) for b in blocks]
        l_scr[rows, :] = alpha * l_scr[rows, :] + functools.reduce(jnp.add, p_blocks)
        p = jnp.concatenate([b.astype(BF16) for b in p_blocks], axis=1)
        acc_scr[rows, :] = alpha * acc_scr[rows, :] + _dot(p, v_ref[c * tk:(c + 1) * tk, :])
        m_scr[rows, :] = m_new

    for n, (g, r) in enumerate(groups):
        rows = slice(n * sub, (n + 1) * sub)
        l = jnp.sum(l_scr[rows, :], axis=-1, keepdims=True)
        o_ref[r * sub:(r + 1) * sub, g * dv:(g + 1) * dv] = acc_scr[rows, :] / l


def _attention(q, k, v, *, B, S, Hk, G, dq, dv, tq, tk):
    assert dv == 128, "row statistics are kept 128 lanes wide to match the value width"
    kern = functools.partial(_attn_kernel, G=G, dq=dq, dv=dv, tk=tk, sub=256)
    M = G * tq
    return pl.pallas_call(
        kern,
        grid=(B, Hk, S // tq),
        in_specs=[pl.BlockSpec((None, tq, G * dq), lambda b, h, i: (b, i, h)),
                  pl.BlockSpec((None, S, dq), lambda b, h, i: (b, 0, h)),
                  pl.BlockSpec((None, S, dv), lambda b, h, i: (b, 0, h))],
        out_specs=pl.BlockSpec((None, tq, G * dv), lambda b, h, i: (b, i, h)),
        out_shape=jax.ShapeDtypeStruct((B, S, Hk * G * dv), F32),
        scratch_shapes=[pltpu.VMEM((M, 128), F32), pltpu.VMEM((M, 128), F32), pltpu.VMEM((M, dv), F32)],
        compiler_params=_cparams(("parallel", "parallel", "arbitrary")),
        name=f"attn_g{G}_d{dq}",
    )(q, k, v)


def _conv_kernel(x_ref, w_ref, b_ref, o_ref, *, k_scale_from):
    x = x_ref[...]
    S = x.shape[0]
    row = lax.broadcasted_iota(jnp.int32, x.shape, 0)
    pad = CONV_W // 2
    y = x * w_ref[pad:pad + 1, :] + b_ref[...]
    for j in range(CONV_W):
        d = j - pad
        if d == 0:
            continue
        shifted = pltpu.roll(x, (-d) % S, 0)
        valid = (row + d >= 0) & (row + d < S)
        y = y + jnp.where(valid, shifted, 0.0) * w_ref[j:j + 1, :]
    y = y * jax.nn.sigmoid(y)
    scale = jnp.where(pl.program_id(1) >= k_scale_from, C_DQK ** -0.5, 1.0)
    o_ref[...] = y * scale


def _conv_silu(pc3, conv_w, conv_b):
    B, S, _ = pc3.shape
    W = 2 * C_HEADS * C_DQK
    tc = 128
    kern = functools.partial(_conv_kernel, k_scale_from=(C_HEADS * C_DQK) // tc)
    return pl.pallas_call(
        kern,
        grid=(B, W // tc),
        in_specs=[pl.BlockSpec((None, S, tc), lambda b, j: (b, 0, PC_CQK // tc + j)),
                  pl.BlockSpec((CONV_W, tc), lambda b, j: (0, j)),
                  pl.BlockSpec((1, tc), lambda b, j: (0, j))],
        out_specs=pl.BlockSpec((None, S, tc), lambda b, j: (b, 0, j)),
        out_shape=jax.ShapeDtypeStruct((B, S, W), F32),
        compiler_params=_cparams(("parallel", "parallel")),
        name="conv_silu",
    )(pc3, conv_w, conv_b)


def _log_sigmoid(x):
    return jnp.minimum(x, 0.0) - jnp.log1p(jnp.exp(-jnp.abs(x)))


def _mlstm_kernel(qf_ref, kf_ref, vf_ref, gf_ref, qr_ref, kr_ref, vr_ref, gr_ref, gb_ref,
                  of_ref, or_ref, C_scr, n_scr, m_scr, *, BB):
    L = MLSTM_CHUNK

    @pl.when(pl.program_id(1) == 0)
    def _():
        C_scr[...] = jnp.zeros(C_scr.shape, F32)
        n_scr[...] = jnp.zeros(n_scr.shape, F32)
        m_scr[...] = jnp.zeros(m_scr.shape, F32)

    glane = lax.broadcasted_iota(jnp.int32, (L, 128), 1)
    jj = lax.broadcasted_iota(jnp.int32, (L, L), 0)
    ss = lax.broadcasted_iota(jnp.int32, (L, L), 1)
    eye = jj == ss
    neg_inf = jnp.float32(-jnp.inf)
    dirs = ((qf_ref, kf_ref, vf_ref, gf_ref, of_ref, ss <= jj, jj <= ss),
            (qr_ref, kr_ref, vr_ref, gr_ref, or_ref, ss >= jj, jj >= ss))

    chains = []
    for d, (q_ref, k_ref, v_ref, g_ref, o_ref, seen, seen_t) in enumerate(dirs):
        for bb in range(BB):
            gates = g_ref[bb] + gb_ref[...]
            lsig = _log_sigmoid(gates)
            for h in range(C_HEADS):
                c = dict(st=(d * BB + bb) * C_HEADS + h, sl=slice(h * C_DQK, (h + 1) * C_DQK), bb=bb,
                         q_ref=q_ref, k_ref=k_ref, v_ref=v_ref, o_ref=o_ref, seen=seen, seen_t=seen_t)
                i_lane = MISC_GATE_LANE + d * (2 * C_HEADS) + h
                c["i_col"] = jnp.sum(jnp.where(glane == i_lane, gates, 0.0), axis=1, keepdims=True)
                c["f_col"] = jnp.sum(jnp.where(glane == i_lane + C_HEADS, lsig, 0.0), axis=1, keepdims=True)
                chains.append(c)
    for c in chains:
        c["f_row"] = jnp.sum(jnp.where(eye, c["f_col"], 0.0), axis=0, keepdims=True)
        c["i_row"] = jnp.sum(jnp.where(eye, c["i_col"], 0.0), axis=0, keepdims=True)
        c["b_row"] = jnp.sum(jnp.where(c["seen_t"], c["f_col"], 0.0), axis=0, keepdims=True)
        c["g_tot"] = jnp.sum(c["f_col"], axis=0, keepdims=True)
    for c in chains:
        c["b_col"] = jnp.sum(jnp.where(c["seen"], c["f_row"], 0.0), axis=1, keepdims=True)
    for c in chains:
        c["m_prev"] = m_scr[c["st"]][:, 0:1]
        c["dmat"] = jnp.where(c["seen"], c["b_col"] - c["b_row"] + c["i_row"], neg_inf)
        c["m_inter"] = c["b_col"] + c["m_prev"]
        c["a_col"] = c["g_tot"] - c["b_col"] + c["i_col"]
    for c in chains:
        c["m_j"] = jnp.maximum(c["m_inter"], jnp.max(c["dmat"], axis=1, keepdims=True))
        c["m_new"] = jnp.maximum(c["g_tot"] + c["m_prev"], jnp.max(c["a_col"], axis=0, keepdims=True))
    for c in chains:
        c["q"] = c["q_ref"][c["bb"], :, c["sl"]]
        c["k"] = c["k_ref"][c["bb"], :, c["sl"]]
        c["qb"], c["kb"] = c["q"].astype(BF16), c["k"].astype(BF16)
        c["vb"] = c["v_ref"][c["bb"], :, c["sl"]].astype(BF16)
        c["qk"] = _dot_t(c["qb"], c["kb"])
    for c in chains:
        c["qC"] = _dot(c["qb"], C_scr[c["st"]].astype(BF16))
    for c in chains:
        c["s"] = c["qk"] * jnp.exp(c["dmat"] - c["m_j"])
        c["inter"] = jnp.exp(c["m_inter"] - c["m_j"])
        c["wk"] = jnp.exp(c["a_col"] - c["m_new"]) * c["k"]
        c["decay"] = jnp.exp(c["g_tot"] + c["m_prev"] - c["m_new"])
    for c in chains:
        c["sv"] = _dot(c["s"].astype(BF16), c["vb"])
    for c in chains:
        c["upd"] = lax.dot_general(c["wk"].astype(BF16), c["vb"], (((0,), (0,)), ((), ())),
                                   preferred_element_type=F32)
    for c in chains:
        n_prev = n_scr[c["st"]]
        den = (jnp.sum(c["s"], axis=1, keepdims=True)
               + c["inter"] * jnp.sum(c["q"] * n_prev, axis=1, keepdims=True))
        num = c["sv"] + c["inter"] * c["qC"]
        c["o_ref"][c["bb"], :, c["sl"]] = num / jnp.maximum(jnp.abs(den), jnp.exp(-c["m_j"]))
        n_scr[c["st"]] = c["decay"] * n_prev + jnp.sum(c["wk"], axis=0, keepdims=True)
    for c in chains:
        C_scr[c["st"]] = c["decay"] * C_scr[c["st"]] + c["upd"]
        m_scr[c["st"]] = jnp.broadcast_to(c["m_new"], m_scr.shape[1:])


def _mlstm(qk3, pc3, misc3, gate_b_row, BB=MLSTM_BB):
    B, S, _ = pc3.shape
    L = MLSTM_CHUNK
    nc = S // L
    W = C_HEADS * C_DQK
    n_chain = 2 * BB * C_HEADS
    fwd = lambda col: (lambda b, c: (b, c, col))
    rev = lambda col: (lambda b, c: (b, nc - 1 - c, col))
    blk = lambda w, imap: pl.BlockSpec((BB, L, w), imap)
    out = jax.ShapeDtypeStruct((B, S, W), F32)
    return pl.pallas_call(
        functools.partial(_mlstm_kernel, BB=BB),
        grid=(B // BB, nc),
        in_specs=[blk(W, fwd(0)), blk(W, fwd(1)), blk(W, fwd(PC_CV // W)), blk(128, fwd(0)),
                  blk(W, rev(0)), blk(W, rev(1)), blk(W, rev(PC_CV // W)), blk(128, rev(0)),
                  pl.BlockSpec((1, 128), lambda b, c: (0, 0))],
        out_specs=[blk(W, fwd(0)), blk(W, rev(0))],
        out_shape=[out, out],
        scratch_shapes=[pltpu.VMEM((n_chain, C_DQK, C_DV), F32),
                        pltpu.VMEM((n_chain, 1, C_DQK), F32),
                        pltpu.VMEM((n_chain, 1, 128), F32)],
        compiler_params=_cparams(("arbitrary", "arbitrary")),
        name="mlstm",
    )(qk3, qk3, pc3, misc3, qk3, qk3, pc3, misc3, gate_b_row)


ROW_TILE = 8
HALF_D = D_MODEL // 2
U32 = jnp.uint32
HI_MASK = 0xFFFF0000


def _pack_rows(y, o_ref):
    n = y.shape[0]
    for s in range(ROW_TILE):
        lo = y[:, s * 128:(s + 1) * 128].astype(BF16).astype(F32)
        hi = y[:, HALF_D + s * 128: HALF_D + (s + 1) * 128].astype(BF16).astype(F32)
        w = (lax.bitcast_convert_type(lo, U32) >> 16) | (lax.bitcast_convert_type(hi, U32) & U32(HI_MASK))
        o_ref[pl.ds(s, n, stride=ROW_TILE), :] = w


def _unpack_rows(ref, n):
    lo, hi = [], []
    for s in range(ROW_TILE):
        w = ref[pl.ds(s, n, stride=ROW_TILE), :]
        lo.append(lax.bitcast_convert_type(w << 16, F32))
        hi.append(lax.bitcast_convert_type(w & U32(HI_MASK), F32))
    return lo + hi


def _merge_kernel(ya_ref, yb_ref, hf_ref, hr_ref, co_ref, x_ref, mod_ref, hg_ref, wout_ref, pmg_ref,
                  pfg_ref, rw_ref, xo_ref, h2_ref, h2p_ref, lg_ref, y_scr):
    attn_heads = A_HEADS + B_HEADS
    for h in range(attn_heads + C_HEADS):
        sl = slice(h * HEAD_DIM, (h + 1) * HEAD_DIM)
        g = hg_ref[:, sl]
        if h < A_HEADS:
            y = _rms(ya_ref[:, sl], g)
        elif h < attn_heads:
            y = _rms(yb_ref[:, (h - A_HEADS) * HEAD_DIM:(h - A_HEADS + 1) * HEAD_DIM], g)
        else:
            cs = slice((h - attn_heads) * HEAD_DIM, (h - attn_heads + 1) * HEAD_DIM)
            y = _rms(hf_ref[:, cs] + hr_ref[:, cs], g) * jax.nn.sigmoid(co_ref[:, cs])
        y_scr[:, sl] = y.astype(BF16)
    y = _dot(y_scr[...], wout_ref[...])
    x_new = x_ref[...] + mod_ref[2:3, :] * _rms(y, pmg_ref[...])
    xo_ref[...] = x_new
    h2 = _rms(x_new, pfg_ref[...]) * (1.0 + mod_ref[4:5, :]) + mod_ref[3:4, :]
    h2_ref[...] = h2.astype(BF16)
    _pack_rows(h2, h2p_ref)
    h_hi = h2.astype(BF16)
    h_lo = (h2 - h_hi.astype(F32)).astype(BF16)
    a = _dot(h_hi, rw_ref[...])
    b = _dot(h_lo, rw_ref[...])
    E = N_EXPERTS
    lg_ref[...] = a[:, :E] + (a[:, E:] + b[:, :E]) + b[:, E:]


def _merge(ya, yb, hf, hr, pc, x2, mod, head_gain, w_out_b, pm_gain, pf_gain, router_w, S):
    T, D = x2.shape
    tm = 256
    nb = S // tm
    aw, cw = A_HEADS * HEAD_DIM, C_HEADS * C_DV
    row = lambda w: pl.BlockSpec((tm, w), lambda i: (i, 0))
    full = lambda a: pl.BlockSpec(a.shape, lambda i: (0,) * a.ndim)
    return pl.pallas_call(
        _merge_kernel,
        grid=(T // tm,),
        in_specs=[row(aw), row(aw), row(cw), row(cw),
                  pl.BlockSpec((tm, cw), lambda i: (i, PC_CO // cw)),
                  row(D),
                  pl.BlockSpec((None, N_MOD, D), lambda i: (i // nb, 0, 0)),
                  full(head_gain), full(w_out_b), full(pm_gain), full(pf_gain), full(router_w)],
        out_specs=[row(D), row(D), pl.BlockSpec((tm * ROW_TILE, 128), lambda i: (i, 0)),
                   row(N_EXPERTS)],
        out_shape=[jax.ShapeDtypeStruct((T, D), F32),
                   jax.ShapeDtypeStruct((T, D), BF16),
                   jax.ShapeDtypeStruct((T * ROW_TILE, 128), U32),
                   jax.ShapeDtypeStruct((T, N_EXPERTS), F32)],
        scratch_shapes=[pltpu.VMEM((tm, D), BF16)],
        compiler_params=_cparams(("parallel",)),
        name="merge_out_proj",
    )(ya, yb, hf, hr, pc, x2, mod, head_gain, w_out_b, pm_gain, pf_gain, router_w)


def _router_kernel(lg_ref, bias_ref, idx_ref, gate_ref, mask_ref):
    scores = jax.nn.sigmoid(lg_ref[...])
    sel = scores + bias_ref[...]
    lane = lax.broadcasted_iota(jnp.int32, scores.shape, 1).astype(F32)
    col = lax.broadcasted_iota(jnp.int32, idx_ref.shape, 1)
    idx = jnp.zeros(idx_ref.shape, F32)
    gate = jnp.zeros(gate_ref.shape, F32)
    mask = jnp.zeros(scores.shape, F32)
    for kk in range(TOP_K):
        mx = jnp.max(sel, axis=1, keepdims=True)
        am = jnp.min(jnp.where(sel == mx, lane, float(N_EXPERTS)), axis=1, keepdims=True)
        hit = lane == am
        sc = jnp.sum(jnp.where(hit, scores, 0.0), axis=1, keepdims=True)
        idx = jnp.where(col == kk, am, idx)
        gate = jnp.where(col == kk, sc, gate)
        mask = jnp.where(hit, 1.0, mask)
        sel = jnp.where(hit, -jnp.inf, sel)
    gate = gate / jnp.sum(gate, axis=1, keepdims=True) * ROUTED_SCALE
    idx_ref[...] = idx.astype(jnp.int32)
    gate_ref[...] = gate
    mask_ref[...] = mask.astype(jnp.int32)


def _router(logits, bias):
    T, E = logits.shape
    tm = 1024
    row = lambda w: pl.BlockSpec((tm, w), lambda i: (i, 0))
    return pl.pallas_call(
        _router_kernel,
        grid=(T // tm,),
        in_specs=[row(E), pl.BlockSpec((1, E), lambda i: (0, 0))],
        out_specs=[row(8), row(8), row(E)],
        out_shape=[jax.ShapeDtypeStruct((T, 8), jnp.int32),
                   jax.ShapeDtypeStruct((T, 8), F32),
                   jax.ShapeDtypeStruct((T, E), jnp.int32)],
        compiler_params=_cparams(("parallel",)),
        name="router_topk",
    )(logits, bias)


TAIL_BITS = MOE_BLK.bit_length() - 1


def _tail_copies(tail_ref, zero_scr, xs_ref, zsem, fn):
    for e in range(N_EXPERTS):
        start_row = tail_ref[2 * e]
        n_tail = tail_ref[2 * e + 1]
        for bit in range(TAIL_BITS):
            size = 1 << bit
            cur = start_row + (n_tail & ~(2 * size - 1))
            cur = pl.multiple_of(cur * ROW_TILE, ROW_TILE)

            @pl.when((n_tail & size) != 0)
            def _():
                fn(pltpu.make_async_copy(zero_scr.at[pl.ds(0, size * ROW_TILE), :],
                                         xs_ref.at[pl.ds(cur, size * ROW_TILE), :], zsem))


def _row_tile(ref, row):
    return ref.at[pl.ds(pl.multiple_of(row * ROW_TILE, ROW_TILE), ROW_TILE), :]


def _dispatch_kernel(tail_ref, pos_ref, h_ref, xs_ref, zero_scr, sem, zsem, *, tm):
    i = pl.program_id(0)

    def start(r, c):
        src = _row_tile(h_ref, r)
        for kk in range(TOP_K):
            pltpu.make_async_copy(src, _row_tile(xs_ref, pos_ref[0, r * TOP_K + kk]),
                                  sem).start(priority=kk % 2)
        return c

    lax.fori_loop(0, tm, start, 0)

    @pl.when(i == 0)
    def _():
        zero_scr[...] = jnp.zeros(zero_scr.shape, zero_scr.dtype)
        _tail_copies(tail_ref, zero_scr, xs_ref, zsem, lambda cp: cp.start())
        _tail_copies(tail_ref, zero_scr, xs_ref, zsem, lambda cp: cp.wait())

    for kk in range(TOP_K):
        pltpu.make_async_copy(h_ref, xs_ref.at[pl.ds(0, tm * ROW_TILE), :], sem).wait()


def _dispatch(h2p, pos3, tail_info, P):
    T = h2p.shape[0] // ROW_TILE
    tm = pos3.shape[2] // TOP_K
    kern = functools.partial(_dispatch_kernel, tm=tm)
    return pl.pallas_call(
        kern,
        grid_spec=pltpu.PrefetchScalarGridSpec(
            num_scalar_prefetch=1,
            grid=(T // tm,),
            in_specs=[pl.BlockSpec((None, 1, tm * TOP_K), lambda i, tail: (i, 0, 0),
                                   memory_space=pltpu.SMEM),
                      pl.BlockSpec((tm * ROW_TILE, 128), lambda i, tail: (i, 0))],
            out_specs=pl.BlockSpec(memory_space=pl.ANY),
            scratch_shapes=[pltpu.VMEM((MOE_BLK // 2 * ROW_TILE, 128), U32),
                            pltpu.SemaphoreType.DMA(()), pltpu.SemaphoreType.DMA(())]),
        out_shape=jax.ShapeDtypeStruct((P * ROW_TILE, 128), U32),
        compiler_params=_cparams(("arbitrary",)),
        name="moe_dispatch",
    )(tail_info, pos3, h2p)


def _expert_kernel(blk_e_ref, nused_ref, next_ref, xs_ref, wg_hbm, wu_hbm, wd_hbm, ys_ref,
                   wg_f, wu_f, wd_f, wg_b, wu_b, wd_b, sems, *, layer):
    i = pl.program_id(0)
    e = blk_e_ref[i]
    e_prev = blk_e_ref[jnp.maximum(i - 1, 0)]
    used = i < nused_ref[0]

    def fetch(ex):
        return [pltpu.make_async_copy(hbm.at[layer, ex], buf, sems.at[n])
                for n, (hbm, buf) in enumerate(((wg_hbm, wg_f), (wu_hbm, wu_f), (wd_hbm, wd_f)))]

    @pl.when(i == 0)
    def _():
        for cp in fetch(e):
            cp.start()

    @pl.when(used & ((i == 0) | (e != e_prev)))
    def _():
        for cp in fetch(e):
            cp.wait()
        wg_b[...] = wg_f[...].astype(BF16)
        wu_b[...] = wu_f[...].astype(BF16)
        wd_b[...] = wd_f[...].astype(BF16)
        e_next = next_ref[e]

        @pl.when(e_next >= 0)
        def _():
            for cp in fetch(e_next):
                cp.start()

    @pl.when(used)
    def _():
        x = jnp.concatenate([c.astype(BF16) for c in _unpack_rows(xs_ref, MOE_BLK)], axis=1)
        gte = _dot(x, wg_b[...])
        a = gte * jax.nn.sigmoid(gte) * _dot(x, wu_b[...])
        _pack_rows(_dot(a.astype(BF16), wd_b[...]), ys_ref)

    @pl.when(jnp.logical_not(used))
    def _():
        ys_ref[...] = jnp.zeros(ys_ref.shape, ys_ref.dtype)


def _experts(xs, blk_e, nused, next_e, w_gate, w_up, w_down, layer):
    P = xs.shape[0] // ROW_TILE
    D, F = w_gate.shape[-2:]
    NB = P // MOE_BLK
    last = lambda i, nu: jnp.minimum(i, nu[0] - 1)
    hbm = pl.BlockSpec(memory_space=pl.ANY)
    return pl.pallas_call(
        functools.partial(_expert_kernel, layer=layer),
        grid_spec=pltpu.PrefetchScalarGridSpec(
            num_scalar_prefetch=3,
            grid=(NB,),
            in_specs=[pl.BlockSpec((MOE_BLK * ROW_TILE, 128), lambda i, be, nu, nx: (last(i, nu), 0)),
                      hbm, hbm, hbm],
            out_specs=pl.BlockSpec((MOE_BLK * ROW_TILE, 128), lambda i, be, nu, nx: (i, 0)),
            scratch_shapes=[pltpu.VMEM((D, F), F32), pltpu.VMEM((D, F), F32), pltpu.VMEM((F, D), F32),
                            pltpu.VMEM((D, F), BF16), pltpu.VMEM((D, F), BF16),
                            pltpu.VMEM((F, D), BF16), pltpu.SemaphoreType.DMA((3,))]),
        out_shape=jax.ShapeDtypeStruct((P * ROW_TILE, 128), U32),
        compiler_params=_cparams(("arbitrary",)),
        name="moe_experts",
    )(blk_e, nused, next_e, xs, w_gate, w_up, w_down)


def _combine_kernel(pos_ref, pos_next_ref, h_ref, gate_ref, x_ref, mod_ref, wsg_ref, wsu_ref, wsd_ref,
                    pg_ref, ys_ref, xo_ref, gbuf, sems, *, tm):
    i = pl.program_id(0)
    slot = i % 2

    def gather(p_ref, s):
        def start(r, c):
            for kk in range(TOP_K):
                pltpu.make_async_copy(_row_tile(ys_ref, p_ref[0, r * TOP_K + kk]),
                                      _row_tile(gbuf.at[s, kk], r), sems.at[s]).start(priority=kk % 2)
            return c
        lax.fori_loop(0, tm, start, 0)

    @pl.when(i == 0)
    def _():
        gather(pos_ref, 0)

    for s in range(2):
        @pl.when((i + 1 < pl.num_programs(0)) & (slot == 1 - s))
        def _():
            gather(pos_next_ref, s)

    for kk in range(TOP_K):
        pltpu.make_async_copy(ys_ref.at[pl.ds(0, tm * ROW_TILE), :], gbuf.at[slot, kk],
                              sems.at[slot]).wait()
    h = h_ref[...]
    gte = _dot(h, wsg_ref[...])
    up = _dot(h, wsu_ref[...])
    RG = 64

    def routed(rg):
        rows = slice(rg * RG, (rg + 1) * RG)
        gate = gate_ref[rows, :]
        g_wide = [jnp.broadcast_to(gate[:, kk:kk + 1], (RG, 128)) for kk in range(TOP_K)]
        for s in range(ROW_TILE):
            lo = hi = None
            for kk in range(TOP_K):
                w = gbuf[slot, kk, pl.ds(rg * RG * ROW_TILE + s, RG, stride=ROW_TILE), :]
                t_lo = g_wide[kk] * lax.bitcast_convert_type(w << 16, F32)
                t_hi = g_wide[kk] * lax.bitcast_convert_type(w & U32(HI_MASK), F32)
                lo = t_lo if lo is None else lo + t_lo
                hi = t_hi if hi is None else hi + t_hi
            xo_ref[rows, s * 128:(s + 1) * 128] = lo
            xo_ref[rows, HALF_D + s * 128: HALF_D + (s + 1) * 128] = hi

    n_rg = tm // RG
    for rg in range(n_rg // 2):
        routed(rg)
    a = (gte * jax.nn.sigmoid(gte) * up).astype(BF16)
    y_sh = _dot(a, wsd_ref[...])
    for rg in range(n_rg // 2, n_rg):
        routed(rg)
    y = xo_ref[...] + y_sh
    xo_ref[...] = x_ref[...] + mod_ref[5:6, :] * _rms(y, pg_ref[...])


def _combine(pos3, h2, gate8, x2, mod, ws_gate_b, ws_up_b, ws_down_b, post_gain, ys, S):
    T, D = x2.shape
    tm = pos3.shape[2] // TOP_K
    nb = S // tm
    kern = functools.partial(_combine_kernel, tm=tm)
    row = lambda w: pl.BlockSpec((tm, w), lambda i: (i, 0))
    full = lambda a: pl.BlockSpec(a.shape, lambda i: (0,) * a.ndim)
    n_steps = T // tm
    return pl.pallas_call(
        kern,
        grid=(n_steps,),
        in_specs=[pl.BlockSpec((None, 1, tm * TOP_K), lambda i: (i, 0, 0), memory_space=pltpu.SMEM),
                  pl.BlockSpec((None, 1, tm * TOP_K), lambda i: (jnp.minimum(i + 1, n_steps - 1), 0, 0),
                               memory_space=pltpu.SMEM),
                  row(D), row(8), row(D),
                  pl.BlockSpec((None, N_MOD, D), lambda i: (i // nb, 0, 0)),
                  full(ws_gate_b), full(ws_up_b), full(ws_down_b), full(post_gain),
                  pl.BlockSpec(memory_space=pl.ANY)],
        out_specs=row(D),
        out_shape=jax.ShapeDtypeStruct((T, D), F32),
        scratch_shapes=[pltpu.VMEM((2, TOP_K, tm * ROW_TILE, 128), U32),
                        pltpu.SemaphoreType.DMA((2,))],
        compiler_params=_cparams(("arbitrary",)),
        name="moe_combine",
    )(pos3, pos3, h2, gate8, x2, mod, ws_gate_b, ws_up_b, ws_down_b, post_gain, ys)


def _route_plan(idx8, mask, n_blocks):
    idx = idx8[:, :TOP_K]
    counts = jnp.sum(mask, axis=0)
    rank = jnp.cumsum(mask, axis=0) - mask
    padded = (counts + MOE_BLK - 1) // MOE_BLK * MOE_BLK
    pend = jnp.cumsum(padded)
    pstart = pend - padded
    pos = jnp.take_along_axis(pstart[None, :] + rank, idx, axis=1).astype(jnp.int32)
    blk_start = jnp.arange(n_blocks, dtype=jnp.int32) * MOE_BLK
    blk_e = jnp.minimum(jnp.sum(pend[None, :] <= blk_start[:, None], axis=1), N_EXPERTS - 1)
    nused = (pend[-1:] // MOE_BLK).astype(jnp.int32)
    tail = jnp.stack([pstart + counts, padded - counts], axis=1).reshape(-1).astype(jnp.int32)
    ids = jnp.where(counts > 0, jnp.arange(N_EXPERTS), N_EXPERTS)
    after = jnp.concatenate([lax.cummin(ids, reverse=True)[1:], jnp.full((1,), N_EXPERTS)])
    next_e = jnp.where(after < N_EXPERTS, after, -1).astype(jnp.int32)
    return pos, blk_e.astype(jnp.int32), nused, tail, next_e


def _rope_tables(S, rot_dim):
    rows = S // GRID_W
    row = jnp.repeat(jnp.arange(rows, dtype=F32), GRID_W)
    col = jnp.tile(jnp.arange(GRID_W, dtype=F32), rows)
    axis_dim = rot_dim // 2
    inv_freq = ROPE_THETA ** (-jnp.arange(0, axis_dim, 2, dtype=F32) / axis_dim)
    ang_r = row[:, None] * inv_freq
    ang_c = col[:, None] * inv_freq
    z = jnp.zeros_like(ang_r)
    cos = jnp.concatenate([jnp.cos(ang_r)] * 2 + [jnp.cos(ang_c)] * 2, axis=1)
    sin_lo = jnp.concatenate([-jnp.sin(ang_r), z, -jnp.sin(ang_c), z], axis=1)
    sin_hi = jnp.concatenate([z, jnp.sin(ang_r), z, jnp.sin(ang_c)], axis=1)
    reps = 128 // rot_dim
    return tuple(jnp.tile(t, (1, reps)) for t in (cos, sin_lo, sin_hi))


def _prep_w_in(w):
    widths = (768, 256, 256, B_Q_LORA, B_KV_LORA, B_QK_ROPE, 512, 512, 512, 512, 4 * C_HEADS)
    offs = np.cumsum(widths)[:-1].tolist()
    a_q, a_k, a_v, b_cq, b_ckv, b_kr, c_q, c_k, c_v, c_o, c_g = jnp.split(w, offs, axis=-1)
    pad = jnp.zeros(w.shape[:-1] + (128 - B_QK_ROPE - 4 * C_HEADS,), w.dtype)
    out = jnp.concatenate([a_q, a_k, a_v, b_cq, b_ckv, b_kr, c_g, pad, c_v, c_o, c_q, c_k], axis=-1)
    assert out.shape[-1] == PROJ_PAD
    return out.astype(BF16)


def _split_hi_lo(w):
    hi = w.astype(BF16)
    lo = (w - hi.astype(F32)).astype(BF16)
    return jnp.concatenate([hi, lo], axis=1)


def _heads_split(w, n_heads, first):
    K = w.shape[0]
    w3 = w.reshape(K, n_heads, -1)
    return jnp.concatenate([w3[:, :, :first].reshape(K, -1), w3[:, :, first:].reshape(K, -1)],
                           axis=1).astype(BF16)


def kernel(x, c, ada_w, ada_b, pre_mix_gain, w_in, a_q_gain, a_k_gain, b_cq_gain, b_ckv_gain, w_uq, w_ukv, c_conv_w, c_conv_b, c_gate_b, head_out_gain, w_out, post_mix_gain, pre_ffn_gain, router_w, router_bias, exp_w_gate, exp_w_up, exp_w_down, sh_w_gate, sh_w_up, sh_w_down, post_ffn_gain):
    B, S, D = x.shape
    T = B * S
    L = ada_w.shape[0]
    n_blocks = T * TOP_K // MOE_BLK + N_EXPERTS
    row2 = lambda v: v.reshape(1, -1)

    tabs_a = _rope_tables(S, HEAD_DIM)
    tabs_b = _rope_tables(S, B_QK_ROPE)
    c_pad = jnp.zeros((8, D), F32).at[:B].set(c)
    mods = _modulation(c_pad, ada_w, ada_b)[:, :B].reshape(L, B, N_MOD, D)

    x2 = x.reshape(T, D)
    for l in range(L):
        mod = mods[l]
        qa, ka, va, qb, kb, vb, pc, misc = _in_proj(
            x2, row2(pre_mix_gain[l]), mod, _prep_w_in(w_in[l]), tabs_a, tabs_b,
            row2(a_q_gain[l]), row2(a_k_gain[l]), row2(b_cq_gain[l]), row2(b_ckv_gain[l]),
            _heads_split(w_uq[l], B_HEADS, B_QK_NOPE), _heads_split(w_ukv[l], B_HEADS, B_QK_NOPE), S)
        pc3 = pc.reshape(B, S, -1)

        ya = _attention(qa.reshape(B, S, -1), ka.reshape(B, S, -1), va.reshape(B, S, -1),
                        B=B, S=S, Hk=A_KV_HEADS, G=A_HEADS // A_KV_HEADS, dq=HEAD_DIM, dv=HEAD_DIM,
                        tq=min(512, S), tk=512)

        yb = _attention(qb.reshape(B, S, -1), kb.reshape(B, S, -1), vb.reshape(B, S, -1),
                        B=B, S=S, Hk=B_HEADS, G=1, dq=256, dv=B_V_DIM, tq=min(1024, S), tk=512)

        qk3 = _conv_silu(pc3, c_conv_w[l], row2(c_conv_b[l]))
        gate_b_row = jnp.zeros((1, 128), F32).at[0, MISC_GATE_LANE:MISC_GATE_LANE + 4 * C_HEADS].set(c_gate_b[l])
        hf, hr = _mlstm(qk3, pc3, misc.reshape(B, S, 128), gate_b_row)

        x2, h2, h2p, logits = _merge(ya.reshape(T, -1), yb.reshape(T, -1), hf.reshape(T, -1),
                                hr.reshape(T, -1), pc, x2, mod,
                                row2(head_out_gain[l]), w_out[l].astype(BF16), row2(post_mix_gain[l]),
                                row2(pre_ffn_gain[l]), _split_hi_lo(router_w[l]), S)

        idx8, gate8, mask = _router(logits, row2(router_bias[l]))
        pos, blk_e, nused, tail, next_e = _route_plan(idx8, mask, n_blocks)
        xs = _dispatch(h2p, pos.reshape(T // DISPATCH_TM, 1, DISPATCH_TM * TOP_K), tail,
                       n_blocks * MOE_BLK)
        ys = _experts(xs, blk_e, nused, next_e, exp_w_gate, exp_w_up, exp_w_down, l)
        x2 = _combine(pos.reshape(T // COMBINE_TM, 1, COMBINE_TM * TOP_K), h2, gate8, x2, mod, sh_w_gate[l].astype(BF16), sh_w_up[l].astype(BF16),
                      sh_w_down[l].astype(BF16), row2(post_ffn_gain[l]), ys, S)
    return x2.reshape(B, S, D)
```

```python
import functools

import numpy as np
import jax
import jax.numpy as jnp
from jax import lax
from jax.experimental import pallas as pl
from jax.experimental.pallas import tpu as pltpu

D_MODEL = 2048
GRID_W = 64
ROPE_THETA = 10000.0
EPS = 1e-6
HEAD_DIM = 128
A_HEADS, A_KV_HEADS = 6, 2
B_HEADS, B_Q_LORA, B_KV_LORA = 6, 384, 256
B_QK_NOPE, B_QK_ROPE, B_V_DIM = 128, 64, 128
C_HEADS, C_DQK, C_DV = 4, 128, 128
CONV_W = 5
MLSTM_CHUNK = 64
N_EXPERTS, TOP_K, EXPERT_FF, SHARED_FF = 64, 6, 512, 512
ROUTED_SCALE = 2.5
N_MOD = 6

COL_AQ, COL_AK, COL_AV = 0, 768, 1024
COL_BC = 1280
COL_MISC = 1920
COL_CV, COL_CO = 2048, 2560
COL_CQK = 3072
PROJ_PAD = 4096
MISC_GATE_LANE = 64
PC_CV, PC_CO, PC_CQK = 0, COL_CO - COL_CV, COL_CQK - COL_CV

MOE_BLK = 256
DISPATCH_TM = 512
COMBINE_TM = 256
MLSTM_BB = 2
VMEM_LIMIT = 56 * 1024 * 1024

BF16 = jnp.bfloat16
F32 = jnp.float32
LOG2_E = 1.4426950408889634


def _cparams(sem):
    return pltpu.CompilerParams(dimension_semantics=("arbitrary",) * len(sem),
                                vmem_limit_bytes=VMEM_LIMIT)


def _rms(x, gain):
    return x * lax.rsqrt(jnp.mean(x * x, axis=-1, keepdims=True) + EPS) * gain


def _dot(a, b):
    return jnp.dot(a, b, preferred_element_type=F32)


def _dot_t(a, b):
    return lax.dot_general(a, b, (((1,), (1,)), ((), ())), preferred_element_type=F32)


def _mod_kernel(c_ref, w_ref, b_ref, o_ref):
    c = c_ref[...]
    c_act = (c * jax.nn.sigmoid(c)).astype(BF16)
    o_ref[...] = _dot(c_act, w_ref[...].astype(BF16)) + b_ref[...]


def _modulation(c_pad, ada_w, ada_b):
    L, D, N = ada_w.shape
    M = c_pad.shape[0]
    tn = 1536
    return pl.pallas_call(
        _mod_kernel,
        grid=(L, N // tn),
        in_specs=[pl.BlockSpec((M, D), lambda l, j: (0, 0)),
                  pl.BlockSpec((None, D, tn), lambda l, j: (l, 0, j)),
                  pl.BlockSpec((None, 1, tn), lambda l, j: (l, 0, j))],
        out_specs=pl.BlockSpec((None, M, tn), lambda l, j: (l, 0, j)),
        out_shape=jax.ShapeDtypeStruct((L, M, N), F32),
        compiler_params=_cparams(("parallel", "parallel")),
        name="adaln_mod",
    )(c_pad, ada_w, ada_b.reshape(L, 1, N))


def _in_proj_kernel(x_ref, gain_ref, mod_ref, w_ref,
                    cosa_ref, sloa_ref, shia_ref, cosb_ref, slob_ref, shib_ref,
                    qg_ref, kg_ref, cqg_ref, ckvg_ref, wuq_ref, wukv_ref,
                    qa_ref, ka_ref, va_ref, qb_ref, kb_ref, vb_ref, pc_ref, misc_ref):
    h = _rms(x_ref[...], gain_ref[...])
    h = (h * (1.0 + mod_ref[1:2, :]) + mod_ref[0:1, :]).astype(BF16)
    pa = _dot(h, w_ref[:, COL_AQ:COL_BC])
    _gqa_prep(pa, (cosa_ref[...], sloa_ref[...], shia_ref[...]), qg_ref[...], kg_ref[...],
              qa_ref, ka_ref, va_ref)
    pb = _dot(h, w_ref[:, COL_BC:COL_CV])
    misc = pb[:, COL_MISC - COL_BC:]
    misc_ref[...] = misc
    _mla_prep(pb[:, :COL_MISC - COL_BC], misc, (cosb_ref[...], slob_ref[...], shib_ref[...]),
              cqg_ref[...], ckvg_ref[...], wuq_ref, wukv_ref, qb_ref, kb_ref, vb_ref)
    tn = 1024
    for j in range((PROJ_PAD - COL_CV) // tn):
        pc_ref[:, j * tn:(j + 1) * tn] = _dot(h, w_ref[:, COL_CV + j * tn: COL_CV + (j + 1) * tn])


def _in_proj(x2, gain, mod, w_in_p, tabs_a, tabs_b, q_gain, k_gain, cq_gain, ckv_gain, w_uq_p, w_ukv_p, S):
    T, D = x2.shape
    N = w_in_p.shape[1]
    tm = 256
    nb = S // tm
    row = lambda w: pl.BlockSpec((tm, w), lambda i: (i, 0))
    full = lambda a: pl.BlockSpec(a.shape, lambda i: (0,) * a.ndim)
    tab = pl.BlockSpec((tm, 128), lambda i: (i % nb, 0))
    qw, kw = A_HEADS * HEAD_DIM, A_KV_HEADS * HEAD_DIM
    bqw, bvw, cw = B_HEADS * 256, B_HEADS * B_V_DIM, PROJ_PAD - COL_CV
    widths = (qw, kw, kw, bqw, bqw, bvw)
    return pl.pallas_call(
        _in_proj_kernel,
        grid=(T // tm,),
        in_specs=[row(D), full(gain),
                  pl.BlockSpec((None, N_MOD, D), lambda i: (i // nb, 0, 0)),
                  pl.BlockSpec(w_in_p.shape, lambda i: (0, 0), pipeline_mode=pl.Buffered(1)),
                  tab, tab, tab, tab, tab, tab,
                  full(q_gain), full(k_gain), full(cq_gain), full(ckv_gain), full(w_uq_p), full(w_ukv_p)],
        out_specs=[row(w) for w in widths] + [row(cw), row(128)],
        out_shape=[jax.ShapeDtypeStruct((T, w), BF16) for w in widths]
                  + [jax.ShapeDtypeStruct((T, cw), F32), jax.ShapeDtypeStruct((T, 128), F32)],
        compiler_params=_cparams(("arbitrary",)),
        name="in_proj",
    )(x2, gain, mod, w_in_p, *tabs_a, *tabs_b, q_gain, k_gain, cq_gain, ckv_gain, w_uq_p, w_ukv_p)


def _rope128(y, cos, sin_lo, sin_hi):
    return y * cos + pltpu.roll(y, 96, 1) * sin_lo + pltpu.roll(y, 32, 1) * sin_hi


def _gqa_prep(pa, tabs, q_gain, k_gain, qo_ref, ko_ref, vo_ref):
    cos, slo, shi = tabs
    scale = HEAD_DIM ** -0.5 * LOG2_E
    for h in range(A_HEADS):
        sl = slice(h * HEAD_DIM, (h + 1) * HEAD_DIM)
        y = _rope128(_rms(pa[:, COL_AQ + h * HEAD_DIM: COL_AQ + (h + 1) * HEAD_DIM], q_gain), cos, slo, shi)
        qo_ref[:, sl] = (y * scale).astype(BF16)
    for h in range(A_KV_HEADS):
        sl = slice(h * HEAD_DIM, (h + 1) * HEAD_DIM)
        y = _rope128(_rms(pa[:, COL_AK + h * HEAD_DIM: COL_AK + (h + 1) * HEAD_DIM], k_gain), cos, slo, shi)
        ko_ref[:, sl] = y.astype(BF16)
    vo_ref[...] = pa[:, COL_AV:COL_BC].astype(BF16)


def _rope64x2(y, cos, sin_lo, sin_hi):
    return y * cos + pltpu.roll(y, 112, 1) * sin_lo + pltpu.roll(y, 16, 1) * sin_hi


def _mla_prep(bc, misc, tabs, cq_gain, ckv_gain, wuq_ref, wukv_ref, qo_ref, ko_ref, vo_ref):
    cos, slo, shi = tabs
    lane = lax.broadcasted_iota(jnp.int32, cos.shape, 1)
    first = lane < B_QK_ROPE
    scale = (B_QK_NOPE + B_QK_ROPE) ** -0.5 * LOG2_E
    nope_w = B_HEADS * B_QK_NOPE

    cq = _rms(bc[:, :B_Q_LORA], cq_gain).astype(BF16)
    q = _dot(cq, wuq_ref[...]) * scale
    for p in range(B_HEADS // 2):
        pair = _rope64x2(q[:, nope_w + p * 128: nope_w + (p + 1) * 128], cos, slo, shi)
        for half in range(2):
            h = 2 * p + half
            qo_ref[:, h * 256: h * 256 + 128] = q[:, h * 128:(h + 1) * 128].astype(BF16)
            keep = first if half == 0 else jnp.logical_not(first)
            qo_ref[:, h * 256 + 128: (h + 1) * 256] = jnp.where(keep, pair, 0.0).astype(BF16)

    kr = jnp.where(first, misc, 0.0)
    kr = kr + pltpu.roll(kr, B_QK_ROPE, 1)
    kr = _rope64x2(kr, cos, slo, shi).astype(BF16)
    ckv = _rms(bc[:, B_Q_LORA:], ckv_gain).astype(BF16)
    kv = _dot(ckv, wukv_ref[...])
    for h in range(B_HEADS):
        ko_ref[:, h * 256: h * 256 + 128] = kv[:, h * 128:(h + 1) * 128].astype(BF16)
        ko_ref[:, h * 256 + 128: (h + 1) * 256] = kr
    vo_ref[...] = kv[:, nope_w:].astype(BF16)


def _attn_kernel(q_ref, k_ref, v_ref, o_ref, m_scr, l_scr, acc_scr, *, G, dq, dv, tk, sub):
    tq = q_ref.shape[0]
    S = k_ref.shape[0]
    groups = [(g, r) for g in range(G) for r in range(tq // sub)]
    m_scr[...] = jnp.full(m_scr.shape, -jnp.inf, F32)
    l_scr[...] = jnp.zeros(l_scr.shape, F32)
    acc_scr[...] = jnp.zeros(acc_scr.shape, F32)

    stages = [(c, n) for c in range(S // tk) for n in range(len(groups))]

    def scores(stage):
        c, n = stage
        g, r = groups[n]
        return _dot_t(q_ref[r * sub:(r + 1) * sub, g * dq:(g + 1) * dq], k_ref[c * tk:(c + 1) * tk, :])

    s_next = scores(stages[0])
    for i, (c, n) in enumerate(stages):
        s = s_next
        if i + 1 < len(stages):
            s_next = scores(stages[i + 1])
        rows = slice(n * sub, (n + 1) * sub)
        blocks = [s[:, j * 128:(j + 1) * 128] for j in range(tk // 128)]
        m_blk = functools.reduce(jnp.maximum, blocks)
        m_old = m_scr[rows, :]
        m_new = jnp.maximum(m_old, jnp.max(m_blk, axis=-1, keepdims=True))
        alpha = jnp.exp2(m_old - m_new)
        p_blocks = [jnp.exp2(b - m_new) for b in blocks]
        l_scr[rows, :] = alpha * l_scr[rows, :] + functools.reduce(jnp.add, p_blocks)
        p = jnp.concatenate([b.astype(BF16) for b in p_blocks], axis=1)
        acc_scr[rows, :] = alpha * acc_scr[rows, :] + _dot(p, v_ref[c * tk:(c + 1) * tk, :])
        m_scr[rows, :] = m_new

    for n, (g, r) in enumerate(groups):
        rows = slice(n * sub, (n + 1) * sub)
        l = jnp.sum(l_scr[rows, :], axis=-1, keepdims=True)
        o_ref[r * sub:(r + 1) * sub, g * dv:(g + 1) * dv] = acc_scr[rows, :] / l


def _attention(q, k, v, *, B, S, Hk, G, dq, dv, tq, tk):
    assert dv == 128, "row statistics are kept 128 lanes wide to match the value width"
    kern = functools.partial(_attn_kernel, G=G, dq=dq, dv=dv, tk=tk, sub=256)
    M = G * tq
    return pl.pallas_call(
        kern,
        grid=(B, Hk, S // tq),
        in_specs=[pl.BlockSpec((None, tq, G * dq), lambda b, h, i: (b, i, h)),
                  pl.BlockSpec((None, S, dq), lambda b, h, i: (b, 0, h)),
                  pl.BlockSpec((None, S, dv), lambda b, h, i: (b, 0, h))],
        out_specs=pl.BlockSpec((None, tq, G * dv), lambda b, h, i: (b, i, h)),
        out_shape=jax.ShapeDtypeStruct((B, S, Hk * G * dv), F32),
        scratch_shapes=[pltpu.VMEM((M, 128), F32), pltpu.VMEM((M, 128), F32), pltpu.VMEM((M, dv), F32)],
        compiler_params=_cparams(("parallel", "parallel", "arbitrary")),
        name=f"attn_g{G}_d{dq}",
    )(q, k, v)


def _conv_kernel(x_ref, w_ref, b_ref, o_ref, *, k_scale_from):
    x = x_ref[...]
    S = x.shape[0]
    row = lax.broadcasted_iota(jnp.int32, x.shape, 0)
    pad = CONV_W // 2
    y = x * w_ref[pad:pad + 1, :] + b_ref[...]
    for j in range(CONV_W):
        d = j - pad
        if d == 0:
            continue
        shifted = pltpu.roll(x, (-d) % S, 0)
        valid = (row + d >= 0) & (row + d < S)
        y = y + jnp.where(valid, shifted, 0.0) * w_ref[j:j + 1, :]
    y = y * jax.nn.sigmoid(y)
    scale = jnp.where(pl.program_id(1) >= k_scale_from, C_DQK ** -0.5, 1.0)
    o_ref[...] = y * scale


def _conv_silu(pc3, conv_w, conv_b):
    B, S, _ = pc3.shape
    W = 2 * C_HEADS * C_DQK
    tc = 128
    kern = functools.partial(_conv_kernel, k_scale_from=(C_HEADS * C_DQK) // tc)
    return pl.pallas_call(
        kern,
        grid=(B, W // tc),
        in_specs=[pl.BlockSpec((None, S, tc), lambda b, j: (b, 0, PC_CQK // tc + j)),
                  pl.BlockSpec((CONV_W, tc), lambda b, j: (0, j)),
                  pl.BlockSpec((1, tc), lambda b, j: (0, j))],
        out_specs=pl.BlockSpec((None, S, tc), lambda b, j: (b, 0, j)),
        out_shape=jax.ShapeDtypeStruct((B, S, W), F32),
        compiler_params=_cparams(("parallel", "parallel")),
        name="conv_silu",
    )(pc3, conv_w, conv_b)


def _log_sigmoid(x):
    return jnp.minimum(x, 0.0) - jnp.log1p(jnp.exp(-jnp.abs(x)))


def _mlstm_kernel(qf_ref, kf_ref, vf_ref, gf_ref, qr_ref, kr_ref, vr_ref, gr_ref, gb_ref,
                  of_ref, or_ref, C_scr, n_scr, m_scr, *, BB):
    L = MLSTM_CHUNK

    @pl.when(pl.program_id(1) == 0)
    def _():
        C_scr[...] = jnp.zeros(C_scr.shape, F32)
        n_scr[...] = jnp.zeros(n_scr.shape, F32)
        m_scr[...] = jnp.zeros(m_scr.shape, F32)

    glane = lax.broadcasted_iota(jnp.int32, (L, 128), 1)
    jj = lax.broadcasted_iota(jnp.int32, (L, L), 0)
    ss = lax.broadcasted_iota(jnp.int32, (L, L), 1)
    eye = jj == ss
    neg_inf = jnp.float32(-jnp.inf)
    dirs = ((qf_ref, kf_ref, vf_ref, gf_ref, of_ref, ss <= jj, jj <= ss),
            (qr_ref, kr_ref, vr_ref, gr_ref, or_ref, ss >= jj, jj >= ss))

    chains = []
    for d, (q_ref, k_ref, v_ref, g_ref, o_ref, seen, seen_t) in enumerate(dirs):
        for bb in range(BB):
            gates = g_ref[bb] + gb_ref[...]
            lsig = _log_sigmoid(gates)
            for h in range(C_HEADS):
                c = dict(st=(d * BB + bb) * C_HEADS + h, sl=slice(h * C_DQK, (h + 1) * C_DQK), bb=bb,
                         q_ref=q_ref, k_ref=k_ref, v_ref=v_ref, o_ref=o_ref, seen=seen, seen_t=seen_t)
                i_lane = MISC_GATE_LANE + d * (2 * C_HEADS) + h
                c["i_col"] = jnp.sum(jnp.where(glane == i_lane, gates, 0.0), axis=1, keepdims=True)
                c["f_col"] = jnp.sum(jnp.where(glane == i_lane + C_HEADS, lsig, 0.0), axis=1, keepdims=True)
                chains.append(c)
    for c in chains:
        c["f_row"] = jnp.sum(jnp.where(eye, c["f_col"], 0.0), axis=0, keepdims=True)
        c["i_row"] = jnp.sum(jnp.where(eye, c["i_col"], 0.0), axis=0, keepdims=True)
        c["b_row"] = jnp.sum(jnp.where(c["seen_t"], c["f_col"], 0.0), axis=0, keepdims=True)
        c["g_tot"] = jnp.sum(c["f_col"], axis=0, keepdims=True)
    for c in chains:
        c["b_col"] = jnp.sum(jnp.where(c["seen"], c["f_row"], 0.0), axis=1, keepdims=True)
    for c in chains:
        c["m_prev"] = m_scr[c["st"]][:, 0:1]
        c["dmat"] = jnp.where(c["seen"], c["b_col"] - c["b_row"] + c["i_row"], neg_inf)
        c["m_inter"] = c["b_col"] + c["m_prev"]
        c["a_col"] = c["g_tot"] - c["b_col"] + c["i_col"]
    for c in chains:
        c["m_j"] = jnp.maximum(c["m_inter"], jnp.max(c["dmat"], axis=1, keepdims=True))
        c["m_new"] = jnp.maximum(c["g_tot"] + c["m_prev"], jnp.max(c["a_col"], axis=0, keepdims=True))
    for c in chains:
        c["q"] = c["q_ref"][c["bb"], :, c["sl"]]
        c["k"] = c["k_ref"][c["bb"], :, c["sl"]]
        c["qb"], c["kb"] = c["q"].astype(BF16), c["k"].astype(BF16)
        c["vb"] = c["v_ref"][c["bb"], :, c["sl"]].astype(BF16)
        c["qk"] = _dot_t(c["qb"], c["kb"])
    for c in chains:
        c["qC"] = _dot(c["qb"], C_scr[c["st"]].astype(BF16))
    for c in chains:
        c["s"] = c["qk"] * jnp.exp(c["dmat"] - c["m_j"])
        c["inter"] = jnp.exp(c["m_inter"] - c["m_j"])
        c["wk"] = jnp.exp(c["a_col"] - c["m_new"]) * c["k"]
        c["decay"] = jnp.exp(c["g_tot"] + c["m_prev"] - c["m_new"])
    for c in chains:
        c["sv"] = _dot(c["s"].astype(BF16), c["vb"])
    for c in chains:
        c["upd"] = lax.dot_general(c["wk"].astype(BF16), c["vb"], (((0,), (0,)), ((), ())),
                                   preferred_element_type=F32)
    for c in chains:
        n_prev = n_scr[c["st"]]
        den = (jnp.sum(c["s"], axis=1, keepdims=True)
               + c["inter"] * jnp.sum(c["q"] * n_prev, axis=1, keepdims=True))
        num = c["sv"] + c["inter"] * c["qC"]
        c["o_ref"][c["bb"], :, c["sl"]] = num / jnp.maximum(jnp.abs(den), jnp.exp(-c["m_j"]))
        n_scr[c["st"]] = c["decay"] * n_prev + jnp.sum(c["wk"], axis=0, keepdims=True)
    for c in chains:
        C_scr[c["st"]] = c["decay"] * C_scr[c["st"]] + c["upd"]
        m_scr[c["st"]] = jnp.broadcast_to(c["m_new"], m_scr.shape[1:])


def _mlstm(qk3, pc3, misc3, gate_b_row, BB=MLSTM_BB):
    B, S, _ = pc3.shape
    L = MLSTM_CHUNK
    nc = S // L
    W = C_HEADS * C_DQK
    n_chain = 2 * BB * C_HEADS
    fwd = lambda col: (lambda b, c: (b, c, col))
    rev = lambda col: (lambda b, c: (b, nc - 1 - c, col))
    blk = lambda w, imap: pl.BlockSpec((BB, L, w), imap)
    out = jax.ShapeDtypeStruct((B, S, W), F32)
    return pl.pallas_call(
        functools.partial(_mlstm_kernel, BB=BB),
        grid=(B // BB, nc),
        in_specs=[blk(W, fwd(0)), blk(W, fwd(1)), blk(W, fwd(PC_CV // W)), blk(128, fwd(0)),
                  blk(W, rev(0)), blk(W, rev(1)), blk(W, rev(PC_CV // W)), blk(128, rev(0)),
                  pl.BlockSpec((1, 128), lambda b, c: (0, 0))],
        out_specs=[blk(W, fwd(0)), blk(W, rev(0))],
        out_shape=[out, out],
        scratch_shapes=[pltpu.VMEM((n_chain, C_DQK, C_DV), F32),
                        pltpu.VMEM((n_chain, 1, C_DQK), F32),
                        pltpu.VMEM((n_chain, 1, 128), F32)],
        compiler_params=_cparams(("arbitrary", "arbitrary")),
        name="mlstm",
    )(qk3, qk3, pc3, misc3, qk3, qk3, pc3, misc3, gate_b_row)


ROW_TILE = 8
HALF_D = D_MODEL // 2
U32 = jnp.uint32
HI_MASK = 0xFFFF0000


def _pack_rows(y, o_ref):
    n = y.shape[0]
    for s in range(ROW_TILE):
        lo = y[:, s * 128:(s + 1) * 128].astype(BF16).astype(F32)
        hi = y[:, HALF_D + s * 128: HALF_D + (s + 1) * 128].astype(BF16).astype(F32)
        w = (lax.bitcast_convert_type(lo, U32) >> 16) | (lax.bitcast_convert_type(hi, U32) & U32(HI_MASK))
        o_ref[pl.ds(s, n, stride=ROW_TILE), :] = w


def _unpack_rows(ref, n):
    lo, hi = [], []
    for s in range(ROW_TILE):
        w = ref[pl.ds(s, n, stride=ROW_TILE), :]
        lo.append(lax.bitcast_convert_type(w << 16, F32))
        hi.append(lax.bitcast_convert_type(w & U32(HI_MASK), F32))
    return lo + hi


def _merge_kernel(ya_ref, yb_ref, hf_ref, hr_ref, co_ref, x_ref, mod_ref, hg_ref, wout_ref, pmg_ref,
                  pfg_ref, rw_ref, xo_ref, h2_ref, h2p_ref, lg_ref, y_scr):
    attn_heads = A_HEADS + B_HEADS
    for h in range(attn_heads + C_HEADS):
        sl = slice(h * HEAD_DIM, (h + 1) * HEAD_DIM)
        g = hg_ref[:, sl]
        if h < A_HEADS:
            y = _rms(ya_ref[:, sl], g)
        elif h < attn_heads:
            y = _rms(yb_ref[:, (h - A_HEADS) * HEAD_DIM:(h - A_HEADS + 1) * HEAD_DIM], g)
        else:
            cs = slice((h - attn_heads) * HEAD_DIM, (h - attn_heads + 1) * HEAD_DIM)
            y = _rms(hf_ref[:, cs] + hr_ref[:, cs], g) * jax.nn.sigmoid(co_ref[:, cs])
        y_scr[:, sl] = y.astype(BF16)
    y = _dot(y_scr[...], wout_ref[...])
    x_new = x_ref[...] + mod_ref[2:3, :] * _rms(y, pmg_ref[...])
    xo_ref[...] = x_new
    h2 = _rms(x_new, pfg_ref[...]) * (1.0 + mod_ref[4:5, :]) + mod_ref[3:4, :]
    h2_ref[...] = h2.astype(BF16)
    _pack_rows(h2, h2p_ref)
    h_hi = h2.astype(BF16)
    h_lo = (h2 - h_hi.astype(F32)).astype(BF16)
    a = _dot(h_hi, rw_ref[...])
    b = _dot(h_lo, rw_ref[...])
    E = N_EXPERTS
    lg_ref[...] = a[:, :E] + (a[:, E:] + b[:, :E]) + b[:, E:]


def _merge(ya, yb, hf, hr, pc, x2, mod, head_gain, w_out_b, pm_gain, pf_gain, router_w, S):
    T, D = x2.shape
    tm = 256
    nb = S // tm
    aw, cw = A_HEADS * HEAD_DIM, C_HEADS * C_DV
    row = lambda w: pl.BlockSpec((tm, w), lambda i: (i, 0))
    full = lambda a: pl.BlockSpec(a.shape, lambda i: (0,) * a.ndim)
    return pl.pallas_call(
        _merge_kernel,
        grid=(T // tm,),
        in_specs=[row(aw), row(aw), row(cw), row(cw),
                  pl.BlockSpec((tm, cw), lambda i: (i, PC_CO // cw)),
                  row(D),
                  pl.BlockSpec((None, N_MOD, D), lambda i: (i // nb, 0, 0)),
                  full(head_gain), full(w_out_b), full(pm_gain), full(pf_gain), full(router_w)],
        out_specs=[row(D), row(D), pl.BlockSpec((tm * ROW_TILE, 128), lambda i: (i, 0)),
                   row(N_EXPERTS)],
        out_shape=[jax.ShapeDtypeStruct((T, D), F32),
                   jax.ShapeDtypeStruct((T, D), BF16),
                   jax.ShapeDtypeStruct((T * ROW_TILE, 128), U32),
                   jax.ShapeDtypeStruct((T, N_EXPERTS), F32)],
        scratch_shapes=[pltpu.VMEM((tm, D), BF16)],
        compiler_params=_cparams(("parallel",)),
        name="merge_out_proj",
    )(ya, yb, hf, hr, pc, x2, mod, head_gain, w_out_b, pm_gain, pf_gain, router_w)


def _router_kernel(lg_ref, bias_ref, idx_ref, gate_ref, mask_ref):
    scores = jax.nn.sigmoid(lg_ref[...])
    sel = scores + bias_ref[...]
    lane = lax.broadcasted_iota(jnp.int32, scores.shape, 1).astype(F32)
    col = lax.broadcasted_iota(jnp.int32, idx_ref.shape, 1)
    idx = jnp.zeros(idx_ref.shape, F32)
    gate = jnp.zeros(gate_ref.shape, F32)
    mask = jnp.zeros(scores.shape, F32)
    for kk in range(TOP_K):
        mx = jnp.max(sel, axis=1, keepdims=True)
        am = jnp.min(jnp.where(sel == mx, lane, float(N_EXPERTS)), axis=1, keepdims=True)
        hit = lane == am
        sc = jnp.sum(jnp.where(hit, scores, 0.0), axis=1, keepdims=True)
        idx = jnp.where(col == kk, am, idx)
        gate = jnp.where(col == kk, sc, gate)
        mask = jnp.where(hit, 1.0, mask)
        sel = jnp.where(hit, -jnp.inf, sel)
    gate = gate / jnp.sum(gate, axis=1, keepdims=True) * ROUTED_SCALE
    idx_ref[...] = idx.astype(jnp.int32)
    gate_ref[...] = gate
    mask_ref[...] = mask.astype(jnp.int32)


def _router(logits, bias):
    T, E = logits.shape
    tm = 1024
    row = lambda w: pl.BlockSpec((tm, w), lambda i: (i, 0))
    return pl.pallas_call(
        _router_kernel,
        grid=(T // tm,),
        in_specs=[row(E), pl.BlockSpec((1, E), lambda i: (0, 0))],
        out_specs=[row(8), row(8), row(E)],
        out_shape=[jax.ShapeDtypeStruct((T, 8), jnp.int32),
                   jax.ShapeDtypeStruct((T, 8), F32),
                   jax.ShapeDtypeStruct((T, E), jnp.int32)],
        compiler_params=_cparams(("parallel",)),
        name="router_topk",
    )(logits, bias)


TAIL_BITS = MOE_BLK.bit_length() - 1


def _tail_copies(tail_ref, zero_scr, xs_ref, zsem, fn):
    for e in range(N_EXPERTS):
        start_row = tail_ref[2 * e]
        n_tail = tail_ref[2 * e + 1]
        for bit in range(TAIL_BITS):
            size = 1 << bit
            cur = start_row + (n_tail & ~(2 * size - 1))
            cur = pl.multiple_of(cur * ROW_TILE, ROW_TILE)

            @pl.when((n_tail & size) != 0)
            def _():
                fn(pltpu.make_async_copy(zero_scr.at[pl.ds(0, size * ROW_TILE), :],
                                         xs_ref.at[pl.ds(cur, size * ROW_TILE), :], zsem))


def _row_tile(ref, row):
    return ref.at[pl.ds(pl.multiple_of(row * ROW_TILE, ROW_TILE), ROW_TILE), :]


def _dispatch_kernel(tail_ref, pos_ref, h_ref, xs_ref, zero_scr, sem, zsem, *, tm):
    i = pl.program_id(0)

    def start(r, c):
        src = _row_tile(h_ref, r)
        for kk in range(TOP_K):
            pltpu.make_async_copy(src, _row_tile(xs_ref, pos_ref[0, r * TOP_K + kk]),
                                  sem).start(priority=kk % 2)
        return c

    lax.fori_loop(0, tm, start, 0)

    @pl.when(i == 0)
    def _():
        zero_scr[...] = jnp.zeros(zero_scr.shape, zero_scr.dtype)
        _tail_copies(tail_ref, zero_scr, xs_ref, zsem, lambda cp: cp.start())
        _tail_copies(tail_ref, zero_scr, xs_ref, zsem, lambda cp: cp.wait())

    for kk in range(TOP_K):
        pltpu.make_async_copy(h_ref, xs_ref.at[pl.ds(0, tm * ROW_TILE), :], sem).wait()


def _dispatch(h2p, pos3, tail_info, P):
    T = h2p.shape[0] // ROW_TILE
    tm = pos3.shape[2] // TOP_K
    kern = functools.partial(_dispatch_kernel, tm=tm)
    return pl.pallas_call(
        kern,
        grid_spec=pltpu.PrefetchScalarGridSpec(
            num_scalar_prefetch=1,
            grid=(T // tm,),
            in_specs=[pl.BlockSpec((None, 1, tm * TOP_K), lambda i, tail: (i, 0, 0),
                                   memory_space=pltpu.SMEM),
                      pl.BlockSpec((tm * ROW_TILE, 128), lambda i, tail: (i, 0))],
            out_specs=pl.BlockSpec(memory_space=pl.ANY),
            scratch_shapes=[pltpu.VMEM((MOE_BLK // 2 * ROW_TILE, 128), U32),
                            pltpu.SemaphoreType.DMA(()), pltpu.SemaphoreType.DMA(())]),
        out_shape=jax.ShapeDtypeStruct((P * ROW_TILE, 128), U32),
        compiler_params=_cparams(("arbitrary",)),
        name="moe_dispatch",
    )(tail_info, pos3, h2p)


EXPERT_TILES = 2


def _expert_kernel(blk_e_ref, nused_ref, next_ref, xs_ref, wg_hbm, wu_hbm, wd_hbm, ys_ref,
                   wg_f, wu_f, wd_f, wg_b, wu_b, wd_b, sems, *, layer):
    step = pl.program_id(0)
    rows = MOE_BLK * ROW_TILE

    def fetch(ex):
        return [pltpu.make_async_copy(hbm.at[layer, ex], buf, sems.at[n])
                for n, (hbm, buf) in enumerate(((wg_hbm, wg_f), (wu_hbm, wu_f), (wd_hbm, wd_f)))]

    @pl.when(step == 0)
    def _():
        for cp in fetch(blk_e_ref[0]):
            cp.start()

    for j in range(EXPERT_TILES):
        i = step * EXPERT_TILES + j
        e = blk_e_ref[i]
        e_prev = blk_e_ref[jnp.maximum(i - 1, 0)]
        used = i < nused_ref[0]
        first = used & ((i == 0) | (e != e_prev))
        x_view = xs_ref.at[pl.ds(j * rows, rows), :]
        y_view = ys_ref.at[pl.ds(j * rows, rows), :]

        def compute(switch, e=e, x_view=x_view, y_view=y_view):
            x = jnp.concatenate([c.astype(BF16) for c in _unpack_rows(x_view, MOE_BLK)], axis=1)
            if switch:
                for cp in fetch(e):
                    cp.wait()
                wg_b[...] = wg_f[...].astype(BF16)
            gte = _dot(x, wg_b[...])
            if switch:
                wu_b[...] = wu_f[...].astype(BF16)
            up = _dot(x, wu_b[...])
            if switch:
                wd_b[...] = wd_f[...].astype(BF16)
                e_next = next_ref[e]

                @pl.when(e_next >= 0)
                def _():
                    for cp in fetch(e_next):
                        cp.start()
            a = gte * jax.nn.sigmoid(gte) * up
            _pack_rows(_dot(a.astype(BF16), wd_b[...]), y_view)

        pl.when(first)(functools.partial(compute, True))
        pl.when(used & jnp.logical_not(first))(functools.partial(compute, False))

        @pl.when(jnp.logical_not(used))
        def _(y_view=y_view):
            y_view[...] = jnp.zeros(y_view.shape, y_view.dtype)


def _experts(xs, blk_e, nused, next_e, w_gate, w_up, w_down, layer):
    P = xs.shape[0] // ROW_TILE
    D, F = w_gate.shape[-2:]
    NB = P // MOE_BLK
    assert NB % EXPERT_TILES == 0
    rows = EXPERT_TILES * MOE_BLK * ROW_TILE
    last = lambda s, nu: jnp.minimum(s, (nu[0] - 1) // EXPERT_TILES)
    hbm = pl.BlockSpec(memory_space=pl.ANY)
    return pl.pallas_call(
        functools.partial(_expert_kernel, layer=layer),
        grid_spec=pltpu.PrefetchScalarGridSpec(
            num_scalar_prefetch=3,
            grid=(NB // EXPERT_TILES,),
            in_specs=[pl.BlockSpec((rows, 128), lambda s, be, nu, nx: (last(s, nu), 0)),
                      hbm, hbm, hbm],
            out_specs=pl.BlockSpec((rows, 128), lambda s, be, nu, nx: (s, 0)),
            scratch_shapes=[pltpu.VMEM((D, F), F32), pltpu.VMEM((D, F), F32), pltpu.VMEM((F, D), F32),
                            pltpu.VMEM((D, F), BF16), pltpu.VMEM((D, F), BF16),
                            pltpu.VMEM((F, D), BF16), pltpu.SemaphoreType.DMA((3,))]),
        out_shape=jax.ShapeDtypeStruct((P * ROW_TILE, 128), U32),
        compiler_params=_cparams(("arbitrary",)),
        name="moe_experts",
    )(blk_e, nused, next_e, xs, w_gate, w_up, w_down)


def _combine_kernel(pos_ref, pos_next_ref, h_ref, gate_ref, x_ref, mod_ref, wsg_ref, wsu_ref, wsd_ref,
                    pg_ref, ys_ref, xo_ref, gbuf, sems, *, tm):
    i = pl.program_id(0)
    slot = i % 2

    def gather(p_ref, s):
        def start(r, c):
            for kk in range(TOP_K):
                pltpu.make_async_copy(_row_tile(ys_ref, p_ref[0, r * TOP_K + kk]),
                                      _row_tile(gbuf.at[s, kk], r), sems.at[s]).start(priority=kk % 2)
            return c
        lax.fori_loop(0, tm, start, 0)

    @pl.when(i == 0)
    def _():
        gather(pos_ref, 0)

    for s in range(2):
        @pl.when((i + 1 < pl.num_programs(0)) & (slot == 1 - s))
        def _():
            gather(pos_next_ref, s)

    for kk in range(TOP_K):
        pltpu.make_async_copy(ys_ref.at[pl.ds(0, tm * ROW_TILE), :], gbuf.at[slot, kk],
                              sems.at[slot]).wait()
    h = h_ref[...]
    gte = _dot(h, wsg_ref[...])
    up = _dot(h, wsu_ref[...])
    RG = 64

    def routed(rg):
        rows = slice(rg * RG, (rg + 1) * RG)
        gate = gate_ref[rows, :]
        g_wide = [jnp.broadcast_to(gate[:, kk:kk + 1], (RG, 128)) for kk in range(TOP_K)]
        for s in range(ROW_TILE):
            lo = hi = None
            for kk in range(TOP_K):
                w = gbuf[slot, kk, pl.ds(rg * RG * ROW_TILE + s, RG, stride=ROW_TILE), :]
                t_lo = g_wide[kk] * lax.bitcast_convert_type(w << 16, F32)
                t_hi = g_wide[kk] * lax.bitcast_convert_type(w & U32(HI_MASK), F32)
                lo = t_lo if lo is None else lo + t_lo
                hi = t_hi if hi is None else hi + t_hi
            xo_ref[rows, s * 128:(s + 1) * 128] = lo
            xo_ref[rows, HALF_D + s * 128: HALF_D + (s + 1) * 128] = hi

    n_rg = tm // RG
    for rg in range(n_rg // 2):
        routed(rg)
    a = (gte * jax.nn.sigmoid(gte) * up).astype(BF16)
    y_sh = _dot(a, wsd_ref[...])
    for rg in range(n_rg // 2, n_rg):
        routed(rg)
    y = xo_ref[...] + y_sh
    xo_ref[...] = x_ref[...] + mod_ref[5:6, :] * _rms(y, pg_ref[...])


def _combine(pos3, h2, gate8, x2, mod, ws_gate_b, ws_up_b, ws_down_b, post_gain, ys, S):
    T, D = x2.shape
    tm = pos3.shape[2] // TOP_K
    nb = S // tm
    kern = functools.partial(_combine_kernel, tm=tm)
    row = lambda w: pl.BlockSpec((tm, w), lambda i: (i, 0))
    full = lambda a: pl.BlockSpec(a.shape, lambda i: (0,) * a.ndim)
    n_steps = T // tm
    return pl.pallas_call(
        kern,
        grid=(n_steps,),
        in_specs=[pl.BlockSpec((None, 1, tm * TOP_K), lambda i: (i, 0, 0), memory_space=pltpu.SMEM),
                  pl.BlockSpec((None, 1, tm * TOP_K), lambda i: (jnp.minimum(i + 1, n_steps - 1), 0, 0),
                               memory_space=pltpu.SMEM),
                  row(D), row(8), row(D),
                  pl.BlockSpec((None, N_MOD, D), lambda i: (i // nb, 0, 0)),
                  full(ws_gate_b), full(ws_up_b), full(ws_down_b), full(post_gain),
                  pl.BlockSpec(memory_space=pl.ANY)],
        out_specs=row(D),
        out_shape=jax.ShapeDtypeStruct((T, D), F32),
        scratch_shapes=[pltpu.VMEM((2, TOP_K, tm * ROW_TILE, 128), U32),
                        pltpu.SemaphoreType.DMA((2,))],
        compiler_params=_cparams(("arbitrary",)),
        name="moe_combine",
    )(pos3, pos3, h2, gate8, x2, mod, ws_gate_b, ws_up_b, ws_down_b, post_gain, ys)


def _route_plan(idx8, mask, n_blocks):
    idx = idx8[:, :TOP_K]
    counts = jnp.sum(mask, axis=0)
    rank = jnp.cumsum(mask, axis=0) - mask
    padded = (counts + MOE_BLK - 1) // MOE_BLK * MOE_BLK
    pend = jnp.cumsum(padded)
    pstart = pend - padded
    pos = jnp.take_along_axis(pstart[None, :] + rank, idx, axis=1).astype(jnp.int32)
    blk_start = jnp.arange(n_blocks, dtype=jnp.int32) * MOE_BLK
    blk_e = jnp.minimum(jnp.sum(pend[None, :] <= blk_start[:, None], axis=1), N_EXPERTS - 1)
    nused = (pend[-1:] // MOE_BLK).astype(jnp.int32)
    tail = jnp.stack([pstart + counts, padded - counts], axis=1).reshape(-1).astype(jnp.int32)
    ids = jnp.where(counts > 0, jnp.arange(N_EXPERTS), N_EXPERTS)
    after = jnp.concatenate([lax.cummin(ids, reverse=True)[1:], jnp.full((1,), N_EXPERTS)])
    next_e = jnp.where(after < N_EXPERTS, after, -1).astype(jnp.int32)
    return pos, blk_e.astype(jnp.int32), nused, tail, next_e


def _rope_tables(S, rot_dim):
    rows = S // GRID_W
    row = jnp.repeat(jnp.arange(rows, dtype=F32), GRID_W)
    col = jnp.tile(jnp.arange(GRID_W, dtype=F32), rows)
    axis_dim = rot_dim // 2
    inv_freq = ROPE_THETA ** (-jnp.arange(0, axis_dim, 2, dtype=F32) / axis_dim)
    ang_r = row[:, None] * inv_freq
    ang_c = col[:, None] * inv_freq
    z = jnp.zeros_like(ang_r)
    cos = jnp.concatenate([jnp.cos(ang_r)] * 2 + [jnp.cos(ang_c)] * 2, axis=1)
    sin_lo = jnp.concatenate([-jnp.sin(ang_r), z, -jnp.sin(ang_c), z], axis=1)
    sin_hi = jnp.concatenate([z, jnp.sin(ang_r), z, jnp.sin(ang_c)], axis=1)
    reps = 128 // rot_dim
    return tuple(jnp.tile(t, (1, reps)) for t in (cos, sin_lo, sin_hi))


def _prep_w_in(w):
    widths = (768, 256, 256, B_Q_LORA, B_KV_LORA, B_QK_ROPE, 512, 512, 512, 512, 4 * C_HEADS)
    offs = np.cumsum(widths)[:-1].tolist()
    a_q, a_k, a_v, b_cq, b_ckv, b_kr, c_q, c_k, c_v, c_o, c_g = jnp.split(w, offs, axis=-1)
    pad = jnp.zeros(w.shape[:-1] + (128 - B_QK_ROPE - 4 * C_HEADS,), w.dtype)
    out = jnp.concatenate([a_q, a_k, a_v, b_cq, b_ckv, b_kr, c_g, pad, c_v, c_o, c_q, c_k], axis=-1)
    assert out.shape[-1] == PROJ_PAD
    return out.astype(BF16)


def _split_hi_lo(w):
    hi = w.astype(BF16)
    lo = (w - hi.astype(F32)).astype(BF16)
    return jnp.concatenate([hi, lo], axis=1)


def _heads_split(w, n_heads, first):
    K = w.shape[0]
    w3 = w.reshape(K, n_heads, -1)
    return jnp.concatenate([w3[:, :, :first].reshape(K, -1), w3[:, :, first:].reshape(K, -1)],
                           axis=1).astype(BF16)


def kernel(x, c, ada_w, ada_b, pre_mix_gain, w_in, a_q_gain, a_k_gain, b_cq_gain, b_ckv_gain, w_uq, w_ukv, c_conv_w, c_conv_b, c_gate_b, head_out_gain, w_out, post_mix_gain, pre_ffn_gain, router_w, router_bias, exp_w_gate, exp_w_up, exp_w_down, sh_w_gate, sh_w_up, sh_w_down, post_ffn_gain):
    B, S, D = x.shape
    T = B * S
    L = ada_w.shape[0]
    n_blocks = T * TOP_K // MOE_BLK + N_EXPERTS
    row2 = lambda v: v.reshape(1, -1)

    tabs_a = _rope_tables(S, HEAD_DIM)
    tabs_b = _rope_tables(S, B_QK_ROPE)
    c_pad = jnp.zeros((8, D), F32).at[:B].set(c)
    mods = _modulation(c_pad, ada_w, ada_b)[:, :B].reshape(L, B, N_MOD, D)

    x2 = x.reshape(T, D)
    for l in range(L):
        mod = mods[l]
        qa, ka, va, qb, kb, vb, pc, misc = _in_proj(
            x2, row2(pre_mix_gain[l]), mod, _prep_w_in(w_in[l]), tabs_a, tabs_b,
            row2(a_q_gain[l]), row2(a_k_gain[l]), row2(b_cq_gain[l]), row2(b_ckv_gain[l]),
            _heads_split(w_uq[l], B_HEADS, B_QK_NOPE), _heads_split(w_ukv[l], B_HEADS, B_QK_NOPE), S)
        pc3 = pc.reshape(B, S, -1)

        ya = _attention(qa.reshape(B, S, -1), ka.reshape(B, S, -1), va.reshape(B, S, -1),
                        B=B, S=S, Hk=A_KV_HEADS, G=A_HEADS // A_KV_HEADS, dq=HEAD_DIM, dv=HEAD_DIM,
                        tq=min(512, S), tk=512)

        yb = _attention(qb.reshape(B, S, -1), kb.reshape(B, S, -1), vb.reshape(B, S, -1),
                        B=B, S=S, Hk=B_HEADS, G=1, dq=256, dv=B_V_DIM, tq=min(1024, S), tk=512)

        qk3 = _conv_silu(pc3, c_conv_w[l], row2(c_conv_b[l]))
        gate_b_row = jnp.zeros((1, 128), F32).at[0, MISC_GATE_LANE:MISC_GATE_LANE + 4 * C_HEADS].set(c_gate_b[l])
        hf, hr = _mlstm(qk3, pc3, misc.reshape(B, S, 128), gate_b_row)

        x2, h2, h2p, logits = _merge(ya.reshape(T, -1), yb.reshape(T, -1), hf.reshape(T, -1),
                                hr.reshape(T, -1), pc, x2, mod,
                                row2(head_out_gain[l]), w_out[l].astype(BF16), row2(post_mix_gain[l]),
                                row2(pre_ffn_gain[l]), _split_hi_lo(router_w[l]), S)

        idx8, gate8, mask = _router(logits, row2(router_bias[l]))
        pos, blk_e, nused, tail, next_e = _route_plan(idx8, mask, n_blocks)
        xs = _dispatch(h2p, pos.reshape(T // DISPATCH_TM, 1, DISPATCH_TM * TOP_K), tail,
                       n_blocks * MOE_BLK)
        ys = _experts(xs, blk_e, nused, next_e, exp_w_gate, exp_w_up, exp_w_down, l)
        x2 = _combine(pos.reshape(T // COMBINE_TM, 1, COMBINE_TM * TOP_K), h2, gate8, x2, mod, sh_w_gate[l].astype(BF16), sh_w_up[l].astype(BF16),
                      sh_w_down[l].astype(BF16), row2(post_ffn_gain[l]), ys, S)
    return x2.reshape(B, S, D)
```

```python
import functools

import numpy as np
import jax
import jax.numpy as jnp
from jax import lax
from jax.experimental import pallas as pl
from jax.experimental.pallas import tpu as pltpu

D_MODEL = 2048
GRID_W = 64
ROPE_THETA = 10000.0
EPS = 1e-6
HEAD_DIM = 128
A_HEADS, A_KV_HEADS = 6, 2
B_HEADS, B_Q_LORA, B_KV_LORA = 6, 384, 256
B_QK_NOPE, B_QK_ROPE, B_V_DIM = 128, 64, 128
C_HEADS, C_DQK, C_DV = 4, 128, 128
CONV_W = 5
MLSTM_CHUNK = 64
N_EXPERTS, TOP_K, EXPERT_FF, SHARED_FF = 64, 6, 512, 512
ROUTED_SCALE = 2.5
N_MOD = 6

COL_AQ, COL_AK, COL_AV = 0, 768, 1024
COL_BC = 1280
COL_MISC = 1920
COL_CV, COL_CO = 2048, 2560
COL_CQK = 3072
PROJ_PAD = 4096
MISC_GATE_LANE = 64
PC_CV, PC_CO, PC_CQK = 0, COL_CO - COL_CV, COL_CQK - COL_CV

MOE_BLK = 256
DISPATCH_TM = 512
COMBINE_TM = 256
MLSTM_BB = 2
VMEM_LIMIT = 56 * 1024 * 1024

BF16 = jnp.bfloat16
F32 = jnp.float32
LOG2_E = 1.4426950408889634


def _cparams(sem):
    return pltpu.CompilerParams(dimension_semantics=("arbitrary",) * len(sem),
                                vmem_limit_bytes=VMEM_LIMIT)


def _rms(x, gain):
    return x * lax.rsqrt(jnp.mean(x * x, axis=-1, keepdims=True) + EPS) * gain


def _dot(a, b):
    return jnp.dot(a, b, preferred_element_type=F32)


def _dot_t(a, b):
    return lax.dot_general(a, b, (((1,), (1,)), ((), ())), preferred_element_type=F32)


def _mod_kernel(c_ref, w_ref, b_ref, o_ref):
    c = c_ref[...]
    c_act = (c * jax.nn.sigmoid(c)).astype(BF16)
    o_ref[...] = _dot(c_act, w_ref[...].astype(BF16)) + b_ref[...]


def _modulation(c_pad, ada_w, ada_b):
    L, D, N = ada_w.shape
    M = c_pad.shape[0]
    tn = 1536
    return pl.pallas_call(
        _mod_kernel,
        grid=(L, N // tn),
        in_specs=[pl.BlockSpec((M, D), lambda l, j: (0, 0)),
                  pl.BlockSpec((None, D, tn), lambda l, j: (l, 0, j)),
                  pl.BlockSpec((None, 1, tn), lambda l, j: (l, 0, j))],
        out_specs=pl.BlockSpec((None, M, tn), lambda l, j: (l, 0, j)),
        out_shape=jax.ShapeDtypeStruct((L, M, N), F32),
        compiler_params=_cparams(("parallel", "parallel")),
        name="adaln_mod",
    )(c_pad, ada_w, ada_b.reshape(L, 1, N))


def _in_proj_kernel(x_ref, gain_ref, mod_ref, w_ref,
                    cosa_ref, sloa_ref, shia_ref, cosb_ref, slob_ref, shib_ref,
                    qg_ref, kg_ref, cqg_ref, ckvg_ref, wuq_ref, wukv_ref,
                    qa_ref, ka_ref, va_ref, qb_ref, kb_ref, vb_ref, pc_ref, misc_ref):
    h = _rms(x_ref[...], gain_ref[...])
    h = (h * (1.0 + mod_ref[1:2, :]) + mod_ref[0:1, :]).astype(BF16)
    pa = _dot(h, w_ref[:, COL_AQ:COL_BC])
    _gqa_prep(pa, (cosa_ref[...], sloa_ref[...], shia_ref[...]), qg_ref[...], kg_ref[...],
              qa_ref, ka_ref, va_ref)
    pb = _dot(h, w_ref[:, COL_BC:COL_CV])
    misc = pb[:, COL_MISC - COL_BC:]
    misc_ref[...] = misc
    _mla_prep(pb[:, :COL_MISC - COL_BC], misc, (cosb_ref[...], slob_ref[...], shib_ref[...]),
              cqg_ref[...], ckvg_ref[...], wuq_ref, wukv_ref, qb_ref, kb_ref, vb_ref)
    tn = 1024
    for j in range((PROJ_PAD - COL_CV) // tn):
        pc_ref[:, j * tn:(j + 1) * tn] = _dot(h, w_ref[:, COL_CV + j * tn: COL_CV + (j + 1) * tn])


def _in_proj(x2, gain, mod, w_in_p, tabs_a, tabs_b, q_gain, k_gain, cq_gain, ckv_gain, w_uq_p, w_ukv_p, S):
    T, D = x2.shape
    N = w_in_p.shape[1]
    tm = 256
    nb = S // tm
    row = lambda w: pl.BlockSpec((tm, w), lambda i: (i, 0))
    full = lambda a: pl.BlockSpec(a.shape, lambda i: (0,) * a.ndim)
    tab = pl.BlockSpec((tm, 128), lambda i: (i % nb, 0))
    qw, kw = A_HEADS * HEAD_DIM, A_KV_HEADS * HEAD_DIM
    bqw, bvw, cw = B_HEADS * 256, B_HEADS * B_V_DIM, PROJ_PAD - COL_CV
    widths = (qw, kw, kw, bqw, bqw, bvw)
    return pl.pallas_call(
        _in_proj_kernel,
        grid=(T // tm,),
        in_specs=[row(D), full(gain),
                  pl.BlockSpec((None, N_MOD, D), lambda i: (i // nb, 0, 0)),
                  pl.BlockSpec(w_in_p.shape, lambda i: (0, 0), pipeline_mode=pl.Buffered(1)),
                  tab, tab, tab, tab, tab, tab,
                  full(q_gain), full(k_gain), full(cq_gain), full(ckv_gain), full(w_uq_p), full(w_ukv_p)],
        out_specs=[row(w) for w in widths] + [row(cw), row(128)],
        out_shape=[jax.ShapeDtypeStruct((T, w), BF16) for w in widths]
                  + [jax.ShapeDtypeStruct((T, cw), F32), jax.ShapeDtypeStruct((T, 128), F32)],
        compiler_params=_cparams(("arbitrary",)),
        name="in_proj",
    )(x2, gain, mod, w_in_p, *tabs_a, *tabs_b, q_gain, k_gain, cq_gain, ckv_gain, w_uq_p, w_ukv_p)


def _rope128(y, cos, sin_lo, sin_hi):
    return y * cos + pltpu.roll(y, 96, 1) * sin_lo + pltpu.roll(y, 32, 1) * sin_hi


def _gqa_prep(pa, tabs, q_gain, k_gain, qo_ref, ko_ref, vo_ref):
    cos, slo, shi = tabs
    scale = HEAD_DIM ** -0.5 * LOG2_E
    for h in range(A_HEADS):
        sl = slice(h * HEAD_DIM, (h + 1) * HEAD_DIM)
        y = _rope128(_rms(pa[:, COL_AQ + h * HEAD_DIM: COL_AQ + (h + 1) * HEAD_DIM], q_gain), cos, slo, shi)
        qo_ref[:, sl] = (y * scale).astype(BF16)
    for h in range(A_KV_HEADS):
        sl = slice(h * HEAD_DIM, (h + 1) * HEAD_DIM)
        y = _rope128(_rms(pa[:, COL_AK + h * HEAD_DIM: COL_AK + (h + 1) * HEAD_DIM], k_gain), cos, slo, shi)
        ko_ref[:, sl] = y.astype(BF16)
    vo_ref[...] = pa[:, COL_AV:COL_BC].astype(BF16)


def _rope64x2(y, cos, sin_lo, sin_hi):
    return y * cos + pltpu.roll(y, 112, 1) * sin_lo + pltpu.roll(y, 16, 1) * sin_hi


def _mla_prep(bc, misc, tabs, cq_gain, ckv_gain, wuq_ref, wukv_ref, qo_ref, ko_ref, vo_ref):
    cos, slo, shi = tabs
    lane = lax.broadcasted_iota(jnp.int32, cos.shape, 1)
    first = lane < B_QK_ROPE
    scale = (B_QK_NOPE + B_QK_ROPE) ** -0.5 * LOG2_E
    nope_w = B_HEADS * B_QK_NOPE

    cq = _rms(bc[:, :B_Q_LORA], cq_gain).astype(BF16)
    q = _dot(cq, wuq_ref[...]) * scale
    for p in range(B_HEADS // 2):
        pair = _rope64x2(q[:, nope_w + p * 128: nope_w + (p + 1) * 128], cos, slo, shi)
        for half in range(2):
            h = 2 * p + half
            qo_ref[:, h * 256: h * 256 + 128] = q[:, h * 128:(h + 1) * 128].astype(BF16)
            keep = first if half == 0 else jnp.logical_not(first)
            qo_ref[:, h * 256 + 128: (h + 1) * 256] = jnp.where(keep, pair, 0.0).astype(BF16)

    kr = jnp.where(first, misc, 0.0)
    kr = kr + pltpu.roll(kr, B_QK_ROPE, 1)
    kr = _rope64x2(kr, cos, slo, shi).astype(BF16)
    ckv = _rms(bc[:, B_Q_LORA:], ckv_gain).astype(BF16)
    kv = _dot(ckv, wukv_ref[...])
    for h in range(B_HEADS):
        ko_ref[:, h * 256: h * 256 + 128] = kv[:, h * 128:(h + 1) * 128].astype(BF16)
        ko_ref[:, h * 256 + 128: (h + 1) * 256] = kr
    vo_ref[...] = kv[:, nope_w:].astype(BF16)


def _attn_kernel(q_ref, k_ref, v_ref, o_ref, m_scr, l_scr, acc_scr, *, G, dq, dv, tk, sub):
    tq = q_ref.shape[0]
    S = k_ref.shape[0]
    groups = [(g, r) for g in range(G) for r in range(tq // sub)]
    m_scr[...] = jnp.full(m_scr.shape, -jnp.inf, F32)
    l_scr[...] = jnp.zeros(l_scr.shape, F32)
    acc_scr[...] = jnp.zeros(acc_scr.shape, F32)

    stages = [(c, n) for c in range(S // tk) for n in range(len(groups))]

    def scores(stage):
        c, n = stage
        g, r = groups[n]
        return _dot_t(q_ref[r * sub:(r + 1) * sub, g * dq:(g + 1) * dq], k_ref[c * tk:(c + 1) * tk, :])

    s_next = scores(stages[0])
    for i, (c, n) in enumerate(stages):
        s = s_next
        if i + 1 < len(stages):
            s_next = scores(stages[i + 1])
        rows = slice(n * sub, (n + 1) * sub)
        blocks = [s[:, j * 128:(j + 1) * 128] for j in range(tk // 128)]
        m_blk = functools.reduce(jnp.maximum, blocks)
        m_old = m_scr[rows, :]
        m_new = jnp.maximum(m_old, jnp.max(m_blk, axis=-1, keepdims=True))
        alpha = jnp.exp2(m_old - m_new)
        p_blocks = [jnp.exp2(b - m_new) for b in blocks]
        l_scr[rows, :] = alpha * l_scr[rows, :] + functools.reduce(jnp.add, p_blocks)
        p = jnp.concatenate([b.astype(BF16) for b in p_blocks], axis=1)
        acc_scr[rows, :] = alpha * acc_scr[rows, :] + _dot(p, v_ref[c * tk:(c + 1) * tk, :])
        m_scr[rows, :] = m_new

    for n, (g, r) in enumerate(groups):
        rows = slice(n * sub, (n + 1) * sub)
        l = jnp.sum(l_scr[rows, :], axis=-1, keepdims=True)
        o_ref[r * sub:(r + 1) * sub, g * dv:(g + 1) * dv] = acc_scr[rows, :] / l


def _attention(q, k, v, *, B, S, Hk, G, dq, dv, tq, tk):
    assert dv == 128, "row statistics are kept 128 lanes wide to match the value width"
    kern = functools.partial(_attn_kernel, G=G, dq=dq, dv=dv, tk=tk, sub=256)
    M = G * tq
    return pl.pallas_call(
        kern,
        grid=(B, Hk, S // tq),
        in_specs=[pl.BlockSpec((None, tq, G * dq), lambda b, h, i: (b, i, h)),
                  pl.BlockSpec((None, S, dq), lambda b, h, i: (b, 0, h)),
                  pl.BlockSpec((None, S, dv), lambda b, h, i: (b, 0, h))],
        out_specs=pl.BlockSpec((None, tq, G * dv), lambda b, h, i: (b, i, h)),
        out_shape=jax.ShapeDtypeStruct((B, S, Hk * G * dv), F32),
        scratch_shapes=[pltpu.VMEM((M, 128), F32), pltpu.VMEM((M, 128), F32), pltpu.VMEM((M, dv), F32)],
        compiler_params=_cparams(("parallel", "parallel", "arbitrary")),
        name=f"attn_g{G}_d{dq}",
    )(q, k, v)


def _conv_kernel(x_ref, w_ref, b_ref, o_ref, *, k_scale_from):
    x = x_ref[...]
    S = x.shape[0]
    row = lax.broadcasted_iota(jnp.int32, x.shape, 0)
    pad = CONV_W // 2
    y = x * w_ref[pad:pad + 1, :] + b_ref[...]
    for j in range(CONV_W):
        d = j - pad
        if d == 0:
            continue
        shifted = pltpu.roll(x, (-d) % S, 0)
        valid = (row + d >= 0) & (row + d < S)
        y = y + jnp.where(valid, shifted, 0.0) * w_ref[j:j + 1, :]
    y = y * jax.nn.sigmoid(y)
    scale = jnp.where(pl.program_id(1) >= k_scale_from, C_DQK ** -0.5, 1.0)
    o_ref[...] = y * scale


def _conv_silu(pc3, conv_w, conv_b):
    B, S, _ = pc3.shape
    W = 2 * C_HEADS * C_DQK
    tc = 128
    kern = functools.partial(_conv_kernel, k_scale_from=(C_HEADS * C_DQK) // tc)
    return pl.pallas_call(
        kern,
        grid=(B, W // tc),
        in_specs=[pl.BlockSpec((None, S, tc), lambda b, j: (b, 0, PC_CQK // tc + j)),
                  pl.BlockSpec((CONV_W, tc), lambda b, j: (0, j)),
                  pl.BlockSpec((1, tc), lambda b, j: (0, j))],
        out_specs=pl.BlockSpec((None, S, tc), lambda b, j: (b, 0, j)),
        out_shape=jax.ShapeDtypeStruct((B, S, W), F32),
        compiler_params=_cparams(("parallel", "parallel")),
        name="conv_silu",
    )(pc3, conv_w, conv_b)


def _log_sigmoid(x):
    return jnp.minimum(x, 0.0) - jnp.log1p(jnp.exp(-jnp.abs(x)))


def _mlstm_kernel(qf_ref, kf_ref, vf_ref, gf_ref, qr_ref, kr_ref, vr_ref, gr_ref, gb_ref,
                  of_ref, or_ref, C_scr, n_scr, m_scr, *, BB):
    L = MLSTM_CHUNK

    @pl.when(pl.program_id(1) == 0)
    def _():
        C_scr[...] = jnp.zeros(C_scr.shape, F32)
        n_scr[...] = jnp.zeros(n_scr.shape, F32)
        m_scr[...] = jnp.zeros(m_scr.shape, F32)

    glane = lax.broadcasted_iota(jnp.int32, (L, 128), 1)
    jj = lax.broadcasted_iota(jnp.int32, (L, L), 0)
    ss = lax.broadcasted_iota(jnp.int32, (L, L), 1)
    eye = jj == ss
    neg_inf = jnp.float32(-jnp.inf)
    dirs = ((qf_ref, kf_ref, vf_ref, gf_ref, of_ref, ss <= jj, jj <= ss),
            (qr_ref, kr_ref, vr_ref, gr_ref, or_ref, ss >= jj, jj >= ss))

    chains = []
    for d, (q_ref, k_ref, v_ref, g_ref, o_ref, seen, seen_t) in enumerate(dirs):
        for bb in range(BB):
            gates = g_ref[bb] + gb_ref[...]
            lsig = _log_sigmoid(gates)
            for h in range(C_HEADS):
                c = dict(st=(d * BB + bb) * C_HEADS + h, sl=slice(h * C_DQK, (h + 1) * C_DQK), bb=bb,
                         q_ref=q_ref, k_ref=k_ref, v_ref=v_ref, o_ref=o_ref, seen=seen, seen_t=seen_t)
                i_lane = MISC_GATE_LANE + d * (2 * C_HEADS) + h
                c["i_col"] = jnp.sum(jnp.where(glane == i_lane, gates, 0.0), axis=1, keepdims=True)
                c["f_col"] = jnp.sum(jnp.where(glane == i_lane + C_HEADS, lsig, 0.0), axis=1, keepdims=True)
                chains.append(c)
    for c in chains:
        c["f_row"] = jnp.sum(jnp.where(eye, c["f_col"], 0.0), axis=0, keepdims=True)
        c["i_row"] = jnp.sum(jnp.where(eye, c["i_col"], 0.0), axis=0, keepdims=True)
        c["b_row"] = jnp.sum(jnp.where(c["seen_t"], c["f_col"], 0.0), axis=0, keepdims=True)
        c["g_tot"] = jnp.sum(c["f_col"], axis=0, keepdims=True)
    for c in chains:
        c["b_col"] = jnp.sum(jnp.where(c["seen"], c["f_row"], 0.0), axis=1, keepdims=True)
    for c in chains:
        c["m_prev"] = m_scr[c["st"]][:, 0:1]
        c["dmat"] = jnp.where(c["seen"], c["b_col"] - c["b_row"] + c["i_row"], neg_inf)
        c["m_inter"] = c["b_col"] + c["m_prev"]
        c["a_col"] = c["g_tot"] - c["b_col"] + c["i_col"]
    for c in chains:
        c["m_j"] = jnp.maximum(c["m_inter"], jnp.max(c["dmat"], axis=1, keepdims=True))
        c["m_new"] = jnp.maximum(c["g_tot"] + c["m_prev"], jnp.max(c["a_col"], axis=0, keepdims=True))
    for c in chains:
        c["q"] = c["q_ref"][c["bb"], :, c["sl"]]
        c["k"] = c["k_ref"][c["bb"], :, c["sl"]]
        c["qb"], c["kb"] = c["q"].astype(BF16), c["k"].astype(BF16)
        c["vb"] = c["v_ref"][c["bb"], :, c["sl"]].astype(BF16)
        c["qk"] = _dot_t(c["qb"], c["kb"])
    for c in chains:
        c["qC"] = _dot(c["qb"], C_scr[c["st"]].astype(BF16))
    for c in chains:
        c["s"] = c["qk"] * jnp.exp(c["dmat"] - c["m_j"])
        c["inter"] = jnp.exp(c["m_inter"] - c["m_j"])
        c["wk"] = jnp.exp(c["a_col"] - c["m_new"]) * c["k"]
        c["decay"] = jnp.exp(c["g_tot"] + c["m_prev"] - c["m_new"])
    for c in chains:
        c["sv"] = _dot(c["s"].astype(BF16), c["vb"])
    for c in chains:
        c["upd"] = lax.dot_general(c["wk"].astype(BF16), c["vb"], (((0,), (0,)), ((), ())),
                                   preferred_element_type=F32)
    for c in chains:
        n_prev = n_scr[c["st"]]
        den = (jnp.sum(c["s"], axis=1, keepdims=True)
               + c["inter"] * jnp.sum(c["q"] * n_prev, axis=1, keepdims=True))
        num = c["sv"] + c["inter"] * c["qC"]
        c["o_ref"][c["bb"], :, c["sl"]] = num / jnp.maximum(jnp.abs(den), jnp.exp(-c["m_j"]))
        n_scr[c["st"]] = c["decay"] * n_prev + jnp.sum(c["wk"], axis=0, keepdims=True)
    for c in chains:
        C_scr[c["st"]] = c["decay"] * C_scr[c["st"]] + c["upd"]
        m_scr[c["st"]] = jnp.broadcast_to(c["m_new"], m_scr.shape[1:])


def _mlstm(qk3, pc3, misc3, gate_b_row, BB=MLSTM_BB):
    B, S, _ = pc3.shape
    L = MLSTM_CHUNK
    nc = S // L
    W = C_HEADS * C_DQK
    n_chain = 2 * BB * C_HEADS
    fwd = lambda col: (lambda b, c: (b, c, col))
    rev = lambda col: (lambda b, c: (b, nc - 1 - c, col))
    blk = lambda w, imap: pl.BlockSpec((BB, L, w), imap)
    out = jax.ShapeDtypeStruct((B, S, W), F32)
    return pl.pallas_call(
        functools.partial(_mlstm_kernel, BB=BB),
        grid=(B // BB, nc),
        in_specs=[blk(W, fwd(0)), blk(W, fwd(1)), blk(W, fwd(PC_CV // W)), blk(128, fwd(0)),
                  blk(W, rev(0)), blk(W, rev(1)), blk(W, rev(PC_CV // W)), blk(128, rev(0)),
                  pl.BlockSpec((1, 128), lambda b, c: (0, 0))],
        out_specs=[blk(W, fwd(0)), blk(W, rev(0))],
        out_shape=[out, out],
        scratch_shapes=[pltpu.VMEM((n_chain, C_DQK, C_DV), F32),
                        pltpu.VMEM((n_chain, 1, C_DQK), F32),
                        pltpu.VMEM((n_chain, 1, 128), F32)],
        compiler_params=_cparams(("arbitrary", "arbitrary")),
        name="mlstm",
    )(qk3, qk3, pc3, misc3, qk3, qk3, pc3, misc3, gate_b_row)


ROW_TILE = 8
HALF_D = D_MODEL // 2
U32 = jnp.uint32
HI_MASK = 0xFFFF0000


def _pack_rows(y, o_ref):
    n = y.shape[0]
    for s in range(ROW_TILE):
        lo = y[:, s * 128:(s + 1) * 128].astype(BF16).astype(F32)
        hi = y[:, HALF_D + s * 128: HALF_D + (s + 1) * 128].astype(BF16).astype(F32)
        w = (lax.bitcast_convert_type(lo, U32) >> 16) | (lax.bitcast_convert_type(hi, U32) & U32(HI_MASK))
        o_ref[pl.ds(s, n, stride=ROW_TILE), :] = w


def _unpack_rows(ref, n):
    lo, hi = [], []
    for s in range(ROW_TILE):
        w = ref[pl.ds(s, n, stride=ROW_TILE), :]
        lo.append(lax.bitcast_convert_type(w << 16, F32))
        hi.append(lax.bitcast_convert_type(w & U32(HI_MASK), F32))
    return lo + hi


def _merge_kernel(ya_ref, yb_ref, hf_ref, hr_ref, co_ref, x_ref, mod_ref, hg_ref, wout_ref, pmg_ref,
                  pfg_ref, rw_ref, xo_ref, h2_ref, h2p_ref, lg_ref, y_scr):
    attn_heads = A_HEADS + B_HEADS
    for h in range(attn_heads + C_HEADS):
        sl = slice(h * HEAD_DIM, (h + 1) * HEAD_DIM)
        g = hg_ref[:, sl]
        if h < A_HEADS:
            y = _rms(ya_ref[:, sl], g)
        elif h < attn_heads:
            y = _rms(yb_ref[:, (h - A_HEADS) * HEAD_DIM:(h - A_HEADS + 1) * HEAD_DIM], g)
        else:
            cs = slice((h - attn_heads) * HEAD_DIM, (h - attn_heads + 1) * HEAD_DIM)
            y = _rms(hf_ref[:, cs] + hr_ref[:, cs], g) * jax.nn.sigmoid(co_ref[:, cs])
        y_scr[:, sl] = y.astype(BF16)
    y = _dot(y_scr[...], wout_ref[...])
    x_new = x_ref[...] + mod_ref[2:3, :] * _rms(y, pmg_ref[...])
    xo_ref[...] = x_new
    h2 = _rms(x_new, pfg_ref[...]) * (1.0 + mod_ref[4:5, :]) + mod_ref[3:4, :]
    h2_ref[...] = h2.astype(BF16)
    _pack_rows(h2, h2p_ref)
    h_hi = h2.astype(BF16)
    h_lo = (h2 - h_hi.astype(F32)).astype(BF16)
    a = _dot(h_hi, rw_ref[...])
    b = _dot(h_lo, rw_ref[...])
    E = N_EXPERTS
    lg_ref[...] = a[:, :E] + (a[:, E:] + b[:, :E]) + b[:, E:]


def _merge(ya, yb, hf, hr, pc, x2, mod, head_gain, w_out_b, pm_gain, pf_gain, router_w, S):
    T, D = x2.shape
    tm = 256
    nb = S // tm
    aw, cw = A_HEADS * HEAD_DIM, C_HEADS * C_DV
    row = lambda w: pl.BlockSpec((tm, w), lambda i: (i, 0))
    full = lambda a: pl.BlockSpec(a.shape, lambda i: (0,) * a.ndim)
    return pl.pallas_call(
        _merge_kernel,
        grid=(T // tm,),
        in_specs=[row(aw), row(aw), row(cw), row(cw),
                  pl.BlockSpec((tm, cw), lambda i: (i, PC_CO // cw)),
                  row(D),
                  pl.BlockSpec((None, N_MOD, D), lambda i: (i // nb, 0, 0)),
                  full(head_gain), full(w_out_b), full(pm_gain), full(pf_gain), full(router_w)],
        out_specs=[row(D), row(D), pl.BlockSpec((tm * ROW_TILE, 128), lambda i: (i, 0)),
                   row(N_EXPERTS)],
        out_shape=[jax.ShapeDtypeStruct((T, D), F32),
                   jax.ShapeDtypeStruct((T, D), BF16),
                   jax.ShapeDtypeStruct((T * ROW_TILE, 128), U32),
                   jax.ShapeDtypeStruct((T, N_EXPERTS), F32)],
        scratch_shapes=[pltpu.VMEM((tm, D), BF16)],
        compiler_params=_cparams(("parallel",)),
        name="merge_out_proj",
    )(ya, yb, hf, hr, pc, x2, mod, head_gain, w_out_b, pm_gain, pf_gain, router_w)


def _router_kernel(lg_ref, bias_ref, idx_ref, gate_ref, mask_ref):
    scores = jax.nn.sigmoid(lg_ref[...])
    sel = scores + bias_ref[...]
    lane = lax.broadcasted_iota(jnp.int32, scores.shape, 1).astype(F32)
    col = lax.broadcasted_iota(jnp.int32, idx_ref.shape, 1)
    idx = jnp.zeros(idx_ref.shape, F32)
    gate = jnp.zeros(gate_ref.shape, F32)
    mask = jnp.zeros(scores.shape, F32)
    for kk in range(TOP_K):
        mx = jnp.max(sel, axis=1, keepdims=True)
        am = jnp.min(jnp.where(sel == mx, lane, float(N_EXPERTS)), axis=1, keepdims=True)
        hit = lane == am
        sc = jnp.sum(jnp.where(hit, scores, 0.0), axis=1, keepdims=True)
        idx = jnp.where(col == kk, am, idx)
        gate = jnp.where(col == kk, sc, gate)
        mask = jnp.where(hit, 1.0, mask)
        sel = jnp.where(hit, -jnp.inf, sel)
    gate = gate / jnp.sum(gate, axis=1, keepdims=True) * ROUTED_SCALE
    idx_ref[...] = idx.astype(jnp.int32)
    gate_ref[...] = gate
    mask_ref[...] = mask.astype(jnp.int32)


def _router(logits, bias):
    T, E = logits.shape
    tm = 1024
    row = lambda w: pl.BlockSpec((tm, w), lambda i: (i, 0))
    return pl.pallas_call(
        _router_kernel,
        grid=(T // tm,),
        in_specs=[row(E), pl.BlockSpec((1, E), lambda i: (0, 0))],
        out_specs=[row(8), row(8), row(E)],
        out_shape=[jax.ShapeDtypeStruct((T, 8), jnp.int32),
                   jax.ShapeDtypeStruct((T, 8), F32),
                   jax.ShapeDtypeStruct((T, E), jnp.int32)],
        compiler_params=_cparams(("parallel",)),
        name="router_topk",
    )(logits, bias)


TAIL_BITS = MOE_BLK.bit_length() - 1


def _tail_copies(tail_ref, zero_scr, xs_ref, zsem, fn):
    for e in range(N_EXPERTS):
        start_row = tail_ref[2 * e]
        n_tail = tail_ref[2 * e + 1]
        for bit in range(TAIL_BITS):
            size = 1 << bit
            cur = start_row + (n_tail & ~(2 * size - 1))
            cur = pl.multiple_of(cur * ROW_TILE, ROW_TILE)

            @pl.when((n_tail & size) != 0)
            def _():
                fn(pltpu.make_async_copy(zero_scr.at[pl.ds(0, size * ROW_TILE), :],
                                         xs_ref.at[pl.ds(cur, size * ROW_TILE), :], zsem))


def _row_tile(ref, row):
    return ref.at[pl.ds(pl.multiple_of(row * ROW_TILE, ROW_TILE), ROW_TILE), :]


def _dispatch_kernel(tail_ref, pos_ref, hp_ref, h_ref, wsg_ref, wsu_ref, wsd_ref, xs_ref, ysh_ref,
                     zero_scr, sem, zsem, *, tm):
    i = pl.program_id(0)

    def start(r, c):
        src = _row_tile(hp_ref, r)
        for kk in range(TOP_K):
            pltpu.make_async_copy(src, _row_tile(xs_ref, pos_ref[0, r * TOP_K + kk]),
                                  sem).start(priority=kk % 2)
        return c

    lax.fori_loop(0, tm, start, 0)

    @pl.when(i == 0)
    def _():
        zero_scr[...] = jnp.zeros(zero_scr.shape, zero_scr.dtype)
        _tail_copies(tail_ref, zero_scr, xs_ref, zsem, lambda cp: cp.start())
        _tail_copies(tail_ref, zero_scr, xs_ref, zsem, lambda cp: cp.wait())

    h = h_ref[...]
    gte = _dot(h, wsg_ref[...])
    a = gte * jax.nn.sigmoid(gte) * _dot(h, wsu_ref[...])
    ysh_ref[...] = _dot(a.astype(BF16), wsd_ref[...])

    for kk in range(TOP_K):
        pltpu.make_async_copy(hp_ref, xs_ref.at[pl.ds(0, tm * ROW_TILE), :], sem).wait()


def _dispatch(h2p, h2, pos3, tail_info, ws_gate_b, ws_up_b, ws_down_b, P):
    T, D = h2.shape
    tm = pos3.shape[2] // TOP_K
    kern = functools.partial(_dispatch_kernel, tm=tm)
    full = lambda a: pl.BlockSpec(a.shape, lambda i, tail: (0,) * a.ndim)
    return pl.pallas_call(
        kern,
        grid_spec=pltpu.PrefetchScalarGridSpec(
            num_scalar_prefetch=1,
            grid=(T // tm,),
            in_specs=[pl.BlockSpec((None, 1, tm * TOP_K), lambda i, tail: (i, 0, 0),
                                   memory_space=pltpu.SMEM),
                      pl.BlockSpec((tm * ROW_TILE, 128), lambda i, tail: (i, 0)),
                      pl.BlockSpec((tm, D), lambda i, tail: (i, 0)),
                      full(ws_gate_b), full(ws_up_b), full(ws_down_b)],
            out_specs=[pl.BlockSpec(memory_space=pl.ANY),
                       pl.BlockSpec((tm, D), lambda i, tail: (i, 0))],
            scratch_shapes=[pltpu.VMEM((MOE_BLK // 2 * ROW_TILE, 128), U32),
                            pltpu.SemaphoreType.DMA(()), pltpu.SemaphoreType.DMA(())]),
        out_shape=[jax.ShapeDtypeStruct((P * ROW_TILE, 128), U32),
                   jax.ShapeDtypeStruct((T, D), F32)],
        compiler_params=_cparams(("arbitrary",)),
        name="moe_dispatch",
    )(tail_info, pos3, h2p, h2, ws_gate_b, ws_up_b, ws_down_b)


EXPERT_TILES = 2


def _expert_kernel(blk_e_ref, nused_ref, next_ref, xs_ref, wg_hbm, wu_hbm, wd_hbm, ys_ref,
                   wg_f, wu_f, wd_f, wg_b, wu_b, wd_b, sems, *, layer):
    step = pl.program_id(0)
    rows = MOE_BLK * ROW_TILE

    def fetch(ex):
        return [pltpu.make_async_copy(hbm.at[layer, ex], buf, sems.at[n])
                for n, (hbm, buf) in enumerate(((wg_hbm, wg_f), (wu_hbm, wu_f), (wd_hbm, wd_f)))]

    @pl.when(step == 0)
    def _():
        for cp in fetch(blk_e_ref[0]):
            cp.start(priority=1)

    for j in range(EXPERT_TILES):
        i = step * EXPERT_TILES + j
        e = blk_e_ref[i]
        e_prev = blk_e_ref[jnp.maximum(i - 1, 0)]
        used = i < nused_ref[0]
        first = used & ((i == 0) | (e != e_prev))
        x_view = xs_ref.at[pl.ds(j * rows, rows), :]
        y_view = ys_ref.at[pl.ds(j * rows, rows), :]

        def compute(switch, e=e, x_view=x_view, y_view=y_view):
            x = jnp.concatenate([c.astype(BF16) for c in _unpack_rows(x_view, MOE_BLK)], axis=1)
            if switch:
                for cp in fetch(e):
                    cp.wait()
                wg_b[...] = wg_f[...].astype(BF16)
            gte = _dot(x, wg_b[...])
            if switch:
                wu_b[...] = wu_f[...].astype(BF16)
            up = _dot(x, wu_b[...])
            if switch:
                wd_b[...] = wd_f[...].astype(BF16)
                e_next = next_ref[e]

                @pl.when(e_next >= 0)
                def _():
                    for cp in fetch(e_next):
                        cp.start(priority=1)
            a = gte * jax.nn.sigmoid(gte) * up
            _pack_rows(_dot(a.astype(BF16), wd_b[...]), y_view)

        pl.when(first)(functools.partial(compute, True))
        pl.when(used & jnp.logical_not(first))(functools.partial(compute, False))

        @pl.when(jnp.logical_not(used))
        def _(y_view=y_view):
            y_view[...] = jnp.zeros(y_view.shape, y_view.dtype)


def _experts(xs, blk_e, nused, next_e, w_gate, w_up, w_down, layer):
    P = xs.shape[0] // ROW_TILE
    D, F = w_gate.shape[-2:]
    NB = P // MOE_BLK
    assert NB % EXPERT_TILES == 0
    rows = EXPERT_TILES * MOE_BLK * ROW_TILE
    last = lambda s, nu: jnp.minimum(s, (nu[0] - 1) // EXPERT_TILES)
    hbm = pl.BlockSpec(memory_space=pl.ANY)
    return pl.pallas_call(
        functools.partial(_expert_kernel, layer=layer),
        grid_spec=pltpu.PrefetchScalarGridSpec(
            num_scalar_prefetch=3,
            grid=(NB // EXPERT_TILES,),
            in_specs=[pl.BlockSpec((rows, 128), lambda s, be, nu, nx: (last(s, nu), 0)),
                      hbm, hbm, hbm],
            out_specs=pl.BlockSpec((rows, 128), lambda s, be, nu, nx: (s, 0)),
            scratch_shapes=[pltpu.VMEM((D, F), F32), pltpu.VMEM((D, F), F32), pltpu.VMEM((F, D), F32),
                            pltpu.VMEM((D, F), BF16), pltpu.VMEM((D, F), BF16),
                            pltpu.VMEM((F, D), BF16), pltpu.SemaphoreType.DMA((3,))]),
        out_shape=jax.ShapeDtypeStruct((P * ROW_TILE, 128), U32),
        compiler_params=_cparams(("arbitrary",)),
        name="moe_experts",
    )(blk_e, nused, next_e, xs, w_gate, w_up, w_down)


def _combine_kernel(pos_ref, pos_next_ref, ysh_ref, gate_ref, x_ref, mod_ref, pg_ref, ys_ref,
                    xo_ref, gbuf, sems, *, tm):
    i = pl.program_id(0)
    slot = i % 2

    def gather(p_ref, s):
        def start(r, c):
            for kk in range(TOP_K):
                pltpu.make_async_copy(_row_tile(ys_ref, p_ref[0, r * TOP_K + kk]),
                                      _row_tile(gbuf.at[s, kk], r), sems.at[s]).start(priority=kk % 2)
            return c
        lax.fori_loop(0, tm, start, 0)

    @pl.when(i == 0)
    def _():
        gather(pos_ref, 0)

    for s in range(2):
        @pl.when((i + 1 < pl.num_programs(0)) & (slot == 1 - s))
        def _():
            gather(pos_next_ref, s)

    for kk in range(TOP_K):
        pltpu.make_async_copy(ys_ref.at[pl.ds(0, tm * ROW_TILE), :], gbuf.at[slot, kk],
                              sems.at[slot]).wait()
    RG = 64

    def routed(rg):
        rows = slice(rg * RG, (rg + 1) * RG)
        gate = gate_ref[rows, :]
        g_wide = [jnp.broadcast_to(gate[:, kk:kk + 1], (RG, 128)) for kk in range(TOP_K)]
        for s in range(ROW_TILE):
            lo = hi = None
            for kk in range(TOP_K):
                w = gbuf[slot, kk, pl.ds(rg * RG * ROW_TILE + s, RG, stride=ROW_TILE), :]
                t_lo = g_wide[kk] * lax.bitcast_convert_type(w << 16, F32)
                t_hi = g_wide[kk] * lax.bitcast_convert_type(w & U32(HI_MASK), F32)
                lo = t_lo if lo is None else lo + t_lo
                hi = t_hi if hi is None else hi + t_hi
            xo_ref[rows, s * 128:(s + 1) * 128] = lo
            xo_ref[rows, HALF_D + s * 128: HALF_D + (s + 1) * 128] = hi

    for rg in range(tm // RG):
        routed(rg)
    y = xo_ref[...] + ysh_ref[...]
    xo_ref[...] = x_ref[...] + mod_ref[5:6, :] * _rms(y, pg_ref[...])


def _combine(pos3, ysh, gate8, x2, mod, post_gain, ys, S):
    T, D = x2.shape
    tm = pos3.shape[2] // TOP_K
    nb = S // tm
    kern = functools.partial(_combine_kernel, tm=tm)
    row = lambda w: pl.BlockSpec((tm, w), lambda i: (i, 0))
    full = lambda a: pl.BlockSpec(a.shape, lambda i: (0,) * a.ndim)
    n_steps = T // tm
    return pl.pallas_call(
        kern,
        grid=(n_steps,),
        in_specs=[pl.BlockSpec((None, 1, tm * TOP_K), lambda i: (i, 0, 0), memory_space=pltpu.SMEM),
                  pl.BlockSpec((None, 1, tm * TOP_K), lambda i: (jnp.minimum(i + 1, n_steps - 1), 0, 0),
                               memory_space=pltpu.SMEM),
                  row(D), row(8), row(D),
                  pl.BlockSpec((None, N_MOD, D), lambda i: (i // nb, 0, 0)),
                  full(post_gain),
                  pl.BlockSpec(memory_space=pl.ANY)],
        out_specs=row(D),
        out_shape=jax.ShapeDtypeStruct((T, D), F32),
        scratch_shapes=[pltpu.VMEM((2, TOP_K, tm * ROW_TILE, 128), U32),
                        pltpu.SemaphoreType.DMA((2,))],
        compiler_params=_cparams(("arbitrary",)),
        name="moe_combine",
    )(pos3, pos3, ysh, gate8, x2, mod, post_gain, ys)


def _route_plan(idx8, mask, n_blocks):
    idx = idx8[:, :TOP_K]
    counts = jnp.sum(mask, axis=0)
    rank = jnp.cumsum(mask, axis=0) - mask
    padded = (counts + MOE_BLK - 1) // MOE_BLK * MOE_BLK
    pend = jnp.cumsum(padded)
    pstart = pend - padded
    pos = jnp.take_along_axis(pstart[None, :] + rank, idx, axis=1).astype(jnp.int32)
    blk_start = jnp.arange(n_blocks, dtype=jnp.int32) * MOE_BLK
    blk_e = jnp.minimum(jnp.sum(pend[None, :] <= blk_start[:, None], axis=1), N_EXPERTS - 1)
    nused = (pend[-1:] // MOE_BLK).astype(jnp.int32)
    tail = jnp.stack([pstart + counts, padded - counts], axis=1).reshape(-1).astype(jnp.int32)
    ids = jnp.where(counts > 0, jnp.arange(N_EXPERTS), N_EXPERTS)
    after = jnp.concatenate([lax.cummin(ids, reverse=True)[1:], jnp.full((1,), N_EXPERTS)])
    next_e = jnp.where(after < N_EXPERTS, after, -1).astype(jnp.int32)
    return pos, blk_e.astype(jnp.int32), nused, tail, next_e


def _rope_tables(S, rot_dim):
    rows = S // GRID_W
    row = jnp.repeat(jnp.arange(rows, dtype=F32), GRID_W)
    col = jnp.tile(jnp.arange(GRID_W, dtype=F32), rows)
    axis_dim = rot_dim // 2
    inv_freq = ROPE_THETA ** (-jnp.arange(0, axis_dim, 2, dtype=F32) / axis_dim)
    ang_r = row[:, None] * inv_freq
    ang_c = col[:, None] * inv_freq
    z = jnp.zeros_like(ang_r)
    cos = jnp.concatenate([jnp.cos(ang_r)] * 2 + [jnp.cos(ang_c)] * 2, axis=1)
    sin_lo = jnp.concatenate([-jnp.sin(ang_r), z, -jnp.sin(ang_c), z], axis=1)
    sin_hi = jnp.concatenate([z, jnp.sin(ang_r), z, jnp.sin(ang_c)], axis=1)
    reps = 128 // rot_dim
    return tuple(jnp.tile(t, (1, reps)) for t in (cos, sin_lo, sin_hi))


def _prep_w_in(w):
    widths = (768, 256, 256, B_Q_LORA, B_KV_LORA, B_QK_ROPE, 512, 512, 512, 512, 4 * C_HEADS)
    offs = np.cumsum(widths)[:-1].tolist()
    a_q, a_k, a_v, b_cq, b_ckv, b_kr, c_q, c_k, c_v, c_o, c_g = jnp.split(w, offs, axis=-1)
    pad = jnp.zeros(w.shape[:-1] + (128 - B_QK_ROPE - 4 * C_HEADS,), w.dtype)
    out = jnp.concatenate([a_q, a_k, a_v, b_cq, b_ckv, b_kr, c_g, pad, c_v, c_o, c_q, c_k], axis=-1)
    assert out.shape[-1] == PROJ_PAD
    return out.astype(BF16)


def _split_hi_lo(w):
    hi = w.astype(BF16)
    lo = (w - hi.astype(F32)).astype(BF16)
    return jnp.concatenate([hi, lo], axis=1)


def _heads_split(w, n_heads, first):
    K = w.shape[0]
    w3 = w.reshape(K, n_heads, -1)
    return jnp.concatenate([w3[:, :, :first].reshape(K, -1), w3[:, :, first:].reshape(K, -1)],
                           axis=1).astype(BF16)


def kernel(x, c, ada_w, ada_b, pre_mix_gain, w_in, a_q_gain, a_k_gain, b_cq_gain, b_ckv_gain, w_uq, w_ukv, c_conv_w, c_conv_b, c_gate_b, head_out_gain, w_out, post_mix_gain, pre_ffn_gain, router_w, router_bias, exp_w_gate, exp_w_up, exp_w_down, sh_w_gate, sh_w_up, sh_w_down, post_ffn_gain):
    B, S, D = x.shape
    T = B * S
    L = ada_w.shape[0]
    n_blocks = T * TOP_K // MOE_BLK + N_EXPERTS
    row2 = lambda v: v.reshape(1, -1)

    tabs_a = _rope_tables(S, HEAD_DIM)
    tabs_b = _rope_tables(S, B_QK_ROPE)
    c_pad = jnp.zeros((8, D), F32).at[:B].set(c)
    mods = _modulation(c_pad, ada_w, ada_b)[:, :B].reshape(L, B, N_MOD, D)

    x2 = x.reshape(T, D)
    for l in range(L):
        mod = mods[l]
        qa, ka, va, qb, kb, vb, pc, misc = _in_proj(
            x2, row2(pre_mix_gain[l]), mod, _prep_w_in(w_in[l]), tabs_a, tabs_b,
            row2(a_q_gain[l]), row2(a_k_gain[l]), row2(b_cq_gain[l]), row2(b_ckv_gain[l]),
            _heads_split(w_uq[l], B_HEADS, B_QK_NOPE), _heads_split(w_ukv[l], B_HEADS, B_QK_NOPE), S)
        pc3 = pc.reshape(B, S, -1)

        ya = _attention(qa.reshape(B, S, -1), ka.reshape(B, S, -1), va.reshape(B, S, -1),
                        B=B, S=S, Hk=A_KV_HEADS, G=A_HEADS // A_KV_HEADS, dq=HEAD_DIM, dv=HEAD_DIM,
                        tq=min(512, S), tk=512)

        yb = _attention(qb.reshape(B, S, -1), kb.reshape(B, S, -1), vb.reshape(B, S, -1),
                        B=B, S=S, Hk=B_HEADS, G=1, dq=256, dv=B_V_DIM, tq=min(1024, S), tk=512)

        qk3 = _conv_silu(pc3, c_conv_w[l], row2(c_conv_b[l]))
        gate_b_row = jnp.zeros((1, 128), F32).at[0, MISC_GATE_LANE:MISC_GATE_LANE + 4 * C_HEADS].set(c_gate_b[l])
        hf, hr = _mlstm(qk3, pc3, misc.reshape(B, S, 128), gate_b_row)

        x2, h2, h2p, logits = _merge(ya.reshape(T, -1), yb.reshape(T, -1), hf.reshape(T, -1),
                                hr.reshape(T, -1), pc, x2, mod,
                                row2(head_out_gain[l]), w_out[l].astype(BF16), row2(post_mix_gain[l]),
                                row2(pre_ffn_gain[l]), _split_hi_lo(router_w[l]), S)

        idx8, gate8, mask = _router(logits, row2(router_bias[l]))
        pos, blk_e, nused, tail, next_e = _route_plan(idx8, mask, n_blocks)
        xs, ysh = _dispatch(h2p, h2, pos.reshape(T // DISPATCH_TM, 1, DISPATCH_TM * TOP_K), tail,
                            sh_w_gate[l].astype(BF16), sh_w_up[l].astype(BF16),
                            sh_w_down[l].astype(BF16), n_blocks * MOE_BLK)
        ys = _experts(xs, blk_e, nused, next_e, exp_w_gate, exp_w_up, exp_w_down, l)
        x2 = _combine(pos.reshape(T // COMBINE_TM, 1, COMBINE_TM * TOP_K), ysh, gate8, x2, mod,
                      row2(post_ffn_gain[l]), ys, S)
    return x2.reshape(B, S, D)
```

```python
import functools

import numpy as np
import jax
import jax.numpy as jnp
from jax import lax
from jax.experimental import pallas as pl
from jax.experimental.pallas import tpu as pltpu

D_MODEL = 2048
GRID_W = 64
ROPE_THETA = 10000.0
EPS = 1e-6
HEAD_DIM = 128
A_HEADS, A_KV_HEADS = 6, 2
B_HEADS, B_Q_LORA, B_KV_LORA = 6, 384, 256
B_QK_NOPE, B_QK_ROPE, B_V_DIM = 128, 64, 128
C_HEADS, C_DQK, C_DV = 4, 128, 128
CONV_W = 5
MLSTM_CHUNK = 64
N_EXPERTS, TOP_K, EXPERT_FF, SHARED_FF = 64, 6, 512, 512
ROUTED_SCALE = 2.5
N_MOD = 6

COL_AQ, COL_AK, COL_AV = 0, 768, 1024
COL_BC = 1280
COL_MISC = 1920
COL_CV, COL_CO = 2048, 2560
COL_CQK = 3072
PROJ_PAD = 4096
MISC_GATE_LANE = 64
PC_CV, PC_CO = 0, COL_CO - COL_CV

MOE_BLK = 256
DISPATCH_TM = 512
COMBINE_TM = 256
MLSTM_BB = 2
VMEM_LIMIT = 56 * 1024 * 1024

BF16 = jnp.bfloat16
F32 = jnp.float32
LOG2_E = 1.4426950408889634


def _cparams(sem):
    return pltpu.CompilerParams(dimension_semantics=("arbitrary",) * len(sem),
                                vmem_limit_bytes=VMEM_LIMIT)


def _rms(x, gain):
    return x * lax.rsqrt(jnp.mean(x * x, axis=-1, keepdims=True) + EPS) * gain


def _dot(a, b):
    return jnp.dot(a, b, preferred_element_type=F32)


def _dot_t(a, b):
    return lax.dot_general(a, b, (((1,), (1,)), ((), ())), preferred_element_type=F32)


def _mod_kernel(c_ref, w_ref, b_ref, o_ref):
    c = c_ref[...]
    c_act = (c * jax.nn.sigmoid(c)).astype(BF16)
    o_ref[...] = _dot(c_act, w_ref[...].astype(BF16)) + b_ref[...]


def _modulation(c_pad, ada_w, ada_b):
    L, D, N = ada_w.shape
    M = c_pad.shape[0]
    tn = 1536
    return pl.pallas_call(
        _mod_kernel,
        grid=(L, N // tn),
        in_specs=[pl.BlockSpec((M, D), lambda l, j: (0, 0)),
                  pl.BlockSpec((None, D, tn), lambda l, j: (l, 0, j)),
                  pl.BlockSpec((None, 1, tn), lambda l, j: (l, 0, j))],
        out_specs=pl.BlockSpec((None, M, tn), lambda l, j: (l, 0, j)),
        out_shape=jax.ShapeDtypeStruct((L, M, N), F32),
        compiler_params=_cparams(("parallel", "parallel")),
        name="adaln_mod",
    )(c_pad, ada_w, ada_b.reshape(L, 1, N))


def _conv_silu(main, prev, nxt, w_ref, b_ref):
    tm, C = main.shape
    row = lax.broadcasted_iota(jnp.int32, main.shape, 0)
    pad = CONV_W // 2
    y = main * w_ref[pad:pad + 1, :] + b_ref[...]
    for j in range(CONV_W):
        d = j - pad
        if d == 0:
            continue
        shifted = pltpu.roll(main, (-d) % tm, 0)
        for e in range(abs(d)):
            if d < 0:
                shifted = jnp.where(row == e, prev[8 + d + e: 9 + d + e, :], shifted)
            else:
                shifted = jnp.where(row == tm - d + e, nxt[e:e + 1, :], shifted)
        y = y + shifted * w_ref[j:j + 1, :]
    y = y * jax.nn.sigmoid(y)
    lane = lax.broadcasted_iota(jnp.int32, main.shape, 1)
    return y * jnp.where(lane >= C // 2, C_DQK ** -0.5, 1.0)


def _in_proj_kernel(x_ref, xp_ref, xn_ref, gain_ref, mod_ref, w_ref,
                    cosa_ref, sloa_ref, shia_ref, cosb_ref, slob_ref, shib_ref,
                    qg_ref, kg_ref, cqg_ref, ckvg_ref, wuq_ref, wukv_ref, cw_ref, cb_ref,
                    qa_ref, ka_ref, va_ref, qb_ref, kb_ref, vb_ref, pc_ref, qk_ref, misc_ref, *, nb):
    def premix(x):
        y = _rms(x, gain_ref[...])
        return (y * (1.0 + mod_ref[1:2, :]) + mod_ref[0:1, :]).astype(BF16)

    h = premix(x_ref[...])
    pa = _dot(h, w_ref[:, COL_AQ:COL_BC])
    _gqa_prep(pa, (cosa_ref[...], sloa_ref[...], shia_ref[...]), qg_ref[...], kg_ref[...],
              qa_ref, ka_ref, va_ref)
    pb = _dot(h, w_ref[:, COL_BC:COL_CV])
    misc = pb[:, COL_MISC - COL_BC:]
    misc_ref[...] = misc
    _mla_prep(pb[:, :COL_MISC - COL_BC], misc, (cosb_ref[...], slob_ref[...], shib_ref[...]),
              cqg_ref[...], ckvg_ref[...], wuq_ref, wukv_ref, qb_ref, kb_ref, vb_ref)
    pc_ref[...] = _dot(h, w_ref[:, COL_CV:COL_CQK])
    w_qk = w_ref[:, COL_CQK:]
    i = pl.program_id(0)
    halo = premix(jnp.concatenate([xp_ref[...], xn_ref[...]], axis=0))
    halo = _dot(halo, w_qk)
    prev = jnp.where(i % nb == 0, 0.0, halo[:8, :])
    nxt = jnp.where(i % nb == nb - 1, 0.0, halo[8:, :])
    qk_ref[...] = _conv_silu(_dot(h, w_qk), prev, nxt, cw_ref, cb_ref)


def _in_proj(x2, gain, mod, w_in_p, tabs_a, tabs_b, q_gain, k_gain, cq_gain, ckv_gain, w_uq_p, w_ukv_p,
             conv_w, conv_b, S):
    T, D = x2.shape
    tm = 256
    nb = S // tm
    n8 = tm // 8
    row = lambda w: pl.BlockSpec((tm, w), lambda i: (i, 0))
    full = lambda a: pl.BlockSpec(a.shape, lambda i: (0,) * a.ndim)
    tab = pl.BlockSpec((tm, 128), lambda i: (i % nb, 0))
    qw, kw = A_HEADS * HEAD_DIM, A_KV_HEADS * HEAD_DIM
    bqw, bvw = B_HEADS * 256, B_HEADS * B_V_DIM
    cvo, cqk = COL_CQK - COL_CV, PROJ_PAD - COL_CQK
    widths = (qw, kw, kw, bqw, bqw, bvw)
    return pl.pallas_call(
        functools.partial(_in_proj_kernel, nb=nb),
        grid=(T // tm,),
        in_specs=[row(D),
                  pl.BlockSpec((8, D), lambda i: (jnp.maximum(i * n8 - 1, 0), 0)),
                  pl.BlockSpec((8, D), lambda i: (jnp.minimum((i + 1) * n8, T // 8 - 1), 0)),
                  full(gain),
                  pl.BlockSpec((None, N_MOD, D), lambda i: (i // nb, 0, 0)),
                  pl.BlockSpec(w_in_p.shape, lambda i: (0, 0), pipeline_mode=pl.Buffered(1)),
                  tab, tab, tab, tab, tab, tab,
                  full(q_gain), full(k_gain), full(cq_gain), full(ckv_gain), full(w_uq_p), full(w_ukv_p),
                  full(conv_w), full(conv_b)],
        out_specs=[row(w) for w in widths] + [row(cvo), row(cqk), row(128)],
        out_shape=[jax.ShapeDtypeStruct((T, w), BF16) for w in widths]
                  + [jax.ShapeDtypeStruct((T, cvo), F32), jax.ShapeDtypeStruct((T, cqk), F32),
                     jax.ShapeDtypeStruct((T, 128), F32)],
        compiler_params=_cparams(("arbitrary",)),
        name="in_proj",
    )(x2, x2, x2, gain, mod, w_in_p, *tabs_a, *tabs_b, q_gain, k_gain, cq_gain, ckv_gain, w_uq_p, w_ukv_p,
      conv_w, conv_b)


def _rope128(y, cos, sin_lo, sin_hi):
    return y * cos + pltpu.roll(y, 96, 1) * sin_lo + pltpu.roll(y, 32, 1) * sin_hi


def _gqa_prep(pa, tabs, q_gain, k_gain, qo_ref, ko_ref, vo_ref):
    cos, slo, shi = tabs
    scale = HEAD_DIM ** -0.5 * LOG2_E
    for h in range(A_HEADS):
        sl = slice(h * HEAD_DIM, (h + 1) * HEAD_DIM)
        y = _rope128(_rms(pa[:, COL_AQ + h * HEAD_DIM: COL_AQ + (h + 1) * HEAD_DIM], q_gain), cos, slo, shi)
        qo_ref[:, sl] = (y * scale).astype(BF16)
    for h in range(A_KV_HEADS):
        sl = slice(h * HEAD_DIM, (h + 1) * HEAD_DIM)
        y = _rope128(_rms(pa[:, COL_AK + h * HEAD_DIM: COL_AK + (h + 1) * HEAD_DIM], k_gain), cos, slo, shi)
        ko_ref[:, sl] = y.astype(BF16)
    vo_ref[...] = pa[:, COL_AV:COL_BC].astype(BF16)


def _rope64x2(y, cos, sin_lo, sin_hi):
    return y * cos + pltpu.roll(y, 112, 1) * sin_lo + pltpu.roll(y, 16, 1) * sin_hi


def _mla_prep(bc, misc, tabs, cq_gain, ckv_gain, wuq_ref, wukv_ref, qo_ref, ko_ref, vo_ref):
    cos, slo, shi = tabs
    lane = lax.broadcasted_iota(jnp.int32, cos.shape, 1)
    first = lane < B_QK_ROPE
    scale = (B_QK_NOPE + B_QK_ROPE) ** -0.5 * LOG2_E
    nope_w = B_HEADS * B_QK_NOPE

    cq = _rms(bc[:, :B_Q_LORA], cq_gain).astype(BF16)
    q = _dot(cq, wuq_ref[...]) * scale
    for p in range(B_HEADS // 2):
        pair = _rope64x2(q[:, nope_w + p * 128: nope_w + (p + 1) * 128], cos, slo, shi)
        for half in range(2):
            h = 2 * p + half
            qo_ref[:, h * 256: h * 256 + 128] = q[:, h * 128:(h + 1) * 128].astype(BF16)
            keep = first if half == 0 else jnp.logical_not(first)
            qo_ref[:, h * 256 + 128: (h + 1) * 256] = jnp.where(keep, pair, 0.0).astype(BF16)

    kr = jnp.where(first, misc, 0.0)
    kr = kr + pltpu.roll(kr, B_QK_ROPE, 1)
    kr = _rope64x2(kr, cos, slo, shi).astype(BF16)
    ckv = _rms(bc[:, B_Q_LORA:], ckv_gain).astype(BF16)
    kv = _dot(ckv, wukv_ref[...])
    for h in range(B_HEADS):
        ko_ref[:, h * 256: h * 256 + 128] = kv[:, h * 128:(h + 1) * 128].astype(BF16)
        ko_ref[:, h * 256 + 128: (h + 1) * 256] = kr
    vo_ref[...] = kv[:, nope_w:].astype(BF16)


def _attn_kernel(q_ref, k_ref, v_ref, o_ref, m_scr, l_scr, acc_scr, *, G, dq, dv, tk, sub):
    tq = q_ref.shape[0]
    S = k_ref.shape[0]
    groups = [(g, r) for g in range(G) for r in range(tq // sub)]
    m_scr[...] = jnp.full(m_scr.shape, -jnp.inf, F32)
    l_scr[...] = jnp.zeros(l_scr.shape, F32)
    acc_scr[...] = jnp.zeros(acc_scr.shape, F32)

    stages = [(c, n) for c in range(S // tk) for n in range(len(groups))]

    def scores(stage):
        c, n = stage
        g, r = groups[n]
        return _dot_t(q_ref[r * sub:(r + 1) * sub, g * dq:(g + 1) * dq], k_ref[c * tk:(c + 1) * tk, :])

    s_next = scores(stages[0])
    for i, (c, n) in enumerate(stages):
        s = s_next
        if i + 1 < len(stages):
            s_next = scores(stages[i + 1])
        rows = slice(n * sub, (n + 1) * sub)
        blocks = [s[:, j * 128:(j + 1) * 128] for j in range(tk // 128)]
        m_blk = functools.reduce(jnp.maximum, blocks)
        m_old = m_scr[rows, :]
        m_new = jnp.maximum(m_old, jnp.max(m_blk, axis=-1, keepdims=True))
        alpha = jnp.exp2(m_old - m_new)
        p_blocks = [jnp.exp2(b - m_new) for b in blocks]
        l_scr[rows, :] = alpha * l_scr[rows, :] + functools.reduce(jnp.add, p_blocks)
        p = jnp.concatenate([b.astype(BF16) for b in p_blocks], axis=1)
        acc_scr[rows, :] = alpha * acc_scr[rows, :] + _dot(p, v_ref[c * tk:(c + 1) * tk, :])
        m_scr[rows, :] = m_new

    for n, (g, r) in enumerate(groups):
        rows = slice(n * sub, (n + 1) * sub)
        l = jnp.sum(l_scr[rows, :], axis=-1, keepdims=True)
        o_ref[r * sub:(r + 1) * sub, g * dv:(g + 1) * dv] = (acc_scr[rows, :] / l).astype(o_ref.dtype)


def _attention(q, k, v, *, B, S, Hk, G, dq, dv, tq, tk):
    assert dv == 128, "row statistics are kept 128 lanes wide to match the value width"
    kern = functools.partial(_attn_kernel, G=G, dq=dq, dv=dv, tk=tk, sub=256)
    M = G * tq
    return pl.pallas_call(
        kern,
        grid=(B, Hk, S // tq),
        in_specs=[pl.BlockSpec((None, tq, G * dq), lambda b, h, i: (b, i, h)),
                  pl.BlockSpec((None, S, dq), lambda b, h, i: (b, 0, h)),
                  pl.BlockSpec((None, S, dv), lambda b, h, i: (b, 0, h))],
        out_specs=pl.BlockSpec((None, tq, G * dv), lambda b, h, i: (b, i, h)),
        out_shape=jax.ShapeDtypeStruct((B, S, Hk * G * dv), BF16),
        scratch_shapes=[pltpu.VMEM((M, 128), F32), pltpu.VMEM((M, 128), F32), pltpu.VMEM((M, dv), F32)],
        compiler_params=_cparams(("parallel", "parallel", "arbitrary")),
        name=f"attn_g{G}_d{dq}",
    )(q, k, v)


def _log_sigmoid(x):
    return jnp.minimum(x, 0.0) - jnp.log1p(jnp.exp(-jnp.abs(x)))


def _mlstm_kernel(qf_ref, kf_ref, vf_ref, gf_ref, qr_ref, kr_ref, vr_ref, gr_ref, gb_ref,
                  of_ref, or_ref, C_scr, n_scr, m_scr, *, BB):
    L = MLSTM_CHUNK

    @pl.when(pl.program_id(1) == 0)
    def _():
        C_scr[...] = jnp.zeros(C_scr.shape, F32)
        n_scr[...] = jnp.zeros(n_scr.shape, F32)
        m_scr[...] = jnp.zeros(m_scr.shape, F32)

    glane = lax.broadcasted_iota(jnp.int32, (L, 128), 1)
    jj = lax.broadcasted_iota(jnp.int32, (L, L), 0)
    ss = lax.broadcasted_iota(jnp.int32, (L, L), 1)
    eye = jj == ss
    neg_inf = jnp.float32(-jnp.inf)
    dirs = ((qf_ref, kf_ref, vf_ref, gf_ref, of_ref, ss <= jj, jj <= ss),
            (qr_ref, kr_ref, vr_ref, gr_ref, or_ref, ss >= jj, jj >= ss))

    chains = []
    for d, (q_ref, k_ref, v_ref, g_ref, o_ref, seen, seen_t) in enumerate(dirs):
        for bb in range(BB):
            gates = g_ref[bb] + gb_ref[...]
            lsig = _log_sigmoid(gates)
            for h in range(C_HEADS):
                c = dict(st=(d * BB + bb) * C_HEADS + h, sl=slice(h * C_DQK, (h + 1) * C_DQK), bb=bb,
                         q_ref=q_ref, k_ref=k_ref, v_ref=v_ref, o_ref=o_ref, seen=seen, seen_t=seen_t)
                i_lane = MISC_GATE_LANE + d * (2 * C_HEADS) + h
                c["i_col"] = jnp.sum(jnp.where(glane == i_lane, gates, 0.0), axis=1, keepdims=True)
                c["f_col"] = jnp.sum(jnp.where(glane == i_lane + C_HEADS, lsig, 0.0), axis=1, keepdims=True)
                chains.append(c)
    for c in chains:
        c["f_row"] = jnp.sum(jnp.where(eye, c["f_col"], 0.0), axis=0, keepdims=True)
        c["i_row"] = jnp.sum(jnp.where(eye, c["i_col"], 0.0), axis=0, keepdims=True)
        c["b_row"] = jnp.sum(jnp.where(c["seen_t"], c["f_col"], 0.0), axis=0, keepdims=True)
        c["g_tot"] = jnp.sum(c["f_col"], axis=0, keepdims=True)
    for c in chains:
        c["b_col"] = jnp.sum(jnp.where(c["seen"], c["f_row"], 0.0), axis=1, keepdims=True)
    for c in chains:
        c["m_prev"] = m_scr[c["st"]][:, 0:1]
        c["dmat"] = jnp.where(c["seen"], c["b_col"] - c["b_row"] + c["i_row"], neg_inf)
        c["m_inter"] = c["b_col"] + c["m_prev"]
        c["a_col"] = c["g_tot"] - c["b_col"] + c["i_col"]
    for c in chains:
        c["m_j"] = jnp.maximum(c["m_inter"], jnp.max(c["dmat"], axis=1, keepdims=True))
        c["m_new"] = jnp.maximum(c["g_tot"] + c["m_prev"], jnp.max(c["a_col"], axis=0, keepdims=True))
    for c in chains:
        c["q"] = c["q_ref"][c["bb"], :, c["sl"]]
        c["k"] = c["k_ref"][c["bb"], :, c["sl"]]
        c["qb"], c["kb"] = c["q"].astype(BF16), c["k"].astype(BF16)
        c["vb"] = c["v_ref"][c["bb"], :, c["sl"]].astype(BF16)
        c["qk"] = _dot_t(c["qb"], c["kb"])
    for c in chains:
        c["qC"] = _dot(c["qb"], C_scr[c["st"]].astype(BF16))
    for c in chains:
        c["s"] = c["qk"] * jnp.exp(c["dmat"] - c["m_j"])
        c["inter"] = jnp.exp(c["m_inter"] - c["m_j"])
        c["wk"] = jnp.exp(c["a_col"] - c["m_new"]) * c["k"]
        c["decay"] = jnp.exp(c["g_tot"] + c["m_prev"] - c["m_new"])
    for c in chains:
        c["sv"] = _dot(c["s"].astype(BF16), c["vb"])
    for c in chains:
        c["upd"] = lax.dot_general(c["wk"].astype(BF16), c["vb"], (((0,), (0,)), ((), ())),
                                   preferred_element_type=F32)
    for c in chains:
        n_prev = n_scr[c["st"]]
        den = (jnp.sum(c["s"], axis=1, keepdims=True)
               + c["inter"] * jnp.sum(c["q"] * n_prev, axis=1, keepdims=True))
        num = c["sv"] + c["inter"] * c["qC"]
        c["o_ref"][c["bb"], :, c["sl"]] = num / jnp.maximum(jnp.abs(den), jnp.exp(-c["m_j"]))
        n_scr[c["st"]] = c["decay"] * n_prev + jnp.sum(c["wk"], axis=0, keepdims=True)
    for c in chains:
        C_scr[c["st"]] = c["decay"] * C_scr[c["st"]] + c["upd"]
        m_scr[c["st"]] = jnp.broadcast_to(c["m_new"], m_scr.shape[1:])


def _mlstm(qk3, pc3, misc3, gate_b_row, BB=MLSTM_BB):
    B, S, _ = pc3.shape
    L = MLSTM_CHUNK
    nc = S // L
    W = C_HEADS * C_DQK
    n_chain = 2 * BB * C_HEADS
    fwd = lambda col: (lambda b, c: (b, c, col))
    rev = lambda col: (lambda b, c: (b, nc - 1 - c, col))
    blk = lambda w, imap: pl.BlockSpec((BB, L, w), imap)
    out = jax.ShapeDtypeStruct((B, S, W), F32)
    return pl.pallas_call(
        functools.partial(_mlstm_kernel, BB=BB),
        grid=(B // BB, nc),
        in_specs=[blk(W, fwd(0)), blk(W, fwd(1)), blk(W, fwd(PC_CV // W)), blk(128, fwd(0)),
                  blk(W, rev(0)), blk(W, rev(1)), blk(W, rev(PC_CV // W)), blk(128, rev(0)),
                  pl.BlockSpec((1, 128), lambda b, c: (0, 0))],
        out_specs=[blk(W, fwd(0)), blk(W, rev(0))],
        out_shape=[out, out],
        scratch_shapes=[pltpu.VMEM((n_chain, C_DQK, C_DV), F32),
                        pltpu.VMEM((n_chain, 1, C_DQK), F32),
                        pltpu.VMEM((n_chain, 1, 128), F32)],
        compiler_params=_cparams(("arbitrary", "arbitrary")),
        name="mlstm",
    )(qk3, qk3, pc3, misc3, qk3, qk3, pc3, misc3, gate_b_row)


ROW_TILE = 8
HALF_D = D_MODEL // 2
U32 = jnp.uint32
HI_MASK = 0xFFFF0000


def _pack_rows(y, o_ref):
    n = y.shape[0]
    for s in range(ROW_TILE):
        lo = y[:, s * 128:(s + 1) * 128].astype(BF16).astype(F32)
        hi = y[:, HALF_D + s * 128: HALF_D + (s + 1) * 128].astype(BF16).astype(F32)
        w = (lax.bitcast_convert_type(lo, U32) >> 16) | (lax.bitcast_convert_type(hi, U32) & U32(HI_MASK))
        o_ref[pl.ds(s, n, stride=ROW_TILE), :] = w


def _unpack_rows(ref, n):
    lo, hi = [], []
    for s in range(ROW_TILE):
        w = ref[pl.ds(s, n, stride=ROW_TILE), :]
        lo.append(lax.bitcast_convert_type(w << 16, F32))
        hi.append(lax.bitcast_convert_type(w & U32(HI_MASK), F32))
    return lo + hi


def _merge_kernel(ya_ref, yb_ref, hf_ref, hr_ref, co_ref, x_ref, mod_ref, hg_ref, wout_ref, pmg_ref,
                  pfg_ref, rw_ref, xo_ref, h2p_ref, lg_ref, y_scr):
    attn_heads = A_HEADS + B_HEADS
    for h in range(attn_heads + C_HEADS):
        sl = slice(h * HEAD_DIM, (h + 1) * HEAD_DIM)
        g = hg_ref[:, sl]
        if h < A_HEADS:
            y = _rms(ya_ref[:, sl].astype(F32), g)
        elif h < attn_heads:
            y = _rms(yb_ref[:, (h - A_HEADS) * HEAD_DIM:(h - A_HEADS + 1) * HEAD_DIM].astype(F32), g)
        else:
            cs = slice((h - attn_heads) * HEAD_DIM, (h - attn_heads + 1) * HEAD_DIM)
            y = _rms(hf_ref[:, cs] + hr_ref[:, cs], g) * jax.nn.sigmoid(co_ref[:, cs])
        y_scr[:, sl] = y.astype(BF16)
    y = _dot(y_scr[...], wout_ref[...])
    x_new = x_ref[...] + mod_ref[2:3, :] * _rms(y, pmg_ref[...])
    xo_ref[...] = x_new
    h2 = _rms(x_new, pfg_ref[...]) * (1.0 + mod_ref[4:5, :]) + mod_ref[3:4, :]
    _pack_rows(h2, h2p_ref)
    h_hi = h2.astype(BF16)
    h_lo = (h2 - h_hi.astype(F32)).astype(BF16)
    a = _dot(h_hi, rw_ref[...])
    b = _dot(h_lo, rw_ref[...])
    E = N_EXPERTS
    lg_ref[...] = a[:, :E] + (a[:, E:] + b[:, :E]) + b[:, E:]


def _merge(ya, yb, hf, hr, pc, x2, mod, head_gain, w_out_b, pm_gain, pf_gain, router_w, S):
    T, D = x2.shape
    tm = 256
    nb = S // tm
    aw, cw = A_HEADS * HEAD_DIM, C_HEADS * C_DV
    row = lambda w: pl.BlockSpec((tm, w), lambda i: (i, 0))
    full = lambda a: pl.BlockSpec(a.shape, lambda i: (0,) * a.ndim)
    return pl.pallas_call(
        _merge_kernel,
        grid=(T // tm,),
        in_specs=[row(aw), row(aw), row(cw), row(cw),
                  pl.BlockSpec((tm, cw), lambda i: (i, PC_CO // cw)),
                  row(D),
                  pl.BlockSpec((None, N_MOD, D), lambda i: (i // nb, 0, 0)),
                  full(head_gain), full(w_out_b), full(pm_gain), full(pf_gain), full(router_w)],
        out_specs=[row(D), pl.BlockSpec((tm * ROW_TILE, 128), lambda i: (i, 0)),
                   row(N_EXPERTS)],
        out_shape=[jax.ShapeDtypeStruct((T, D), F32),
                   jax.ShapeDtypeStruct((T * ROW_TILE, 128), U32),
                   jax.ShapeDtypeStruct((T, N_EXPERTS), F32)],
        scratch_shapes=[pltpu.VMEM((tm, D), BF16)],
        compiler_params=_cparams(("parallel",)),
        name="merge_out_proj",
    )(ya, yb, hf, hr, pc, x2, mod, head_gain, w_out_b, pm_gain, pf_gain, router_w)


def _router_kernel(lg_ref, bias_ref, idx_ref, gate_ref, mask_ref):
    scores = jax.nn.sigmoid(lg_ref[...])
    sel = scores + bias_ref[...]
    lane = lax.broadcasted_iota(jnp.int32, scores.shape, 1).astype(F32)
    col = lax.broadcasted_iota(jnp.int32, idx_ref.shape, 1)
    idx = jnp.zeros(idx_ref.shape, F32)
    gate = jnp.zeros(gate_ref.shape, F32)
    mask = jnp.zeros(scores.shape, F32)
    for kk in range(TOP_K):
        mx = jnp.max(sel, axis=1, keepdims=True)
        am = jnp.min(jnp.where(sel == mx, lane, float(N_EXPERTS)), axis=1, keepdims=True)
        hit = lane == am
        sc = jnp.sum(jnp.where(hit, scores, 0.0), axis=1, keepdims=True)
        idx = jnp.where(col == kk, am, idx)
        gate = jnp.where(col == kk, sc, gate)
        mask = jnp.where(hit, 1.0, mask)
        sel = jnp.where(hit, -jnp.inf, sel)
    gate = gate / jnp.sum(gate, axis=1, keepdims=True) * ROUTED_SCALE
    idx_ref[...] = idx.astype(jnp.int32)
    gate_ref[...] = gate
    mask_ref[...] = mask.astype(jnp.int32)


def _router(logits, bias):
    T, E = logits.shape
    tm = 1024
    row = lambda w: pl.BlockSpec((tm, w), lambda i: (i, 0))
    return pl.pallas_call(
        _router_kernel,
        grid=(T // tm,),
        in_specs=[row(E), pl.BlockSpec((1, E), lambda i: (0, 0))],
        out_specs=[row(8), row(8), row(E)],
        out_shape=[jax.ShapeDtypeStruct((T, 8), jnp.int32),
                   jax.ShapeDtypeStruct((T, 8), F32),
                   jax.ShapeDtypeStruct((T, E), jnp.int32)],
        compiler_params=_cparams(("parallel",)),
        name="router_topk",
    )(logits, bias)


TAIL_BITS = MOE_BLK.bit_length() - 1


def _tail_copies(tail_ref, zero_scr, xs_ref, zsem, fn):
    for e in range(N_EXPERTS):
        start_row = tail_ref[2 * e]
        n_tail = tail_ref[2 * e + 1]
        for bit in range(TAIL_BITS):
            size = 1 << bit
            cur = start_row + (n_tail & ~(2 * size - 1))
            cur = pl.multiple_of(cur * ROW_TILE, ROW_TILE)

            @pl.when((n_tail & size) != 0)
            def _():
                fn(pltpu.make_async_copy(zero_scr.at[pl.ds(0, size * ROW_TILE), :],
                                         xs_ref.at[pl.ds(cur, size * ROW_TILE), :], zsem))


def _row_tile(ref, row):
    return ref.at[pl.ds(pl.multiple_of(row * ROW_TILE, ROW_TILE), ROW_TILE), :]


def _dispatch_kernel(tail_ref, pos_ref, hp_ref, wsg_ref, wsu_ref, wsd_ref, xs_ref, ysh_ref,
                     zero_scr, sem, zsem, *, tm):
    i = pl.program_id(0)

    def start(r, c):
        src = _row_tile(hp_ref, r)
        for kk in range(TOP_K):
            pltpu.make_async_copy(src, _row_tile(xs_ref, pos_ref[0, r * TOP_K + kk]),
                                  sem).start(priority=kk % 2)
        return c

    lax.fori_loop(0, tm, start, 0)

    @pl.when(i == 0)
    def _():
        zero_scr[...] = jnp.zeros(zero_scr.shape, zero_scr.dtype)
        _tail_copies(tail_ref, zero_scr, xs_ref, zsem, lambda cp: cp.start())
        _tail_copies(tail_ref, zero_scr, xs_ref, zsem, lambda cp: cp.wait())

    h = jnp.concatenate([c.astype(BF16) for c in _unpack_rows(hp_ref, tm)], axis=1)
    gte = _dot(h, wsg_ref[...])
    a = gte * jax.nn.sigmoid(gte) * _dot(h, wsu_ref[...])
    ysh_ref[...] = _dot(a.astype(BF16), wsd_ref[...])

    for kk in range(TOP_K):
        pltpu.make_async_copy(hp_ref, xs_ref.at[pl.ds(0, tm * ROW_TILE), :], sem).wait()


def _dispatch(h2p, pos3, tail_info, ws_gate_b, ws_up_b, ws_down_b, P):
    T, D = h2p.shape[0] // ROW_TILE, D_MODEL
    tm = pos3.shape[2] // TOP_K
    kern = functools.partial(_dispatch_kernel, tm=tm)
    full = lambda a: pl.BlockSpec(a.shape, lambda i, tail: (0,) * a.ndim)
    return pl.pallas_call(
        kern,
        grid_spec=pltpu.PrefetchScalarGridSpec(
            num_scalar_prefetch=1,
            grid=(T // tm,),
            in_specs=[pl.BlockSpec((None, 1, tm * TOP_K), lambda i, tail: (i, 0, 0),
                                   memory_space=pltpu.SMEM),
                      pl.BlockSpec((tm * ROW_TILE, 128), lambda i, tail: (i, 0)),
                      full(ws_gate_b), full(ws_up_b), full(ws_down_b)],
            out_specs=[pl.BlockSpec(memory_space=pl.ANY),
                       pl.BlockSpec((tm, D), lambda i, tail: (i, 0))],
            scratch_shapes=[pltpu.VMEM((MOE_BLK // 2 * ROW_TILE, 128), U32),
                            pltpu.SemaphoreType.DMA(()), pltpu.SemaphoreType.DMA(())]),
        out_shape=[jax.ShapeDtypeStruct((P * ROW_TILE, 128), U32),
                   jax.ShapeDtypeStruct((T, D), F32)],
        compiler_params=_cparams(("arbitrary",)),
        name="moe_dispatch",
    )(tail_info, pos3, h2p, ws_gate_b, ws_up_b, ws_down_b)


EXPERT_TILES = 2


def _expert_kernel(blk_e_ref, nused_ref, next_ref, xs_ref, wg_hbm, wu_hbm, wd_hbm, ys_ref,
                   wg_f, wu_f, wd_f, wg_b, wu_b, wd_b, sems, *, layer):
    step = pl.program_id(0)
    rows = MOE_BLK * ROW_TILE

    def fetch(ex):
        return [pltpu.make_async_copy(hbm.at[layer, ex], buf, sems.at[n])
                for n, (hbm, buf) in enumerate(((wg_hbm, wg_f), (wu_hbm, wu_f), (wd_hbm, wd_f)))]

    @pl.when(step == 0)
    def _():
        for cp in fetch(blk_e_ref[0]):
            cp.start(priority=1)

    for j in range(EXPERT_TILES):
        i = step * EXPERT_TILES + j
        e = blk_e_ref[i]
        e_prev = blk_e_ref[jnp.maximum(i - 1, 0)]
        used = i < nused_ref[0]
        first = used & ((i == 0) | (e != e_prev))
        x_view = xs_ref.at[pl.ds(j * rows, rows), :]
        y_view = ys_ref.at[pl.ds(j * rows, rows), :]

        def compute(switch, e=e, x_view=x_view, y_view=y_view):
            x = jnp.concatenate([c.astype(BF16) for c in _unpack_rows(x_view, MOE_BLK)], axis=1)
            if switch:
                for cp in fetch(e):
                    cp.wait()
                wg_b[...] = wg_f[...].astype(BF16)
            gte = _dot(x, wg_b[...])
            if switch:
                wu_b[...] = wu_f[...].astype(BF16)
            up = _dot(x, wu_b[...])
            if switch:
                wd_b[...] = wd_f[...].astype(BF16)
                e_next = next_ref[e]

                @pl.when(e_next >= 0)
                def _():
                    for cp in fetch(e_next):
                        cp.start(priority=1)
            a = gte * jax.nn.sigmoid(gte) * up
            _pack_rows(_dot(a.astype(BF16), wd_b[...]), y_view)

        pl.when(first)(functools.partial(compute, True))
        pl.when(used & jnp.logical_not(first))(functools.partial(compute, False))

        @pl.when(jnp.logical_not(used))
        def _(y_view=y_view):
            y_view[...] = jnp.zeros(y_view.shape, y_view.dtype)


def _experts(xs, blk_e, nused, next_e, w_gate, w_up, w_down, layer):
    P = xs.shape[0] // ROW_TILE
    D, F = w_gate.shape[-2:]
    NB = P // MOE_BLK
    assert NB % EXPERT_TILES == 0
    rows = EXPERT_TILES * MOE_BLK * ROW_TILE
    last = lambda s, nu: jnp.minimum(s, (nu[0] - 1) // EXPERT_TILES)
    hbm = pl.BlockSpec(memory_space=pl.ANY)
    return pl.pallas_call(
        functools.partial(_expert_kernel, layer=layer),
        grid_spec=pltpu.PrefetchScalarGridSpec(
            num_scalar_prefetch=3,
            grid=(NB // EXPERT_TILES,),
            in_specs=[pl.BlockSpec((rows, 128), lambda s, be, nu, nx: (last(s, nu), 0)),
                      hbm, hbm, hbm],
            out_specs=pl.BlockSpec((rows, 128), lambda s, be, nu, nx: (s, 0)),
            scratch_shapes=[pltpu.VMEM((D, F), F32), pltpu.VMEM((D, F), F32), pltpu.VMEM((F, D), F32),
                            pltpu.VMEM((D, F), BF16), pltpu.VMEM((D, F), BF16),
                            pltpu.VMEM((F, D), BF16), pltpu.SemaphoreType.DMA((3,))]),
        out_shape=jax.ShapeDtypeStruct((P * ROW_TILE, 128), U32),
        compiler_params=_cparams(("arbitrary",)),
        name="moe_experts",
    )(blk_e, nused, next_e, xs, w_gate, w_up, w_down)


def _combine_kernel(pos_ref, pos_next_ref, ysh_ref, gate_ref, x_ref, mod_ref, pg_ref, ys_ref,
                    xo_ref, gbuf, sems, *, tm):
    i = pl.program_id(0)
    slot = i % 2

    def gather(p_ref, s):
        def start(r, c):
            for kk in range(TOP_K):
                pltpu.make_async_copy(_row_tile(ys_ref, p_ref[0, r * TOP_K + kk]),
                                      _row_tile(gbuf.at[s, kk], r), sems.at[s]).start(priority=kk % 2)
            return c
        lax.fori_loop(0, tm, start, 0)

    @pl.when(i == 0)
    def _():
        gather(pos_ref, 0)

    for s in range(2):
        @pl.when((i + 1 < pl.num_programs(0)) & (slot == 1 - s))
        def _():
            gather(pos_next_ref, s)

    for kk in range(TOP_K):
        pltpu.make_async_copy(ys_ref.at[pl.ds(0, tm * ROW_TILE), :], gbuf.at[slot, kk],
                              sems.at[slot]).wait()
    RG = 64

    def routed(rg):
        rows = slice(rg * RG, (rg + 1) * RG)
        gate = gate_ref[rows, :]
        g_wide = [jnp.broadcast_to(gate[:, kk:kk + 1], (RG, 128)) for kk in range(TOP_K)]
        for s in range(ROW_TILE):
            lo = hi = None
            for kk in range(TOP_K):
                w = gbuf[slot, kk, pl.ds(rg * RG * ROW_TILE + s, RG, stride=ROW_TILE), :]
                t_lo = g_wide[kk] * lax.bitcast_convert_type(w << 16, F32)
                t_hi = g_wide[kk] * lax.bitcast_convert_type(w & U32(HI_MASK), F32)
                lo = t_lo if lo is None else lo + t_lo
                hi = t_hi if hi is None else hi + t_hi
            xo_ref[rows, s * 128:(s + 1) * 128] = lo
            xo_ref[rows, HALF_D + s * 128: HALF_D + (s + 1) * 128] = hi

    for rg in range(tm // RG):
        routed(rg)
    y = xo_ref[...] + ysh_ref[...]
    xo_ref[...] = x_ref[...] + mod_ref[5:6, :] * _rms(y, pg_ref[...])


def _combine(pos3, ysh, gate8, x2, mod, post_gain, ys, S):
    T, D = x2.shape
    tm = pos3.shape[2] // TOP_K
    nb = S // tm
    kern = functools.partial(_combine_kernel, tm=tm)
    row = lambda w: pl.BlockSpec((tm, w), lambda i: (i, 0))
    full = lambda a: pl.BlockSpec(a.shape, lambda i: (0,) * a.ndim)
    n_steps = T // tm
    return pl.pallas_call(
        kern,
        grid=(n_steps,),
        in_specs=[pl.BlockSpec((None, 1, tm * TOP_K), lambda i: (i, 0, 0), memory_space=pltpu.SMEM),
                  pl.BlockSpec((None, 1, tm * TOP_K), lambda i: (jnp.minimum(i + 1, n_steps - 1), 0, 0),
                               memory_space=pltpu.SMEM),
                  row(D), row(8), row(D),
                  pl.BlockSpec((None, N_MOD, D), lambda i: (i // nb, 0, 0)),
                  full(post_gain),
                  pl.BlockSpec(memory_space=pl.ANY)],
        out_specs=row(D),
        out_shape=jax.ShapeDtypeStruct((T, D), F32),
        scratch_shapes=[pltpu.VMEM((2, TOP_K, tm * ROW_TILE, 128), U32),
                        pltpu.SemaphoreType.DMA((2,))],
        compiler_params=_cparams(("arbitrary",)),
        name="moe_combine",
    )(pos3, pos3, ysh, gate8, x2, mod, post_gain, ys)


def _route_plan(idx8, mask, n_blocks):
    idx = idx8[:, :TOP_K]
    counts = jnp.sum(mask, axis=0)
    rank = jnp.cumsum(mask, axis=0) - mask
    padded = (counts + MOE_BLK - 1) // MOE_BLK * MOE_BLK
    pend = jnp.cumsum(padded)
    pstart = pend - padded
    pos = jnp.take_along_axis(pstart[None, :] + rank, idx, axis=1).astype(jnp.int32)
    blk_start = jnp.arange(n_blocks, dtype=jnp.int32) * MOE_BLK
    blk_e = jnp.minimum(jnp.sum(pend[None, :] <= blk_start[:, None], axis=1), N_EXPERTS - 1)
    nused = (pend[-1:] // MOE_BLK).astype(jnp.int32)
    tail = jnp.stack([pstart + counts, padded - counts], axis=1).reshape(-1).astype(jnp.int32)
    ids = jnp.where(counts > 0, jnp.arange(N_EXPERTS), N_EXPERTS)
    after = jnp.concatenate([lax.cummin(ids, reverse=True)[1:], jnp.full((1,), N_EXPERTS)])
    next_e = jnp.where(after < N_EXPERTS, after, -1).astype(jnp.int32)
    return pos, blk_e.astype(jnp.int32), nused, tail, next_e


def _rope_tables(S, rot_dim):
    rows = S // GRID_W
    row = jnp.repeat(jnp.arange(rows, dtype=F32), GRID_W)
    col = jnp.tile(jnp.arange(GRID_W, dtype=F32), rows)
    axis_dim = rot_dim // 2
    inv_freq = ROPE_THETA ** (-jnp.arange(0, axis_dim, 2, dtype=F32) / axis_dim)
    ang_r = row[:, None] * inv_freq
    ang_c = col[:, None] * inv_freq
    z = jnp.zeros_like(ang_r)
    cos = jnp.concatenate([jnp.cos(ang_r)] * 2 + [jnp.cos(ang_c)] * 2, axis=1)
    sin_lo = jnp.concatenate([-jnp.sin(ang_r), z, -jnp.sin(ang_c), z], axis=1)
    sin_hi = jnp.concatenate([z, jnp.sin(ang_r), z, jnp.sin(ang_c)], axis=1)
    reps = 128 // rot_dim
    return tuple(jnp.tile(t, (1, reps)) for t in (cos, sin_lo, sin_hi))


def _prep_w_in(w):
    widths = (768, 256, 256, B_Q_LORA, B_KV_LORA, B_QK_ROPE, 512, 512, 512, 512, 4 * C_HEADS)
    offs = np.cumsum(widths)[:-1].tolist()
    a_q, a_k, a_v, b_cq, b_ckv, b_kr, c_q, c_k, c_v, c_o, c_g = jnp.split(w, offs, axis=-1)
    pad = jnp.zeros(w.shape[:-1] + (128 - B_QK_ROPE - 4 * C_HEADS,), w.dtype)
    out = jnp.concatenate([a_q, a_k, a_v, b_cq, b_ckv, b_kr, c_g, pad, c_v, c_o, c_q, c_k], axis=-1)
    assert out.shape[-1] == PROJ_PAD
    return out.astype(BF16)


def _split_hi_lo(w):
    hi = w.astype(BF16)
    lo = (w - hi.astype(F32)).astype(BF16)
    return jnp.concatenate([hi, lo], axis=1)


def _heads_split(w, n_heads, first):
    K = w.shape[0]
    w3 = w.reshape(K, n_heads, -1)
    return jnp.concatenate([w3[:, :, :first].reshape(K, -1), w3[:, :, first:].reshape(K, -1)],
                           axis=1).astype(BF16)


def kernel(x, c, ada_w, ada_b, pre_mix_gain, w_in, a_q_gain, a_k_gain, b_cq_gain, b_ckv_gain, w_uq, w_ukv, c_conv_w, c_conv_b, c_gate_b, head_out_gain, w_out, post_mix_gain, pre_ffn_gain, router_w, router_bias, exp_w_gate, exp_w_up, exp_w_down, sh_w_gate, sh_w_up, sh_w_down, post_ffn_gain):
    B, S, D = x.shape
    T = B * S
    L = ada_w.shape[0]
    n_blocks = T * TOP_K // MOE_BLK + N_EXPERTS
    row2 = lambda v: v.reshape(1, -1)

    tabs_a = _rope_tables(S, HEAD_DIM)
    tabs_b = _rope_tables(S, B_QK_ROPE)
    c_pad = jnp.zeros((8, D), F32).at[:B].set(c)
    mods = _modulation(c_pad, ada_w, ada_b)[:, :B].reshape(L, B, N_MOD, D)

    x2 = x.reshape(T, D)
    for l in range(L):
        mod = mods[l]
        qa, ka, va, qb, kb, vb, pc, qk, misc = _in_proj(
            x2, row2(pre_mix_gain[l]), mod, _prep_w_in(w_in[l]), tabs_a, tabs_b,
            row2(a_q_gain[l]), row2(a_k_gain[l]), row2(b_cq_gain[l]), row2(b_ckv_gain[l]),
            _heads_split(w_uq[l], B_HEADS, B_QK_NOPE), _heads_split(w_ukv[l], B_HEADS, B_QK_NOPE),
            c_conv_w[l], row2(c_conv_b[l]), S)
        pc3 = pc.reshape(B, S, -1)

        ya = _attention(qa.reshape(B, S, -1), ka.reshape(B, S, -1), va.reshape(B, S, -1),
                        B=B, S=S, Hk=A_KV_HEADS, G=A_HEADS // A_KV_HEADS, dq=HEAD_DIM, dv=HEAD_DIM,
                        tq=min(512, S), tk=512)

        yb = _attention(qb.reshape(B, S, -1), kb.reshape(B, S, -1), vb.reshape(B, S, -1),
                        B=B, S=S, Hk=B_HEADS, G=1, dq=256, dv=B_V_DIM, tq=min(1024, S), tk=512)

        qk3 = qk.reshape(B, S, -1)
        gate_b_row = jnp.zeros((1, 128), F32).at[0, MISC_GATE_LANE:MISC_GATE_LANE + 4 * C_HEADS].set(c_gate_b[l])
        hf, hr = _mlstm(qk3, pc3, misc.reshape(B, S, 128), gate_b_row)

        x2, h2p, logits = _merge(ya.reshape(T, -1), yb.reshape(T, -1), hf.reshape(T, -1),
                                hr.reshape(T, -1), pc, x2, mod,
                                row2(head_out_gain[l]), w_out[l].astype(BF16), row2(post_mix_gain[l]),
                                row2(pre_ffn_gain[l]), _split_hi_lo(router_w[l]), S)

        idx8, gate8, mask = _router(logits, row2(router_bias[l]))
        pos, blk_e, nused, tail, next_e = _route_plan(idx8, mask, n_blocks)
        xs, ysh = _dispatch(h2p, pos.reshape(T // DISPATCH_TM, 1, DISPATCH_TM * TOP_K), tail,
                            sh_w_gate[l].astype(BF16), sh_w_up[l].astype(BF16),
                            sh_w_down[l].astype(BF16), n_blocks * MOE_BLK)
        ys = _experts(xs, blk_e, nused, next_e, exp_w_gate, exp_w_up, exp_w_down, l)
        x2 = _combine(pos.reshape(T // COMBINE_TM, 1, COMBINE_TM * TOP_K), ysh, gate8, x2, mod,
                      row2(post_ffn_gain[l]), ys, S)
    return x2.reshape(B, S, D)
```

```python
import functools

import numpy as np
import jax
import jax.numpy as jnp
from jax import lax
from jax.experimental import pallas as pl
from jax.experimental.pallas import tpu as pltpu

D_MODEL = 2048
GRID_W = 64
ROPE_THETA = 10000.0
EPS = 1e-6
HEAD_DIM = 128
A_HEADS, A_KV_HEADS = 6, 2
B_HEADS, B_Q_LORA, B_KV_LORA = 6, 384, 256
B_QK_NOPE, B_QK_ROPE, B_V_DIM = 128, 64, 128
C_HEADS, C_DQK, C_DV = 4, 128, 128
CONV_W = 5
MLSTM_CHUNK = 64
N_EXPERTS, TOP_K, EXPERT_FF, SHARED_FF = 64, 6, 512, 512
ROUTED_SCALE = 2.5
N_MOD = 6

COL_AQ, COL_AK, COL_AV = 0, 768, 1024
COL_BC = 1280
COL_MISC = 1920
COL_CV, COL_CO = 2048, 2560
COL_CQK = 3072
PROJ_PAD = 4096
MISC_GATE_LANE = 64
PC_CV, PC_CO = 0, COL_CO - COL_CV

MOE_BLK = 256
DISPATCH_TM = 256
COMBINE_TM = 256
MLSTM_BB = 4
VMEM_LIMIT = 56 * 1024 * 1024

BF16 = jnp.bfloat16
F32 = jnp.float32
LOG2_E = 1.4426950408889634


def _cparams(grid_rank):
    return pltpu.CompilerParams(dimension_semantics=("arbitrary",) * grid_rank,
                                vmem_limit_bytes=VMEM_LIMIT)


def _rms(x, gain):
    return x * lax.rsqrt(jnp.mean(x * x, axis=-1, keepdims=True) + EPS) * gain


def _dot(a, b):
    return jnp.dot(a, b, preferred_element_type=F32)


def _dot_t(a, b):
    return lax.dot_general(a, b, (((1,), (1,)), ((), ())), preferred_element_type=F32)


def _mod_kernel(c_ref, w_ref, b_ref, o_ref):
    c = c_ref[...]
    c_act = (c * jax.nn.sigmoid(c)).astype(BF16)
    o_ref[...] = _dot(c_act, w_ref[...].astype(BF16)) + b_ref[...]


def _modulation(c_pad, ada_w, ada_b):
    L, D, N = ada_w.shape
    M = c_pad.shape[0]
    tn = 1536
    return pl.pallas_call(
        _mod_kernel,
        grid=(L, N // tn),
        in_specs=[pl.BlockSpec((M, D), lambda l, j: (0, 0)),
                  pl.BlockSpec((None, D, tn), lambda l, j: (l, 0, j)),
                  pl.BlockSpec((None, 1, tn), lambda l, j: (l, 0, j))],
        out_specs=pl.BlockSpec((None, M, tn), lambda l, j: (l, 0, j)),
        out_shape=jax.ShapeDtypeStruct((L, M, N), F32),
        compiler_params=_cparams(2),
        name="adaln_mod",
    )(c_pad, ada_w, ada_b.reshape(L, 1, N))


def _conv_silu(main, prev, nxt, w_ref, b_ref):
    tm, C = main.shape
    row = lax.broadcasted_iota(jnp.int32, main.shape, 0)
    pad = CONV_W // 2
    y = main * w_ref[pad:pad + 1, :] + b_ref[...]
    for j in range(CONV_W):
        d = j - pad
        if d == 0:
            continue
        shifted = pltpu.roll(main, (-d) % tm, 0)
        for e in range(abs(d)):
            if d < 0:
                shifted = jnp.where(row == e, prev[8 + d + e: 9 + d + e, :], shifted)
            else:
                shifted = jnp.where(row == tm - d + e, nxt[e:e + 1, :], shifted)
        y = y + shifted * w_ref[j:j + 1, :]
    y = y * jax.nn.sigmoid(y)
    lane = lax.broadcasted_iota(jnp.int32, main.shape, 1)
    return y * jnp.where(lane >= C // 2, C_DQK ** -0.5, 1.0)


def _in_proj_kernel(x_ref, xp_ref, xn_ref, gain_ref, mod_ref, w_ref,
                    cosa_ref, sloa_ref, shia_ref, cosb_ref, slob_ref, shib_ref,
                    qg_ref, kg_ref, cqg_ref, ckvg_ref, wuq_ref, wukv_ref, cw_ref, cb_ref,
                    qa_ref, ka_ref, va_ref, qb_ref, kb_ref, vb_ref, pc_ref, qk_ref, misc_ref, *, nb):
    def premix(x):
        y = _rms(x, gain_ref[...])
        return (y * (1.0 + mod_ref[1:2, :]) + mod_ref[0:1, :]).astype(BF16)

    h = premix(x_ref[...])
    pa = _dot(h, w_ref[:, COL_AQ:COL_BC])
    _gqa_prep(pa, (cosa_ref[...], sloa_ref[...], shia_ref[...]), qg_ref[...], kg_ref[...],
              qa_ref, ka_ref, va_ref)
    pb = _dot(h, w_ref[:, COL_BC:COL_CV])
    misc = pb[:, COL_MISC - COL_BC:]
    misc_ref[...] = misc
    _mla_prep(pb[:, :COL_MISC - COL_BC], misc, (cosb_ref[...], slob_ref[...], shib_ref[...]),
              cqg_ref[...], ckvg_ref[...], wuq_ref, wukv_ref, qb_ref, kb_ref, vb_ref)
    pc_ref[...] = _dot(h, w_ref[:, COL_CV:COL_CQK])
    w_qk = w_ref[:, COL_CQK:]
    i = pl.program_id(0)
    halo = premix(jnp.concatenate([xp_ref[...], xn_ref[...]], axis=0))
    halo = _dot(halo, w_qk)
    prev = jnp.where(i % nb == 0, 0.0, halo[:8, :])
    nxt = jnp.where(i % nb == nb - 1, 0.0, halo[8:, :])
    qk_ref[...] = _conv_silu(_dot(h, w_qk), prev, nxt, cw_ref, cb_ref)


def _in_proj(x2, gain, mod, w_in_p, tabs_a, tabs_b, q_gain, k_gain, cq_gain, ckv_gain, w_uq_p, w_ukv_p,
             conv_w, conv_b, S):
    T, D = x2.shape
    tm = 256
    nb = S // tm
    n8 = tm // 8
    row = lambda w: pl.BlockSpec((tm, w), lambda i: (i, 0))
    full = lambda a: pl.BlockSpec(a.shape, lambda i: (0,) * a.ndim)
    tab = pl.BlockSpec((tm, 128), lambda i: (i % nb, 0))
    qw, kw = A_HEADS * HEAD_DIM, A_KV_HEADS * HEAD_DIM
    bqw, bvw = B_HEADS * 256, B_HEADS * B_V_DIM
    cvo, cqk = COL_CQK - COL_CV, PROJ_PAD - COL_CQK
    widths = (qw, kw, kw, bqw, bqw, bvw)
    return pl.pallas_call(
        functools.partial(_in_proj_kernel, nb=nb),
        grid=(T // tm,),
        in_specs=[row(D),
                  pl.BlockSpec((8, D), lambda i: (jnp.maximum(i * n8 - 1, 0), 0)),
                  pl.BlockSpec((8, D), lambda i: (jnp.minimum((i + 1) * n8, T // 8 - 1), 0)),
                  full(gain),
                  pl.BlockSpec((None, N_MOD, D), lambda i: (i // nb, 0, 0)),
                  pl.BlockSpec(w_in_p.shape, lambda i: (0, 0), pipeline_mode=pl.Buffered(1)),
                  tab, tab, tab, tab, tab, tab,
                  full(q_gain), full(k_gain), full(cq_gain), full(ckv_gain), full(w_uq_p), full(w_ukv_p),
                  full(conv_w), full(conv_b)],
        out_specs=[row(w) for w in widths] + [row(cvo), row(cqk), row(128)],
        out_shape=[jax.ShapeDtypeStruct((T, w), BF16) for w in widths]
                  + [jax.ShapeDtypeStruct((T, cvo), F32), jax.ShapeDtypeStruct((T, cqk), F32),
                     jax.ShapeDtypeStruct((T, 128), F32)],
        compiler_params=_cparams(1),
        name="in_proj",
    )(x2, x2, x2, gain, mod, w_in_p, *tabs_a, *tabs_b, q_gain, k_gain, cq_gain, ckv_gain, w_uq_p, w_ukv_p,
      conv_w, conv_b)


def _rope128(y, cos, sin_lo, sin_hi):
    return y * cos + pltpu.roll(y, 96, 1) * sin_lo + pltpu.roll(y, 32, 1) * sin_hi


def _gqa_prep(pa, tabs, q_gain, k_gain, qo_ref, ko_ref, vo_ref):
    cos, slo, shi = tabs
    scale = HEAD_DIM ** -0.5 * LOG2_E
    for h in range(A_HEADS):
        sl = slice(h * HEAD_DIM, (h + 1) * HEAD_DIM)
        y = _rope128(_rms(pa[:, COL_AQ + h * HEAD_DIM: COL_AQ + (h + 1) * HEAD_DIM], q_gain), cos, slo, shi)
        qo_ref[:, sl] = (y * scale).astype(BF16)
    for h in range(A_KV_HEADS):
        sl = slice(h * HEAD_DIM, (h + 1) * HEAD_DIM)
        y = _rope128(_rms(pa[:, COL_AK + h * HEAD_DIM: COL_AK + (h + 1) * HEAD_DIM], k_gain), cos, slo, shi)
        ko_ref[:, sl] = y.astype(BF16)
    vo_ref[...] = pa[:, COL_AV:COL_BC].astype(BF16)


def _rope64x2(y, cos, sin_lo, sin_hi):
    return y * cos + pltpu.roll(y, 112, 1) * sin_lo + pltpu.roll(y, 16, 1) * sin_hi


def _mla_prep(bc, misc, tabs, cq_gain, ckv_gain, wuq_ref, wukv_ref, qo_ref, ko_ref, vo_ref):
    cos, slo, shi = tabs
    lane = lax.broadcasted_iota(jnp.int32, cos.shape, 1)
    first = lane < B_QK_ROPE
    scale = (B_QK_NOPE + B_QK_ROPE) ** -0.5 * LOG2_E
    nope_w = B_HEADS * B_QK_NOPE

    cq = _rms(bc[:, :B_Q_LORA], cq_gain).astype(BF16)
    q = _dot(cq, wuq_ref[...]) * scale
    for p in range(B_HEADS // 2):
        pair = _rope64x2(q[:, nope_w + p * 128: nope_w + (p + 1) * 128], cos, slo, shi)
        for half in range(2):
            h = 2 * p + half
            qo_ref[:, h * 256: h * 256 + 128] = q[:, h * 128:(h + 1) * 128].astype(BF16)
            keep = first if half == 0 else jnp.logical_not(first)
            qo_ref[:, h * 256 + 128: (h + 1) * 256] = jnp.where(keep, pair, 0.0).astype(BF16)

    kr = jnp.where(first, misc, 0.0)
    kr = kr + pltpu.roll(kr, B_QK_ROPE, 1)
    kr = _rope64x2(kr, cos, slo, shi).astype(BF16)
    ckv = _rms(bc[:, B_Q_LORA:], ckv_gain).astype(BF16)
    kv = _dot(ckv, wukv_ref[...])
    for h in range(B_HEADS):
        ko_ref[:, h * 256: h * 256 + 128] = kv[:, h * 128:(h + 1) * 128].astype(BF16)
        ko_ref[:, h * 256 + 128: (h + 1) * 256] = kr
    vo_ref[...] = kv[:, nope_w:].astype(BF16)


def _attn_kernel(q_ref, k_ref, v_ref, o_ref, m_scr, l_scr, acc_scr, *, G, dq, dv, tk, sub):
    tq = q_ref.shape[0]
    S = k_ref.shape[0]
    groups = [(g, r) for g in range(G) for r in range(tq // sub)]
    m_scr[...] = jnp.full(m_scr.shape, -jnp.inf, F32)
    l_scr[...] = jnp.zeros(l_scr.shape, F32)
    acc_scr[...] = jnp.zeros(acc_scr.shape, F32)

    stages = [(c, n) for c in range(S // tk) for n in range(len(groups))]

    def scores(stage):
        c, n = stage
        g, r = groups[n]
        return _dot_t(q_ref[r * sub:(r + 1) * sub, g * dq:(g + 1) * dq], k_ref[c * tk:(c + 1) * tk, :])

    s_next = scores(stages[0])
    for i, (c, n) in enumerate(stages):
        s = s_next
        if i + 1 < len(stages):
            s_next = scores(stages[i + 1])
        rows = slice(n * sub, (n + 1) * sub)
        blocks = [s[:, j * 128:(j + 1) * 128] for j in range(tk // 128)]
        m_blk = functools.reduce(jnp.maximum, blocks)
        m_old = m_scr[rows, :]
        m_new = jnp.maximum(m_old, jnp.max(m_blk, axis=-1, keepdims=True))
        alpha = jnp.exp2(m_old - m_new)
        p_blocks = [jnp.exp2(b - m_new) for b in blocks]
        l_scr[rows, :] = alpha * l_scr[rows, :] + functools.reduce(jnp.add, p_blocks)
        p = jnp.concatenate([b.astype(BF16) for b in p_blocks], axis=1)
        acc_scr[rows, :] = alpha * acc_scr[rows, :] + _dot(p, v_ref[c * tk:(c + 1) * tk, :])
        m_scr[rows, :] = m_new

    for n, (g, r) in enumerate(groups):
        rows = slice(n * sub, (n + 1) * sub)
        l = jnp.sum(l_scr[rows, :], axis=-1, keepdims=True)
        o_ref[r * sub:(r + 1) * sub, g * dv:(g + 1) * dv] = (acc_scr[rows, :] / l).astype(o_ref.dtype)


def _attention(q, k, v, *, B, S, Hk, G, dq, dv, tq, tk):
    assert dv == 128, "row statistics are kept 128 lanes wide to match the value width"
    kern = functools.partial(_attn_kernel, G=G, dq=dq, dv=dv, tk=tk, sub=256)
    M = G * tq
    return pl.pallas_call(
        kern,
        grid=(B, Hk, S // tq),
        in_specs=[pl.BlockSpec((None, tq, G * dq), lambda b, h, i: (b, i, h)),
                  pl.BlockSpec((None, S, dq), lambda b, h, i: (b, 0, h)),
                  pl.BlockSpec((None, S, dv), lambda b, h, i: (b, 0, h))],
        out_specs=pl.BlockSpec((None, tq, G * dv), lambda b, h, i: (b, i, h)),
        out_shape=jax.ShapeDtypeStruct((B, S, Hk * G * dv), BF16),
        scratch_shapes=[pltpu.VMEM((M, 128), F32), pltpu.VMEM((M, 128), F32), pltpu.VMEM((M, dv), F32)],
        compiler_params=_cparams(3),
        name=f"attn_g{G}_d{dq}",
    )(q, k, v)


def _log_sigmoid(x):
    return jnp.minimum(x, 0.0) - jnp.log1p(jnp.exp(-jnp.abs(x)))


def _mlstm_kernel(qf_ref, kf_ref, vf_ref, gf_ref, qr_ref, kr_ref, vr_ref, gr_ref, gb_ref,
                  of_ref, or_ref, C_scr, n_scr, m_scr, *, BB):
    L = MLSTM_CHUNK

    @pl.when(pl.program_id(1) == 0)
    def _():
        C_scr[...] = jnp.zeros(C_scr.shape, F32)
        n_scr[...] = jnp.zeros(n_scr.shape, F32)
        m_scr[...] = jnp.zeros(m_scr.shape, F32)

    glane = lax.broadcasted_iota(jnp.int32, (L, 128), 1)
    jj = lax.broadcasted_iota(jnp.int32, (L, L), 0)
    ss = lax.broadcasted_iota(jnp.int32, (L, L), 1)
    eye = jj == ss
    neg_inf = jnp.float32(-jnp.inf)
    dirs = ((qf_ref, kf_ref, vf_ref, gf_ref, of_ref, ss <= jj, jj <= ss),
            (qr_ref, kr_ref, vr_ref, gr_ref, or_ref, ss >= jj, jj >= ss))

    chains = []
    for d, (q_ref, k_ref, v_ref, g_ref, o_ref, seen, seen_t) in enumerate(dirs):
        for bb in range(BB):
            gates = g_ref[bb] + gb_ref[...]
            lsig = _log_sigmoid(gates)
            for h in range(C_HEADS):
                c = dict(st=(d * BB + bb) * C_HEADS + h, sl=slice(h * C_DQK, (h + 1) * C_DQK), bb=bb,
                         q_ref=q_ref, k_ref=k_ref, v_ref=v_ref, o_ref=o_ref, seen=seen, seen_t=seen_t)
                i_lane = MISC_GATE_LANE + d * (2 * C_HEADS) + h
                c["i_col"] = jnp.sum(jnp.where(glane == i_lane, gates, 0.0), axis=1, keepdims=True)
                c["f_col"] = jnp.sum(jnp.where(glane == i_lane + C_HEADS, lsig, 0.0), axis=1, keepdims=True)
                chains.append(c)
    for c in chains:
        c["f_row"] = jnp.sum(jnp.where(eye, c["f_col"], 0.0), axis=0, keepdims=True)
        c["i_row"] = jnp.sum(jnp.where(eye, c["i_col"], 0.0), axis=0, keepdims=True)
        c["b_row"] = jnp.sum(jnp.where(c["seen_t"], c["f_col"], 0.0), axis=0, keepdims=True)
        c["g_tot"] = jnp.sum(c["f_col"], axis=0, keepdims=True)
    for c in chains:
        c["b_col"] = jnp.sum(jnp.where(c["seen"], c["f_row"], 0.0), axis=1, keepdims=True)
    for c in chains:
        c["m_prev"] = m_scr[c["st"]][:, 0:1]
        c["dmat"] = jnp.where(c["seen"], c["b_col"] - c["b_row"] + c["i_row"], neg_inf)
        c["m_inter"] = c["b_col"] + c["m_prev"]
        c["a_col"] = c["g_tot"] - c["b_col"] + c["i_col"]
    for c in chains:
        c["m_j"] = jnp.maximum(c["m_inter"], jnp.max(c["dmat"], axis=1, keepdims=True))
        c["m_new"] = jnp.maximum(c["g_tot"] + c["m_prev"], jnp.max(c["a_col"], axis=0, keepdims=True))
    for c in chains:
        c["q"] = c["q_ref"][c["bb"], :, c["sl"]]
        c["k"] = c["k_ref"][c["bb"], :, c["sl"]]
        c["qb"], c["kb"] = c["q"].astype(BF16), c["k"].astype(BF16)
        c["vb"] = c["v_ref"][c["bb"], :, c["sl"]].astype(BF16)
        c["qk"] = _dot_t(c["qb"], c["kb"])
    for c in chains:
        c["qC"] = _dot(c["qb"], C_scr[c["st"]].astype(BF16))
    for c in chains:
        c["s"] = c["qk"] * jnp.exp(c["dmat"] - c["m_j"])
        c["inter"] = jnp.exp(c["m_inter"] - c["m_j"])
        c["wk"] = jnp.exp(c["a_col"] - c["m_new"]) * c["k"]
        c["decay"] = jnp.exp(c["g_tot"] + c["m_prev"] - c["m_new"])
    for c in chains:
        c["sv"] = _dot(c["s"].astype(BF16), c["vb"])
    for c in chains:
        c["upd"] = lax.dot_general(c["wk"].astype(BF16), c["vb"], (((0,), (0,)), ((), ())),
                                   preferred_element_type=F32)
    for c in chains:
        n_prev = n_scr[c["st"]]
        den = (jnp.sum(c["s"], axis=1, keepdims=True)
               + c["inter"] * jnp.sum(c["q"] * n_prev, axis=1, keepdims=True))
        num = c["sv"] + c["inter"] * c["qC"]
        c["o_ref"][c["bb"], :, c["sl"]] = num / jnp.maximum(jnp.abs(den), jnp.exp(-c["m_j"]))
        n_scr[c["st"]] = c["decay"] * n_prev + jnp.sum(c["wk"], axis=0, keepdims=True)
    for c in chains:
        C_scr[c["st"]] = c["decay"] * C_scr[c["st"]] + c["upd"]
        m_scr[c["st"]] = jnp.broadcast_to(c["m_new"], m_scr.shape[1:])


def _mlstm(qk3, pc3, misc3, gate_b_row, BB=MLSTM_BB):
    B, S, _ = pc3.shape
    BB = min(BB, B)
    L = MLSTM_CHUNK
    nc = S // L
    W = C_HEADS * C_DQK
    n_chain = 2 * BB * C_HEADS
    fwd = lambda col: (lambda b, c: (b, c, col))
    rev = lambda col: (lambda b, c: (b, nc - 1 - c, col))
    blk = lambda w, imap: pl.BlockSpec((BB, L, w), imap)
    out = jax.ShapeDtypeStruct((B, S, W), F32)
    return pl.pallas_call(
        functools.partial(_mlstm_kernel, BB=BB),
        grid=(B // BB, nc),
        in_specs=[blk(W, fwd(0)), blk(W, fwd(1)), blk(W, fwd(PC_CV // W)), blk(128, fwd(0)),
                  blk(W, rev(0)), blk(W, rev(1)), blk(W, rev(PC_CV // W)), blk(128, rev(0)),
                  pl.BlockSpec((1, 128), lambda b, c: (0, 0))],
        out_specs=[blk(W, fwd(0)), blk(W, rev(0))],
        out_shape=[out, out],
        scratch_shapes=[pltpu.VMEM((n_chain, C_DQK, C_DV), F32),
                        pltpu.VMEM((n_chain, 1, C_DQK), F32),
                        pltpu.VMEM((n_chain, 1, 128), F32)],
        compiler_params=_cparams(2),
        name="mlstm",
    )(qk3, qk3, pc3, misc3, qk3, qk3, pc3, misc3, gate_b_row)


ROW_TILE = 8
HALF_D = D_MODEL // 2
U32 = jnp.uint32
HI_MASK = 0xFFFF0000


def _pack_rows(y, o_ref):
    n = y.shape[0]
    for s in range(ROW_TILE):
        lo = y[:, s * 128:(s + 1) * 128].astype(BF16).astype(F32)
        hi = y[:, HALF_D + s * 128: HALF_D + (s + 1) * 128].astype(BF16).astype(F32)
        w = (lax.bitcast_convert_type(lo, U32) >> 16) | (lax.bitcast_convert_type(hi, U32) & U32(HI_MASK))
        o_ref[pl.ds(s, n, stride=ROW_TILE), :] = w


def _unpack_rows(ref, n):
    lo, hi = [], []
    for s in range(ROW_TILE):
        w = ref[pl.ds(s, n, stride=ROW_TILE), :]
        lo.append(lax.bitcast_convert_type(w << 16, F32))
        hi.append(lax.bitcast_convert_type(w & U32(HI_MASK), F32))
    return lo + hi


def _merge_kernel(ya_ref, yb_ref, hf_ref, hr_ref, co_ref, x_ref, mod_ref, hg_ref, wout_ref, pmg_ref,
                  pfg_ref, rw_ref, xo_ref, h2p_ref, lg_ref, y_scr):
    attn_heads = A_HEADS + B_HEADS
    for h in range(attn_heads + C_HEADS):
        sl = slice(h * HEAD_DIM, (h + 1) * HEAD_DIM)
        g = hg_ref[:, sl]
        if h < A_HEADS:
            y = _rms(ya_ref[:, sl].astype(F32), g)
        elif h < attn_heads:
            y = _rms(yb_ref[:, (h - A_HEADS) * HEAD_DIM:(h - A_HEADS + 1) * HEAD_DIM].astype(F32), g)
        else:
            cs = slice((h - attn_heads) * HEAD_DIM, (h - attn_heads + 1) * HEAD_DIM)
            y = _rms(hf_ref[:, cs] + hr_ref[:, cs], g) * jax.nn.sigmoid(co_ref[:, cs])
        y_scr[:, sl] = y.astype(BF16)
    y = _dot(y_scr[...], wout_ref[...])
    x_new = x_ref[...] + mod_ref[2:3, :] * _rms(y, pmg_ref[...])
    xo_ref[...] = x_new
    h2 = _rms(x_new, pfg_ref[...]) * (1.0 + mod_ref[4:5, :]) + mod_ref[3:4, :]
    _pack_rows(h2, h2p_ref)
    h_hi = h2.astype(BF16)
    h_lo = (h2 - h_hi.astype(F32)).astype(BF16)
    a = _dot(h_hi, rw_ref[...])
    b = _dot(h_lo, rw_ref[...])
    E = N_EXPERTS
    lg_ref[...] = a[:, :E] + (a[:, E:] + b[:, :E]) + b[:, E:]


def _merge(ya, yb, hf, hr, pc, x2, mod, head_gain, w_out_b, pm_gain, pf_gain, router_w, S):
    T, D = x2.shape
    tm = 256
    nb = S // tm
    aw, cw = A_HEADS * HEAD_DIM, C_HEADS * C_DV
    row = lambda w: pl.BlockSpec((tm, w), lambda i: (i, 0))
    full = lambda a: pl.BlockSpec(a.shape, lambda i: (0,) * a.ndim)
    return pl.pallas_call(
        _merge_kernel,
        grid=(T // tm,),
        in_specs=[row(aw), row(aw), row(cw), row(cw),
                  pl.BlockSpec((tm, cw), lambda i: (i, PC_CO // cw)),
                  row(D),
                  pl.BlockSpec((None, N_MOD, D), lambda i: (i // nb, 0, 0)),
                  full(head_gain), full(w_out_b), full(pm_gain), full(pf_gain), full(router_w)],
        out_specs=[row(D), pl.BlockSpec((tm * ROW_TILE, 128), lambda i: (i, 0)),
                   row(N_EXPERTS)],
        out_shape=[jax.ShapeDtypeStruct((T, D), F32),
                   jax.ShapeDtypeStruct((T * ROW_TILE, 128), U32),
                   jax.ShapeDtypeStruct((T, N_EXPERTS), F32)],
        scratch_shapes=[pltpu.VMEM((tm, D), BF16)],
        compiler_params=_cparams(1),
        name="merge_out_proj",
    )(ya, yb, hf, hr, pc, x2, mod, head_gain, w_out_b, pm_gain, pf_gain, router_w)


def _router_kernel(lg_ref, bias_ref, idx_ref, gate_ref, mask_ref):
    scores = jax.nn.sigmoid(lg_ref[...])
    sel = scores + bias_ref[...]
    lane = lax.broadcasted_iota(jnp.int32, scores.shape, 1).astype(F32)
    col = lax.broadcasted_iota(jnp.int32, idx_ref.shape, 1)
    idx = jnp.zeros(idx_ref.shape, F32)
    gate = jnp.zeros(gate_ref.shape, F32)
    mask = jnp.zeros(scores.shape, F32)
    for kk in range(TOP_K):
        mx = jnp.max(sel, axis=1, keepdims=True)
        am = jnp.min(jnp.where(sel == mx, lane, float(N_EXPERTS)), axis=1, keepdims=True)
        hit = lane == am
        sc = jnp.sum(jnp.where(hit, scores, 0.0), axis=1, keepdims=True)
        idx = jnp.where(col == kk, am, idx)
        gate = jnp.where(col == kk, sc, gate)
        mask = jnp.where(hit, 1.0, mask)
        sel = jnp.where(hit, -jnp.inf, sel)
    gate = gate / jnp.sum(gate, axis=1, keepdims=True) * ROUTED_SCALE
    idx_ref[...] = idx.astype(jnp.int32)
    gate_ref[...] = gate
    mask_ref[...] = mask.astype(jnp.int32)


def _router(logits, bias):
    T, E = logits.shape
    tm = 1024
    row = lambda w: pl.BlockSpec((tm, w), lambda i: (i, 0))
    return pl.pallas_call(
        _router_kernel,
        grid=(T // tm,),
        in_specs=[row(E), pl.BlockSpec((1, E), lambda i: (0, 0))],
        out_specs=[row(8), row(8), row(E)],
        out_shape=[jax.ShapeDtypeStruct((T, 8), jnp.int32),
                   jax.ShapeDtypeStruct((T, 8), F32),
                   jax.ShapeDtypeStruct((T, E), jnp.int32)],
        compiler_params=_cparams(1),
        name="router_topk",
    )(logits, bias)


TAIL_BITS = MOE_BLK.bit_length() - 1


def _tail_copies(tail_ref, zero_scr, xs_ref, zsem, fn):
    for e in range(N_EXPERTS):
        start_row = tail_ref[2 * e]
        n_tail = tail_ref[2 * e + 1]
        for bit in range(TAIL_BITS):
            size = 1 << bit
            cur = start_row + (n_tail & ~(2 * size - 1))
            cur = pl.multiple_of(cur * ROW_TILE, ROW_TILE)

            @pl.when((n_tail & size) != 0)
            def _():
                fn(pltpu.make_async_copy(zero_scr.at[pl.ds(0, size * ROW_TILE), :],
                                         xs_ref.at[pl.ds(cur, size * ROW_TILE), :], zsem))


def _row_tile(ref, row):
    start = row * ROW_TILE
    if not isinstance(row, int):
        start = pl.multiple_of(start, ROW_TILE)
    return ref.at[pl.ds(start, ROW_TILE), :]


def _dispatch_kernel(tail_ref, pos_ref, hp_ref, wsg_ref, wsu_ref, wsd_ref, xs_ref, ysh_ref,
                     zero_scr, sem, zsem, *, tm):
    i = pl.program_id(0)

    for r in range(tm):
        src = _row_tile(hp_ref, r)
        for kk in range(TOP_K):
            pltpu.make_async_copy(src, _row_tile(xs_ref, pos_ref[0, r * TOP_K + kk]),
                                  sem).start(priority=kk % 2)

    @pl.when(i == 0)
    def _():
        zero_scr[...] = jnp.zeros(zero_scr.shape, zero_scr.dtype)
        _tail_copies(tail_ref, zero_scr, xs_ref, zsem, lambda cp: cp.start())
        _tail_copies(tail_ref, zero_scr, xs_ref, zsem, lambda cp: cp.wait())

    h = jnp.concatenate([c.astype(BF16) for c in _unpack_rows(hp_ref, tm)], axis=1)
    gte = _dot(h, wsg_ref[...])
    a = gte * jax.nn.sigmoid(gte) * _dot(h, wsu_ref[...])
    ysh_ref[...] = _dot(a.astype(BF16), wsd_ref[...])

    for kk in range(TOP_K):
        pltpu.make_async_copy(hp_ref, xs_ref.at[pl.ds(0, tm * ROW_TILE), :], sem).wait()


def _dispatch(h2p, pos3, tail_info, ws_gate_b, ws_up_b, ws_down_b, P):
    T, D = h2p.shape[0] // ROW_TILE, D_MODEL
    tm = pos3.shape[2] // TOP_K
    kern = functools.partial(_dispatch_kernel, tm=tm)
    full = lambda a: pl.BlockSpec(a.shape, lambda i, tail: (0,) * a.ndim)
    return pl.pallas_call(
        kern,
        grid_spec=pltpu.PrefetchScalarGridSpec(
            num_scalar_prefetch=1,
            grid=(T // tm,),
            in_specs=[pl.BlockSpec((None, 1, tm * TOP_K), lambda i, tail: (i, 0, 0),
                                   memory_space=pltpu.SMEM),
                      pl.BlockSpec((tm * ROW_TILE, 128), lambda i, tail: (i, 0)),
                      full(ws_gate_b), full(ws_up_b), full(ws_down_b)],
            out_specs=[pl.BlockSpec(memory_space=pl.ANY),
                       pl.BlockSpec((tm, D), lambda i, tail: (i, 0))],
            scratch_shapes=[pltpu.VMEM((MOE_BLK // 2 * ROW_TILE, 128), U32),
                            pltpu.SemaphoreType.DMA(()), pltpu.SemaphoreType.DMA(())]),
        out_shape=[jax.ShapeDtypeStruct((P * ROW_TILE, 128), U32),
                   jax.ShapeDtypeStruct((T, D), F32)],
        compiler_params=_cparams(1),
        name="moe_dispatch",
    )(tail_info, pos3, h2p, ws_gate_b, ws_up_b, ws_down_b)


EXPERT_TILES = 2


def _expert_kernel(blk_e_ref, nused_ref, next_ref, xs_ref, wg_hbm, wu_hbm, wd_hbm, ys_ref,
                   wg_f, wu_f, wd_f, wg_b, wu_b, wd_b, sems, *, layer):
    step = pl.program_id(0)
    rows = MOE_BLK * ROW_TILE

    def fetch(ex):
        return [pltpu.make_async_copy(hbm.at[layer, ex], buf, sems.at[n])
                for n, (hbm, buf) in enumerate(((wg_hbm, wg_f), (wu_hbm, wu_f), (wd_hbm, wd_f)))]

    @pl.when(step == 0)
    def _():
        for cp in fetch(blk_e_ref[0]):
            cp.start(priority=1)

    for j in range(EXPERT_TILES):
        i = step * EXPERT_TILES + j
        e = blk_e_ref[i]
        e_prev = blk_e_ref[jnp.maximum(i - 1, 0)]
        used = i < nused_ref[0]
        first = used & ((i == 0) | (e != e_prev))
        x_view = xs_ref.at[pl.ds(j * rows, rows), :]
        y_view = ys_ref.at[pl.ds(j * rows, rows), :]

        def compute(switch, e=e, x_view=x_view, y_view=y_view):
            x = jnp.concatenate([c.astype(BF16) for c in _unpack_rows(x_view, MOE_BLK)], axis=1)
            if switch:
                for cp in fetch(e):
                    cp.wait()
                wg_b[...] = wg_f[...].astype(BF16)
            gte = _dot(x, wg_b[...])
            if switch:
                wu_b[...] = wu_f[...].astype(BF16)
            up = _dot(x, wu_b[...])
            if switch:
                wd_b[...] = wd_f[...].astype(BF16)
                e_next = next_ref[e]

                @pl.when(e_next >= 0)
                def _():
                    for cp in fetch(e_next):
                        cp.start(priority=1)
            a = gte * jax.nn.sigmoid(gte) * up
            _pack_rows(_dot(a.astype(BF16), wd_b[...]), y_view)

        pl.when(first)(functools.partial(compute, True))
        pl.when(used & jnp.logical_not(first))(functools.partial(compute, False))

        @pl.when(jnp.logical_not(used))
        def _(y_view=y_view):
            y_view[...] = jnp.zeros(y_view.shape, y_view.dtype)


def _experts(xs, blk_e, nused, next_e, w_gate, w_up, w_down, layer):
    P = xs.shape[0] // ROW_TILE
    D, F = w_gate.shape[-2:]
    NB = P // MOE_BLK
    assert NB % EXPERT_TILES == 0
    rows = EXPERT_TILES * MOE_BLK * ROW_TILE
    last = lambda s, nu: jnp.minimum(s, (nu[0] - 1) // EXPERT_TILES)
    hbm = pl.BlockSpec(memory_space=pl.ANY)
    return pl.pallas_call(
        functools.partial(_expert_kernel, layer=layer),
        grid_spec=pltpu.PrefetchScalarGridSpec(
            num_scalar_prefetch=3,
            grid=(NB // EXPERT_TILES,),
            in_specs=[pl.BlockSpec((rows, 128), lambda s, be, nu, nx: (last(s, nu), 0)),
                      hbm, hbm, hbm],
            out_specs=pl.BlockSpec((rows, 128), lambda s, be, nu, nx: (s, 0)),
            scratch_shapes=[pltpu.VMEM((D, F), F32), pltpu.VMEM((D, F), F32), pltpu.VMEM((F, D), F32),
                            pltpu.VMEM((D, F), BF16), pltpu.VMEM((D, F), BF16),
                            pltpu.VMEM((F, D), BF16), pltpu.SemaphoreType.DMA((3,))]),
        out_shape=jax.ShapeDtypeStruct((P * ROW_TILE, 128), U32),
        compiler_params=_cparams(1),
        name="moe_experts",
    )(blk_e, nused, next_e, xs, w_gate, w_up, w_down)


def _combine_kernel(pos_ref, pos_next_ref, ysh_ref, gate_ref, x_ref, mod_ref, pg_ref, ys_ref,
                    xo_ref, gbuf, sems, *, tm):
    i = pl.program_id(0)
    slot = i % 2

    def gather(p_ref, s, straight_line):
        def start(r, c=None):
            for kk in range(TOP_K):
                pltpu.make_async_copy(_row_tile(ys_ref, p_ref[0, r * TOP_K + kk]),
                                      _row_tile(gbuf.at[s, kk], r), sems.at[s]).start(priority=kk % 2)
            return c
        if straight_line:
            for r in range(tm):
                start(r)
        else:
            lax.fori_loop(0, tm, start, 0)

    @pl.when(i == 0)
    def _():
        gather(pos_ref, 0, False)

    for s in range(2):
        @pl.when((i + 1 < pl.num_programs(0)) & (slot == 1 - s))
        def _():
            gather(pos_next_ref, s, True)

    for kk in range(TOP_K):
        pltpu.make_async_copy(ys_ref.at[pl.ds(0, tm * ROW_TILE), :], gbuf.at[slot, kk],
                              sems.at[slot]).wait()
    RG = 64

    def routed(rg):
        rows = slice(rg * RG, (rg + 1) * RG)
        gate = gate_ref[rows, :]
        g_wide = [jnp.broadcast_to(gate[:, kk:kk + 1], (RG, 128)) for kk in range(TOP_K)]
        for s in range(ROW_TILE):
            lo = hi = None
            for kk in range(TOP_K):
                w = gbuf[slot, kk, pl.ds(rg * RG * ROW_TILE + s, RG, stride=ROW_TILE), :]
                t_lo = g_wide[kk] * lax.bitcast_convert_type(w << 16, F32)
                t_hi = g_wide[kk] * lax.bitcast_convert_type(w & U32(HI_MASK), F32)
                lo = t_lo if lo is None else lo + t_lo
                hi = t_hi if hi is None else hi + t_hi
            xo_ref[rows, s * 128:(s + 1) * 128] = lo
            xo_ref[rows, HALF_D + s * 128: HALF_D + (s + 1) * 128] = hi

    for rg in range(tm // RG):
        routed(rg)
    y = xo_ref[...] + ysh_ref[...]
    xo_ref[...] = x_ref[...] + mod_ref[5:6, :] * _rms(y, pg_ref[...])


def _combine(pos3, ysh, gate8, x2, mod, post_gain, ys, S):
    T, D = x2.shape
    tm = pos3.shape[2] // TOP_K
    nb = S // tm
    kern = functools.partial(_combine_kernel, tm=tm)
    row = lambda w: pl.BlockSpec((tm, w), lambda i: (i, 0))
    full = lambda a: pl.BlockSpec(a.shape, lambda i: (0,) * a.ndim)
    n_steps = T // tm
    return pl.pallas_call(
        kern,
        grid=(n_steps,),
        in_specs=[pl.BlockSpec((None, 1, tm * TOP_K), lambda i: (i, 0, 0), memory_space=pltpu.SMEM),
                  pl.BlockSpec((None, 1, tm * TOP_K), lambda i: (jnp.minimum(i + 1, n_steps - 1), 0, 0),
                               memory_space=pltpu.SMEM),
                  row(D), row(8), row(D),
                  pl.BlockSpec((None, N_MOD, D), lambda i: (i // nb, 0, 0)),
                  full(post_gain),
                  pl.BlockSpec(memory_space=pl.ANY)],
        out_specs=row(D),
        out_shape=jax.ShapeDtypeStruct((T, D), F32),
        scratch_shapes=[pltpu.VMEM((2, TOP_K, tm * ROW_TILE, 128), U32),
                        pltpu.SemaphoreType.DMA((2,))],
        compiler_params=_cparams(1),
        name="moe_combine",
    )(pos3, pos3, ysh, gate8, x2, mod, post_gain, ys)


def _route_plan(idx8, mask, n_blocks):
    idx = idx8[:, :TOP_K]
    counts = jnp.sum(mask, axis=0)
    rank = jnp.cumsum(mask, axis=0) - mask
    padded = (counts + MOE_BLK - 1) // MOE_BLK * MOE_BLK
    pend = jnp.cumsum(padded)
    pstart = pend - padded
    pos = jnp.take_along_axis(pstart[None, :] + rank, idx, axis=1).astype(jnp.int32)
    blk_start = jnp.arange(n_blocks, dtype=jnp.int32) * MOE_BLK
    blk_e = jnp.minimum(jnp.sum(pend[None, :] <= blk_start[:, None], axis=1), N_EXPERTS - 1)
    nused = (pend[-1:] // MOE_BLK).astype(jnp.int32)
    tail = jnp.stack([pstart + counts, padded - counts], axis=1).reshape(-1).astype(jnp.int32)
    ids = jnp.where(counts > 0, jnp.arange(N_EXPERTS), N_EXPERTS)
    after = jnp.concatenate([lax.cummin(ids, reverse=True)[1:], jnp.full((1,), N_EXPERTS)])
    next_e = jnp.where(after < N_EXPERTS, after, -1).astype(jnp.int32)
    return pos, blk_e.astype(jnp.int32), nused, tail, next_e


def _rope_tables(S, rot_dim):
    rows = S // GRID_W
    row = jnp.repeat(jnp.arange(rows, dtype=F32), GRID_W)
    col = jnp.tile(jnp.arange(GRID_W, dtype=F32), rows)
    axis_dim = rot_dim // 2
    inv_freq = ROPE_THETA ** (-jnp.arange(0, axis_dim, 2, dtype=F32) / axis_dim)
    ang_r = row[:, None] * inv_freq
    ang_c = col[:, None] * inv_freq
    z = jnp.zeros_like(ang_r)
    cos = jnp.concatenate([jnp.cos(ang_r)] * 2 + [jnp.cos(ang_c)] * 2, axis=1)
    sin_lo = jnp.concatenate([-jnp.sin(ang_r), z, -jnp.sin(ang_c), z], axis=1)
    sin_hi = jnp.concatenate([z, jnp.sin(ang_r), z, jnp.sin(ang_c)], axis=1)
    reps = 128 // rot_dim
    return tuple(jnp.tile(t, (1, reps)) for t in (cos, sin_lo, sin_hi))


def _prep_w_in(w):
    widths = (768, 256, 256, B_Q_LORA, B_KV_LORA, B_QK_ROPE, 512, 512, 512, 512, 4 * C_HEADS)
    offs = np.cumsum(widths)[:-1].tolist()
    a_q, a_k, a_v, b_cq, b_ckv, b_kr, c_q, c_k, c_v, c_o, c_g = jnp.split(w, offs, axis=-1)
    pad = jnp.zeros(w.shape[:-1] + (128 - B_QK_ROPE - 4 * C_HEADS,), w.dtype)
    out = jnp.concatenate([a_q, a_k, a_v, b_cq, b_ckv, b_kr, c_g, pad, c_v, c_o, c_q, c_k], axis=-1)
    assert out.shape[-1] == PROJ_PAD
    return out.astype(BF16)


def _split_hi_lo(w):
    hi = w.astype(BF16)
    lo = (w - hi.astype(F32)).astype(BF16)
    return jnp.concatenate([hi, lo], axis=1)


def _heads_split(w, n_heads, first):
    K = w.shape[0]
    w3 = w.reshape(K, n_heads, -1)
    return jnp.concatenate([w3[:, :, :first].reshape(K, -1), w3[:, :, first:].reshape(K, -1)],
                           axis=1).astype(BF16)


def kernel(x, c, ada_w, ada_b, pre_mix_gain, w_in, a_q_gain, a_k_gain, b_cq_gain, b_ckv_gain, w_uq, w_ukv, c_conv_w, c_conv_b, c_gate_b, head_out_gain, w_out, post_mix_gain, pre_ffn_gain, router_w, router_bias, exp_w_gate, exp_w_up, exp_w_down, sh_w_gate, sh_w_up, sh_w_down, post_ffn_gain):
    B, S, D = x.shape
    T = B * S
    L = ada_w.shape[0]
    n_blocks = T * TOP_K // MOE_BLK + N_EXPERTS
    row2 = lambda v: v.reshape(1, -1)

    tabs_a = _rope_tables(S, HEAD_DIM)
    tabs_b = _rope_tables(S, B_QK_ROPE)
    c_pad = jnp.zeros((8, D), F32).at[:B].set(c)
    mods = _modulation(c_pad, ada_w, ada_b)[:, :B].reshape(L, B, N_MOD, D)

    x2 = x.reshape(T, D)
    for l in range(L):
        mod = mods[l]
        qa, ka, va, qb, kb, vb, pc, qk, misc = _in_proj(
            x2, row2(pre_mix_gain[l]), mod, _prep_w_in(w_in[l]), tabs_a, tabs_b,
            row2(a_q_gain[l]), row2(a_k_gain[l]), row2(b_cq_gain[l]), row2(b_ckv_gain[l]),
            _heads_split(w_uq[l], B_HEADS, B_QK_NOPE), _heads_split(w_ukv[l], B_HEADS, B_QK_NOPE),
            c_conv_w[l], row2(c_conv_b[l]), S)
        pc3 = pc.reshape(B, S, -1)

        ya = _attention(qa.reshape(B, S, -1), ka.reshape(B, S, -1), va.reshape(B, S, -1),
                        B=B, S=S, Hk=A_KV_HEADS, G=A_HEADS // A_KV_HEADS, dq=HEAD_DIM, dv=HEAD_DIM,
                        tq=min(512, S), tk=512)

        yb = _attention(qb.reshape(B, S, -1), kb.reshape(B, S, -1), vb.reshape(B, S, -1),
                        B=B, S=S, Hk=B_HEADS, G=1, dq=256, dv=B_V_DIM, tq=min(1024, S), tk=512)

        qk3 = qk.reshape(B, S, -1)
        gate_b_row = jnp.zeros((1, 128), F32).at[0, MISC_GATE_LANE:MISC_GATE_LANE + 4 * C_HEADS].set(c_gate_b[l])
        hf, hr = _mlstm(qk3, pc3, misc.reshape(B, S, 128), gate_b_row)

        x2, h2p, logits = _merge(ya.reshape(T, -1), yb.reshape(T, -1), hf.reshape(T, -1),
                                hr.reshape(T, -1), pc, x2, mod,
                                row2(head_out_gain[l]), w_out[l].astype(BF16), row2(post_mix_gain[l]),
                                row2(pre_ffn_gain[l]), _split_hi_lo(router_w[l]), S)

        idx8, gate8, mask = _router(logits, row2(router_bias[l]))
        pos, blk_e, nused, tail, next_e = _route_plan(idx8, mask, n_blocks)
        xs, ysh = _dispatch(h2p, pos.reshape(T // DISPATCH_TM, 1, DISPATCH_TM * TOP_K), tail,
                            sh_w_gate[l].astype(BF16), sh_w_up[l].astype(BF16),
                            sh_w_down[l].astype(BF16), n_blocks * MOE_BLK)
        ys = _experts(xs, blk_e, nused, next_e, exp_w_gate, exp_w_up, exp_w_down, l)
        x2 = _combine(pos.reshape(T // COMBINE_TM, 1, COMBINE_TM * TOP_K), ysh, gate8, x2, mod,
                      row2(post_ffn_gain[l]), ys, S)
    return x2.reshape(B, S, D)
```

```python
import functools

import numpy as np
import jax
import jax.numpy as jnp
from jax import lax
from jax.experimental import pallas as pl
from jax.experimental.pallas import tpu as pltpu

D_MODEL = 2048
GRID_W = 64
ROPE_THETA = 10000.0
EPS = 1e-6
HEAD_DIM = 128
A_HEADS, A_KV_HEADS = 6, 2
B_HEADS, B_Q_LORA, B_KV_LORA = 6, 384, 256
B_QK_NOPE, B_QK_ROPE, B_V_DIM = 128, 64, 128
C_HEADS, C_DQK, C_DV = 4, 128, 128
CONV_W = 5
MLSTM_CHUNK = 64
N_EXPERTS, TOP_K, EXPERT_FF, SHARED_FF = 64, 6, 512, 512
ROUTED_SCALE = 2.5
N_MOD = 6

COL_AQ, COL_AK, COL_AV = 0, 768, 1024
COL_BC = 1280
COL_MISC = 1920
COL_CV, COL_CO = 2048, 2560
COL_CQK = 3072
PROJ_PAD = 4096
MISC_GATE_LANE = 64
PC_CV, PC_CO = 0, COL_CO - COL_CV

MOE_BLK = 256
DISPATCH_TM = 256
COMBINE_TM = 256
MLSTM_BB = 4
ATTN_ROW_GROUP = 256
VMEM_LIMIT = 56 * 1024 * 1024

BF16 = jnp.bfloat16
F32 = jnp.float32
LOG2_E = 1.4426950408889634


def _cparams(grid_rank):
    return pltpu.CompilerParams(dimension_semantics=("arbitrary",) * grid_rank,
                                vmem_limit_bytes=VMEM_LIMIT)


def _rms(x, gain):
    return x * lax.rsqrt(jnp.mean(x * x, axis=-1, keepdims=True) + EPS) * gain


def _dot(a, b):
    return jnp.dot(a, b, preferred_element_type=F32)


def _dot_t(a, b):
    return lax.dot_general(a, b, (((1,), (1,)), ((), ())), preferred_element_type=F32)


def _mod_kernel(c_ref, w_ref, b_ref, o_ref):
    c = c_ref[...]
    c_act = (c * jax.nn.sigmoid(c)).astype(BF16)
    o_ref[...] = _dot(c_act, w_ref[...].astype(BF16)) + b_ref[...]


def _modulation(c_pad, ada_w, ada_b):
    L, D, N = ada_w.shape
    M = c_pad.shape[0]
    tn = 1536
    return pl.pallas_call(
        _mod_kernel,
        grid=(L, N // tn),
        in_specs=[pl.BlockSpec((M, D), lambda l, j: (0, 0)),
                  pl.BlockSpec((None, D, tn), lambda l, j: (l, 0, j)),
                  pl.BlockSpec((None, 1, tn), lambda l, j: (l, 0, j))],
        out_specs=pl.BlockSpec((None, M, tn), lambda l, j: (l, 0, j)),
        out_shape=jax.ShapeDtypeStruct((L, M, N), F32),
        compiler_params=_cparams(2),
        name="adaln_mod",
    )(c_pad, ada_w, ada_b.reshape(L, 1, N))


def _conv_silu(main, prev, nxt, w_ref, b_ref):
    tm, C = main.shape
    row = lax.broadcasted_iota(jnp.int32, main.shape, 0)
    pad = CONV_W // 2
    y = main * w_ref[pad:pad + 1, :] + b_ref[...]
    for j in range(CONV_W):
        d = j - pad
        if d == 0:
            continue
        shifted = pltpu.roll(main, (-d) % tm, 0)
        for e in range(abs(d)):
            if d < 0:
                shifted = jnp.where(row == e, prev[8 + d + e: 9 + d + e, :], shifted)
            else:
                shifted = jnp.where(row == tm - d + e, nxt[e:e + 1, :], shifted)
        y = y + shifted * w_ref[j:j + 1, :]
    y = y * jax.nn.sigmoid(y)
    lane = lax.broadcasted_iota(jnp.int32, main.shape, 1)
    return y * jnp.where(lane >= C // 2, C_DQK ** -0.5, 1.0)


def _in_proj_kernel(x_ref, xp_ref, xn_ref, gain_ref, mod_ref, w_ref,
                    cosa_ref, sloa_ref, shia_ref, cosb_ref, slob_ref, shib_ref,
                    qg_ref, kg_ref, cqg_ref, ckvg_ref, wuq_ref, wukv_ref, cw_ref, cb_ref,
                    qa_ref, ka_ref, va_ref, qb_ref, kb_ref, vb_ref, pc_ref, qk_ref, misc_ref, *, nb):
    def premix(x):
        y = _rms(x, gain_ref[...])
        return (y * (1.0 + mod_ref[1:2, :]) + mod_ref[0:1, :]).astype(BF16)

    h = premix(x_ref[...])
    pa = _dot(h, w_ref[:, COL_AQ:COL_BC])
    _gqa_prep(pa, (cosa_ref[...], sloa_ref[...], shia_ref[...]), qg_ref[...], kg_ref[...],
              qa_ref, ka_ref, va_ref)
    pb = _dot(h, w_ref[:, COL_BC:COL_CV])
    misc = pb[:, COL_MISC - COL_BC:]
    misc_ref[...] = misc
    _mla_prep(pb[:, :COL_MISC - COL_BC], misc, (cosb_ref[...], slob_ref[...], shib_ref[...]),
              cqg_ref[...], ckvg_ref[...], wuq_ref, wukv_ref, qb_ref, kb_ref, vb_ref)
    pc_ref[...] = _dot(h, w_ref[:, COL_CV:COL_CQK])
    w_qk = w_ref[:, COL_CQK:]
    i = pl.program_id(0)
    halo = premix(jnp.concatenate([xp_ref[...], xn_ref[...]], axis=0))
    halo = _dot(halo, w_qk)
    prev = jnp.where(i % nb == 0, 0.0, halo[:8, :])
    nxt = jnp.where(i % nb == nb - 1, 0.0, halo[8:, :])
    qk_ref[...] = _conv_silu(_dot(h, w_qk), prev, nxt, cw_ref, cb_ref)


def _in_proj(x2, gain, mod, w_in_p, tabs_a, tabs_b, q_gain, k_gain, cq_gain, ckv_gain, w_uq_p, w_ukv_p,
             conv_w, conv_b, S):
    T, D = x2.shape
    tm = 256
    nb = S // tm
    n8 = tm // 8
    row = lambda w: pl.BlockSpec((tm, w), lambda i: (i, 0))
    full = lambda a: pl.BlockSpec(a.shape, lambda i: (0,) * a.ndim)
    tab = pl.BlockSpec((tm, 128), lambda i: (i % nb, 0))
    qw, kw = A_HEADS * HEAD_DIM, A_KV_HEADS * HEAD_DIM
    bqw, bvw = B_HEADS * 256, B_HEADS * B_V_DIM
    cvo, cqk = COL_CQK - COL_CV, PROJ_PAD - COL_CQK
    widths = (qw, kw, kw, bqw, bqw, bvw)
    return pl.pallas_call(
        functools.partial(_in_proj_kernel, nb=nb),
        grid=(T // tm,),
        in_specs=[row(D),
                  pl.BlockSpec((8, D), lambda i: (jnp.maximum(i * n8 - 1, 0), 0)),
                  pl.BlockSpec((8, D), lambda i: (jnp.minimum((i + 1) * n8, T // 8 - 1), 0)),
                  full(gain),
                  pl.BlockSpec((None, N_MOD, D), lambda i: (i // nb, 0, 0)),
                  pl.BlockSpec(w_in_p.shape, lambda i: (0, 0), pipeline_mode=pl.Buffered(1)),
                  tab, tab, tab, tab, tab, tab,
                  full(q_gain), full(k_gain), full(cq_gain), full(ckv_gain), full(w_uq_p), full(w_ukv_p),
                  full(conv_w), full(conv_b)],
        out_specs=[row(w) for w in widths] + [row(cvo), row(cqk), row(128)],
        out_shape=[jax.ShapeDtypeStruct((T, w), BF16) for w in widths]
                  + [jax.ShapeDtypeStruct((T, cvo), F32), jax.ShapeDtypeStruct((T, cqk), F32),
                     jax.ShapeDtypeStruct((T, 128), F32)],
        compiler_params=_cparams(1),
        name="in_proj",
    )(x2, x2, x2, gain, mod, w_in_p, *tabs_a, *tabs_b, q_gain, k_gain, cq_gain, ckv_gain, w_uq_p, w_ukv_p,
      conv_w, conv_b)


def _rope128(y, cos, sin_lo, sin_hi):
    return y * cos + pltpu.roll(y, 96, 1) * sin_lo + pltpu.roll(y, 32, 1) * sin_hi


def _gqa_prep(pa, tabs, q_gain, k_gain, qo_ref, ko_ref, vo_ref):
    cos, slo, shi = tabs
    scale = HEAD_DIM ** -0.5 * LOG2_E
    for h in range(A_HEADS):
        sl = slice(h * HEAD_DIM, (h + 1) * HEAD_DIM)
        y = _rope128(_rms(pa[:, COL_AQ + h * HEAD_DIM: COL_AQ + (h + 1) * HEAD_DIM], q_gain), cos, slo, shi)
        qo_ref[:, sl] = (y * scale).astype(BF16)
    for h in range(A_KV_HEADS):
        sl = slice(h * HEAD_DIM, (h + 1) * HEAD_DIM)
        y = _rope128(_rms(pa[:, COL_AK + h * HEAD_DIM: COL_AK + (h + 1) * HEAD_DIM], k_gain), cos, slo, shi)
        ko_ref[:, sl] = y.astype(BF16)
    vo_ref[...] = pa[:, COL_AV:COL_BC].astype(BF16)


def _rope64x2(y, cos, sin_lo, sin_hi):
    return y * cos + pltpu.roll(y, 112, 1) * sin_lo + pltpu.roll(y, 16, 1) * sin_hi


def _mla_prep(bc, misc, tabs, cq_gain, ckv_gain, wuq_ref, wukv_ref, qo_ref, ko_ref, vo_ref):
    cos, slo, shi = tabs
    lane = lax.broadcasted_iota(jnp.int32, cos.shape, 1)
    first = lane < B_QK_ROPE
    scale = (B_QK_NOPE + B_QK_ROPE) ** -0.5 * LOG2_E
    nope_w = B_HEADS * B_QK_NOPE

    cq = _rms(bc[:, :B_Q_LORA], cq_gain).astype(BF16)
    q = _dot(cq, wuq_ref[...]) * scale
    for p in range(B_HEADS // 2):
        pair = _rope64x2(q[:, nope_w + p * 128: nope_w + (p + 1) * 128], cos, slo, shi)
        for half in range(2):
            h = 2 * p + half
            qo_ref[:, h * 256: h * 256 + 128] = q[:, h * 128:(h + 1) * 128].astype(BF16)
            keep = first if half == 0 else jnp.logical_not(first)
            qo_ref[:, h * 256 + 128: (h + 1) * 256] = jnp.where(keep, pair, 0.0).astype(BF16)

    kr = jnp.where(first, misc, 0.0)
    kr = kr + pltpu.roll(kr, B_QK_ROPE, 1)
    kr = _rope64x2(kr, cos, slo, shi).astype(BF16)
    ckv = _rms(bc[:, B_Q_LORA:], ckv_gain).astype(BF16)
    kv = _dot(ckv, wukv_ref[...])
    for h in range(B_HEADS):
        ko_ref[:, h * 256: h * 256 + 128] = kv[:, h * 128:(h + 1) * 128].astype(BF16)
        ko_ref[:, h * 256 + 128: (h + 1) * 256] = kr
    vo_ref[...] = kv[:, nope_w:].astype(BF16)


def _attn_kernel(q_ref, k_ref, v_ref, o_ref, m_scr, l_scr, acc_scr, *, G, dq, dv, tk, sub):
    tq = q_ref.shape[0]
    S = k_ref.shape[0]
    groups = [(g, r) for g in range(G) for r in range(tq // sub)]
    m_scr[...] = jnp.full(m_scr.shape, -jnp.inf, F32)
    l_scr[...] = jnp.zeros(l_scr.shape, F32)
    acc_scr[...] = jnp.zeros(acc_scr.shape, F32)

    stages = [(c, n) for c in range(S // tk) for n in range(len(groups))]

    def scores(stage):
        c, n = stage
        g, r = groups[n]
        return _dot_t(q_ref[r * sub:(r + 1) * sub, g * dq:(g + 1) * dq], k_ref[c * tk:(c + 1) * tk, :])

    s_next = scores(stages[0])
    for i, (c, n) in enumerate(stages):
        s = s_next
        if i + 1 < len(stages):
            s_next = scores(stages[i + 1])
        rows = slice(n * sub, (n + 1) * sub)
        blocks = [s[:, j * 128:(j + 1) * 128] for j in range(tk // 128)]
        m_blk = functools.reduce(jnp.maximum, blocks)
        m_old = m_scr[rows, :]
        m_new = jnp.maximum(m_old, jnp.max(m_blk, axis=-1, keepdims=True))
        alpha = jnp.exp2(m_old - m_new)
        p_blocks = [jnp.exp2(b - m_new) for b in blocks]
        l_scr[rows, :] = alpha * l_scr[rows, :] + functools.reduce(jnp.add, p_blocks)
        p = jnp.concatenate([b.astype(BF16) for b in p_blocks], axis=1)
        acc_scr[rows, :] = alpha * acc_scr[rows, :] + _dot(p, v_ref[c * tk:(c + 1) * tk, :])
        m_scr[rows, :] = m_new

    for n, (g, r) in enumerate(groups):
        rows = slice(n * sub, (n + 1) * sub)
        l = jnp.sum(l_scr[rows, :], axis=-1, keepdims=True)
        o_ref[r * sub:(r + 1) * sub, g * dv:(g + 1) * dv] = (acc_scr[rows, :] / l).astype(o_ref.dtype)


def _attention(q, k, v, *, B, S, Hk, G, dq, dv, tq, tk):
    assert dv == 128, "row statistics are kept 128 lanes wide to match the value width"
    kern = functools.partial(_attn_kernel, G=G, dq=dq, dv=dv, tk=tk, sub=ATTN_ROW_GROUP)
    M = G * tq
    return pl.pallas_call(
        kern,
        grid=(B, Hk, S // tq),
        in_specs=[pl.BlockSpec((None, tq, G * dq), lambda b, h, i: (b, i, h)),
                  pl.BlockSpec((None, S, dq), lambda b, h, i: (b, 0, h)),
                  pl.BlockSpec((None, S, dv), lambda b, h, i: (b, 0, h))],
        out_specs=pl.BlockSpec((None, tq, G * dv), lambda b, h, i: (b, i, h)),
        out_shape=jax.ShapeDtypeStruct((B, S, Hk * G * dv), BF16),
        scratch_shapes=[pltpu.VMEM((M, 128), F32), pltpu.VMEM((M, 128), F32), pltpu.VMEM((M, dv), F32)],
        compiler_params=_cparams(3),
        name=f"attn_g{G}_d{dq}",
    )(q, k, v)


def _log_sigmoid(x):
    return jnp.minimum(x, 0.0) - jnp.log1p(jnp.exp(-jnp.abs(x)))


def _mlstm_kernel(qf_ref, kf_ref, vf_ref, gf_ref, qr_ref, kr_ref, vr_ref, gr_ref, gb_ref,
                  of_ref, or_ref, C_scr, n_scr, m_scr, *, BB):
    L = MLSTM_CHUNK

    @pl.when(pl.program_id(1) == 0)
    def _():
        C_scr[...] = jnp.zeros(C_scr.shape, F32)
        n_scr[...] = jnp.zeros(n_scr.shape, F32)
        m_scr[...] = jnp.zeros(m_scr.shape, F32)

    glane = lax.broadcasted_iota(jnp.int32, (L, 128), 1)
    jj = lax.broadcasted_iota(jnp.int32, (L, L), 0)
    ss = lax.broadcasted_iota(jnp.int32, (L, L), 1)
    eye = jj == ss
    neg_inf = jnp.float32(-jnp.inf)
    dirs = ((qf_ref, kf_ref, vf_ref, gf_ref, of_ref, ss <= jj, jj <= ss),
            (qr_ref, kr_ref, vr_ref, gr_ref, or_ref, ss >= jj, jj >= ss))

    chains = []
    for d, (q_ref, k_ref, v_ref, g_ref, o_ref, seen, seen_t) in enumerate(dirs):
        for bb in range(BB):
            gates = g_ref[bb] + gb_ref[...]
            lsig = _log_sigmoid(gates)
            for h in range(C_HEADS):
                c = dict(st=(d * BB + bb) * C_HEADS + h, sl=slice(h * C_DQK, (h + 1) * C_DQK), bb=bb,
                         q_ref=q_ref, k_ref=k_ref, v_ref=v_ref, o_ref=o_ref, seen=seen, seen_t=seen_t)
                i_lane = MISC_GATE_LANE + d * (2 * C_HEADS) + h
                c["i_col"] = jnp.sum(jnp.where(glane == i_lane, gates, 0.0), axis=1, keepdims=True)
                c["f_col"] = jnp.sum(jnp.where(glane == i_lane + C_HEADS, lsig, 0.0), axis=1, keepdims=True)
                chains.append(c)
    for c in chains:
        c["f_row"] = jnp.sum(jnp.where(eye, c["f_col"], 0.0), axis=0, keepdims=True)
        c["i_row"] = jnp.sum(jnp.where(eye, c["i_col"], 0.0), axis=0, keepdims=True)
        c["b_row"] = jnp.sum(jnp.where(c["seen_t"], c["f_col"], 0.0), axis=0, keepdims=True)
        c["g_tot"] = jnp.sum(c["f_col"], axis=0, keepdims=True)
    for c in chains:
        c["b_col"] = jnp.sum(jnp.where(c["seen"], c["f_row"], 0.0), axis=1, keepdims=True)
    for c in chains:
        c["m_prev"] = m_scr[c["st"]][:, 0:1]
        c["dmat"] = jnp.where(c["seen"], c["b_col"] - c["b_row"] + c["i_row"], neg_inf)
        c["m_inter"] = c["b_col"] + c["m_prev"]
        c["a_col"] = c["g_tot"] - c["b_col"] + c["i_col"]
    for c in chains:
        c["m_j"] = jnp.maximum(c["m_inter"], jnp.max(c["dmat"], axis=1, keepdims=True))
        c["m_new"] = jnp.maximum(c["g_tot"] + c["m_prev"], jnp.max(c["a_col"], axis=0, keepdims=True))
    for c in chains:
        c["q"] = c["q_ref"][c["bb"], :, c["sl"]]
        c["k"] = c["k_ref"][c["bb"], :, c["sl"]]
        c["qb"], c["kb"] = c["q"].astype(BF16), c["k"].astype(BF16)
        c["vb"] = c["v_ref"][c["bb"], :, c["sl"]].astype(BF16)
        c["qk"] = _dot_t(c["qb"], c["kb"])
    for c in chains:
        c["qC"] = _dot(c["qb"], C_scr[c["st"]].astype(BF16))
    for c in chains:
        c["s"] = c["qk"] * jnp.exp(c["dmat"] - c["m_j"])
        c["inter"] = jnp.exp(c["m_inter"] - c["m_j"])
        c["wk"] = jnp.exp(c["a_col"] - c["m_new"]) * c["k"]
        c["decay"] = jnp.exp(c["g_tot"] + c["m_prev"] - c["m_new"])
    for c in chains:
        c["sv"] = _dot(c["s"].astype(BF16), c["vb"])
    for c in chains:
        c["upd"] = lax.dot_general(c["wk"].astype(BF16), c["vb"], (((0,), (0,)), ((), ())),
                                   preferred_element_type=F32)
    for c in chains:
        n_prev = n_scr[c["st"]]
        den = (jnp.sum(c["s"], axis=1, keepdims=True)
               + c["inter"] * jnp.sum(c["q"] * n_prev, axis=1, keepdims=True))
        num = c["sv"] + c["inter"] * c["qC"]
        c["o_ref"][c["bb"], :, c["sl"]] = num / jnp.maximum(jnp.abs(den), jnp.exp(-c["m_j"]))
        n_scr[c["st"]] = c["decay"] * n_prev + jnp.sum(c["wk"], axis=0, keepdims=True)
    for c in chains:
        C_scr[c["st"]] = c["decay"] * C_scr[c["st"]] + c["upd"]
        m_scr[c["st"]] = jnp.broadcast_to(c["m_new"], m_scr.shape[1:])


def _mlstm(qk3, pc3, misc3, gate_b_row, BB=MLSTM_BB):
    B, S, _ = pc3.shape
    BB = min(BB, B)
    L = MLSTM_CHUNK
    nc = S // L
    W = C_HEADS * C_DQK
    n_chain = 2 * BB * C_HEADS
    fwd = lambda col: (lambda b, c: (b, c, col))
    rev = lambda col: (lambda b, c: (b, nc - 1 - c, col))
    blk = lambda w, imap: pl.BlockSpec((BB, L, w), imap)
    out = jax.ShapeDtypeStruct((B, S, W), F32)
    return pl.pallas_call(
        functools.partial(_mlstm_kernel, BB=BB),
        grid=(B // BB, nc),
        in_specs=[blk(W, fwd(0)), blk(W, fwd(1)), blk(W, fwd(PC_CV // W)), blk(128, fwd(0)),
                  blk(W, rev(0)), blk(W, rev(1)), blk(W, rev(PC_CV // W)), blk(128, rev(0)),
                  pl.BlockSpec((1, 128), lambda b, c: (0, 0))],
        out_specs=[blk(W, fwd(0)), blk(W, rev(0))],
        out_shape=[out, out],
        scratch_shapes=[pltpu.VMEM((n_chain, C_DQK, C_DV), F32),
                        pltpu.VMEM((n_chain, 1, C_DQK), F32),
                        pltpu.VMEM((n_chain, 1, 128), F32)],
        compiler_params=_cparams(2),
        name="mlstm",
    )(qk3, qk3, pc3, misc3, qk3, qk3, pc3, misc3, gate_b_row)


ROW_TILE = 8
HALF_D = D_MODEL // 2
U32 = jnp.uint32
HI_MASK = 0xFFFF0000


def _pack_rows(y, o_ref):
    n = y.shape[0]
    for s in range(ROW_TILE):
        lo = y[:, s * 128:(s + 1) * 128].astype(BF16).astype(F32)
        hi = y[:, HALF_D + s * 128: HALF_D + (s + 1) * 128].astype(BF16).astype(F32)
        w = (lax.bitcast_convert_type(lo, U32) >> 16) | (lax.bitcast_convert_type(hi, U32) & U32(HI_MASK))
        o_ref[pl.ds(s, n, stride=ROW_TILE), :] = w


def _unpack_rows(ref, n):
    lo, hi = [], []
    for s in range(ROW_TILE):
        w = ref[pl.ds(s, n, stride=ROW_TILE), :]
        lo.append(lax.bitcast_convert_type(w << 16, F32))
        hi.append(lax.bitcast_convert_type(w & U32(HI_MASK), F32))
    return lo + hi


def _merge_kernel(ya_ref, yb_ref, hf_ref, hr_ref, co_ref, x_ref, mod_ref, hg_ref, wout_ref, pmg_ref,
                  pfg_ref, rw_ref, xo_ref, h2p_ref, lg_ref, y_scr):
    attn_heads = A_HEADS + B_HEADS
    for h in range(attn_heads + C_HEADS):
        sl = slice(h * HEAD_DIM, (h + 1) * HEAD_DIM)
        g = hg_ref[:, sl]
        if h < A_HEADS:
            y = _rms(ya_ref[:, sl].astype(F32), g)
        elif h < attn_heads:
            y = _rms(yb_ref[:, (h - A_HEADS) * HEAD_DIM:(h - A_HEADS + 1) * HEAD_DIM].astype(F32), g)
        else:
            cs = slice((h - attn_heads) * HEAD_DIM, (h - attn_heads + 1) * HEAD_DIM)
            y = _rms(hf_ref[:, cs] + hr_ref[:, cs], g) * jax.nn.sigmoid(co_ref[:, cs])
        y_scr[:, sl] = y.astype(BF16)
    y = _dot(y_scr[...], wout_ref[...])
    x_new = x_ref[...] + mod_ref[2:3, :] * _rms(y, pmg_ref[...])
    xo_ref[...] = x_new
    h2 = _rms(x_new, pfg_ref[...]) * (1.0 + mod_ref[4:5, :]) + mod_ref[3:4, :]
    _pack_rows(h2, h2p_ref)
    h_hi = h2.astype(BF16)
    h_lo = (h2 - h_hi.astype(F32)).astype(BF16)
    a = _dot(h_hi, rw_ref[...])
    b = _dot(h_lo, rw_ref[...])
    E = N_EXPERTS
    lg_ref[...] = a[:, :E] + (a[:, E:] + b[:, :E]) + b[:, E:]


def _merge(ya, yb, hf, hr, pc, x2, mod, head_gain, w_out_b, pm_gain, pf_gain, router_w, S):
    T, D = x2.shape
    tm = 256
    nb = S // tm
    aw, cw = A_HEADS * HEAD_DIM, C_HEADS * C_DV
    row = lambda w: pl.BlockSpec((tm, w), lambda i: (i, 0))
    full = lambda a: pl.BlockSpec(a.shape, lambda i: (0,) * a.ndim)
    return pl.pallas_call(
        _merge_kernel,
        grid=(T // tm,),
        in_specs=[row(aw), row(aw), row(cw), row(cw),
                  pl.BlockSpec((tm, cw), lambda i: (i, PC_CO // cw)),
                  row(D),
                  pl.BlockSpec((None, N_MOD, D), lambda i: (i // nb, 0, 0)),
                  full(head_gain), full(w_out_b), full(pm_gain), full(pf_gain), full(router_w)],
        out_specs=[row(D), pl.BlockSpec((tm * ROW_TILE, 128), lambda i: (i, 0)),
                   row(N_EXPERTS)],
        out_shape=[jax.ShapeDtypeStruct((T, D), F32),
                   jax.ShapeDtypeStruct((T * ROW_TILE, 128), U32),
                   jax.ShapeDtypeStruct((T, N_EXPERTS), F32)],
        scratch_shapes=[pltpu.VMEM((tm, D), BF16)],
        compiler_params=_cparams(1),
        name="merge_out_proj",
    )(ya, yb, hf, hr, pc, x2, mod, head_gain, w_out_b, pm_gain, pf_gain, router_w)


def _router_kernel(lg_ref, bias_ref, idx_ref, gate_ref, mask_ref):
    scores = jax.nn.sigmoid(lg_ref[...])
    sel = scores + bias_ref[...]
    lane = lax.broadcasted_iota(jnp.int32, scores.shape, 1).astype(F32)
    col = lax.broadcasted_iota(jnp.int32, idx_ref.shape, 1)
    idx = jnp.zeros(idx_ref.shape, F32)
    gate = jnp.zeros(gate_ref.shape, F32)
    mask = jnp.zeros(scores.shape, F32)
    for kk in range(TOP_K):
        mx = jnp.max(sel, axis=1, keepdims=True)
        am = jnp.min(jnp.where(sel == mx, lane, float(N_EXPERTS)), axis=1, keepdims=True)
        hit = lane == am
        sc = jnp.sum(jnp.where(hit, scores, 0.0), axis=1, keepdims=True)
        idx = jnp.where(col == kk, am, idx)
        gate = jnp.where(col == kk, sc, gate)
        mask = jnp.where(hit, 1.0, mask)
        sel = jnp.where(hit, -jnp.inf, sel)
    gate = gate / jnp.sum(gate, axis=1, keepdims=True) * ROUTED_SCALE
    idx_ref[...] = idx.astype(jnp.int32)
    gate_ref[...] = gate
    mask_ref[...] = mask.astype(jnp.int32)


def _router(logits, bias):
    T, E = logits.shape
    tm = 1024
    row = lambda w: pl.BlockSpec((tm, w), lambda i: (i, 0))
    return pl.pallas_call(
        _router_kernel,
        grid=(T // tm,),
        in_specs=[row(E), pl.BlockSpec((1, E), lambda i: (0, 0))],
        out_specs=[row(8), row(8), row(E)],
        out_shape=[jax.ShapeDtypeStruct((T, 8), jnp.int32),
                   jax.ShapeDtypeStruct((T, 8), F32),
                   jax.ShapeDtypeStruct((T, E), jnp.int32)],
        compiler_params=_cparams(1),
        name="router_topk",
    )(logits, bias)


TAIL_BITS = MOE_BLK.bit_length() - 1


def _tail_copies(tail_ref, zero_scr, xs_ref, zsem, fn):
    for e in range(N_EXPERTS):
        start_row = tail_ref[2 * e]
        n_tail = tail_ref[2 * e + 1]
        for bit in range(TAIL_BITS):
            size = 1 << bit
            cur = start_row + (n_tail & ~(2 * size - 1))
            cur = pl.multiple_of(cur * ROW_TILE, ROW_TILE)

            @pl.when((n_tail & size) != 0)
            def _():
                fn(pltpu.make_async_copy(zero_scr.at[pl.ds(0, size * ROW_TILE), :],
                                         xs_ref.at[pl.ds(cur, size * ROW_TILE), :], zsem))


def _row_tile(ref, row):
    start = row * ROW_TILE
    if not isinstance(row, int):
        start = pl.multiple_of(start, ROW_TILE)
    return ref.at[pl.ds(start, ROW_TILE), :]


def _dispatch_kernel(tail_ref, pos_ref, hp_ref, wsg_ref, wsu_ref, wsd_ref, xs_ref, ysh_ref,
                     zero_scr, sem, zsem, *, tm):
    i = pl.program_id(0)

    for r in range(tm):
        src = _row_tile(hp_ref, r)
        for kk in range(TOP_K):
            pltpu.make_async_copy(src, _row_tile(xs_ref, pos_ref[0, r * TOP_K + kk]),
                                  sem).start(priority=kk % 2)

    @pl.when(i == 0)
    def _():
        zero_scr[...] = jnp.zeros(zero_scr.shape, zero_scr.dtype)
        _tail_copies(tail_ref, zero_scr, xs_ref, zsem, lambda cp: cp.start())
        _tail_copies(tail_ref, zero_scr, xs_ref, zsem, lambda cp: cp.wait())

    h = jnp.concatenate([c.astype(BF16) for c in _unpack_rows(hp_ref, tm)], axis=1)
    gte = _dot(h, wsg_ref[...])
    a = gte * jax.nn.sigmoid(gte) * _dot(h, wsu_ref[...])
    ysh_ref[...] = _dot(a.astype(BF16), wsd_ref[...])

    for kk in range(TOP_K):
        pltpu.make_async_copy(hp_ref, xs_ref.at[pl.ds(0, tm * ROW_TILE), :], sem).wait()


def _dispatch(h2p, pos3, tail_info, ws_gate_b, ws_up_b, ws_down_b, P):
    T, D = h2p.shape[0] // ROW_TILE, D_MODEL
    tm = pos3.shape[2] // TOP_K
    kern = functools.partial(_dispatch_kernel, tm=tm)
    full = lambda a: pl.BlockSpec(a.shape, lambda i, tail: (0,) * a.ndim)
    return pl.pallas_call(
        kern,
        grid_spec=pltpu.PrefetchScalarGridSpec(
            num_scalar_prefetch=1,
            grid=(T // tm,),
            in_specs=[pl.BlockSpec((None, 1, tm * TOP_K), lambda i, tail: (i, 0, 0),
                                   memory_space=pltpu.SMEM),
                      pl.BlockSpec((tm * ROW_TILE, 128), lambda i, tail: (i, 0)),
                      full(ws_gate_b), full(ws_up_b), full(ws_down_b)],
            out_specs=[pl.BlockSpec(memory_space=pl.ANY),
                       pl.BlockSpec((tm, D), lambda i, tail: (i, 0))],
            scratch_shapes=[pltpu.VMEM((MOE_BLK // 2 * ROW_TILE, 128), U32),
                            pltpu.SemaphoreType.DMA(()), pltpu.SemaphoreType.DMA(())]),
        out_shape=[jax.ShapeDtypeStruct((P * ROW_TILE, 128), U32),
                   jax.ShapeDtypeStruct((T, D), F32)],
        compiler_params=_cparams(1),
        name="moe_dispatch",
    )(tail_info, pos3, h2p, ws_gate_b, ws_up_b, ws_down_b)


EXPERT_TILES = 4


def _expert_kernel(blk_e_ref, nused_ref, next_ref, xs_ref, wg_hbm, wu_hbm, wd_hbm, ys_ref,
                   wg_f, wu_f, wd_f, wg_b, wu_b, wd_b, sems, *, layer):
    step = pl.program_id(0)
    rows = MOE_BLK * ROW_TILE

    def fetch(ex):
        return [pltpu.make_async_copy(hbm.at[layer, ex], buf, sems.at[n])
                for n, (hbm, buf) in enumerate(((wg_hbm, wg_f), (wu_hbm, wu_f), (wd_hbm, wd_f)))]

    @pl.when(step == 0)
    def _():
        for cp in fetch(blk_e_ref[0]):
            cp.start(priority=1)

    for j in range(EXPERT_TILES):
        i = step * EXPERT_TILES + j
        e = blk_e_ref[i]
        e_prev = blk_e_ref[jnp.maximum(i - 1, 0)]
        used = i < nused_ref[0]
        first = used & ((i == 0) | (e != e_prev))
        x_view = xs_ref.at[pl.ds(j * rows, rows), :]
        y_view = ys_ref.at[pl.ds(j * rows, rows), :]

        def compute(switch, e=e, x_view=x_view, y_view=y_view):
            x = jnp.concatenate([c.astype(BF16) for c in _unpack_rows(x_view, MOE_BLK)], axis=1)
            if switch:
                for cp in fetch(e):
                    cp.wait()
                wg_b[...] = wg_f[...].astype(BF16)
            gte = _dot(x, wg_b[...])
            if switch:
                wu_b[...] = wu_f[...].astype(BF16)
            up = _dot(x, wu_b[...])
            if switch:
                wd_b[...] = wd_f[...].astype(BF16)
                e_next = next_ref[e]

                @pl.when(e_next >= 0)
                def _():
                    for cp in fetch(e_next):
                        cp.start(priority=1)
            a = gte * jax.nn.sigmoid(gte) * up
            _pack_rows(_dot(a.astype(BF16), wd_b[...]), y_view)

        pl.when(first)(functools.partial(compute, True))
        pl.when(used & jnp.logical_not(first))(functools.partial(compute, False))

        @pl.when(jnp.logical_not(used))
        def _(y_view=y_view):
            y_view[...] = jnp.zeros(y_view.shape, y_view.dtype)


def _experts(xs, blk_e, nused, next_e, w_gate, w_up, w_down, layer):
    P = xs.shape[0] // ROW_TILE
    D, F = w_gate.shape[-2:]
    NB = P // MOE_BLK
    assert NB % EXPERT_TILES == 0
    rows = EXPERT_TILES * MOE_BLK * ROW_TILE
    last = lambda s, nu: jnp.minimum(s, (nu[0] - 1) // EXPERT_TILES)
    hbm = pl.BlockSpec(memory_space=pl.ANY)
    return pl.pallas_call(
        functools.partial(_expert_kernel, layer=layer),
        grid_spec=pltpu.PrefetchScalarGridSpec(
            num_scalar_prefetch=3,
            grid=(NB // EXPERT_TILES,),
            in_specs=[pl.BlockSpec((rows, 128), lambda s, be, nu, nx: (last(s, nu), 0)),
                      hbm, hbm, hbm],
            out_specs=pl.BlockSpec((rows, 128), lambda s, be, nu, nx: (s, 0)),
            scratch_shapes=[pltpu.VMEM((D, F), F32), pltpu.VMEM((D, F), F32), pltpu.VMEM((F, D), F32),
                            pltpu.VMEM((D, F), BF16), pltpu.VMEM((D, F), BF16),
                            pltpu.VMEM((F, D), BF16), pltpu.SemaphoreType.DMA((3,))]),
        out_shape=jax.ShapeDtypeStruct((P * ROW_TILE, 128), U32),
        compiler_params=_cparams(1),
        name="moe_experts",
    )(blk_e, nused, next_e, xs, w_gate, w_up, w_down)


def _combine_kernel(pos_ref, pos_next_ref, ysh_ref, gate_ref, x_ref, mod_ref, pg_ref, ys_ref,
                    xo_ref, gbuf, sems, *, tm):
    i = pl.program_id(0)
    slot = i % 2

    def gather(p_ref, s, straight_line):
        def start(r, c=None):
            for kk in range(TOP_K):
                pltpu.make_async_copy(_row_tile(ys_ref, p_ref[0, r * TOP_K + kk]),
                                      _row_tile(gbuf.at[s, kk], r), sems.at[s]).start(priority=kk % 2)
            return c
        if straight_line:
            for r in range(tm):
                start(r)
        else:
            lax.fori_loop(0, tm, start, 0)

    @pl.when(i == 0)
    def _():
        gather(pos_ref, 0, False)

    for s in range(2):
        @pl.when((i + 1 < pl.num_programs(0)) & (slot == 1 - s))
        def _():
            gather(pos_next_ref, s, True)

    for kk in range(TOP_K):
        pltpu.make_async_copy(ys_ref.at[pl.ds(0, tm * ROW_TILE), :], gbuf.at[slot, kk],
                              sems.at[slot]).wait()
    RG = 64

    def routed(rg):
        rows = slice(rg * RG, (rg + 1) * RG)
        gate = gate_ref[rows, :]
        g_wide = [jnp.broadcast_to(gate[:, kk:kk + 1], (RG, 128)) for kk in range(TOP_K)]
        for s in range(ROW_TILE):
            lo = hi = None
            for kk in range(TOP_K):
                w = gbuf[slot, kk, pl.ds(rg * RG * ROW_TILE + s, RG, stride=ROW_TILE), :]
                t_lo = g_wide[kk] * lax.bitcast_convert_type(w << 16, F32)
                t_hi = g_wide[kk] * lax.bitcast_convert_type(w & U32(HI_MASK), F32)
                lo = t_lo if lo is None else lo + t_lo
                hi = t_hi if hi is None else hi + t_hi
            xo_ref[rows, s * 128:(s + 1) * 128] = lo
            xo_ref[rows, HALF_D + s * 128: HALF_D + (s + 1) * 128] = hi

    for rg in range(tm // RG):
        routed(rg)
    y = xo_ref[...] + ysh_ref[...]
    xo_ref[...] = x_ref[...] + mod_ref[5:6, :] * _rms(y, pg_ref[...])


def _combine(pos3, ysh, gate8, x2, mod, post_gain, ys, S):
    T, D = x2.shape
    tm = pos3.shape[2] // TOP_K
    nb = S // tm
    kern = functools.partial(_combine_kernel, tm=tm)
    row = lambda w: pl.BlockSpec((tm, w), lambda i: (i, 0))
    full = lambda a: pl.BlockSpec(a.shape, lambda i: (0,) * a.ndim)
    n_steps = T // tm
    return pl.pallas_call(
        kern,
        grid=(n_steps,),
        in_specs=[pl.BlockSpec((None, 1, tm * TOP_K), lambda i: (i, 0, 0), memory_space=pltpu.SMEM),
                  pl.BlockSpec((None, 1, tm * TOP_K), lambda i: (jnp.minimum(i + 1, n_steps - 1), 0, 0),
                               memory_space=pltpu.SMEM),
                  row(D), row(8), row(D),
                  pl.BlockSpec((None, N_MOD, D), lambda i: (i // nb, 0, 0)),
                  full(post_gain),
                  pl.BlockSpec(memory_space=pl.ANY)],
        out_specs=row(D),
        out_shape=jax.ShapeDtypeStruct((T, D), F32),
        scratch_shapes=[pltpu.VMEM((2, TOP_K, tm * ROW_TILE, 128), U32),
                        pltpu.SemaphoreType.DMA((2,))],
        compiler_params=_cparams(1),
        name="moe_combine",
    )(pos3, pos3, ysh, gate8, x2, mod, post_gain, ys)


def _route_plan(idx8, mask, n_blocks):
    idx = idx8[:, :TOP_K]
    counts = jnp.sum(mask, axis=0)
    rank = jnp.cumsum(mask, axis=0) - mask
    padded = (counts + MOE_BLK - 1) // MOE_BLK * MOE_BLK
    pend = jnp.cumsum(padded)
    pstart = pend - padded
    pos = jnp.take_along_axis(pstart[None, :] + rank, idx, axis=1).astype(jnp.int32)
    blk_start = jnp.arange(n_blocks, dtype=jnp.int32) * MOE_BLK
    blk_e = jnp.minimum(jnp.sum(pend[None, :] <= blk_start[:, None], axis=1), N_EXPERTS - 1)
    nused = (pend[-1:] // MOE_BLK).astype(jnp.int32)
    tail = jnp.stack([pstart + counts, padded - counts], axis=1).reshape(-1).astype(jnp.int32)
    ids = jnp.where(counts > 0, jnp.arange(N_EXPERTS), N_EXPERTS)
    after = jnp.concatenate([lax.cummin(ids, reverse=True)[1:], jnp.full((1,), N_EXPERTS)])
    next_e = jnp.where(after < N_EXPERTS, after, -1).astype(jnp.int32)
    return pos, blk_e.astype(jnp.int32), nused, tail, next_e


def _rope_tables(S, rot_dim):
    rows = S // GRID_W
    row = jnp.repeat(jnp.arange(rows, dtype=F32), GRID_W)
    col = jnp.tile(jnp.arange(GRID_W, dtype=F32), rows)
    axis_dim = rot_dim // 2
    inv_freq = ROPE_THETA ** (-jnp.arange(0, axis_dim, 2, dtype=F32) / axis_dim)
    ang_r = row[:, None] * inv_freq
    ang_c = col[:, None] * inv_freq
    z = jnp.zeros_like(ang_r)
    cos = jnp.concatenate([jnp.cos(ang_r)] * 2 + [jnp.cos(ang_c)] * 2, axis=1)
    sin_lo = jnp.concatenate([-jnp.sin(ang_r), z, -jnp.sin(ang_c), z], axis=1)
    sin_hi = jnp.concatenate([z, jnp.sin(ang_r), z, jnp.sin(ang_c)], axis=1)
    reps = 128 // rot_dim
    return tuple(jnp.tile(t, (1, reps)) for t in (cos, sin_lo, sin_hi))


def _prep_w_in(w):
    widths = (768, 256, 256, B_Q_LORA, B_KV_LORA, B_QK_ROPE, 512, 512, 512, 512, 4 * C_HEADS)
    offs = np.cumsum(widths)[:-1].tolist()
    a_q, a_k, a_v, b_cq, b_ckv, b_kr, c_q, c_k, c_v, c_o, c_g = jnp.split(w, offs, axis=-1)
    pad = jnp.zeros(w.shape[:-1] + (128 - B_QK_ROPE - 4 * C_HEADS,), w.dtype)
    out = jnp.concatenate([a_q, a_k, a_v, b_cq, b_ckv, b_kr, c_g, pad, c_v, c_o, c_q, c_k], axis=-1)
    assert out.shape[-1] == PROJ_PAD
    return out.astype(BF16)


def _split_hi_lo(w):
    hi = w.astype(BF16)
    lo = (w - hi.astype(F32)).astype(BF16)
    return jnp.concatenate([hi, lo], axis=1)


def _heads_split(w, n_heads, first):
    K = w.shape[0]
    w3 = w.reshape(K, n_heads, -1)
    return jnp.concatenate([w3[:, :, :first].reshape(K, -1), w3[:, :, first:].reshape(K, -1)],
                           axis=1).astype(BF16)


def kernel(x, c, ada_w, ada_b, pre_mix_gain, w_in, a_q_gain, a_k_gain, b_cq_gain, b_ckv_gain, w_uq, w_ukv, c_conv_w, c_conv_b, c_gate_b, head_out_gain, w_out, post_mix_gain, pre_ffn_gain, router_w, router_bias, exp_w_gate, exp_w_up, exp_w_down, sh_w_gate, sh_w_up, sh_w_down, post_ffn_gain):
    B, S, D = x.shape
    T = B * S
    L = ada_w.shape[0]
    assert D == D_MODEL and S % 512 == 0 and T % 1024 == 0, "row tiles assume these multiples"
    n_blocks = T * TOP_K // MOE_BLK + N_EXPERTS
    row2 = lambda v: v.reshape(1, -1)

    tabs_a = _rope_tables(S, HEAD_DIM)
    tabs_b = _rope_tables(S, B_QK_ROPE)
    c_pad = jnp.zeros((8, D), F32).at[:B].set(c)
    mods = _modulation(c_pad, ada_w, ada_b)[:, :B].reshape(L, B, N_MOD, D)

    x2 = x.reshape(T, D)
    for l in range(L):
        mod = mods[l]
        qa, ka, va, qb, kb, vb, pc, qk, misc = _in_proj(
            x2, row2(pre_mix_gain[l]), mod, _prep_w_in(w_in[l]), tabs_a, tabs_b,
            row2(a_q_gain[l]), row2(a_k_gain[l]), row2(b_cq_gain[l]), row2(b_ckv_gain[l]),
            _heads_split(w_uq[l], B_HEADS, B_QK_NOPE), _heads_split(w_ukv[l], B_HEADS, B_QK_NOPE),
            c_conv_w[l], row2(c_conv_b[l]), S)
        pc3 = pc.reshape(B, S, -1)

        ya = _attention(qa.reshape(B, S, -1), ka.reshape(B, S, -1), va.reshape(B, S, -1),
                        B=B, S=S, Hk=A_KV_HEADS, G=A_HEADS // A_KV_HEADS, dq=HEAD_DIM, dv=HEAD_DIM,
                        tq=min(512, S), tk=512)

        yb = _attention(qb.reshape(B, S, -1), kb.reshape(B, S, -1), vb.reshape(B, S, -1),
                        B=B, S=S, Hk=B_HEADS, G=1, dq=256, dv=B_V_DIM, tq=min(1024, S), tk=512)

        qk3 = qk.reshape(B, S, -1)
        gate_b_row = jnp.zeros((1, 128), F32).at[0, MISC_GATE_LANE:MISC_GATE_LANE + 4 * C_HEADS].set(c_gate_b[l])
        hf, hr = _mlstm(qk3, pc3, misc.reshape(B, S, 128), gate_b_row)

        x2, h2p, logits = _merge(ya.reshape(T, -1), yb.reshape(T, -1), hf.reshape(T, -1),
                                hr.reshape(T, -1), pc, x2, mod,
                                row2(head_out_gain[l]), w_out[l].astype(BF16), row2(post_mix_gain[l]),
                                row2(pre_ffn_gain[l]), _split_hi_lo(router_w[l]), S)

        idx8, gate8, mask = _router(logits, row2(router_bias[l]))
        pos, blk_e, nused, tail, next_e = _route_plan(idx8, mask, n_blocks)
        xs, ysh = _dispatch(h2p, pos.reshape(T // DISPATCH_TM, 1, DISPATCH_TM * TOP_K), tail,
                            sh_w_gate[l].astype(BF16), sh_w_up[l].astype(BF16),
                            sh_w_down[l].astype(BF16), n_blocks * MOE_BLK)
        ys = _experts(xs, blk_e, nused, next_e, exp_w_gate, exp_w_up, exp_w_down, l)
        x2 = _combine(pos.reshape(T // COMBINE_TM, 1, COMBINE_TM * TOP_K), ysh, gate8, x2, mod,
                      row2(post_ffn_gain[l]), ys, S)
    return x2.reshape(B, S, D)
```

```python
import functools

import numpy as np
import jax
import jax.numpy as jnp
from jax import lax
from jax.experimental import pallas as pl
from jax.experimental.pallas import tpu as pltpu

D_MODEL = 2048
GRID_W = 64
ROPE_THETA = 10000.0
EPS = 1e-6
HEAD_DIM = 128
A_HEADS, A_KV_HEADS = 6, 2
B_HEADS, B_Q_LORA, B_KV_LORA = 6, 384, 256
B_QK_NOPE, B_QK_ROPE, B_V_DIM = 128, 64, 128
C_HEADS, C_DQK, C_DV = 4, 128, 128
CONV_W = 5
MLSTM_CHUNK = 64
N_EXPERTS, TOP_K, EXPERT_FF, SHARED_FF = 64, 6, 512, 512
ROUTED_SCALE = 2.5
N_MOD = 6

COL_AQ, COL_AK, COL_AV = 0, 768, 1024
COL_BC = 1280
COL_MISC = 1920
COL_CV, COL_CO = 2048, 2560
COL_CQK = 3072
PROJ_PAD = 4096
MISC_GATE_LANE = 64
PC_CV, PC_CO = 0, COL_CO - COL_CV

MOE_BLK = 256
DISPATCH_TM = 256
COMBINE_TM = 256
MLSTM_BB = 4
ATTN_ROW_GROUP = 256
VMEM_LIMIT = 56 * 1024 * 1024

BF16 = jnp.bfloat16
F32 = jnp.float32
LOG2_E = 1.4426950408889634


def _cparams(grid_rank):
    return pltpu.CompilerParams(dimension_semantics=("arbitrary",) * grid_rank,
                                vmem_limit_bytes=VMEM_LIMIT)


def _rms(x, gain):
    return x * lax.rsqrt(jnp.mean(x * x, axis=-1, keepdims=True) + EPS) * gain


def _dot(a, b):
    return jnp.dot(a, b, preferred_element_type=F32)


def _dot_t(a, b):
    return lax.dot_general(a, b, (((1,), (1,)), ((), ())), preferred_element_type=F32)


def _mod_kernel(c_ref, w_ref, b_ref, o_ref):
    c = c_ref[...]
    c_act = (c * jax.nn.sigmoid(c)).astype(BF16)
    o_ref[...] = _dot(c_act, w_ref[...].astype(BF16)) + b_ref[...]


def _modulation(c_pad, ada_w, ada_b):
    L, D, N = ada_w.shape
    M = c_pad.shape[0]
    tn = 1536
    return pl.pallas_call(
        _mod_kernel,
        grid=(L, N // tn),
        in_specs=[pl.BlockSpec((M, D), lambda l, j: (0, 0)),
                  pl.BlockSpec((None, D, tn), lambda l, j: (l, 0, j)),
                  pl.BlockSpec((None, 1, tn), lambda l, j: (l, 0, j))],
        out_specs=pl.BlockSpec((None, M, tn), lambda l, j: (l, 0, j)),
        out_shape=jax.ShapeDtypeStruct((L, M, N), F32),
        compiler_params=_cparams(2),
        name="adaln_mod",
    )(c_pad, ada_w, ada_b.reshape(L, 1, N))


def _conv_silu(main, prev, nxt, w_ref, b_ref):
    tm, C = main.shape
    row = lax.broadcasted_iota(jnp.int32, main.shape, 0)
    pad = CONV_W // 2
    y = main * w_ref[pad:pad + 1, :] + b_ref[...]
    for j in range(CONV_W):
        d = j - pad
        if d == 0:
            continue
        shifted = pltpu.roll(main, (-d) % tm, 0)
        for e in range(abs(d)):
            if d < 0:
                shifted = jnp.where(row == e, prev[8 + d + e: 9 + d + e, :], shifted)
            else:
                shifted = jnp.where(row == tm - d + e, nxt[e:e + 1, :], shifted)
        y = y + shifted * w_ref[j:j + 1, :]
    y = y * jax.nn.sigmoid(y)
    lane = lax.broadcasted_iota(jnp.int32, main.shape, 1)
    return y * jnp.where(lane >= C // 2, C_DQK ** -0.5, 1.0)


def _in_proj_kernel(x_ref, xp_ref, xn_ref, gain_ref, mod_ref, w_ref,
                    cosa_ref, sloa_ref, shia_ref, cosb_ref, slob_ref, shib_ref,
                    qg_ref, kg_ref, cqg_ref, ckvg_ref, wuq_ref, wukv_ref, cw_ref, cb_ref,
                    qa_ref, ka_ref, va_ref, qb_ref, kb_ref, vb_ref, pc_ref, qk_ref, misc_ref, *, nb):
    def premix(x):
        y = _rms(x, gain_ref[...])
        return (y * (1.0 + mod_ref[1:2, :]) + mod_ref[0:1, :]).astype(BF16)

    h = premix(x_ref[...])
    pa = _dot(h, w_ref[:, COL_AQ:COL_BC])
    _gqa_prep(pa, (cosa_ref[...], sloa_ref[...], shia_ref[...]), qg_ref[...], kg_ref[...],
              qa_ref, ka_ref, va_ref)
    pb = _dot(h, w_ref[:, COL_BC:COL_CV])
    misc = pb[:, COL_MISC - COL_BC:]
    misc_ref[...] = misc
    _mla_prep(pb[:, :COL_MISC - COL_BC], misc, (cosb_ref[...], slob_ref[...], shib_ref[...]),
              cqg_ref[...], ckvg_ref[...], wuq_ref, wukv_ref, qb_ref, kb_ref, vb_ref)
    pc_ref[...] = _dot(h, w_ref[:, COL_CV:COL_CQK])
    w_qk = w_ref[:, COL_CQK:]
    i = pl.program_id(0)
    halo = premix(jnp.concatenate([xp_ref[...], xn_ref[...]], axis=0))
    halo = _dot(halo, w_qk)
    prev = jnp.where(i % nb == 0, 0.0, halo[:8, :])
    nxt = jnp.where(i % nb == nb - 1, 0.0, halo[8:, :])
    qk_ref[...] = _conv_silu(_dot(h, w_qk), prev, nxt, cw_ref, cb_ref)


def _in_proj(x2, gain, mod, w_in_p, tabs_a, tabs_b, q_gain, k_gain, cq_gain, ckv_gain, w_uq_p, w_ukv_p,
             conv_w, conv_b, S):
    T, D = x2.shape
    tm = 512
    nb = S // tm
    n8 = tm // 8
    row = lambda w: pl.BlockSpec((tm, w), lambda i: (i, 0))
    full = lambda a: pl.BlockSpec(a.shape, lambda i: (0,) * a.ndim)
    tab = pl.BlockSpec((tm, 128), lambda i: (i % nb, 0))
    qw, kw = A_HEADS * HEAD_DIM, A_KV_HEADS * HEAD_DIM
    bqw, bvw = B_HEADS * 256, B_HEADS * B_V_DIM
    cvo, cqk = COL_CQK - COL_CV, PROJ_PAD - COL_CQK
    widths = (qw, kw, kw, bqw, bqw, bvw)
    return pl.pallas_call(
        functools.partial(_in_proj_kernel, nb=nb),
        grid=(T // tm,),
        in_specs=[row(D),
                  pl.BlockSpec((8, D), lambda i: (jnp.maximum(i * n8 - 1, 0), 0)),
                  pl.BlockSpec((8, D), lambda i: (jnp.minimum((i + 1) * n8, T // 8 - 1), 0)),
                  full(gain),
                  pl.BlockSpec((None, N_MOD, D), lambda i: (i // nb, 0, 0)),
                  pl.BlockSpec(w_in_p.shape, lambda i: (0, 0), pipeline_mode=pl.Buffered(1)),
                  tab, tab, tab, tab, tab, tab,
                  full(q_gain), full(k_gain), full(cq_gain), full(ckv_gain), full(w_uq_p), full(w_ukv_p),
                  full(conv_w), full(conv_b)],
        out_specs=[row(w) for w in widths] + [row(cvo), row(cqk), row(128)],
        out_shape=[jax.ShapeDtypeStruct((T, w), BF16) for w in widths]
                  + [jax.ShapeDtypeStruct((T, cvo), F32), jax.ShapeDtypeStruct((T, cqk), F32),
                     jax.ShapeDtypeStruct((T, 128), F32)],
        compiler_params=_cparams(1),
        name="in_proj",
    )(x2, x2, x2, gain, mod, w_in_p, *tabs_a, *tabs_b, q_gain, k_gain, cq_gain, ckv_gain, w_uq_p, w_ukv_p,
      conv_w, conv_b)


def _rope128(y, cos, sin_lo, sin_hi):
    return y * cos + pltpu.roll(y, 96, 1) * sin_lo + pltpu.roll(y, 32, 1) * sin_hi


def _gqa_prep(pa, tabs, q_gain, k_gain, qo_ref, ko_ref, vo_ref):
    cos, slo, shi = tabs
    scale = HEAD_DIM ** -0.5 * LOG2_E
    for h in range(A_HEADS):
        sl = slice(h * HEAD_DIM, (h + 1) * HEAD_DIM)
        y = _rope128(_rms(pa[:, COL_AQ + h * HEAD_DIM: COL_AQ + (h + 1) * HEAD_DIM], q_gain), cos, slo, shi)
        qo_ref[:, sl] = (y * scale).astype(BF16)
    for h in range(A_KV_HEADS):
        sl = slice(h * HEAD_DIM, (h + 1) * HEAD_DIM)
        y = _rope128(_rms(pa[:, COL_AK + h * HEAD_DIM: COL_AK + (h + 1) * HEAD_DIM], k_gain), cos, slo, shi)
        ko_ref[:, sl] = y.astype(BF16)
    vo_ref[...] = pa[:, COL_AV:COL_BC].astype(BF16)


def _rope64x2(y, cos, sin_lo, sin_hi):
    return y * cos + pltpu.roll(y, 112, 1) * sin_lo + pltpu.roll(y, 16, 1) * sin_hi


def _mla_prep(bc, misc, tabs, cq_gain, ckv_gain, wuq_ref, wukv_ref, qo_ref, ko_ref, vo_ref):
    cos, slo, shi = tabs
    lane = lax.broadcasted_iota(jnp.int32, cos.shape, 1)
    first = lane < B_QK_ROPE
    scale = (B_QK_NOPE + B_QK_ROPE) ** -0.5 * LOG2_E
    nope_w = B_HEADS * B_QK_NOPE

    cq = _rms(bc[:, :B_Q_LORA], cq_gain).astype(BF16)
    q = _dot(cq, wuq_ref[...]) * scale
    for p in range(B_HEADS // 2):
        pair = _rope64x2(q[:, nope_w + p * 128: nope_w + (p + 1) * 128], cos, slo, shi)
        for half in range(2):
            h = 2 * p + half
            qo_ref[:, h * 256: h * 256 + 128] = q[:, h * 128:(h + 1) * 128].astype(BF16)
            keep = first if half == 0 else jnp.logical_not(first)
            qo_ref[:, h * 256 + 128: (h + 1) * 256] = jnp.where(keep, pair, 0.0).astype(BF16)

    kr = jnp.where(first, misc, 0.0)
    kr = kr + pltpu.roll(kr, B_QK_ROPE, 1)
    kr = _rope64x2(kr, cos, slo, shi).astype(BF16)
    ckv = _rms(bc[:, B_Q_LORA:], ckv_gain).astype(BF16)
    kv = _dot(ckv, wukv_ref[...])
    for h in range(B_HEADS):
        ko_ref[:, h * 256: h * 256 + 128] = kv[:, h * 128:(h + 1) * 128].astype(BF16)
        ko_ref[:, h * 256 + 128: (h + 1) * 256] = kr
    vo_ref[...] = kv[:, nope_w:].astype(BF16)


def _attn_kernel(q_ref, k_ref, v_ref, o_ref, m_scr, l_scr, acc_scr, *, G, dq, dv, tk, sub):
    tq = q_ref.shape[0]
    S = k_ref.shape[0]
    groups = [(g, r) for g in range(G) for r in range(tq // sub)]
    m_scr[...] = jnp.full(m_scr.shape, -jnp.inf, F32)
    l_scr[...] = jnp.zeros(l_scr.shape, F32)
    acc_scr[...] = jnp.zeros(acc_scr.shape, F32)

    stages = [(c, n) for c in range(S // tk) for n in range(len(groups))]

    def scores(stage):
        c, n = stage
        g, r = groups[n]
        return _dot_t(q_ref[r * sub:(r + 1) * sub, g * dq:(g + 1) * dq], k_ref[c * tk:(c + 1) * tk, :])

    s_next = scores(stages[0])
    for i, (c, n) in enumerate(stages):
        s = s_next
        if i + 1 < len(stages):
            s_next = scores(stages[i + 1])
        rows = slice(n * sub, (n + 1) * sub)
        blocks = [s[:, j * 128:(j + 1) * 128] for j in range(tk // 128)]
        m_blk = functools.reduce(jnp.maximum, blocks)
        m_old = m_scr[rows, :]
        m_new = jnp.maximum(m_old, jnp.max(m_blk, axis=-1, keepdims=True))
        alpha = jnp.exp2(m_old - m_new)
        p_blocks = [jnp.exp2(b - m_new) for b in blocks]
        l_scr[rows, :] = alpha * l_scr[rows, :] + functools.reduce(jnp.add, p_blocks)
        p = jnp.concatenate([b.astype(BF16) for b in p_blocks], axis=1)
        acc_scr[rows, :] = alpha * acc_scr[rows, :] + _dot(p, v_ref[c * tk:(c + 1) * tk, :])
        m_scr[rows, :] = m_new

    for n, (g, r) in enumerate(groups):
        rows = slice(n * sub, (n + 1) * sub)
        l = jnp.sum(l_scr[rows, :], axis=-1, keepdims=True)
        o_ref[r * sub:(r + 1) * sub, g * dv:(g + 1) * dv] = (acc_scr[rows, :] / l).astype(o_ref.dtype)


def _attention(q, k, v, *, B, S, Hk, G, dq, dv, tq, tk):
    assert dv == 128, "row statistics are kept 128 lanes wide to match the value width"
    kern = functools.partial(_attn_kernel, G=G, dq=dq, dv=dv, tk=tk, sub=ATTN_ROW_GROUP)
    M = G * tq
    return pl.pallas_call(
        kern,
        grid=(B, Hk, S // tq),
        in_specs=[pl.BlockSpec((None, tq, G * dq), lambda b, h, i: (b, i, h)),
                  pl.BlockSpec((None, S, dq), lambda b, h, i: (b, 0, h)),
                  pl.BlockSpec((None, S, dv), lambda b, h, i: (b, 0, h))],
        out_specs=pl.BlockSpec((None, tq, G * dv), lambda b, h, i: (b, i, h)),
        out_shape=jax.ShapeDtypeStruct((B, S, Hk * G * dv), BF16),
        scratch_shapes=[pltpu.VMEM((M, 128), F32), pltpu.VMEM((M, 128), F32), pltpu.VMEM((M, dv), F32)],
        compiler_params=_cparams(3),
        name=f"attn_g{G}_d{dq}",
    )(q, k, v)


def _log_sigmoid(x):
    return jnp.minimum(x, 0.0) - jnp.log1p(jnp.exp(-jnp.abs(x)))


def _mlstm_kernel(qf_ref, kf_ref, vf_ref, gf_ref, qr_ref, kr_ref, vr_ref, gr_ref, gb_ref,
                  of_ref, or_ref, C_scr, n_scr, m_scr, *, BB):
    L = MLSTM_CHUNK

    @pl.when(pl.program_id(1) == 0)
    def _():
        C_scr[...] = jnp.zeros(C_scr.shape, F32)
        n_scr[...] = jnp.zeros(n_scr.shape, F32)
        m_scr[...] = jnp.zeros(m_scr.shape, F32)

    glane = lax.broadcasted_iota(jnp.int32, (L, 128), 1)
    jj = lax.broadcasted_iota(jnp.int32, (L, L), 0)
    ss = lax.broadcasted_iota(jnp.int32, (L, L), 1)
    eye = jj == ss
    neg_inf = jnp.float32(-jnp.inf)
    dirs = ((qf_ref, kf_ref, vf_ref, gf_ref, of_ref, ss <= jj, jj <= ss),
            (qr_ref, kr_ref, vr_ref, gr_ref, or_ref, ss >= jj, jj >= ss))

    chains = []
    for d, (q_ref, k_ref, v_ref, g_ref, o_ref, seen, seen_t) in enumerate(dirs):
        for bb in range(BB):
            gates = g_ref[bb] + gb_ref[...]
            lsig = _log_sigmoid(gates)
            for h in range(C_HEADS):
                c = dict(st=(d * BB + bb) * C_HEADS + h, sl=slice(h * C_DQK, (h + 1) * C_DQK), bb=bb,
                         q_ref=q_ref, k_ref=k_ref, v_ref=v_ref, o_ref=o_ref, seen=seen, seen_t=seen_t)
                i_lane = MISC_GATE_LANE + d * (2 * C_HEADS) + h
                c["i_col"] = jnp.sum(jnp.where(glane == i_lane, gates, 0.0), axis=1, keepdims=True)
                c["f_col"] = jnp.sum(jnp.where(glane == i_lane + C_HEADS, lsig, 0.0), axis=1, keepdims=True)
                chains.append(c)
    for c in chains:
        c["f_row"] = jnp.sum(jnp.where(eye, c["f_col"], 0.0), axis=0, keepdims=True)
        c["i_row"] = jnp.sum(jnp.where(eye, c["i_col"], 0.0), axis=0, keepdims=True)
        c["b_row"] = jnp.sum(jnp.where(c["seen_t"], c["f_col"], 0.0), axis=0, keepdims=True)
        c["g_tot"] = jnp.sum(c["f_col"], axis=0, keepdims=True)
    for c in chains:
        c["b_col"] = jnp.sum(jnp.where(c["seen"], c["f_row"], 0.0), axis=1, keepdims=True)
    for c in chains:
        c["m_prev"] = m_scr[c["st"]][:, 0:1]
        c["dmat"] = jnp.where(c["seen"], c["b_col"] - c["b_row"] + c["i_row"], neg_inf)
        c["m_inter"] = c["b_col"] + c["m_prev"]
        c["a_col"] = c["g_tot"] - c["b_col"] + c["i_col"]
    for c in chains:
        c["m_j"] = jnp.maximum(c["m_inter"], jnp.max(c["dmat"], axis=1, keepdims=True))
        c["m_new"] = jnp.maximum(c["g_tot"] + c["m_prev"], jnp.max(c["a_col"], axis=0, keepdims=True))
    for c in chains:
        c["q"] = c["q_ref"][c["bb"], :, c["sl"]]
        c["k"] = c["k_ref"][c["bb"], :, c["sl"]]
        c["qb"], c["kb"] = c["q"].astype(BF16), c["k"].astype(BF16)
        c["vb"] = c["v_ref"][c["bb"], :, c["sl"]].astype(BF16)
        c["qk"] = _dot_t(c["qb"], c["kb"])
    for c in chains:
        c["qC"] = _dot(c["qb"], C_scr[c["st"]].astype(BF16))
    for c in chains:
        c["s"] = c["qk"] * jnp.exp(c["dmat"] - c["m_j"])
        c["inter"] = jnp.exp(c["m_inter"] - c["m_j"])
        c["wk"] = jnp.exp(c["a_col"] - c["m_new"]) * c["k"]
        c["decay"] = jnp.exp(c["g_tot"] + c["m_prev"] - c["m_new"])
    for c in chains:
        c["sv"] = _dot(c["s"].astype(BF16), c["vb"])
    for c in chains:
        c["upd"] = lax.dot_general(c["wk"].astype(BF16), c["vb"], (((0,), (0,)), ((), ())),
                                   preferred_element_type=F32)
    for c in chains:
        n_prev = n_scr[c["st"]]
        den = (jnp.sum(c["s"], axis=1, keepdims=True)
               + c["inter"] * jnp.sum(c["q"] * n_prev, axis=1, keepdims=True))
        num = c["sv"] + c["inter"] * c["qC"]
        c["o_ref"][c["bb"], :, c["sl"]] = num / jnp.maximum(jnp.abs(den), jnp.exp(-c["m_j"]))
        n_scr[c["st"]] = c["decay"] * n_prev + jnp.sum(c["wk"], axis=0, keepdims=True)
    for c in chains:
        C_scr[c["st"]] = c["decay"] * C_scr[c["st"]] + c["upd"]
        m_scr[c["st"]] = jnp.broadcast_to(c["m_new"], m_scr.shape[1:])


def _mlstm(qk3, pc3, misc3, gate_b_row, BB=MLSTM_BB):
    B, S, _ = pc3.shape
    BB = min(BB, B)
    L = MLSTM_CHUNK
    nc = S // L
    W = C_HEADS * C_DQK
    n_chain = 2 * BB * C_HEADS
    fwd = lambda col: (lambda b, c: (b, c, col))
    rev = lambda col: (lambda b, c: (b, nc - 1 - c, col))
    blk = lambda w, imap: pl.BlockSpec((BB, L, w), imap)
    out = jax.ShapeDtypeStruct((B, S, W), F32)
    return pl.pallas_call(
        functools.partial(_mlstm_kernel, BB=BB),
        grid=(B // BB, nc),
        in_specs=[blk(W, fwd(0)), blk(W, fwd(1)), blk(W, fwd(PC_CV // W)), blk(128, fwd(0)),
                  blk(W, rev(0)), blk(W, rev(1)), blk(W, rev(PC_CV // W)), blk(128, rev(0)),
                  pl.BlockSpec((1, 128), lambda b, c: (0, 0))],
        out_specs=[blk(W, fwd(0)), blk(W, rev(0))],
        out_shape=[out, out],
        scratch_shapes=[pltpu.VMEM((n_chain, C_DQK, C_DV), F32),
                        pltpu.VMEM((n_chain, 1, C_DQK), F32),
                        pltpu.VMEM((n_chain, 1, 128), F32)],
        compiler_params=_cparams(2),
        name="mlstm",
    )(qk3, qk3, pc3, misc3, qk3, qk3, pc3, misc3, gate_b_row)


ROW_TILE = 8
HALF_D = D_MODEL // 2
U32 = jnp.uint32
HI_MASK = 0xFFFF0000


def _pack_rows(y, o_ref):
    n = y.shape[0]
    for s in range(ROW_TILE):
        lo = y[:, s * 128:(s + 1) * 128].astype(BF16).astype(F32)
        hi = y[:, HALF_D + s * 128: HALF_D + (s + 1) * 128].astype(BF16).astype(F32)
        w = (lax.bitcast_convert_type(lo, U32) >> 16) | (lax.bitcast_convert_type(hi, U32) & U32(HI_MASK))
        o_ref[pl.ds(s, n, stride=ROW_TILE), :] = w


def _unpack_rows(ref, n):
    lo, hi = [], []
    for s in range(ROW_TILE):
        w = ref[pl.ds(s, n, stride=ROW_TILE), :]
        lo.append(lax.bitcast_convert_type(w << 16, F32))
        hi.append(lax.bitcast_convert_type(w & U32(HI_MASK), F32))
    return lo + hi


def _merge_kernel(ya_ref, yb_ref, hf_ref, hr_ref, co_ref, x_ref, mod_ref, hg_ref, wout_ref, pmg_ref,
                  pfg_ref, rw_ref, xo_ref, h2p_ref, lg_ref, y_scr):
    attn_heads = A_HEADS + B_HEADS
    for h in range(attn_heads + C_HEADS):
        sl = slice(h * HEAD_DIM, (h + 1) * HEAD_DIM)
        g = hg_ref[:, sl]
        if h < A_HEADS:
            y = _rms(ya_ref[:, sl].astype(F32), g)
        elif h < attn_heads:
            y = _rms(yb_ref[:, (h - A_HEADS) * HEAD_DIM:(h - A_HEADS + 1) * HEAD_DIM].astype(F32), g)
        else:
            cs = slice((h - attn_heads) * HEAD_DIM, (h - attn_heads + 1) * HEAD_DIM)
            y = _rms(hf_ref[:, cs] + hr_ref[:, cs], g) * jax.nn.sigmoid(co_ref[:, cs])
        y_scr[:, sl] = y.astype(BF16)
    y = _dot(y_scr[...], wout_ref[...])
    x_new = x_ref[...] + mod_ref[2:3, :] * _rms(y, pmg_ref[...])
    xo_ref[...] = x_new
    h2 = _rms(x_new, pfg_ref[...]) * (1.0 + mod_ref[4:5, :]) + mod_ref[3:4, :]
    _pack_rows(h2, h2p_ref)
    h_hi = h2.astype(BF16)
    h_lo = (h2 - h_hi.astype(F32)).astype(BF16)
    a = _dot(h_hi, rw_ref[...])
    b = _dot(h_lo, rw_ref[...])
    E = N_EXPERTS
    lg_ref[...] = a[:, :E] + (a[:, E:] + b[:, :E]) + b[:, E:]


def _merge(ya, yb, hf, hr, pc, x2, mod, head_gain, w_out_b, pm_gain, pf_gain, router_w, S):
    T, D = x2.shape
    tm = 512
    nb = S // tm
    aw, cw = A_HEADS * HEAD_DIM, C_HEADS * C_DV
    row = lambda w: pl.BlockSpec((tm, w), lambda i: (i, 0))
    full = lambda a: pl.BlockSpec(a.shape, lambda i: (0,) * a.ndim)
    return pl.pallas_call(
        _merge_kernel,
        grid=(T // tm,),
        in_specs=[row(aw), row(aw), row(cw), row(cw),
                  pl.BlockSpec((tm, cw), lambda i: (i, PC_CO // cw)),
                  row(D),
                  pl.BlockSpec((None, N_MOD, D), lambda i: (i // nb, 0, 0)),
                  full(head_gain), full(w_out_b), full(pm_gain), full(pf_gain), full(router_w)],
        out_specs=[row(D), pl.BlockSpec((tm * ROW_TILE, 128), lambda i: (i, 0)),
                   row(N_EXPERTS)],
        out_shape=[jax.ShapeDtypeStruct((T, D), F32),
                   jax.ShapeDtypeStruct((T * ROW_TILE, 128), U32),
                   jax.ShapeDtypeStruct((T, N_EXPERTS), F32)],
        scratch_shapes=[pltpu.VMEM((tm, D), BF16)],
        compiler_params=_cparams(1),
        name="merge_out_proj",
    )(ya, yb, hf, hr, pc, x2, mod, head_gain, w_out_b, pm_gain, pf_gain, router_w)


def _router_kernel(lg_ref, bias_ref, idx_ref, gate_ref, mask_ref):
    scores = jax.nn.sigmoid(lg_ref[...])
    sel = scores + bias_ref[...]
    lane = lax.broadcasted_iota(jnp.int32, scores.shape, 1).astype(F32)
    col = lax.broadcasted_iota(jnp.int32, idx_ref.shape, 1)
    idx = jnp.zeros(idx_ref.shape, F32)
    gate = jnp.zeros(gate_ref.shape, F32)
    mask = jnp.zeros(scores.shape, F32)
    for kk in range(TOP_K):
        mx = jnp.max(sel, axis=1, keepdims=True)
        am = jnp.min(jnp.where(sel == mx, lane, float(N_EXPERTS)), axis=1, keepdims=True)
        hit = lane == am
        sc = jnp.sum(jnp.where(hit, scores, 0.0), axis=1, keepdims=True)
        idx = jnp.where(col == kk, am, idx)
        gate = jnp.where(col == kk, sc, gate)
        mask = jnp.where(hit, 1.0, mask)
        sel = jnp.where(hit, -jnp.inf, sel)
    gate = gate / jnp.sum(gate, axis=1, keepdims=True) * ROUTED_SCALE
    idx_ref[...] = idx.astype(jnp.int32)
    gate_ref[...] = gate
    mask_ref[...] = mask.astype(jnp.int32)


def _router(logits, bias):
    T, E = logits.shape
    tm = 1024
    row = lambda w: pl.BlockSpec((tm, w), lambda i: (i, 0))
    return pl.pallas_call(
        _router_kernel,
        grid=(T // tm,),
        in_specs=[row(E), pl.BlockSpec((1, E), lambda i: (0, 0))],
        out_specs=[row(8), row(8), row(E)],
        out_shape=[jax.ShapeDtypeStruct((T, 8), jnp.int32),
                   jax.ShapeDtypeStruct((T, 8), F32),
                   jax.ShapeDtypeStruct((T, E), jnp.int32)],
        compiler_params=_cparams(1),
        name="router_topk",
    )(logits, bias)


TAIL_BITS = MOE_BLK.bit_length() - 1


def _tail_copies(tail_ref, zero_scr, xs_ref, zsem, fn):
    for e in range(N_EXPERTS):
        start_row = tail_ref[2 * e]
        n_tail = tail_ref[2 * e + 1]
        for bit in range(TAIL_BITS):
            size = 1 << bit
            cur = start_row + (n_tail & ~(2 * size - 1))
            cur = pl.multiple_of(cur * ROW_TILE, ROW_TILE)

            @pl.when((n_tail & size) != 0)
            def _():
                fn(pltpu.make_async_copy(zero_scr.at[pl.ds(0, size * ROW_TILE), :],
                                         xs_ref.at[pl.ds(cur, size * ROW_TILE), :], zsem))


def _row_tile(ref, row):
    start = row * ROW_TILE
    if not isinstance(row, int):
        start = pl.multiple_of(start, ROW_TILE)
    return ref.at[pl.ds(start, ROW_TILE), :]


def _dispatch_kernel(tail_ref, pos_ref, hp_ref, wsg_ref, wsu_ref, wsd_ref, xs_ref, ysh_ref,
                     zero_scr, sem, zsem, *, tm):
    i = pl.program_id(0)

    for r in range(tm):
        src = _row_tile(hp_ref, r)
        for kk in range(TOP_K):
            pltpu.make_async_copy(src, _row_tile(xs_ref, pos_ref[0, r * TOP_K + kk]),
                                  sem).start(priority=kk % 2)

    @pl.when(i == 0)
    def _():
        zero_scr[...] = jnp.zeros(zero_scr.shape, zero_scr.dtype)
        _tail_copies(tail_ref, zero_scr, xs_ref, zsem, lambda cp: cp.start())
        _tail_copies(tail_ref, zero_scr, xs_ref, zsem, lambda cp: cp.wait())

    h = jnp.concatenate([c.astype(BF16) for c in _unpack_rows(hp_ref, tm)], axis=1)
    gte = _dot(h, wsg_ref[...])
    a = gte * jax.nn.sigmoid(gte) * _dot(h, wsu_ref[...])
    ysh_ref[...] = _dot(a.astype(BF16), wsd_ref[...])

    for kk in range(TOP_K):
        pltpu.make_async_copy(hp_ref, xs_ref.at[pl.ds(0, tm * ROW_TILE), :], sem).wait()


def _dispatch(h2p, pos3, tail_info, ws_gate_b, ws_up_b, ws_down_b, P):
    T, D = h2p.shape[0] // ROW_TILE, D_MODEL
    tm = pos3.shape[2] // TOP_K
    kern = functools.partial(_dispatch_kernel, tm=tm)
    full = lambda a: pl.BlockSpec(a.shape, lambda i, tail: (0,) * a.ndim)
    return pl.pallas_call(
        kern,
        grid_spec=pltpu.PrefetchScalarGridSpec(
            num_scalar_prefetch=1,
            grid=(T // tm,),
            in_specs=[pl.BlockSpec((None, 1, tm * TOP_K), lambda i, tail: (i, 0, 0),
                                   memory_space=pltpu.SMEM),
                      pl.BlockSpec((tm * ROW_TILE, 128), lambda i, tail: (i, 0)),
                      full(ws_gate_b), full(ws_up_b), full(ws_down_b)],
            out_specs=[pl.BlockSpec(memory_space=pl.ANY),
                       pl.BlockSpec((tm, D), lambda i, tail: (i, 0))],
            scratch_shapes=[pltpu.VMEM((MOE_BLK // 2 * ROW_TILE, 128), U32),
                            pltpu.SemaphoreType.DMA(()), pltpu.SemaphoreType.DMA(())]),
        out_shape=[jax.ShapeDtypeStruct((P * ROW_TILE, 128), U32),
                   jax.ShapeDtypeStruct((T, D), F32)],
        compiler_params=_cparams(1),
        name="moe_dispatch",
    )(tail_info, pos3, h2p, ws_gate_b, ws_up_b, ws_down_b)


EXPERT_TILES = 4


def _expert_kernel(blk_e_ref, nused_ref, next_ref, xs_ref, wg_hbm, wu_hbm, wd_hbm, ys_ref,
                   wg_f, wu_f, wd_f, wg_b, wu_b, wd_b, sems, *, layer):
    step = pl.program_id(0)
    rows = MOE_BLK * ROW_TILE

    def fetch(ex):
        return [pltpu.make_async_copy(hbm.at[layer, ex], buf, sems.at[n])
                for n, (hbm, buf) in enumerate(((wg_hbm, wg_f), (wu_hbm, wu_f), (wd_hbm, wd_f)))]

    @pl.when(step == 0)
    def _():
        for cp in fetch(blk_e_ref[0]):
            cp.start(priority=1)

    for j in range(EXPERT_TILES):
        i = step * EXPERT_TILES + j
        e = blk_e_ref[i]
        e_prev = blk_e_ref[jnp.maximum(i - 1, 0)]
        used = i < nused_ref[0]
        first = used & ((i == 0) | (e != e_prev))
        x_view = xs_ref.at[pl.ds(j * rows, rows), :]
        y_view = ys_ref.at[pl.ds(j * rows, rows), :]

        def compute(switch, e=e, x_view=x_view, y_view=y_view):
            x = jnp.concatenate([c.astype(BF16) for c in _unpack_rows(x_view, MOE_BLK)], axis=1)
            if switch:
                for cp in fetch(e):
                    cp.wait()
                wg_b[...] = wg_f[...].astype(BF16)
            gte = _dot(x, wg_b[...])
            if switch:
                wu_b[...] = wu_f[...].astype(BF16)
            up = _dot(x, wu_b[...])
            if switch:
                wd_b[...] = wd_f[...].astype(BF16)
                e_next = next_ref[e]

                @pl.when(e_next >= 0)
                def _():
                    for cp in fetch(e_next):
                        cp.start(priority=1)
            a = gte * jax.nn.sigmoid(gte) * up
            _pack_rows(_dot(a.astype(BF16), wd_b[...]), y_view)

        pl.when(first)(functools.partial(compute, True))
        pl.when(used & jnp.logical_not(first))(functools.partial(compute, False))

        @pl.when(jnp.logical_not(used))
        def _(y_view=y_view):
            y_view[...] = jnp.zeros(y_view.shape, y_view.dtype)


def _experts(xs, blk_e, nused, next_e, w_gate, w_up, w_down, layer):
    P = xs.shape[0] // ROW_TILE
    D, F = w_gate.shape[-2:]
    NB = P // MOE_BLK
    assert NB % EXPERT_TILES == 0
    rows = EXPERT_TILES * MOE_BLK * ROW_TILE
    last = lambda s, nu: jnp.minimum(s, (nu[0] - 1) // EXPERT_TILES)
    hbm = pl.BlockSpec(memory_space=pl.ANY)
    return pl.pallas_call(
        functools.partial(_expert_kernel, layer=layer),
        grid_spec=pltpu.PrefetchScalarGridSpec(
            num_scalar_prefetch=3,
            grid=(NB // EXPERT_TILES,),
            in_specs=[pl.BlockSpec((rows, 128), lambda s, be, nu, nx: (last(s, nu), 0)),
                      hbm, hbm, hbm],
            out_specs=pl.BlockSpec((rows, 128), lambda s, be, nu, nx: (s, 0)),
            scratch_shapes=[pltpu.VMEM((D, F), F32), pltpu.VMEM((D, F), F32), pltpu.VMEM((F, D), F32),
                            pltpu.VMEM((D, F), BF16), pltpu.VMEM((D, F), BF16),
                            pltpu.VMEM((F, D), BF16), pltpu.SemaphoreType.DMA((3,))]),
        out_shape=jax.ShapeDtypeStruct((P * ROW_TILE, 128), U32),
        compiler_params=_cparams(1),
        name="moe_experts",
    )(blk_e, nused, next_e, xs, w_gate, w_up, w_down)


def _combine_kernel(pos_ref, pos_next_ref, ysh_ref, gate_ref, x_ref, mod_ref, pg_ref, ys_ref,
                    xo_ref, gbuf, sems, *, tm):
    i = pl.program_id(0)
    slot = i % 2

    def gather(p_ref, s, straight_line):
        def start(r, c=None):
            for kk in range(TOP_K):
                pltpu.make_async_copy(_row_tile(ys_ref, p_ref[0, r * TOP_K + kk]),
                                      _row_tile(gbuf.at[s, kk], r), sems.at[s]).start(priority=kk % 2)
            return c
        if straight_line:
            for r in range(tm):
                start(r)
        else:
            lax.fori_loop(0, tm, start, 0)

    @pl.when(i == 0)
    def _():
        gather(pos_ref, 0, False)

    for s in range(2):
        @pl.when((i + 1 < pl.num_programs(0)) & (slot == 1 - s))
        def _():
            gather(pos_next_ref, s, True)

    for kk in range(TOP_K):
        pltpu.make_async_copy(ys_ref.at[pl.ds(0, tm * ROW_TILE), :], gbuf.at[slot, kk],
                              sems.at[slot]).wait()
    RG = 64

    def routed(rg):
        rows = slice(rg * RG, (rg + 1) * RG)
        gate = gate_ref[rows, :]
        g_wide = [jnp.broadcast_to(gate[:, kk:kk + 1], (RG, 128)) for kk in range(TOP_K)]
        for s in range(ROW_TILE):
            lo = hi = None
            for kk in range(TOP_K):
                w = gbuf[slot, kk, pl.ds(rg * RG * ROW_TILE + s, RG, stride=ROW_TILE), :]
                t_lo = g_wide[kk] * lax.bitcast_convert_type(w << 16, F32)
                t_hi = g_wide[kk] * lax.bitcast_convert_type(w & U32(HI_MASK), F32)
                lo = t_lo if lo is None else lo + t_lo
                hi = t_hi if hi is None else hi + t_hi
            xo_ref[rows, s * 128:(s + 1) * 128] = lo
            xo_ref[rows, HALF_D + s * 128: HALF_D + (s + 1) * 128] = hi

    for rg in range(tm // RG):
        routed(rg)
    y = xo_ref[...] + ysh_ref[...]
    xo_ref[...] = x_ref[...] + mod_ref[5:6, :] * _rms(y, pg_ref[...])


def _combine(pos3, ysh, gate8, x2, mod, post_gain, ys, S):
    T, D = x2.shape
    tm = pos3.shape[2] // TOP_K
    nb = S // tm
    kern = functools.partial(_combine_kernel, tm=tm)
    row = lambda w: pl.BlockSpec((tm, w), lambda i: (i, 0))
    full = lambda a: pl.BlockSpec(a.shape, lambda i: (0,) * a.ndim)
    n_steps = T // tm
    return pl.pallas_call(
        kern,
        grid=(n_steps,),
        in_specs=[pl.BlockSpec((None, 1, tm * TOP_K), lambda i: (i, 0, 0), memory_space=pltpu.SMEM),
                  pl.BlockSpec((None, 1, tm * TOP_K), lambda i: (jnp.minimum(i + 1, n_steps - 1), 0, 0),
                               memory_space=pltpu.SMEM),
                  row(D), row(8), row(D),
                  pl.BlockSpec((None, N_MOD, D), lambda i: (i // nb, 0, 0)),
                  full(post_gain),
                  pl.BlockSpec(memory_space=pl.ANY)],
        out_specs=row(D),
        out_shape=jax.ShapeDtypeStruct((T, D), F32),
        scratch_shapes=[pltpu.VMEM((2, TOP_K, tm * ROW_TILE, 128), U32),
                        pltpu.SemaphoreType.DMA((2,))],
        compiler_params=_cparams(1),
        name="moe_combine",
    )(pos3, pos3, ysh, gate8, x2, mod, post_gain, ys)


def _route_plan(idx8, mask, n_blocks):
    idx = idx8[:, :TOP_K]
    counts = jnp.sum(mask, axis=0)
    rank = jnp.cumsum(mask, axis=0) - mask
    padded = (counts + MOE_BLK - 1) // MOE_BLK * MOE_BLK
    pend = jnp.cumsum(padded)
    pstart = pend - padded
    pos = jnp.take_along_axis(pstart[None, :] + rank, idx, axis=1).astype(jnp.int32)
    blk_start = jnp.arange(n_blocks, dtype=jnp.int32) * MOE_BLK
    blk_e = jnp.minimum(jnp.sum(pend[None, :] <= blk_start[:, None], axis=1), N_EXPERTS - 1)
    nused = (pend[-1:] // MOE_BLK).astype(jnp.int32)
    tail = jnp.stack([pstart + counts, padded - counts], axis=1).reshape(-1).astype(jnp.int32)
    ids = jnp.where(counts > 0, jnp.arange(N_EXPERTS), N_EXPERTS)
    after = jnp.concatenate([lax.cummin(ids, reverse=True)[1:], jnp.full((1,), N_EXPERTS)])
    next_e = jnp.where(after < N_EXPERTS, after, -1).astype(jnp.int32)
    return pos, blk_e.astype(jnp.int32), nused, tail, next_e


def _rope_tables(S, rot_dim):
    rows = S // GRID_W
    row = jnp.repeat(jnp.arange(rows, dtype=F32), GRID_W)
    col = jnp.tile(jnp.arange(GRID_W, dtype=F32), rows)
    axis_dim = rot_dim // 2
    inv_freq = ROPE_THETA ** (-jnp.arange(0, axis_dim, 2, dtype=F32) / axis_dim)
    ang_r = row[:, None] * inv_freq
    ang_c = col[:, None] * inv_freq
    z = jnp.zeros_like(ang_r)
    cos = jnp.concatenate([jnp.cos(ang_r)] * 2 + [jnp.cos(ang_c)] * 2, axis=1)
    sin_lo = jnp.concatenate([-jnp.sin(ang_r), z, -jnp.sin(ang_c), z], axis=1)
    sin_hi = jnp.concatenate([z, jnp.sin(ang_r), z, jnp.sin(ang_c)], axis=1)
    reps = 128 // rot_dim
    return tuple(jnp.tile(t, (1, reps)) for t in (cos, sin_lo, sin_hi))


def _prep_w_in(w):
    widths = (768, 256, 256, B_Q_LORA, B_KV_LORA, B_QK_ROPE, 512, 512, 512, 512, 4 * C_HEADS)
    offs = np.cumsum(widths)[:-1].tolist()
    a_q, a_k, a_v, b_cq, b_ckv, b_kr, c_q, c_k, c_v, c_o, c_g = jnp.split(w, offs, axis=-1)
    pad = jnp.zeros(w.shape[:-1] + (128 - B_QK_ROPE - 4 * C_HEADS,), w.dtype)
    out = jnp.concatenate([a_q, a_k, a_v, b_cq, b_ckv, b_kr, c_g, pad, c_v, c_o, c_q, c_k], axis=-1)
    assert out.shape[-1] == PROJ_PAD
    return out.astype(BF16)


def _split_hi_lo(w):
    hi = w.astype(BF16)
    lo = (w - hi.astype(F32)).astype(BF16)
    return jnp.concatenate([hi, lo], axis=1)


def _heads_split(w, n_heads, first):
    K = w.shape[0]
    w3 = w.reshape(K, n_heads, -1)
    return jnp.concatenate([w3[:, :, :first].reshape(K, -1), w3[:, :, first:].reshape(K, -1)],
                           axis=1).astype(BF16)


def kernel(x, c, ada_w, ada_b, pre_mix_gain, w_in, a_q_gain, a_k_gain, b_cq_gain, b_ckv_gain, w_uq, w_ukv, c_conv_w, c_conv_b, c_gate_b, head_out_gain, w_out, post_mix_gain, pre_ffn_gain, router_w, router_bias, exp_w_gate, exp_w_up, exp_w_down, sh_w_gate, sh_w_up, sh_w_down, post_ffn_gain):
    B, S, D = x.shape
    T = B * S
    L = ada_w.shape[0]
    assert D == D_MODEL and S % 512 == 0 and T % 1024 == 0, "row tiles assume these multiples"
    n_blocks = T * TOP_K // MOE_BLK + N_EXPERTS
    row2 = lambda v: v.reshape(1, -1)

    tabs_a = _rope_tables(S, HEAD_DIM)
    tabs_b = _rope_tables(S, B_QK_ROPE)
    c_pad = jnp.zeros((8, D), F32).at[:B].set(c)
    mods = _modulation(c_pad, ada_w, ada_b)[:, :B].reshape(L, B, N_MOD, D)

    x2 = x.reshape(T, D)
    for l in range(L):
        mod = mods[l]
        qa, ka, va, qb, kb, vb, pc, qk, misc = _in_proj(
            x2, row2(pre_mix_gain[l]), mod, _prep_w_in(w_in[l]), tabs_a, tabs_b,
            row2(a_q_gain[l]), row2(a_k_gain[l]), row2(b_cq_gain[l]), row2(b_ckv_gain[l]),
            _heads_split(w_uq[l], B_HEADS, B_QK_NOPE), _heads_split(w_ukv[l], B_HEADS, B_QK_NOPE),
            c_conv_w[l], row2(c_conv_b[l]), S)
        pc3 = pc.reshape(B, S, -1)

        ya = _attention(qa.reshape(B, S, -1), ka.reshape(B, S, -1), va.reshape(B, S, -1),
                        B=B, S=S, Hk=A_KV_HEADS, G=A_HEADS // A_KV_HEADS, dq=HEAD_DIM, dv=HEAD_DIM,
                        tq=min(512, S), tk=512)

        yb = _attention(qb.reshape(B, S, -1), kb.reshape(B, S, -1), vb.reshape(B, S, -1),
                        B=B, S=S, Hk=B_HEADS, G=1, dq=256, dv=B_V_DIM, tq=min(1024, S), tk=512)

        qk3 = qk.reshape(B, S, -1)
        gate_b_row = jnp.zeros((1, 128), F32).at[0, MISC_GATE_LANE:MISC_GATE_LANE + 4 * C_HEADS].set(c_gate_b[l])
        hf, hr = _mlstm(qk3, pc3, misc.reshape(B, S, 128), gate_b_row)

        x2, h2p, logits = _merge(ya.reshape(T, -1), yb.reshape(T, -1), hf.reshape(T, -1),
                                hr.reshape(T, -1), pc, x2, mod,
                                row2(head_out_gain[l]), w_out[l].astype(BF16), row2(post_mix_gain[l]),
                                row2(pre_ffn_gain[l]), _split_hi_lo(router_w[l]), S)

        idx8, gate8, mask = _router(logits, row2(router_bias[l]))
        pos, blk_e, nused, tail, next_e = _route_plan(idx8, mask, n_blocks)
        xs, ysh = _dispatch(h2p, pos.reshape(T // DISPATCH_TM, 1, DISPATCH_TM * TOP_K), tail,
                            sh_w_gate[l].astype(BF16), sh_w_up[l].astype(BF16),
                            sh_w_down[l].astype(BF16), n_blocks * MOE_BLK)
        ys = _experts(xs, blk_e, nused, next_e, exp_w_gate, exp_w_up, exp_w_down, l)
        x2 = _combine(pos.reshape(T // COMBINE_TM, 1, COMBINE_TM * TOP_K), ysh, gate8, x2, mod,
                      row2(post_ffn_gain[l]), ys, S)
    return x2.reshape(B, S, D)
```

```python
import functools

import numpy as np
import jax
import jax.numpy as jnp
from jax import lax
from jax.experimental import pallas as pl
from jax.experimental.pallas import tpu as pltpu

D_MODEL = 2048
GRID_W = 64
ROPE_THETA = 10000.0
EPS = 1e-6
HEAD_DIM = 128
A_HEADS, A_KV_HEADS = 6, 2
B_HEADS, B_Q_LORA, B_KV_LORA = 6, 384, 256
B_QK_NOPE, B_QK_ROPE, B_V_DIM = 128, 64, 128
C_HEADS, C_DQK, C_DV = 4, 128, 128
CONV_W = 5
MLSTM_CHUNK = 64
N_EXPERTS, TOP_K, EXPERT_FF, SHARED_FF = 64, 6, 512, 512
ROUTED_SCALE = 2.5
N_MOD = 6

COL_AQ, COL_AK, COL_AV = 0, 768, 1024
COL_BC = 1280
COL_MISC = 1920
COL_CV, COL_CO = 2048, 2560
COL_CQK = 3072
PROJ_PAD = 4096
MISC_GATE_LANE = 64
PC_CV, PC_CO = 0, COL_CO - COL_CV

MOE_BLK = 256
DISPATCH_TM = 256
COMBINE_TM = 256
MLSTM_BB = 4
ATTN_ROW_GROUP = 256
VMEM_LIMIT = 56 * 1024 * 1024

BF16 = jnp.bfloat16
F32 = jnp.float32
LOG2_E = 1.4426950408889634


def _cparams(grid_rank):
    return pltpu.CompilerParams(dimension_semantics=("arbitrary",) * grid_rank,
                                vmem_limit_bytes=VMEM_LIMIT)


def _rms(x, gain):
    return x * lax.rsqrt(jnp.mean(x * x, axis=-1, keepdims=True) + EPS) * gain


def _dot(a, b):
    return jnp.dot(a, b, preferred_element_type=F32)


def _dot_t(a, b):
    return lax.dot_general(a, b, (((1,), (1,)), ((), ())), preferred_element_type=F32)


def _mod_kernel(c_ref, w_ref, b_ref, o_ref):
    c = c_ref[...]
    c_act = (c * jax.nn.sigmoid(c)).astype(BF16)
    o_ref[...] = _dot(c_act, w_ref[...].astype(BF16)) + b_ref[...]


def _modulation(c_pad, ada_w, ada_b):
    L, D, N = ada_w.shape
    M = c_pad.shape[0]
    tn = 1536
    return pl.pallas_call(
        _mod_kernel,
        grid=(L, N // tn),
        in_specs=[pl.BlockSpec((M, D), lambda l, j: (0, 0)),
                  pl.BlockSpec((None, D, tn), lambda l, j: (l, 0, j)),
                  pl.BlockSpec((None, 1, tn), lambda l, j: (l, 0, j))],
        out_specs=pl.BlockSpec((None, M, tn), lambda l, j: (l, 0, j)),
        out_shape=jax.ShapeDtypeStruct((L, M, N), F32),
        compiler_params=_cparams(2),
        name="adaln_mod",
    )(c_pad, ada_w, ada_b.reshape(L, 1, N))


def _conv_silu(main, prev, nxt, w_ref, b_ref):
    tm, C = main.shape
    row = lax.broadcasted_iota(jnp.int32, main.shape, 0)
    pad = CONV_W // 2
    y = main * w_ref[pad:pad + 1, :] + b_ref[...]
    for j in range(CONV_W):
        d = j - pad
        if d == 0:
            continue
        shifted = pltpu.roll(main, (-d) % tm, 0)
        for e in range(abs(d)):
            if d < 0:
                shifted = jnp.where(row == e, prev[8 + d + e: 9 + d + e, :], shifted)
            else:
                shifted = jnp.where(row == tm - d + e, nxt[e:e + 1, :], shifted)
        y = y + shifted * w_ref[j:j + 1, :]
    y = y * jax.nn.sigmoid(y)
    lane = lax.broadcasted_iota(jnp.int32, main.shape, 1)
    return y * jnp.where(lane >= C // 2, C_DQK ** -0.5, 1.0)


def _in_proj_kernel(x_ref, xp_ref, xn_ref, gain_ref, mod_ref, w_ref,
                    cosa_ref, sloa_ref, shia_ref, cosb_ref, slob_ref, shib_ref,
                    qg_ref, kg_ref, cqg_ref, ckvg_ref, wuq_ref, wukv_ref, cw_ref, cb_ref,
                    qa_ref, ka_ref, va_ref, qb_ref, kb_ref, vb_ref, pc_ref, qk_ref, misc_ref, *, nb):
    def premix(x):
        y = _rms(x, gain_ref[...])
        return (y * (1.0 + mod_ref[1:2, :]) + mod_ref[0:1, :]).astype(BF16)

    h = premix(x_ref[...])
    pa = _dot(h, w_ref[:, COL_AQ:COL_BC])
    _gqa_prep(pa, (cosa_ref[...], sloa_ref[...], shia_ref[...]), qg_ref[...], kg_ref[...],
              qa_ref, ka_ref, va_ref)
    pb = _dot(h, w_ref[:, COL_BC:COL_CV])
    misc = pb[:, COL_MISC - COL_BC:]
    misc_ref[...] = misc
    _mla_prep(pb[:, :COL_MISC - COL_BC], misc, (cosb_ref[...], slob_ref[...], shib_ref[...]),
              cqg_ref[...], ckvg_ref[...], wuq_ref, wukv_ref, qb_ref, kb_ref, vb_ref)
    pc_ref[...] = _dot(h, w_ref[:, COL_CV:COL_CQK])
    w_qk = w_ref[:, COL_CQK:]
    i = pl.program_id(0)
    halo = premix(jnp.concatenate([xp_ref[...], xn_ref[...]], axis=0))
    halo = _dot(halo, w_qk)
    prev = jnp.where(i % nb == 0, 0.0, halo[:8, :])
    nxt = jnp.where(i % nb == nb - 1, 0.0, halo[8:, :])
    qk_ref[...] = _conv_silu(_dot(h, w_qk), prev, nxt, cw_ref, cb_ref)


def _in_proj(x2, gain, mod, w_in_p, tabs_a, tabs_b, q_gain, k_gain, cq_gain, ckv_gain, w_uq_p, w_ukv_p,
             conv_w, conv_b, S):
    T, D = x2.shape
    tm = 512
    nb = S // tm
    n8 = tm // 8
    row = lambda w: pl.BlockSpec((tm, w), lambda i: (i, 0))
    full = lambda a: pl.BlockSpec(a.shape, lambda i: (0,) * a.ndim)
    tab = pl.BlockSpec((tm, 128), lambda i: (i % nb, 0))
    qw, kw = A_HEADS * HEAD_DIM, A_KV_HEADS * HEAD_DIM
    bqw, bvw = B_HEADS * 256, B_HEADS * B_V_DIM
    cvo, cqk = COL_CQK - COL_CV, PROJ_PAD - COL_CQK
    widths = (qw, kw, kw, bqw, bqw, bvw)
    return pl.pallas_call(
        functools.partial(_in_proj_kernel, nb=nb),
        grid=(T // tm,),
        in_specs=[row(D),
                  pl.BlockSpec((8, D), lambda i: (jnp.maximum(i * n8 - 1, 0), 0)),
                  pl.BlockSpec((8, D), lambda i: (jnp.minimum((i + 1) * n8, T // 8 - 1), 0)),
                  full(gain),
                  pl.BlockSpec((None, N_MOD, D), lambda i: (i // nb, 0, 0)),
                  pl.BlockSpec(w_in_p.shape, lambda i: (0, 0), pipeline_mode=pl.Buffered(1)),
                  tab, tab, tab, tab, tab, tab,
                  full(q_gain), full(k_gain), full(cq_gain), full(ckv_gain), full(w_uq_p), full(w_ukv_p),
                  full(conv_w), full(conv_b)],
        out_specs=[row(w) for w in widths] + [row(cvo), row(cqk), row(128)],
        out_shape=[jax.ShapeDtypeStruct((T, w), BF16) for w in widths]
                  + [jax.ShapeDtypeStruct((T, cvo), F32), jax.ShapeDtypeStruct((T, cqk), F32),
                     jax.ShapeDtypeStruct((T, 128), F32)],
        compiler_params=_cparams(1),
        name="in_proj",
    )(x2, x2, x2, gain, mod, w_in_p, *tabs_a, *tabs_b, q_gain, k_gain, cq_gain, ckv_gain, w_uq_p, w_ukv_p,
      conv_w, conv_b)


def _rope128(y, cos, sin_lo, sin_hi):
    return y * cos + pltpu.roll(y, 96, 1) * sin_lo + pltpu.roll(y, 32, 1) * sin_hi


def _gqa_prep(pa, tabs, q_gain, k_gain, qo_ref, ko_ref, vo_ref):
    cos, slo, shi = tabs
    scale = HEAD_DIM ** -0.5 * LOG2_E
    for h in range(A_HEADS):
        sl = slice(h * HEAD_DIM, (h + 1) * HEAD_DIM)
        y = _rope128(_rms(pa[:, COL_AQ + h * HEAD_DIM: COL_AQ + (h + 1) * HEAD_DIM], q_gain), cos, slo, shi)
        qo_ref[:, sl] = (y * scale).astype(BF16)
    for h in range(A_KV_HEADS):
        sl = slice(h * HEAD_DIM, (h + 1) * HEAD_DIM)
        y = _rope128(_rms(pa[:, COL_AK + h * HEAD_DIM: COL_AK + (h + 1) * HEAD_DIM], k_gain), cos, slo, shi)
        ko_ref[:, sl] = y.astype(BF16)
    vo_ref[...] = pa[:, COL_AV:COL_BC].astype(BF16)


def _rope64x2(y, cos, sin_lo, sin_hi):
    return y * cos + pltpu.roll(y, 112, 1) * sin_lo + pltpu.roll(y, 16, 1) * sin_hi


def _mla_prep(bc, misc, tabs, cq_gain, ckv_gain, wuq_ref, wukv_ref, qo_ref, ko_ref, vo_ref):
    cos, slo, shi = tabs
    lane = lax.broadcasted_iota(jnp.int32, cos.shape, 1)
    first = lane < B_QK_ROPE
    scale = (B_QK_NOPE + B_QK_ROPE) ** -0.5 * LOG2_E
    nope_w = B_HEADS * B_QK_NOPE

    cq = _rms(bc[:, :B_Q_LORA], cq_gain).astype(BF16)
    q = _dot(cq, wuq_ref[...]) * scale
    for p in range(B_HEADS // 2):
        pair = _rope64x2(q[:, nope_w + p * 128: nope_w + (p + 1) * 128], cos, slo, shi)
        for half in range(2):
            h = 2 * p + half
            qo_ref[:, h * 256: h * 256 + 128] = q[:, h * 128:(h + 1) * 128].astype(BF16)
            keep = first if half == 0 else jnp.logical_not(first)
            qo_ref[:, h * 256 + 128: (h + 1) * 256] = jnp.where(keep, pair, 0.0).astype(BF16)

    kr = jnp.where(first, misc, 0.0)
    kr = kr + pltpu.roll(kr, B_QK_ROPE, 1)
    kr = _rope64x2(kr, cos, slo, shi).astype(BF16)
    ckv = _rms(bc[:, B_Q_LORA:], ckv_gain).astype(BF16)
    kv = _dot(ckv, wukv_ref[...])
    for h in range(B_HEADS):
        ko_ref[:, h * 256: h * 256 + 128] = kv[:, h * 128:(h + 1) * 128].astype(BF16)
        ko_ref[:, h * 256 + 128: (h + 1) * 256] = kr
    vo_ref[...] = kv[:, nope_w:].astype(BF16)


def _attn_kernel(q_ref, k_ref, v_ref, o_ref, m_scr, l_scr, acc_scr, *, G, dq, dv, tk, sub):
    tq = q_ref.shape[0]
    S = k_ref.shape[0]
    groups = [(g, r) for g in range(G) for r in range(tq // sub)]
    m_scr[...] = jnp.full(m_scr.shape, -jnp.inf, F32)
    l_scr[...] = jnp.zeros(l_scr.shape, F32)
    acc_scr[...] = jnp.zeros(acc_scr.shape, F32)

    stages = [(c, n) for c in range(S // tk) for n in range(len(groups))]

    def scores(stage):
        c, n = stage
        g, r = groups[n]
        return _dot_t(q_ref[r * sub:(r + 1) * sub, g * dq:(g + 1) * dq], k_ref[c * tk:(c + 1) * tk, :])

    s_next = scores(stages[0])
    for i, (c, n) in enumerate(stages):
        s = s_next
        if i + 1 < len(stages):
            s_next = scores(stages[i + 1])
        rows = slice(n * sub, (n + 1) * sub)
        blocks = [s[:, j * 128:(j + 1) * 128] for j in range(tk // 128)]
        m_blk = functools.reduce(jnp.maximum, blocks)
        m_old = m_scr[rows, :]
        m_new = jnp.maximum(m_old, jnp.max(m_blk, axis=-1, keepdims=True))
        alpha = jnp.exp2(m_old - m_new)
        p_blocks = [jnp.exp2(b - m_new) for b in blocks]
        l_scr[rows, :] = alpha * l_scr[rows, :] + functools.reduce(jnp.add, p_blocks)
        p = jnp.concatenate([b.astype(BF16) for b in p_blocks], axis=1)
        acc_scr[rows, :] = alpha * acc_scr[rows, :] + _dot(p, v_ref[c * tk:(c + 1) * tk, :])
        m_scr[rows, :] = m_new

    for n, (g, r) in enumerate(groups):
        rows = slice(n * sub, (n + 1) * sub)
        l = jnp.sum(l_scr[rows, :], axis=-1, keepdims=True)
        o_ref[r * sub:(r + 1) * sub, g * dv:(g + 1) * dv] = (acc_scr[rows, :] / l).astype(o_ref.dtype)


def _attention(q, k, v, *, B, S, Hk, G, dq, dv, tq, tk):
    assert dv == 128, "row statistics are kept 128 lanes wide to match the value width"
    kern = functools.partial(_attn_kernel, G=G, dq=dq, dv=dv, tk=tk, sub=ATTN_ROW_GROUP)
    M = G * tq
    return pl.pallas_call(
        kern,
        grid=(B, Hk, S // tq),
        in_specs=[pl.BlockSpec((None, tq, G * dq), lambda b, h, i: (b, i, h)),
                  pl.BlockSpec((None, S, dq), lambda b, h, i: (b, 0, h)),
                  pl.BlockSpec((None, S, dv), lambda b, h, i: (b, 0, h))],
        out_specs=pl.BlockSpec((None, tq, G * dv), lambda b, h, i: (b, i, h)),
        out_shape=jax.ShapeDtypeStruct((B, S, Hk * G * dv), BF16),
        scratch_shapes=[pltpu.VMEM((M, 128), F32), pltpu.VMEM((M, 128), F32), pltpu.VMEM((M, dv), F32)],
        compiler_params=_cparams(3),
        name=f"attn_g{G}_d{dq}",
    )(q, k, v)


def _log_sigmoid(x):
    return jnp.minimum(x, 0.0) - jnp.log1p(jnp.exp(-jnp.abs(x)))


def _mlstm_kernel(qf_ref, kf_ref, vf_ref, gf_ref, qr_ref, kr_ref, vr_ref, gr_ref, gb_ref,
                  of_ref, or_ref, C_scr, n_scr, m_scr, *, BB):
    L = MLSTM_CHUNK

    @pl.when(pl.program_id(1) == 0)
    def _():
        C_scr[...] = jnp.zeros(C_scr.shape, F32)
        n_scr[...] = jnp.zeros(n_scr.shape, F32)
        m_scr[...] = jnp.zeros(m_scr.shape, F32)

    glane = lax.broadcasted_iota(jnp.int32, (L, 128), 1)
    jj = lax.broadcasted_iota(jnp.int32, (L, L), 0)
    ss = lax.broadcasted_iota(jnp.int32, (L, L), 1)
    eye = jj == ss
    neg_inf = jnp.float32(-jnp.inf)
    dirs = ((qf_ref, kf_ref, vf_ref, gf_ref, of_ref, ss <= jj, jj <= ss),
            (qr_ref, kr_ref, vr_ref, gr_ref, or_ref, ss >= jj, jj >= ss))

    chains = []
    for d, (q_ref, k_ref, v_ref, g_ref, o_ref, seen, seen_t) in enumerate(dirs):
        for bb in range(BB):
            gates = g_ref[bb] + gb_ref[...]
            lsig = _log_sigmoid(gates)
            for h in range(C_HEADS):
                c = dict(st=(d * BB + bb) * C_HEADS + h, sl=slice(h * C_DQK, (h + 1) * C_DQK), bb=bb,
                         q_ref=q_ref, k_ref=k_ref, v_ref=v_ref, o_ref=o_ref, seen=seen, seen_t=seen_t)
                i_lane = MISC_GATE_LANE + d * (2 * C_HEADS) + h
                c["i_col"] = jnp.sum(jnp.where(glane == i_lane, gates, 0.0), axis=1, keepdims=True)
                c["f_col"] = jnp.sum(jnp.where(glane == i_lane + C_HEADS, lsig, 0.0), axis=1, keepdims=True)
                chains.append(c)
    for c in chains:
        c["f_row"] = jnp.sum(jnp.where(eye, c["f_col"], 0.0), axis=0, keepdims=True)
        c["i_row"] = jnp.sum(jnp.where(eye, c["i_col"], 0.0), axis=0, keepdims=True)
        c["b_row"] = jnp.sum(jnp.where(c["seen_t"], c["f_col"], 0.0), axis=0, keepdims=True)
        c["g_tot"] = jnp.sum(c["f_col"], axis=0, keepdims=True)
    for c in chains:
        c["b_col"] = jnp.sum(jnp.where(c["seen"], c["f_row"], 0.0), axis=1, keepdims=True)
    for c in chains:
        c["m_prev"] = m_scr[c["st"]][:, 0:1]
        c["dmat"] = jnp.where(c["seen"], c["b_col"] - c["b_row"] + c["i_row"], neg_inf)
        c["m_inter"] = c["b_col"] + c["m_prev"]
        c["a_col"] = c["g_tot"] - c["b_col"] + c["i_col"]
    for c in chains:
        c["m_j"] = jnp.maximum(c["m_inter"], jnp.max(c["dmat"], axis=1, keepdims=True))
        c["m_new"] = jnp.maximum(c["g_tot"] + c["m_prev"], jnp.max(c["a_col"], axis=0, keepdims=True))
    for c in chains:
        c["q"] = c["q_ref"][c["bb"], :, c["sl"]]
        c["k"] = c["k_ref"][c["bb"], :, c["sl"]]
        c["qb"], c["kb"] = c["q"].astype(BF16), c["k"].astype(BF16)
        c["vb"] = c["v_ref"][c["bb"], :, c["sl"]].astype(BF16)
        c["qk"] = _dot_t(c["qb"], c["kb"])
    for c in chains:
        c["qC"] = _dot(c["qb"], C_scr[c["st"]].astype(BF16))
    for c in chains:
        c["s"] = c["qk"] * jnp.exp(c["dmat"] - c["m_j"])
        c["inter"] = jnp.exp(c["m_inter"] - c["m_j"])
        c["wk"] = jnp.exp(c["a_col"] - c["m_new"]) * c["k"]
        c["decay"] = jnp.exp(c["g_tot"] + c["m_prev"] - c["m_new"])
    for c in chains:
        c["sv"] = _dot(c["s"].astype(BF16), c["vb"])
    for c in chains:
        c["upd"] = lax.dot_general(c["wk"].astype(BF16), c["vb"], (((0,), (0,)), ((), ())),
                                   preferred_element_type=F32)
    for c in chains:
        n_prev = n_scr[c["st"]]
        den = (jnp.sum(c["s"], axis=1, keepdims=True)
               + c["inter"] * jnp.sum(c["q"] * n_prev, axis=1, keepdims=True))
        num = c["sv"] + c["inter"] * c["qC"]
        c["o_ref"][c["bb"], :, c["sl"]] = num / jnp.maximum(jnp.abs(den), jnp.exp(-c["m_j"]))
        n_scr[c["st"]] = c["decay"] * n_prev + jnp.sum(c["wk"], axis=0, keepdims=True)
    for c in chains:
        C_scr[c["st"]] = c["decay"] * C_scr[c["st"]] + c["upd"]
        m_scr[c["st"]] = jnp.broadcast_to(c["m_new"], m_scr.shape[1:])


def _mlstm(qk3, pc3, misc3, gate_b_row, BB=MLSTM_BB):
    B, S, _ = pc3.shape
    BB = min(BB, B)
    L = MLSTM_CHUNK
    nc = S // L
    W = C_HEADS * C_DQK
    n_chain = 2 * BB * C_HEADS
    fwd = lambda col: (lambda b, c: (b, c, col))
    rev = lambda col: (lambda b, c: (b, nc - 1 - c, col))
    blk = lambda w, imap: pl.BlockSpec((BB, L, w), imap)
    out = jax.ShapeDtypeStruct((B, S, W), F32)
    return pl.pallas_call(
        functools.partial(_mlstm_kernel, BB=BB),
        grid=(B // BB, nc),
        in_specs=[blk(W, fwd(0)), blk(W, fwd(1)), blk(W, fwd(PC_CV // W)), blk(128, fwd(0)),
                  blk(W, rev(0)), blk(W, rev(1)), blk(W, rev(PC_CV // W)), blk(128, rev(0)),
                  pl.BlockSpec((1, 128), lambda b, c: (0, 0))],
        out_specs=[blk(W, fwd(0)), blk(W, rev(0))],
        out_shape=[out, out],
        scratch_shapes=[pltpu.VMEM((n_chain, C_DQK, C_DV), F32),
                        pltpu.VMEM((n_chain, 1, C_DQK), F32),
                        pltpu.VMEM((n_chain, 1, 128), F32)],
        compiler_params=_cparams(2),
        name="mlstm",
    )(qk3, qk3, pc3, misc3, qk3, qk3, pc3, misc3, gate_b_row)


ROW_TILE = 8
HALF_D = D_MODEL // 2
U32 = jnp.uint32
HI_MASK = 0xFFFF0000


def _pack_rows(y, o_ref):
    n = y.shape[0]
    for s in range(ROW_TILE):
        lo = y[:, s * 128:(s + 1) * 128].astype(BF16).astype(F32)
        hi = y[:, HALF_D + s * 128: HALF_D + (s + 1) * 128].astype(BF16).astype(F32)
        w = (lax.bitcast_convert_type(lo, U32) >> 16) | (lax.bitcast_convert_type(hi, U32) & U32(HI_MASK))
        o_ref[pl.ds(s, n, stride=ROW_TILE), :] = w


def _unpack_rows(ref, n):
    lo, hi = [], []
    for s in range(ROW_TILE):
        w = ref[pl.ds(s, n, stride=ROW_TILE), :]
        lo.append(lax.bitcast_convert_type(w << 16, F32))
        hi.append(lax.bitcast_convert_type(w & U32(HI_MASK), F32))
    return lo + hi


def _merge_kernel(ya_ref, yb_ref, hf_ref, hr_ref, co_ref, x_ref, mod_ref, hg_ref, wout_ref, pmg_ref,
                  pfg_ref, rw_ref, xo_ref, h2p_ref, lg_ref, y_scr):
    attn_heads = A_HEADS + B_HEADS
    for h in range(attn_heads + C_HEADS):
        sl = slice(h * HEAD_DIM, (h + 1) * HEAD_DIM)
        g = hg_ref[:, sl]
        if h < A_HEADS:
            y = _rms(ya_ref[:, sl].astype(F32), g)
        elif h < attn_heads:
            y = _rms(yb_ref[:, (h - A_HEADS) * HEAD_DIM:(h - A_HEADS + 1) * HEAD_DIM].astype(F32), g)
        else:
            cs = slice((h - attn_heads) * HEAD_DIM, (h - attn_heads + 1) * HEAD_DIM)
            y = _rms(hf_ref[:, cs] + hr_ref[:, cs], g) * jax.nn.sigmoid(co_ref[:, cs])
        y_scr[:, sl] = y.astype(BF16)
    y = _dot(y_scr[...], wout_ref[...])
    x_new = x_ref[...] + mod_ref[2:3, :] * _rms(y, pmg_ref[...])
    xo_ref[...] = x_new
    h2 = _rms(x_new, pfg_ref[...]) * (1.0 + mod_ref[4:5, :]) + mod_ref[3:4, :]
    _pack_rows(h2, h2p_ref)
    h_hi = h2.astype(BF16)
    h_lo = (h2 - h_hi.astype(F32)).astype(BF16)
    a = _dot(h_hi, rw_ref[...])
    b = _dot(h_lo, rw_ref[...])
    E = N_EXPERTS
    lg_ref[...] = a[:, :E] + (a[:, E:] + b[:, :E]) + b[:, E:]


def _merge(ya, yb, hf, hr, pc, x2, mod, head_gain, w_out_b, pm_gain, pf_gain, router_w, S):
    T, D = x2.shape
    tm = 512
    nb = S // tm
    aw, cw = A_HEADS * HEAD_DIM, C_HEADS * C_DV
    row = lambda w: pl.BlockSpec((tm, w), lambda i: (i, 0))
    full = lambda a: pl.BlockSpec(a.shape, lambda i: (0,) * a.ndim)
    return pl.pallas_call(
        _merge_kernel,
        grid=(T // tm,),
        in_specs=[row(aw), row(aw), row(cw), row(cw),
                  pl.BlockSpec((tm, cw), lambda i: (i, PC_CO // cw)),
                  row(D),
                  pl.BlockSpec((None, N_MOD, D), lambda i: (i // nb, 0, 0)),
                  full(head_gain), full(w_out_b), full(pm_gain), full(pf_gain), full(router_w)],
        out_specs=[row(D), pl.BlockSpec((tm * ROW_TILE, 128), lambda i: (i, 0)),
                   row(N_EXPERTS)],
        out_shape=[jax.ShapeDtypeStruct((T, D), F32),
                   jax.ShapeDtypeStruct((T * ROW_TILE, 128), U32),
                   jax.ShapeDtypeStruct((T, N_EXPERTS), F32)],
        scratch_shapes=[pltpu.VMEM((tm, D), BF16)],
        compiler_params=_cparams(1),
        name="merge_out_proj",
    )(ya, yb, hf, hr, pc, x2, mod, head_gain, w_out_b, pm_gain, pf_gain, router_w)


def _router_kernel(lg_ref, bias_ref, idx_ref, gate_ref, mask_ref):
    scores = jax.nn.sigmoid(lg_ref[...])
    sel = scores + bias_ref[...]
    lane = lax.broadcasted_iota(jnp.int32, scores.shape, 1).astype(F32)
    col = lax.broadcasted_iota(jnp.int32, idx_ref.shape, 1)
    idx = jnp.zeros(idx_ref.shape, F32)
    gate = jnp.zeros(gate_ref.shape, F32)
    mask = jnp.zeros(scores.shape, F32)
    for kk in range(TOP_K):
        mx = jnp.max(sel, axis=1, keepdims=True)
        am = jnp.min(jnp.where(sel == mx, lane, float(N_EXPERTS)), axis=1, keepdims=True)
        hit = lane == am
        sc = jnp.sum(jnp.where(hit, scores, 0.0), axis=1, keepdims=True)
        idx = jnp.where(col == kk, am, idx)
        gate = jnp.where(col == kk, sc, gate)
        mask = jnp.where(hit, 1.0, mask)
        sel = jnp.where(hit, -jnp.inf, sel)
    gate = gate / jnp.sum(gate, axis=1, keepdims=True) * ROUTED_SCALE
    idx_ref[...] = idx.astype(jnp.int32)
    gate_ref[...] = gate
    mask_ref[...] = mask.astype(jnp.int32)


def _router(logits, bias):
    T, E = logits.shape
    tm = 1024
    row = lambda w: pl.BlockSpec((tm, w), lambda i: (i, 0))
    return pl.pallas_call(
        _router_kernel,
        grid=(T // tm,),
        in_specs=[row(E), pl.BlockSpec((1, E), lambda i: (0, 0))],
        out_specs=[row(8), row(8), row(E)],
        out_shape=[jax.ShapeDtypeStruct((T, 8), jnp.int32),
                   jax.ShapeDtypeStruct((T, 8), F32),
                   jax.ShapeDtypeStruct((T, E), jnp.int32)],
        compiler_params=_cparams(1),
        name="router_topk",
    )(logits, bias)


TAIL_BITS = MOE_BLK.bit_length() - 1


def _tail_copies(tail_ref, zero_scr, xs_ref, zsem, fn):
    for e in range(N_EXPERTS):
        start_row = tail_ref[2 * e]
        n_tail = tail_ref[2 * e + 1]
        for bit in range(TAIL_BITS):
            size = 1 << bit
            cur = start_row + (n_tail & ~(2 * size - 1))
            cur = pl.multiple_of(cur * ROW_TILE, ROW_TILE)

            @pl.when((n_tail & size) != 0)
            def _():
                fn(pltpu.make_async_copy(zero_scr.at[pl.ds(0, size * ROW_TILE), :],
                                         xs_ref.at[pl.ds(cur, size * ROW_TILE), :], zsem))


def _row_tile(ref, row):
    start = row * ROW_TILE
    if not isinstance(row, int):
        start = pl.multiple_of(start, ROW_TILE)
    return ref.at[pl.ds(start, ROW_TILE), :]


def _dispatch_kernel(tail_ref, pos_ref, hp_ref, wsg_ref, wsu_ref, wsd_ref, xs_ref, ysh_ref,
                     zero_scr, sem, zsem, *, tm):
    i = pl.program_id(0)

    for r in range(tm):
        src = _row_tile(hp_ref, r)
        for kk in range(TOP_K):
            pltpu.make_async_copy(src, _row_tile(xs_ref, pos_ref[0, r * TOP_K + kk]),
                                  sem).start(priority=kk % 2)

    @pl.when(i == 0)
    def _():
        zero_scr[...] = jnp.zeros(zero_scr.shape, zero_scr.dtype)
        _tail_copies(tail_ref, zero_scr, xs_ref, zsem, lambda cp: cp.start())
        _tail_copies(tail_ref, zero_scr, xs_ref, zsem, lambda cp: cp.wait())

    h = jnp.concatenate([c.astype(BF16) for c in _unpack_rows(hp_ref, tm)], axis=1)
    gte = _dot(h, wsg_ref[...])
    a = gte * jax.nn.sigmoid(gte) * _dot(h, wsu_ref[...])
    ysh_ref[...] = _dot(a.astype(BF16), wsd_ref[...])

    for kk in range(TOP_K):
        pltpu.make_async_copy(hp_ref, xs_ref.at[pl.ds(0, tm * ROW_TILE), :], sem).wait()


def _dispatch(h2p, pos3, tail_info, ws_gate_b, ws_up_b, ws_down_b, P):
    T, D = h2p.shape[0] // ROW_TILE, D_MODEL
    tm = pos3.shape[2] // TOP_K
    kern = functools.partial(_dispatch_kernel, tm=tm)
    full = lambda a: pl.BlockSpec(a.shape, lambda i, tail: (0,) * a.ndim)
    return pl.pallas_call(
        kern,
        grid_spec=pltpu.PrefetchScalarGridSpec(
            num_scalar_prefetch=1,
            grid=(T // tm,),
            in_specs=[pl.BlockSpec((None, 1, tm * TOP_K), lambda i, tail: (i, 0, 0),
                                   memory_space=pltpu.SMEM),
                      pl.BlockSpec((tm * ROW_TILE, 128), lambda i, tail: (i, 0)),
                      full(ws_gate_b), full(ws_up_b), full(ws_down_b)],
            out_specs=[pl.BlockSpec(memory_space=pl.ANY),
                       pl.BlockSpec((tm, D), lambda i, tail: (i, 0))],
            scratch_shapes=[pltpu.VMEM((MOE_BLK // 2 * ROW_TILE, 128), U32),
                            pltpu.SemaphoreType.DMA(()), pltpu.SemaphoreType.DMA(())]),
        out_shape=[jax.ShapeDtypeStruct((P * ROW_TILE, 128), U32),
                   jax.ShapeDtypeStruct((T, D), F32)],
        compiler_params=_cparams(1),
        name="moe_dispatch",
    )(tail_info, pos3, h2p, ws_gate_b, ws_up_b, ws_down_b)


EXPERT_TILES = 4


def _expert_kernel(blk_e_ref, nused_ref, next_ref, xs_ref, wg_hbm, wu_hbm, wd_hbm, ys_ref,
                   wg_f, wu_f, wd_f, wg_b, wu_b, wd_b, sems, *, layer):
    step = pl.program_id(0)
    rows = MOE_BLK * ROW_TILE

    def fetch(ex):
        return [pltpu.make_async_copy(hbm.at[layer, ex], buf, sems.at[n])
                for n, (hbm, buf) in enumerate(((wg_hbm, wg_f), (wu_hbm, wu_f), (wd_hbm, wd_f)))]

    @pl.when(step == 0)
    def _():
        for cp in fetch(blk_e_ref[0]):
            cp.start(priority=1)

    for j in range(EXPERT_TILES):
        i = step * EXPERT_TILES + j
        e = blk_e_ref[i]
        e_prev = blk_e_ref[jnp.maximum(i - 1, 0)]
        used = i < nused_ref[0]
        first = used & ((i == 0) | (e != e_prev))
        x_view = xs_ref.at[pl.ds(j * rows, rows), :]
        y_view = ys_ref.at[pl.ds(j * rows, rows), :]

        def compute(switch, e=e, x_view=x_view, y_view=y_view):
            x = jnp.concatenate([c.astype(BF16) for c in _unpack_rows(x_view, MOE_BLK)], axis=1)
            if switch:
                for cp in fetch(e):
                    cp.wait()
                wg_b[...] = wg_f[...].astype(BF16)
            gte = _dot(x, wg_b[...])
            if switch:
                wu_b[...] = wu_f[...].astype(BF16)
            up = _dot(x, wu_b[...])
            if switch:
                wd_b[...] = wd_f[...].astype(BF16)
                e_next = next_ref[e]

                @pl.when(e_next >= 0)
                def _():
                    for cp in fetch(e_next):
                        cp.start(priority=1)
            a = gte * jax.nn.sigmoid(gte) * up
            _pack_rows(_dot(a.astype(BF16), wd_b[...]), y_view)

        pl.when(first)(functools.partial(compute, True))
        pl.when(used & jnp.logical_not(first))(functools.partial(compute, False))

        @pl.when(jnp.logical_not(used))
        def _(y_view=y_view):
            y_view[...] = jnp.zeros(y_view.shape, y_view.dtype)


def _experts(xs, blk_e, nused, next_e, w_gate, w_up, w_down, layer):
    P = xs.shape[0] // ROW_TILE
    D, F = w_gate.shape[-2:]
    NB = P // MOE_BLK
    assert NB % EXPERT_TILES == 0
    rows = EXPERT_TILES * MOE_BLK * ROW_TILE
    last = lambda s, nu: jnp.minimum(s, (nu[0] - 1) // EXPERT_TILES)
    hbm = pl.BlockSpec(memory_space=pl.ANY)
    return pl.pallas_call(
        functools.partial(_expert_kernel, layer=layer),
        grid_spec=pltpu.PrefetchScalarGridSpec(
            num_scalar_prefetch=3,
            grid=(NB // EXPERT_TILES,),
            in_specs=[pl.BlockSpec((rows, 128), lambda s, be, nu, nx: (last(s, nu), 0)),
                      hbm, hbm, hbm],
            out_specs=pl.BlockSpec((rows, 128), lambda s, be, nu, nx: (s, 0)),
            scratch_shapes=[pltpu.VMEM((D, F), F32), pltpu.VMEM((D, F), F32), pltpu.VMEM((F, D), F32),
                            pltpu.VMEM((D, F), BF16), pltpu.VMEM((D, F), BF16),
                            pltpu.VMEM((F, D), BF16), pltpu.SemaphoreType.DMA((3,))]),
        out_shape=jax.ShapeDtypeStruct((P * ROW_TILE, 128), U32),
        compiler_params=_cparams(1),
        name="moe_experts",
    )(blk_e, nused, next_e, xs, w_gate, w_up, w_down)


def _combine_kernel(pos_ref, pos_next_ref, ysh_ref, gate_ref, x_ref, mod_ref, pg_ref, ys_ref,
                    xo_ref, gbuf, sems, *, tm):
    i = pl.program_id(0)
    slot = i % 2

    def gather(p_ref, s, straight_line):
        def start(r, c=None):
            for kk in range(TOP_K):
                pltpu.make_async_copy(_row_tile(ys_ref, p_ref[0, r * TOP_K + kk]),
                                      _row_tile(gbuf.at[s, kk], r), sems.at[s]).start(priority=kk % 2)
            return c
        if straight_line:
            for r in range(tm):
                start(r)
        else:
            lax.fori_loop(0, tm, start, 0)

    @pl.when(i == 0)
    def _():
        gather(pos_ref, 0, False)

    for s in range(2):
        @pl.when((i + 1 < pl.num_programs(0)) & (slot == 1 - s))
        def _():
            gather(pos_next_ref, s, True)

    for kk in range(TOP_K):
        pltpu.make_async_copy(ys_ref.at[pl.ds(0, tm * ROW_TILE), :], gbuf.at[slot, kk],
                              sems.at[slot]).wait()
    RG = 64

    def routed(rg):
        rows = slice(rg * RG, (rg + 1) * RG)
        gate = gate_ref[rows, :]
        g_wide = [jnp.broadcast_to(gate[:, kk:kk + 1], (RG, 128)) for kk in range(TOP_K)]
        for s in range(ROW_TILE):
            lo = hi = None
            for kk in range(TOP_K):
                w = gbuf[slot, kk, pl.ds(rg * RG * ROW_TILE + s, RG, stride=ROW_TILE), :]
                t_lo = g_wide[kk] * lax.bitcast_convert_type(w << 16, F32)
                t_hi = g_wide[kk] * lax.bitcast_convert_type(w & U32(HI_MASK), F32)
                lo = t_lo if lo is None else lo + t_lo
                hi = t_hi if hi is None else hi + t_hi
            xo_ref[rows, s * 128:(s + 1) * 128] = lo
            xo_ref[rows, HALF_D + s * 128: HALF_D + (s + 1) * 128] = hi

    for rg in range(tm // RG):
        routed(rg)
    y = xo_ref[...] + ysh_ref[...]
    xo_ref[...] = x_ref[...] + mod_ref[5:6, :] * _rms(y, pg_ref[...])


def _combine(pos3, ysh, gate8, x2, mod, post_gain, ys, S):
    T, D = x2.shape
    tm = pos3.shape[2] // TOP_K
    nb = S // tm
    kern = functools.partial(_combine_kernel, tm=tm)
    row = lambda w: pl.BlockSpec((tm, w), lambda i: (i, 0))
    full = lambda a: pl.BlockSpec(a.shape, lambda i: (0,) * a.ndim)
    n_steps = T // tm
    return pl.pallas_call(
        kern,
        grid=(n_steps,),
        in_specs=[pl.BlockSpec((None, 1, tm * TOP_K), lambda i: (i, 0, 0), memory_space=pltpu.SMEM),
                  pl.BlockSpec((None, 1, tm * TOP_K), lambda i: (jnp.minimum(i + 1, n_steps - 1), 0, 0),
                               memory_space=pltpu.SMEM),
                  row(D), row(8), row(D),
                  pl.BlockSpec((None, N_MOD, D), lambda i: (i // nb, 0, 0)),
                  full(post_gain),
                  pl.BlockSpec(memory_space=pl.ANY)],
        out_specs=row(D),
        out_shape=jax.ShapeDtypeStruct((T, D), F32),
        scratch_shapes=[pltpu.VMEM((2, TOP_K, tm * ROW_TILE, 128), U32),
                        pltpu.SemaphoreType.DMA((2,))],
        compiler_params=_cparams(1),
        name="moe_combine",
    )(pos3, pos3, ysh, gate8, x2, mod, post_gain, ys)


def _route_plan(idx8, mask, n_blocks):
    idx = idx8[:, :TOP_K]
    counts = jnp.sum(mask, axis=0)
    rank = jnp.cumsum(mask, axis=0) - mask
    padded = (counts + MOE_BLK - 1) // MOE_BLK * MOE_BLK
    pend = jnp.cumsum(padded)
    pstart = pend - padded
    pos = jnp.take_along_axis(pstart[None, :] + rank, idx, axis=1).astype(jnp.int32)
    blk_start = jnp.arange(n_blocks, dtype=jnp.int32) * MOE_BLK
    blk_e = jnp.minimum(jnp.sum(pend[None, :] <= blk_start[:, None], axis=1), N_EXPERTS - 1)
    nused = (pend[-1:] // MOE_BLK).astype(jnp.int32)
    tail = jnp.stack([pstart + counts, padded - counts], axis=1).reshape(-1).astype(jnp.int32)
    ids = jnp.where(counts > 0, jnp.arange(N_EXPERTS), N_EXPERTS)
    after = jnp.concatenate([lax.cummin(ids, reverse=True)[1:], jnp.full((1,), N_EXPERTS)])
    next_e = jnp.where(after < N_EXPERTS, after, -1).astype(jnp.int32)
    return pos, blk_e.astype(jnp.int32), nused, tail, next_e


def _rope_tables(S, rot_dim):
    rows = S // GRID_W
    row = jnp.repeat(jnp.arange(rows, dtype=F32), GRID_W)
    col = jnp.tile(jnp.arange(GRID_W, dtype=F32), rows)
    axis_dim = rot_dim // 2
    inv_freq = ROPE_THETA ** (-jnp.arange(0, axis_dim, 2, dtype=F32) / axis_dim)
    ang_r = row[:, None] * inv_freq
    ang_c = col[:, None] * inv_freq
    z = jnp.zeros_like(ang_r)
    cos = jnp.concatenate([jnp.cos(ang_r)] * 2 + [jnp.cos(ang_c)] * 2, axis=1)
    sin_lo = jnp.concatenate([-jnp.sin(ang_r), z, -jnp.sin(ang_c), z], axis=1)
    sin_hi = jnp.concatenate([z, jnp.sin(ang_r), z, jnp.sin(ang_c)], axis=1)
    reps = 128 // rot_dim
    return tuple(jnp.tile(t, (1, reps)) for t in (cos, sin_lo, sin_hi))


def _prep_w_in(w):
    widths = (768, 256, 256, B_Q_LORA, B_KV_LORA, B_QK_ROPE, 512, 512, 512, 512, 4 * C_HEADS)
    offs = np.cumsum(widths)[:-1].tolist()
    a_q, a_k, a_v, b_cq, b_ckv, b_kr, c_q, c_k, c_v, c_o, c_g = jnp.split(w, offs, axis=-1)
    pad = jnp.zeros(w.shape[:-1] + (128 - B_QK_ROPE - 4 * C_HEADS,), w.dtype)
    out = jnp.concatenate([a_q, a_k, a_v, b_cq, b_ckv, b_kr, c_g, pad, c_v, c_o, c_q, c_k], axis=-1)
    assert out.shape[-1] == PROJ_PAD
    return out.astype(BF16)


def _split_hi_lo(w):
    hi = w.astype(BF16)
    lo = (w - hi.astype(F32)).astype(BF16)
    return jnp.concatenate([hi, lo], axis=1)


def _heads_split(w, n_heads, first):
    K = w.shape[0]
    w3 = w.reshape(K, n_heads, -1)
    return jnp.concatenate([w3[:, :, :first].reshape(K, -1), w3[:, :, first:].reshape(K, -1)],
                           axis=1).astype(BF16)


def kernel(x, c, ada_w, ada_b, pre_mix_gain, w_in, a_q_gain, a_k_gain, b_cq_gain, b_ckv_gain, w_uq, w_ukv, c_conv_w, c_conv_b, c_gate_b, head_out_gain, w_out, post_mix_gain, pre_ffn_gain, router_w, router_bias, exp_w_gate, exp_w_up, exp_w_down, sh_w_gate, sh_w_up, sh_w_down, post_ffn_gain):
    B, S, D = x.shape
    T = B * S
    L = ada_w.shape[0]
    assert D == D_MODEL and S % 512 == 0 and T % 1024 == 0, "row tiles assume these multiples"
    n_blocks = T * TOP_K // MOE_BLK + N_EXPERTS
    row2 = lambda v: v.reshape(1, -1)

    tabs_a = _rope_tables(S, HEAD_DIM)
    tabs_b = _rope_tables(S, B_QK_ROPE)
    c_pad = jnp.zeros((8, D), F32).at[:B].set(c)
    mods = _modulation(c_pad, ada_w, ada_b)[:, :B].reshape(L, B, N_MOD, D)

    x2 = x.reshape(T, D)
    for l in range(L):
        mod = mods[l]
        qa, ka, va, qb, kb, vb, pc, qk, misc = _in_proj(
            x2, row2(pre_mix_gain[l]), mod, _prep_w_in(w_in[l]), tabs_a, tabs_b,
            row2(a_q_gain[l]), row2(a_k_gain[l]), row2(b_cq_gain[l]), row2(b_ckv_gain[l]),
            _heads_split(w_uq[l], B_HEADS, B_QK_NOPE), _heads_split(w_ukv[l], B_HEADS, B_QK_NOPE),
            c_conv_w[l], row2(c_conv_b[l]), S)
        pc3 = pc.reshape(B, S, -1)

        ya = _attention(qa.reshape(B, S, -1), ka.reshape(B, S, -1), va.reshape(B, S, -1),
                        B=B, S=S, Hk=A_KV_HEADS, G=A_HEADS // A_KV_HEADS, dq=HEAD_DIM, dv=HEAD_DIM,
                        tq=min(1024, S), tk=512)

        yb = _attention(qb.reshape(B, S, -1), kb.reshape(B, S, -1), vb.reshape(B, S, -1),
                        B=B, S=S, Hk=B_HEADS, G=1, dq=256, dv=B_V_DIM, tq=min(2048, S), tk=512)

        qk3 = qk.reshape(B, S, -1)
        gate_b_row = jnp.zeros((1, 128), F32).at[0, MISC_GATE_LANE:MISC_GATE_LANE + 4 * C_HEADS].set(c_gate_b[l])
        hf, hr = _mlstm(qk3, pc3, misc.reshape(B, S, 128), gate_b_row)

        x2, h2p, logits = _merge(ya.reshape(T, -1), yb.reshape(T, -1), hf.reshape(T, -1),
                                hr.reshape(T, -1), pc, x2, mod,
                                row2(head_out_gain[l]), w_out[l].astype(BF16), row2(post_mix_gain[l]),
                                row2(pre_ffn_gain[l]), _split_hi_lo(router_w[l]), S)

        idx8, gate8, mask = _router(logits, row2(router_bias[l]))
        pos, blk_e, nused, tail, next_e = _route_plan(idx8, mask, n_blocks)
        xs, ysh = _dispatch(h2p, pos.reshape(T // DISPATCH_TM, 1, DISPATCH_TM * TOP_K), tail,
                            sh_w_gate[l].astype(BF16), sh_w_up[l].astype(BF16),
                            sh_w_down[l].astype(BF16), n_blocks * MOE_BLK)
        ys = _experts(xs, blk_e, nused, next_e, exp_w_gate, exp_w_up, exp_w_down, l)
        x2 = _combine(pos.reshape(T // COMBINE_TM, 1, COMBINE_TM * TOP_K), ysh, gate8, x2, mod,
                      row2(post_ffn_gain[l]), ys, S)
    return x2.reshape(B, S, D)
```

```python
import functools

import numpy as np
import jax
import jax.numpy as jnp
from jax import lax
from jax.experimental import pallas as pl
from jax.experimental.pallas import tpu as pltpu

D_MODEL = 2048
GRID_W = 64
ROPE_THETA = 10000.0
EPS = 1e-6
HEAD_DIM = 128
A_HEADS, A_KV_HEADS = 6, 2
B_HEADS, B_Q_LORA, B_KV_LORA = 6, 384, 256
B_QK_NOPE, B_QK_ROPE, B_V_DIM = 128, 64, 128
C_HEADS, C_DQK, C_DV = 4, 128, 128
CONV_W = 5
MLSTM_CHUNK = 64
N_EXPERTS, TOP_K, EXPERT_FF, SHARED_FF = 64, 6, 512, 512
ROUTED_SCALE = 2.5
N_MOD = 6

COL_AQ, COL_AK, COL_AV = 0, 768, 1024
COL_BC = 1280
COL_MISC = 1920
COL_CV, COL_CO = 2048, 2560
COL_CQK = 3072
PROJ_PAD = 4096
MISC_GATE_LANE = 64
PC_CV, PC_CO = 0, COL_CO - COL_CV

MOE_BLK = 256
DISPATCH_TM = 256
COMBINE_TM = 256
MLSTM_BB = 4
ATTN_ROW_GROUP = 256
VMEM_LIMIT = 56 * 1024 * 1024

BF16 = jnp.bfloat16
F32 = jnp.float32
LOG2_E = 1.4426950408889634


def _cparams(grid_rank):
    return pltpu.CompilerParams(dimension_semantics=("arbitrary",) * grid_rank,
                                vmem_limit_bytes=VMEM_LIMIT)


def _rms(x, gain):
    return x * lax.rsqrt(jnp.mean(x * x, axis=-1, keepdims=True) + EPS) * gain


def _dot(a, b):
    return jnp.dot(a, b, preferred_element_type=F32)


def _dot_t(a, b):
    return lax.dot_general(a, b, (((1,), (1,)), ((), ())), preferred_element_type=F32)


def _mod_kernel(c_ref, w_ref, b_ref, o_ref):
    c = c_ref[...]
    c_act = (c * jax.nn.sigmoid(c)).astype(BF16)
    o_ref[...] = _dot(c_act, w_ref[...].astype(BF16)) + b_ref[...]


def _modulation(c_pad, ada_w, ada_b):
    L, D, N = ada_w.shape
    M = c_pad.shape[0]
    tn = 1536
    return pl.pallas_call(
        _mod_kernel,
        grid=(L, N // tn),
        in_specs=[pl.BlockSpec((M, D), lambda l, j: (0, 0)),
                  pl.BlockSpec((None, D, tn), lambda l, j: (l, 0, j)),
                  pl.BlockSpec((None, 1, tn), lambda l, j: (l, 0, j))],
        out_specs=pl.BlockSpec((None, M, tn), lambda l, j: (l, 0, j)),
        out_shape=jax.ShapeDtypeStruct((L, M, N), F32),
        compiler_params=_cparams(2),
        name="adaln_mod",
    )(c_pad, ada_w, ada_b.reshape(L, 1, N))


def _conv_silu(main, prev, nxt, w_ref, b_ref):
    tm, C = main.shape
    row = lax.broadcasted_iota(jnp.int32, main.shape, 0)
    pad = CONV_W // 2
    y = main * w_ref[pad:pad + 1, :] + b_ref[...]
    for j in range(CONV_W):
        d = j - pad
        if d == 0:
            continue
        shifted = pltpu.roll(main, (-d) % tm, 0)
        for e in range(abs(d)):
            if d < 0:
                shifted = jnp.where(row == e, prev[8 + d + e: 9 + d + e, :], shifted)
            else:
                shifted = jnp.where(row == tm - d + e, nxt[e:e + 1, :], shifted)
        y = y + shifted * w_ref[j:j + 1, :]
    y = y * jax.nn.sigmoid(y)
    lane = lax.broadcasted_iota(jnp.int32, main.shape, 1)
    return y * jnp.where(lane >= C // 2, C_DQK ** -0.5, 1.0)


def _in_proj_kernel(x_ref, xp_ref, xn_ref, gain_ref, mod_ref, w_ref,
                    cosa_ref, sloa_ref, shia_ref, cosb_ref, slob_ref, shib_ref,
                    qg_ref, kg_ref, cqg_ref, ckvg_ref, wuq_ref, wukv_ref, cw_ref, cb_ref,
                    qa_ref, ka_ref, va_ref, qb_ref, kb_ref, vb_ref, pc_ref, qk_ref, misc_ref, *, nb):
    def premix(x):
        y = _rms(x, gain_ref[...])
        return (y * (1.0 + mod_ref[1:2, :]) + mod_ref[0:1, :]).astype(BF16)

    h = premix(x_ref[...])
    pa = _dot(h, w_ref[:, COL_AQ:COL_BC])
    _gqa_prep(pa, (cosa_ref[...], sloa_ref[...], shia_ref[...]), qg_ref[...], kg_ref[...],
              qa_ref, ka_ref, va_ref)
    pb = _dot(h, w_ref[:, COL_BC:COL_CV])
    misc = pb[:, COL_MISC - COL_BC:]
    misc_ref[...] = misc
    _mla_prep(pb[:, :COL_MISC - COL_BC], misc, (cosb_ref[...], slob_ref[...], shib_ref[...]),
              cqg_ref[...], ckvg_ref[...], wuq_ref, wukv_ref, qb_ref, kb_ref, vb_ref)
    pc_ref[...] = _dot(h, w_ref[:, COL_CV:COL_CQK])
    w_qk = w_ref[:, COL_CQK:]
    i = pl.program_id(0)
    halo = premix(jnp.concatenate([xp_ref[...], xn_ref[...]], axis=0))
    halo = _dot(halo, w_qk)
    prev = jnp.where(i % nb == 0, 0.0, halo[:8, :])
    nxt = jnp.where(i % nb == nb - 1, 0.0, halo[8:, :])
    qk_ref[...] = _conv_silu(_dot(h, w_qk), prev, nxt, cw_ref, cb_ref)


def _in_proj(x2, gain, mod, w_in_p, tabs_a, tabs_b, q_gain, k_gain, cq_gain, ckv_gain, w_uq_p, w_ukv_p,
             conv_w, conv_b, S):
    T, D = x2.shape
    tm = 512
    nb = S // tm
    n8 = tm // 8
    row = lambda w: pl.BlockSpec((tm, w), lambda i: (i, 0))
    full = lambda a: pl.BlockSpec(a.shape, lambda i: (0,) * a.ndim)
    tab = pl.BlockSpec((tm, 128), lambda i: (i % nb, 0))
    qw, kw = A_HEADS * HEAD_DIM, A_KV_HEADS * HEAD_DIM
    bqw, bvw = B_HEADS * 256, B_HEADS * B_V_DIM
    cvo, cqk = COL_CQK - COL_CV, PROJ_PAD - COL_CQK
    widths = (qw, kw, kw, bqw, bqw, bvw)
    return pl.pallas_call(
        functools.partial(_in_proj_kernel, nb=nb),
        grid=(T // tm,),
        in_specs=[row(D),
                  pl.BlockSpec((8, D), lambda i: (jnp.maximum(i * n8 - 1, 0), 0)),
                  pl.BlockSpec((8, D), lambda i: (jnp.minimum((i + 1) * n8, T // 8 - 1), 0)),
                  full(gain),
                  pl.BlockSpec((None, N_MOD, D), lambda i: (i // nb, 0, 0)),
                  pl.BlockSpec(w_in_p.shape, lambda i: (0, 0), pipeline_mode=pl.Buffered(1)),
                  tab, tab, tab, tab, tab, tab,
                  full(q_gain), full(k_gain), full(cq_gain), full(ckv_gain), full(w_uq_p), full(w_ukv_p),
                  full(conv_w), full(conv_b)],
        out_specs=[row(w) for w in widths] + [row(cvo), row(cqk), row(128)],
        out_shape=[jax.ShapeDtypeStruct((T, w), BF16) for w in widths]
                  + [jax.ShapeDtypeStruct((T, cvo), F32), jax.ShapeDtypeStruct((T, cqk), F32),
                     jax.ShapeDtypeStruct((T, 128), F32)],
        compiler_params=_cparams(1),
        name="in_proj",
    )(x2, x2, x2, gain, mod, w_in_p, *tabs_a, *tabs_b, q_gain, k_gain, cq_gain, ckv_gain, w_uq_p, w_ukv_p,
      conv_w, conv_b)


def _rope128(y, cos, sin_lo, sin_hi):
    return y * cos + pltpu.roll(y, 96, 1) * sin_lo + pltpu.roll(y, 32, 1) * sin_hi


def _gqa_prep(pa, tabs, q_gain, k_gain, qo_ref, ko_ref, vo_ref):
    cos, slo, shi = tabs
    scale = HEAD_DIM ** -0.5 * LOG2_E
    for h in range(A_HEADS):
        sl = slice(h * HEAD_DIM, (h + 1) * HEAD_DIM)
        y = _rope128(_rms(pa[:, COL_AQ + h * HEAD_DIM: COL_AQ + (h + 1) * HEAD_DIM], q_gain), cos, slo, shi)
        qo_ref[:, sl] = (y * scale).astype(BF16)
    for h in range(A_KV_HEADS):
        sl = slice(h * HEAD_DIM, (h + 1) * HEAD_DIM)
        y = _rope128(_rms(pa[:, COL_AK + h * HEAD_DIM: COL_AK + (h + 1) * HEAD_DIM], k_gain), cos, slo, shi)
        ko_ref[:, sl] = y.astype(BF16)
    vo_ref[...] = pa[:, COL_AV:COL_BC].astype(BF16)


def _rope64x2(y, cos, sin_lo, sin_hi):
    return y * cos + pltpu.roll(y, 112, 1) * sin_lo + pltpu.roll(y, 16, 1) * sin_hi


def _mla_prep(bc, misc, tabs, cq_gain, ckv_gain, wuq_ref, wukv_ref, qo_ref, ko_ref, vo_ref):
    cos, slo, shi = tabs
    lane = lax.broadcasted_iota(jnp.int32, cos.shape, 1)
    first = lane < B_QK_ROPE
    scale = (B_QK_NOPE + B_QK_ROPE) ** -0.5 * LOG2_E
    nope_w = B_HEADS * B_QK_NOPE

    cq = _rms(bc[:, :B_Q_LORA], cq_gain).astype(BF16)
    q = _dot(cq, wuq_ref[...]) * scale
    for p in range(B_HEADS // 2):
        pair = _rope64x2(q[:, nope_w + p * 128: nope_w + (p + 1) * 128], cos, slo, shi)
        for half in range(2):
            h = 2 * p + half
            qo_ref[:, h * 256: h * 256 + 128] = q[:, h * 128:(h + 1) * 128].astype(BF16)
            keep = first if half == 0 else jnp.logical_not(first)
            qo_ref[:, h * 256 + 128: (h + 1) * 256] = jnp.where(keep, pair, 0.0).astype(BF16)

    kr = jnp.where(first, misc, 0.0)
    kr = kr + pltpu.roll(kr, B_QK_ROPE, 1)
    kr = _rope64x2(kr, cos, slo, shi).astype(BF16)
    ckv = _rms(bc[:, B_Q_LORA:], ckv_gain).astype(BF16)
    kv = _dot(ckv, wukv_ref[...])
    for h in range(B_HEADS):
        ko_ref[:, h * 256: h * 256 + 128] = kv[:, h * 128:(h + 1) * 128].astype(BF16)
        ko_ref[:, h * 256 + 128: (h + 1) * 256] = kr
    vo_ref[...] = kv[:, nope_w:].astype(BF16)


def _attn_kernel(q_ref, k_ref, v_ref, o_ref, m_scr, l_scr, acc_scr, *, G, dq, dv, tk, sub):
    tq = q_ref.shape[0]
    S = k_ref.shape[0]
    groups = [(g, r) for g in range(G) for r in range(tq // sub)]
    m_scr[...] = jnp.full(m_scr.shape, -jnp.inf, F32)
    l_scr[...] = jnp.zeros(l_scr.shape, F32)
    acc_scr[...] = jnp.zeros(acc_scr.shape, F32)

    stages = [(c, n) for c in range(S // tk) for n in range(len(groups))]

    def scores(stage):
        c, n = stage
        g, r = groups[n]
        return _dot_t(q_ref[r * sub:(r + 1) * sub, g * dq:(g + 1) * dq], k_ref[c * tk:(c + 1) * tk, :])

    s_next = scores(stages[0])
    for i, (c, n) in enumerate(stages):
        s = s_next
        if i + 1 < len(stages):
            s_next = scores(stages[i + 1])
        rows = slice(n * sub, (n + 1) * sub)
        blocks = [s[:, j * 128:(j + 1) * 128] for j in range(tk // 128)]
        m_blk = functools.reduce(jnp.maximum, blocks)
        m_old = m_scr[rows, :]
        m_new = jnp.maximum(m_old, jnp.max(m_blk, axis=-1, keepdims=True))
        alpha = jnp.exp2(m_old - m_new)
        p_blocks = [jnp.exp2(b - m_new) for b in blocks]
        l_scr[rows, :] = alpha * l_scr[rows, :] + functools.reduce(jnp.add, p_blocks)
        p = jnp.concatenate([b.astype(BF16) for b in p_blocks], axis=1)
        acc_scr[rows, :] = alpha * acc_scr[rows, :] + _dot(p, v_ref[c * tk:(c + 1) * tk, :])
        m_scr[rows, :] = m_new

    for n, (g, r) in enumerate(groups):
        rows = slice(n * sub, (n + 1) * sub)
        l = jnp.sum(l_scr[rows, :], axis=-1, keepdims=True)
        o_ref[r * sub:(r + 1) * sub, g * dv:(g + 1) * dv] = (acc_scr[rows, :] / l).astype(o_ref.dtype)


def _attention(q, k, v, *, B, S, Hk, G, dq, dv, tq, tk):
    assert dv == 128, "row statistics are kept 128 lanes wide to match the value width"
    kern = functools.partial(_attn_kernel, G=G, dq=dq, dv=dv, tk=tk, sub=ATTN_ROW_GROUP)
    M = G * tq
    return pl.pallas_call(
        kern,
        grid=(B, Hk, S // tq),
        in_specs=[pl.BlockSpec((None, tq, G * dq), lambda b, h, i: (b, i, h)),
                  pl.BlockSpec((None, S, dq), lambda b, h, i: (b, 0, h)),
                  pl.BlockSpec((None, S, dv), lambda b, h, i: (b, 0, h))],
        out_specs=pl.BlockSpec((None, tq, G * dv), lambda b, h, i: (b, i, h)),
        out_shape=jax.ShapeDtypeStruct((B, S, Hk * G * dv), BF16),
        scratch_shapes=[pltpu.VMEM((M, 128), F32), pltpu.VMEM((M, 128), F32), pltpu.VMEM((M, dv), F32)],
        compiler_params=_cparams(3),
        name=f"attn_g{G}_d{dq}",
    )(q, k, v)


def _log_sigmoid(x):
    return jnp.minimum(x, 0.0) - jnp.log1p(jnp.exp(-jnp.abs(x)))


def _mlstm_kernel(qf_ref, kf_ref, vf_ref, gf_ref, qr_ref, kr_ref, vr_ref, gr_ref, gb_ref,
                  of_ref, or_ref, C_scr, n_scr, m_scr, *, BB):
    L = MLSTM_CHUNK

    @pl.when(pl.program_id(1) == 0)
    def _():
        C_scr[...] = jnp.zeros(C_scr.shape, F32)
        n_scr[...] = jnp.zeros(n_scr.shape, F32)
        m_scr[...] = jnp.zeros(m_scr.shape, F32)

    glane = lax.broadcasted_iota(jnp.int32, (L, 128), 1)
    jj = lax.broadcasted_iota(jnp.int32, (L, L), 0)
    ss = lax.broadcasted_iota(jnp.int32, (L, L), 1)
    eye = jj == ss
    neg_inf = jnp.float32(-jnp.inf)
    dirs = ((qf_ref, kf_ref, vf_ref, gf_ref, of_ref, ss <= jj, jj <= ss),
            (qr_ref, kr_ref, vr_ref, gr_ref, or_ref, ss >= jj, jj >= ss))

    chains = []
    for d, (q_ref, k_ref, v_ref, g_ref, o_ref, seen, seen_t) in enumerate(dirs):
        for bb in range(BB):
            gates = g_ref[bb] + gb_ref[...]
            lsig = _log_sigmoid(gates)
            for h in range(C_HEADS):
                c = dict(st=(d * BB + bb) * C_HEADS + h, sl=slice(h * C_DQK, (h + 1) * C_DQK), bb=bb,
                         q_ref=q_ref, k_ref=k_ref, v_ref=v_ref, o_ref=o_ref, seen=seen, seen_t=seen_t)
                i_lane = MISC_GATE_LANE + d * (2 * C_HEADS) + h
                c["i_col"] = jnp.sum(jnp.where(glane == i_lane, gates, 0.0), axis=1, keepdims=True)
                c["f_col"] = jnp.sum(jnp.where(glane == i_lane + C_HEADS, lsig, 0.0), axis=1, keepdims=True)
                chains.append(c)
    for c in chains:
        c["f_row"] = jnp.sum(jnp.where(eye, c["f_col"], 0.0), axis=0, keepdims=True)
        c["i_row"] = jnp.sum(jnp.where(eye, c["i_col"], 0.0), axis=0, keepdims=True)
        c["b_row"] = jnp.sum(jnp.where(c["seen_t"], c["f_col"], 0.0), axis=0, keepdims=True)
        c["g_tot"] = jnp.sum(c["f_col"], axis=0, keepdims=True)
    for c in chains:
        c["b_col"] = jnp.sum(jnp.where(c["seen"], c["f_row"], 0.0), axis=1, keepdims=True)
    for c in chains:
        c["m_prev"] = m_scr[c["st"]][:, 0:1]
        c["dmat"] = jnp.where(c["seen"], c["b_col"] - c["b_row"] + c["i_row"], neg_inf)
        c["m_inter"] = c["b_col"] + c["m_prev"]
        c["a_col"] = c["g_tot"] - c["b_col"] + c["i_col"]
    for c in chains:
        c["m_j"] = jnp.maximum(c["m_inter"], jnp.max(c["dmat"], axis=1, keepdims=True))
        c["m_new"] = jnp.maximum(c["g_tot"] + c["m_prev"], jnp.max(c["a_col"], axis=0, keepdims=True))
    for c in chains:
        c["q"] = c["q_ref"][c["bb"], :, c["sl"]]
        c["k"] = c["k_ref"][c["bb"], :, c["sl"]]
        c["qb"], c["kb"] = c["q"].astype(BF16), c["k"].astype(BF16)
        c["vb"] = c["v_ref"][c["bb"], :, c["sl"]].astype(BF16)
        c["qk"] = _dot_t(c["qb"], c["kb"])
    for c in chains:
        c["qC"] = _dot(c["qb"], C_scr[c["st"]].astype(BF16))
    for c in chains:
        c["s"] = c["qk"] * jnp.exp(c["dmat"] - c["m_j"])
        c["inter"] = jnp.exp(c["m_inter"] - c["m_j"])
        c["wk"] = jnp.exp(c["a_col"] - c["m_new"]) * c["k"]
        c["decay"] = jnp.exp(c["g_tot"] + c["m_prev"] - c["m_new"])
    for c in chains:
        c["sv"] = _dot(c["s"].astype(BF16), c["vb"])
    for c in chains:
        c["upd"] = lax.dot_general(c["wk"].astype(BF16), c["vb"], (((0,), (0,)), ((), ())),
                                   preferred_element_type=F32)
    for c in chains:
        n_prev = n_scr[c["st"]]
        den = (jnp.sum(c["s"], axis=1, keepdims=True)
               + c["inter"] * jnp.sum(c["q"] * n_prev, axis=1, keepdims=True))
        num = c["sv"] + c["inter"] * c["qC"]
        c["o_ref"][c["bb"], :, c["sl"]] = num / jnp.maximum(jnp.abs(den), jnp.exp(-c["m_j"]))
        n_scr[c["st"]] = c["decay"] * n_prev + jnp.sum(c["wk"], axis=0, keepdims=True)
    for c in chains:
        C_scr[c["st"]] = c["decay"] * C_scr[c["st"]] + c["upd"]
        m_scr[c["st"]] = jnp.broadcast_to(c["m_new"], m_scr.shape[1:])


def _mlstm(qk3, pc3, misc3, gate_b_row, BB=MLSTM_BB):
    B, S, _ = pc3.shape
    BB = min(BB, B)
    L = MLSTM_CHUNK
    nc = S // L
    W = C_HEADS * C_DQK
    n_chain = 2 * BB * C_HEADS
    fwd = lambda col: (lambda b, c: (b, c, col))
    rev = lambda col: (lambda b, c: (b, nc - 1 - c, col))
    blk = lambda w, imap: pl.BlockSpec((BB, L, w), imap)
    out = jax.ShapeDtypeStruct((B, S, W), F32)
    return pl.pallas_call(
        functools.partial(_mlstm_kernel, BB=BB),
        grid=(B // BB, nc),
        in_specs=[blk(W, fwd(0)), blk(W, fwd(1)), blk(W, fwd(PC_CV // W)), blk(128, fwd(0)),
                  blk(W, rev(0)), blk(W, rev(1)), blk(W, rev(PC_CV // W)), blk(128, rev(0)),
                  pl.BlockSpec((1, 128), lambda b, c: (0, 0))],
        out_specs=[blk(W, fwd(0)), blk(W, rev(0))],
        out_shape=[out, out],
        scratch_shapes=[pltpu.VMEM((n_chain, C_DQK, C_DV), F32),
                        pltpu.VMEM((n_chain, 1, C_DQK), F32),
                        pltpu.VMEM((n_chain, 1, 128), F32)],
        compiler_params=_cparams(2),
        name="mlstm",
    )(qk3, qk3, pc3, misc3, qk3, qk3, pc3, misc3, gate_b_row)


ROW_TILE = 8
HALF_D = D_MODEL // 2
U32 = jnp.uint32
HI_MASK = 0xFFFF0000


def _pack_rows(y, o_ref):
    n = y.shape[0]
    for s in range(ROW_TILE):
        lo = y[:, s * 128:(s + 1) * 128].astype(BF16).astype(F32)
        hi = y[:, HALF_D + s * 128: HALF_D + (s + 1) * 128].astype(BF16).astype(F32)
        w = (lax.bitcast_convert_type(lo, U32) >> 16) | (lax.bitcast_convert_type(hi, U32) & U32(HI_MASK))
        o_ref[pl.ds(s, n, stride=ROW_TILE), :] = w


def _unpack_rows(ref, n):
    lo, hi = [], []
    for s in range(ROW_TILE):
        w = ref[pl.ds(s, n, stride=ROW_TILE), :]
        lo.append(lax.bitcast_convert_type(w << 16, F32))
        hi.append(lax.bitcast_convert_type(w & U32(HI_MASK), F32))
    return lo + hi


def _merge_kernel(ya_ref, yb_ref, hf_ref, hr_ref, co_ref, x_ref, mod_ref, hg_ref, wout_ref, pmg_ref,
                  pfg_ref, rw_ref, rb_ref, xo_ref, h2p_ref, idx_ref, gate_ref, mask_ref, y_scr):
    attn_heads = A_HEADS + B_HEADS
    for h in range(attn_heads + C_HEADS):
        sl = slice(h * HEAD_DIM, (h + 1) * HEAD_DIM)
        g = hg_ref[:, sl]
        if h < A_HEADS:
            y = _rms(ya_ref[:, sl].astype(F32), g)
        elif h < attn_heads:
            y = _rms(yb_ref[:, (h - A_HEADS) * HEAD_DIM:(h - A_HEADS + 1) * HEAD_DIM].astype(F32), g)
        else:
            cs = slice((h - attn_heads) * HEAD_DIM, (h - attn_heads + 1) * HEAD_DIM)
            y = _rms(hf_ref[:, cs] + hr_ref[:, cs], g) * jax.nn.sigmoid(co_ref[:, cs])
        y_scr[:, sl] = y.astype(BF16)
    y = _dot(y_scr[...], wout_ref[...])
    x_new = x_ref[...] + mod_ref[2:3, :] * _rms(y, pmg_ref[...])
    xo_ref[...] = x_new
    h2 = _rms(x_new, pfg_ref[...]) * (1.0 + mod_ref[4:5, :]) + mod_ref[3:4, :]
    _pack_rows(h2, h2p_ref)
    h_hi = h2.astype(BF16)
    h_lo = (h2 - h_hi.astype(F32)).astype(BF16)
    a = _dot(h_hi, rw_ref[...])
    b = _dot(h_lo, rw_ref[...])
    E = N_EXPERTS
    _route(a[:, :E] + (a[:, E:] + b[:, :E]) + b[:, E:], rb_ref[...], idx_ref, gate_ref, mask_ref)


def _merge(ya, yb, hf, hr, pc, x2, mod, head_gain, w_out_b, pm_gain, pf_gain, router_w, router_b, S):
    T, D = x2.shape
    tm = 512
    nb = S // tm
    aw, cw = A_HEADS * HEAD_DIM, C_HEADS * C_DV
    row = lambda w: pl.BlockSpec((tm, w), lambda i: (i, 0))
    full = lambda a: pl.BlockSpec(a.shape, lambda i: (0,) * a.ndim)
    return pl.pallas_call(
        _merge_kernel,
        grid=(T // tm,),
        in_specs=[row(aw), row(aw), row(cw), row(cw),
                  pl.BlockSpec((tm, cw), lambda i: (i, PC_CO // cw)),
                  row(D),
                  pl.BlockSpec((None, N_MOD, D), lambda i: (i // nb, 0, 0)),
                  full(head_gain), full(w_out_b), full(pm_gain), full(pf_gain), full(router_w),
                  full(router_b)],
        out_specs=[row(D), pl.BlockSpec((tm * ROW_TILE, 128), lambda i: (i, 0)),
                   row(8), row(8), row(N_EXPERTS)],
        out_shape=[jax.ShapeDtypeStruct((T, D), F32),
                   jax.ShapeDtypeStruct((T * ROW_TILE, 128), U32),
                   jax.ShapeDtypeStruct((T, 8), jnp.int32),
                   jax.ShapeDtypeStruct((T, 8), F32),
                   jax.ShapeDtypeStruct((T, N_EXPERTS), jnp.int32)],
        scratch_shapes=[pltpu.VMEM((tm, D), BF16)],
        compiler_params=_cparams(1),
        name="merge_out_proj",
    )(ya, yb, hf, hr, pc, x2, mod, head_gain, w_out_b, pm_gain, pf_gain, router_w, router_b)


def _route(logits, bias, idx_ref, gate_ref, mask_ref):
    scores = jax.nn.sigmoid(logits)
    sel = scores + bias
    lane = lax.broadcasted_iota(jnp.int32, scores.shape, 1).astype(F32)
    col = lax.broadcasted_iota(jnp.int32, idx_ref.shape, 1)
    idx = jnp.zeros(idx_ref.shape, F32)
    gate = jnp.zeros(gate_ref.shape, F32)
    mask = jnp.zeros(scores.shape, F32)
    for kk in range(TOP_K):
        mx = jnp.max(sel, axis=1, keepdims=True)
        am = jnp.min(jnp.where(sel == mx, lane, float(N_EXPERTS)), axis=1, keepdims=True)
        hit = lane == am
        sc = jnp.sum(jnp.where(hit, scores, 0.0), axis=1, keepdims=True)
        idx = jnp.where(col == kk, am, idx)
        gate = jnp.where(col == kk, sc, gate)
        mask = jnp.where(hit, 1.0, mask)
        sel = jnp.where(hit, -jnp.inf, sel)
    gate = gate / jnp.sum(gate, axis=1, keepdims=True) * ROUTED_SCALE
    idx_ref[...] = idx.astype(jnp.int32)
    gate_ref[...] = gate
    mask_ref[...] = mask.astype(jnp.int32)


TAIL_BITS = MOE_BLK.bit_length() - 1


def _tail_copies(tail_ref, zero_scr, xs_ref, zsem, fn):
    for e in range(N_EXPERTS):
        start_row = tail_ref[2 * e]
        n_tail = tail_ref[2 * e + 1]
        for bit in range(TAIL_BITS):
            size = 1 << bit
            cur = start_row + (n_tail & ~(2 * size - 1))
            cur = pl.multiple_of(cur * ROW_TILE, ROW_TILE)

            @pl.when((n_tail & size) != 0)
            def _():
                fn(pltpu.make_async_copy(zero_scr.at[pl.ds(0, size * ROW_TILE), :],
                                         xs_ref.at[pl.ds(cur, size * ROW_TILE), :], zsem))


def _row_tile(ref, row):
    start = row * ROW_TILE
    if not isinstance(row, int):
        start = pl.multiple_of(start, ROW_TILE)
    return ref.at[pl.ds(start, ROW_TILE), :]


def _dispatch_kernel(tail_ref, pos_ref, hp_ref, wsg_ref, wsu_ref, wsd_ref, xs_ref, ysh_ref,
                     zero_scr, sem, zsem, *, tm):
    i = pl.program_id(0)

    for r in range(tm):
        src = _row_tile(hp_ref, r)
        for kk in range(TOP_K):
            pltpu.make_async_copy(src, _row_tile(xs_ref, pos_ref[0, r * TOP_K + kk]),
                                  sem).start(priority=kk % 2)

    @pl.when(i == 0)
    def _():
        zero_scr[...] = jnp.zeros(zero_scr.shape, zero_scr.dtype)
        _tail_copies(tail_ref, zero_scr, xs_ref, zsem, lambda cp: cp.start())
        _tail_copies(tail_ref, zero_scr, xs_ref, zsem, lambda cp: cp.wait())

    h = jnp.concatenate([c.astype(BF16) for c in _unpack_rows(hp_ref, tm)], axis=1)
    gte = _dot(h, wsg_ref[...])
    a = gte * jax.nn.sigmoid(gte) * _dot(h, wsu_ref[...])
    ysh_ref[...] = _dot(a.astype(BF16), wsd_ref[...])

    for kk in range(TOP_K):
        pltpu.make_async_copy(hp_ref, xs_ref.at[pl.ds(0, tm * ROW_TILE), :], sem).wait()


def _dispatch(h2p, pos3, tail_info, ws_gate_b, ws_up_b, ws_down_b, P):
    T, D = h2p.shape[0] // ROW_TILE, D_MODEL
    tm = pos3.shape[2] // TOP_K
    kern = functools.partial(_dispatch_kernel, tm=tm)
    full = lambda a: pl.BlockSpec(a.shape, lambda i, tail: (0,) * a.ndim)
    return pl.pallas_call(
        kern,
        grid_spec=pltpu.PrefetchScalarGridSpec(
            num_scalar_prefetch=1,
            grid=(T // tm,),
            in_specs=[pl.BlockSpec((None, 1, tm * TOP_K), lambda i, tail: (i, 0, 0),
                                   memory_space=pltpu.SMEM),
                      pl.BlockSpec((tm * ROW_TILE, 128), lambda i, tail: (i, 0)),
                      full(ws_gate_b), full(ws_up_b), full(ws_down_b)],
            out_specs=[pl.BlockSpec(memory_space=pl.ANY),
                       pl.BlockSpec((tm, D), lambda i, tail: (i, 0))],
            scratch_shapes=[pltpu.VMEM((MOE_BLK // 2 * ROW_TILE, 128), U32),
                            pltpu.SemaphoreType.DMA(()), pltpu.SemaphoreType.DMA(())]),
        out_shape=[jax.ShapeDtypeStruct((P * ROW_TILE, 128), U32),
                   jax.ShapeDtypeStruct((T, D), F32)],
        compiler_params=_cparams(1),
        name="moe_dispatch",
    )(tail_info, pos3, h2p, ws_gate_b, ws_up_b, ws_down_b)


EXPERT_TILES = 4


def _expert_kernel(blk_e_ref, nused_ref, next_ref, xs_ref, wg_hbm, wu_hbm, wd_hbm, ys_ref,
                   wg_f, wu_f, wd_f, wg_b, wu_b, wd_b, sems, *, layer):
    step = pl.program_id(0)
    rows = MOE_BLK * ROW_TILE

    def fetch(ex):
        return [pltpu.make_async_copy(hbm.at[layer, ex], buf, sems.at[n])
                for n, (hbm, buf) in enumerate(((wg_hbm, wg_f), (wu_hbm, wu_f), (wd_hbm, wd_f)))]

    @pl.when(step == 0)
    def _():
        for cp in fetch(blk_e_ref[0]):
            cp.start(priority=1)

    for j in range(EXPERT_TILES):
        i = step * EXPERT_TILES + j
        e = blk_e_ref[i]
        e_prev = blk_e_ref[jnp.maximum(i - 1, 0)]
        used = i < nused_ref[0]
        first = used & ((i == 0) | (e != e_prev))
        x_view = xs_ref.at[pl.ds(j * rows, rows), :]
        y_view = ys_ref.at[pl.ds(j * rows, rows), :]

        def compute(switch, e=e, x_view=x_view, y_view=y_view):
            x = jnp.concatenate([c.astype(BF16) for c in _unpack_rows(x_view, MOE_BLK)], axis=1)
            if switch:
                for cp in fetch(e):
                    cp.wait()
                wg_b[...] = wg_f[...].astype(BF16)
            gte = _dot(x, wg_b[...])
            if switch:
                wu_b[...] = wu_f[...].astype(BF16)
            up = _dot(x, wu_b[...])
            if switch:
                wd_b[...] = wd_f[...].astype(BF16)
                e_next = next_ref[e]

                @pl.when(e_next >= 0)
                def _():
                    for cp in fetch(e_next):
                        cp.start(priority=1)
            a = gte * jax.nn.sigmoid(gte) * up
            _pack_rows(_dot(a.astype(BF16), wd_b[...]), y_view)

        pl.when(first)(functools.partial(compute, True))
        pl.when(used & jnp.logical_not(first))(functools.partial(compute, False))

        @pl.when(jnp.logical_not(used))
        def _(y_view=y_view):
            y_view[...] = jnp.zeros(y_view.shape, y_view.dtype)


def _experts(xs, blk_e, nused, next_e, w_gate, w_up, w_down, layer):
    P = xs.shape[0] // ROW_TILE
    D, F = w_gate.shape[-2:]
    NB = P // MOE_BLK
    assert NB % EXPERT_TILES == 0
    rows = EXPERT_TILES * MOE_BLK * ROW_TILE
    last = lambda s, nu: jnp.minimum(s, (nu[0] - 1) // EXPERT_TILES)
    hbm = pl.BlockSpec(memory_space=pl.ANY)
    return pl.pallas_call(
        functools.partial(_expert_kernel, layer=layer),
        grid_spec=pltpu.PrefetchScalarGridSpec(
            num_scalar_prefetch=3,
            grid=(NB // EXPERT_TILES,),
            in_specs=[pl.BlockSpec((rows, 128), lambda s, be, nu, nx: (last(s, nu), 0)),
                      hbm, hbm, hbm],
            out_specs=pl.BlockSpec((rows, 128), lambda s, be, nu, nx: (s, 0)),
            scratch_shapes=[pltpu.VMEM((D, F), F32), pltpu.VMEM((D, F), F32), pltpu.VMEM((F, D), F32),
                            pltpu.VMEM((D, F), BF16), pltpu.VMEM((D, F), BF16),
                            pltpu.VMEM((F, D), BF16), pltpu.SemaphoreType.DMA((3,))]),
        out_shape=jax.ShapeDtypeStruct((P * ROW_TILE, 128), U32),
        compiler_params=_cparams(1),
        name="moe_experts",
    )(blk_e, nused, next_e, xs, w_gate, w_up, w_down)


def _combine_kernel(pos_ref, pos_next_ref, ysh_ref, gate_ref, x_ref, mod_ref, pg_ref, ys_ref,
                    xo_ref, gbuf, sems, *, tm):
    i = pl.program_id(0)
    slot = i % 2

    def gather(p_ref, s, straight_line):
        def start(r, c=None):
            for kk in range(TOP_K):
                pltpu.make_async_copy(_row_tile(ys_ref, p_ref[0, r * TOP_K + kk]),
                                      _row_tile(gbuf.at[s, kk], r), sems.at[s]).start(priority=kk % 2)
            return c
        if straight_line:
            for r in range(tm):
                start(r)
        else:
            lax.fori_loop(0, tm, start, 0)

    @pl.when(i == 0)
    def _():
        gather(pos_ref, 0, False)

    for s in range(2):
        @pl.when((i + 1 < pl.num_programs(0)) & (slot == 1 - s))
        def _():
            gather(pos_next_ref, s, True)

    for kk in range(TOP_K):
        pltpu.make_async_copy(ys_ref.at[pl.ds(0, tm * ROW_TILE), :], gbuf.at[slot, kk],
                              sems.at[slot]).wait()
    RG = 64

    def routed(rg):
        rows = slice(rg * RG, (rg + 1) * RG)
        gate = gate_ref[rows, :]
        g_wide = [jnp.broadcast_to(gate[:, kk:kk + 1], (RG, 128)) for kk in range(TOP_K)]
        for s in range(ROW_TILE):
            lo = hi = None
            for kk in range(TOP_K):
                w = gbuf[slot, kk, pl.ds(rg * RG * ROW_TILE + s, RG, stride=ROW_TILE), :]
                t_lo = g_wide[kk] * lax.bitcast_convert_type(w << 16, F32)
                t_hi = g_wide[kk] * lax.bitcast_convert_type(w & U32(HI_MASK), F32)
                lo = t_lo if lo is None else lo + t_lo
                hi = t_hi if hi is None else hi + t_hi
            xo_ref[rows, s * 128:(s + 1) * 128] = lo
            xo_ref[rows, HALF_D + s * 128: HALF_D + (s + 1) * 128] = hi

    for rg in range(tm // RG):
        routed(rg)
    y = xo_ref[...] + ysh_ref[...]
    xo_ref[...] = x_ref[...] + mod_ref[5:6, :] * _rms(y, pg_ref[...])


def _combine(pos3, ysh, gate8, x2, mod, post_gain, ys, S):
    T, D = x2.shape
    tm = pos3.shape[2] // TOP_K
    nb = S // tm
    kern = functools.partial(_combine_kernel, tm=tm)
    row = lambda w: pl.BlockSpec((tm, w), lambda i: (i, 0))
    full = lambda a: pl.BlockSpec(a.shape, lambda i: (0,) * a.ndim)
    n_steps = T // tm
    return pl.pallas_call(
        kern,
        grid=(n_steps,),
        in_specs=[pl.BlockSpec((None, 1, tm * TOP_K), lambda i: (i, 0, 0), memory_space=pltpu.SMEM),
                  pl.BlockSpec((None, 1, tm * TOP_K), lambda i: (jnp.minimum(i + 1, n_steps - 1), 0, 0),
                               memory_space=pltpu.SMEM),
                  row(D), row(8), row(D),
                  pl.BlockSpec((None, N_MOD, D), lambda i: (i // nb, 0, 0)),
                  full(post_gain),
                  pl.BlockSpec(memory_space=pl.ANY)],
        out_specs=row(D),
        out_shape=jax.ShapeDtypeStruct((T, D), F32),
        scratch_shapes=[pltpu.VMEM((2, TOP_K, tm * ROW_TILE, 128), U32),
                        pltpu.SemaphoreType.DMA((2,))],
        compiler_params=_cparams(1),
        name="moe_combine",
    )(pos3, pos3, ysh, gate8, x2, mod, post_gain, ys)


def _route_plan(idx8, mask, n_blocks):
    idx = idx8[:, :TOP_K]
    counts = jnp.sum(mask, axis=0)
    rank = jnp.cumsum(mask, axis=0) - mask
    padded = (counts + MOE_BLK - 1) // MOE_BLK * MOE_BLK
    pend = jnp.cumsum(padded)
    pstart = pend - padded
    pos = jnp.take_along_axis(pstart[None, :] + rank, idx, axis=1).astype(jnp.int32)
    blk_start = jnp.arange(n_blocks, dtype=jnp.int32) * MOE_BLK
    blk_e = jnp.minimum(jnp.sum(pend[None, :] <= blk_start[:, None], axis=1), N_EXPERTS - 1)
    nused = (pend[-1:] // MOE_BLK).astype(jnp.int32)
    tail = jnp.stack([pstart + counts, padded - counts], axis=1).reshape(-1).astype(jnp.int32)
    ids = jnp.where(counts > 0, jnp.arange(N_EXPERTS), N_EXPERTS)
    after = jnp.concatenate([lax.cummin(ids, reverse=True)[1:], jnp.full((1,), N_EXPERTS)])
    next_e = jnp.where(after < N_EXPERTS, after, -1).astype(jnp.int32)
    return pos, blk_e.astype(jnp.int32), nused, tail, next_e


def _rope_tables(S, rot_dim):
    rows = S // GRID_W
    row = jnp.repeat(jnp.arange(rows, dtype=F32), GRID_W)
    col = jnp.tile(jnp.arange(GRID_W, dtype=F32), rows)
    axis_dim = rot_dim // 2
    inv_freq = ROPE_THETA ** (-jnp.arange(0, axis_dim, 2, dtype=F32) / axis_dim)
    ang_r = row[:, None] * inv_freq
    ang_c = col[:, None] * inv_freq
    z = jnp.zeros_like(ang_r)
    cos = jnp.concatenate([jnp.cos(ang_r)] * 2 + [jnp.cos(ang_c)] * 2, axis=1)
    sin_lo = jnp.concatenate([-jnp.sin(ang_r), z, -jnp.sin(ang_c), z], axis=1)
    sin_hi = jnp.concatenate([z, jnp.sin(ang_r), z, jnp.sin(ang_c)], axis=1)
    reps = 128 // rot_dim
    return tuple(jnp.tile(t, (1, reps)) for t in (cos, sin_lo, sin_hi))


def _prep_w_in(w):
    widths = (768, 256, 256, B_Q_LORA, B_KV_LORA, B_QK_ROPE, 512, 512, 512, 512, 4 * C_HEADS)
    offs = np.cumsum(widths)[:-1].tolist()
    a_q, a_k, a_v, b_cq, b_ckv, b_kr, c_q, c_k, c_v, c_o, c_g = jnp.split(w, offs, axis=-1)
    pad = jnp.zeros(w.shape[:-1] + (128 - B_QK_ROPE - 4 * C_HEADS,), w.dtype)
    out = jnp.concatenate([a_q, a_k, a_v, b_cq, b_ckv, b_kr, c_g, pad, c_v, c_o, c_q, c_k], axis=-1)
    assert out.shape[-1] == PROJ_PAD
    return out.astype(BF16)


def _split_hi_lo(w):
    hi = w.astype(BF16)
    lo = (w - hi.astype(F32)).astype(BF16)
    return jnp.concatenate([hi, lo], axis=1)


def _heads_split(w, n_heads, first):
    K = w.shape[0]
    w3 = w.reshape(K, n_heads, -1)
    return jnp.concatenate([w3[:, :, :first].reshape(K, -1), w3[:, :, first:].reshape(K, -1)],
                           axis=1).astype(BF16)


def kernel(x, c, ada_w, ada_b, pre_mix_gain, w_in, a_q_gain, a_k_gain, b_cq_gain, b_ckv_gain, w_uq, w_ukv, c_conv_w, c_conv_b, c_gate_b, head_out_gain, w_out, post_mix_gain, pre_ffn_gain, router_w, router_bias, exp_w_gate, exp_w_up, exp_w_down, sh_w_gate, sh_w_up, sh_w_down, post_ffn_gain):
    B, S, D = x.shape
    T = B * S
    L = ada_w.shape[0]
    assert D == D_MODEL and S % 512 == 0 and T % 1024 == 0, "row tiles assume these multiples"
    n_blocks = T * TOP_K // MOE_BLK + N_EXPERTS
    row2 = lambda v: v.reshape(1, -1)

    tabs_a = _rope_tables(S, HEAD_DIM)
    tabs_b = _rope_tables(S, B_QK_ROPE)
    c_pad = jnp.zeros((8, D), F32).at[:B].set(c)
    mods = _modulation(c_pad, ada_w, ada_b)[:, :B].reshape(L, B, N_MOD, D)

    x2 = x.reshape(T, D)
    for l in range(L):
        mod = mods[l]
        qa, ka, va, qb, kb, vb, pc, qk, misc = _in_proj(
            x2, row2(pre_mix_gain[l]), mod, _prep_w_in(w_in[l]), tabs_a, tabs_b,
            row2(a_q_gain[l]), row2(a_k_gain[l]), row2(b_cq_gain[l]), row2(b_ckv_gain[l]),
            _heads_split(w_uq[l], B_HEADS, B_QK_NOPE), _heads_split(w_ukv[l], B_HEADS, B_QK_NOPE),
            c_conv_w[l], row2(c_conv_b[l]), S)
        pc3 = pc.reshape(B, S, -1)

        ya = _attention(qa.reshape(B, S, -1), ka.reshape(B, S, -1), va.reshape(B, S, -1),
                        B=B, S=S, Hk=A_KV_HEADS, G=A_HEADS // A_KV_HEADS, dq=HEAD_DIM, dv=HEAD_DIM,
                        tq=min(1024, S), tk=512)

        yb = _attention(qb.reshape(B, S, -1), kb.reshape(B, S, -1), vb.reshape(B, S, -1),
                        B=B, S=S, Hk=B_HEADS, G=1, dq=256, dv=B_V_DIM, tq=min(2048, S), tk=512)

        qk3 = qk.reshape(B, S, -1)
        gate_b_row = jnp.zeros((1, 128), F32).at[0, MISC_GATE_LANE:MISC_GATE_LANE + 4 * C_HEADS].set(c_gate_b[l])
        hf, hr = _mlstm(qk3, pc3, misc.reshape(B, S, 128), gate_b_row)

        x2, h2p, idx8, gate8, mask = _merge(
            ya.reshape(T, -1), yb.reshape(T, -1), hf.reshape(T, -1), hr.reshape(T, -1), pc, x2, mod,
            row2(head_out_gain[l]), w_out[l].astype(BF16), row2(post_mix_gain[l]),
            row2(pre_ffn_gain[l]), _split_hi_lo(router_w[l]), row2(router_bias[l]), S)
        pos, blk_e, nused, tail, next_e = _route_plan(idx8, mask, n_blocks)
        xs, ysh = _dispatch(h2p, pos.reshape(T // DISPATCH_TM, 1, DISPATCH_TM * TOP_K), tail,
                            sh_w_gate[l].astype(BF16), sh_w_up[l].astype(BF16),
                            sh_w_down[l].astype(BF16), n_blocks * MOE_BLK)
        ys = _experts(xs, blk_e, nused, next_e, exp_w_gate, exp_w_up, exp_w_down, l)
        x2 = _combine(pos.reshape(T // COMBINE_TM, 1, COMBINE_TM * TOP_K), ysh, gate8, x2, mod,
                      row2(post_ffn_gain[l]), ys, S)
    return x2.reshape(B, S, D)
```
